```python
import math
import functools
import jax
import jax.numpy as jnp
from jax import lax
import numpy as np

D_MODEL = 1024
BATCH = 4
SEQ = 4096
DEPTH = 4
DEC_BATCH = 32
DEC_SEQ = 4
PAST_LEN = 8192
PAGE_SIZE = 128

NSA_HEADS = 8
NSA_KV_HEADS = 2
HEAD_DIM = 64
NSA_GROUP = NSA_HEADS // NSA_KV_HEADS
NSA_Q_W = NSA_HEADS * HEAD_DIM
NSA_KV_W = NSA_KV_HEADS * HEAD_DIM
CMP_BLOCK = 32
CMP_STRIDE = CMP_BLOCK // 2
CMP_HIDDEN = HEAD_DIM
SEL_BLOCK = 64
N_SEL = 16
WINDOW = 512
Q_BLOCK = 128
FORCE_BONUS = 1e6
ROPE_THETA = 10000.0
DN_HEADS = 4
DN_DK = 128
DN_DV = 128
DN_QK_W = DN_HEADS * DN_DK
DN_W = DN_HEADS * DN_DV
DN_QKV_W = 2 * DN_QK_W + DN_W
CONV_W = 4
DN_CHUNK = 64
D_FF = 2816
EPS = 1e-6
IN_WIDTHS = (NSA_Q_W, 6 * NSA_KV_W, 3 * NSA_HEADS, DN_QKV_W, DN_HEADS, DN_HEADS, DN_W, 2 * D_MODEL)
IN_W = sum(IN_WIDTHS)

kernel_name = 'hybrid_nsa_gdn_macaron_step'


def rmsnorm(x, w):
    xf = x.astype(jnp.float32)
    y = xf * lax.rsqrt(jnp.mean(xf * xf, axis=-1, keepdims=True) + EPS)
    return (y * w.astype(jnp.float32)).astype(x.dtype)


def l2norm(x):
    xf = x.astype(jnp.float32)
    return xf * lax.rsqrt(jnp.sum(xf * xf, axis=-1, keepdims=True) + EPS)


def rope(x, pos):
    half = HEAD_DIM // 2
    freq = ROPE_THETA ** (-jnp.arange(half, dtype=jnp.float32) / half)
    ang = pos.astype(jnp.float32)[:, None] * freq[None, :]
    cos, sin = jnp.cos(ang)[:, None, :], jnp.sin(ang)[:, None, :]
    xf = x.astype(jnp.float32)
    x1, x2 = xf[..., :half], xf[..., half:]
    return jnp.concatenate([x1 * cos - x2 * sin, x2 * cos + x1 * sin], axis=-1).astype(x.dtype)


def masked_softmax(s, mask):
    s = jnp.where(mask, s.astype(jnp.float32), -1e30)
    p = jnp.exp(s - jnp.max(s, axis=-1, keepdims=True)) * mask
    return p / jnp.maximum(jnp.sum(p, axis=-1, keepdims=True), 1e-30)


def half_ffn(x, norm_w, w_in, w_out):
    gate, up = jnp.split(rmsnorm(x, norm_w) @ w_in, 2, axis=-1)
    return x + 0.5 * ((jax.nn.silu(gate) * up) @ w_out)


def split_in(z):
    offs = np.cumsum(IN_WIDTHS)[:-1].tolist()
    return jnp.split(z, offs, axis=-1)


def compress(rows, pos_emb, w1, w2):
    B, T = rows.shape[:2]
    nc = (T - CMP_BLOCK) // CMP_STRIDE + 1
    seg = rows[:, :(nc + 1) * CMP_STRIDE].reshape(B, nc + 1, CMP_STRIDE, NSA_KV_HEADS, HEAD_DIM)
    blk = jnp.concatenate([seg[:, :-1], seg[:, 1:]], axis=2) + pos_emb[:, None, :]
    flat = jnp.moveaxis(blk, 3, 2).reshape(B, nc, NSA_KV_HEADS, CMP_BLOCK * HEAD_DIM)
    return jax.nn.gelu(flat @ w1) @ w2


def to_blocks(rows):
    B, T = rows.shape[:2]
    ns = -(-T // SEL_BLOCK)
    rows = jnp.pad(rows, ((0, 0), (0, ns * SEL_BLOCK - T), (0, 0), (0, 0)))
    return rows.reshape(B, ns, SEL_BLOCK, NSA_KV_HEADS, HEAD_DIM)


def overlap_matrix(nc, ns):
    i = jnp.arange(nc)[:, None]
    j = jnp.arange(ns)[None, :]
    return ((i * CMP_STRIDE < (j + 1) * SEL_BLOCK) & (i * CMP_STRIDE + CMP_BLOCK > j * SEL_BLOCK)).astype(jnp.float32)


def nsa_prepare(q, kv, gate, pos):
    B, T = q.shape[:2]
    q = q.reshape(B, T, NSA_HEADS, HEAD_DIM)
    kv = kv.reshape(B, T, 6, NSA_KV_HEADS, HEAD_DIM)
    rows = jnp.stack([kv[:, :, 0], kv[:, :, 1], rope(kv[:, :, 2], pos), kv[:, :, 3]], axis=2)
    win_rows = jnp.stack([rope(kv[:, :, 4], pos), kv[:, :, 5]], axis=2)
    gates = jax.nn.sigmoid(gate).reshape(B, T, NSA_HEADS, 3)
    return q, rope(q, pos), rows, win_rows, gates


def nsa_views(rows, cmp_params):
    pos_emb, w1, w2 = cmp_params
    kc = compress(rows[:, :, 0], pos_emb[0], w1[0], w2[0])
    vc = compress(rows[:, :, 1], pos_emb[1], w1[1], w2[1])
    cmp_end = jnp.arange(kc.shape[1], dtype=jnp.int32) * CMP_STRIDE + CMP_BLOCK - 1
    return kc, vc, cmp_end, to_blocks(rows[:, :, 2]), to_blocks(rows[:, :, 3])


def nsa_attend(q_raw, q_rot, qpos, kc, vc, cmp_end, ks_blk, vs_blk, kw, vw, kpos_w, gates):
    B, Tq = q_raw.shape[:2]
    scale = HEAD_DIM ** -0.5
    qr = q_raw.reshape(B, Tq, NSA_KV_HEADS, NSA_GROUP, HEAD_DIM)
    qs = q_rot.reshape(B, Tq, NSA_KV_HEADS, NSA_GROUP, HEAD_DIM)
    s_c = jnp.einsum('btkgd,bckd->bkgtc', qr, kc) * scale
    p_c = masked_softmax(s_c, cmp_end[None, :] <= qpos[:, None])
    o_c = jnp.einsum('bkgtc,bckd->btkgd', p_c.astype(vc.dtype), vc)
    nc, ns = kc.shape[1], ks_blk.shape[1]
    imp = jnp.einsum('bkgtc,cs->bkts', p_c, overlap_matrix(nc, ns))
    blk = jnp.arange(ns, dtype=jnp.int32)[None, :]
    cur = (qpos // SEL_BLOCK)[:, None]
    valid = blk * SEL_BLOCK <= qpos[:, None]
    forced = (blk == 0) | (blk == cur) | (blk == cur - 1)
    imp = jnp.where(valid, imp + jnp.where(forced, FORCE_BONUS, 0.0), -jnp.inf)
    _, idx = lax.top_k(imp, min(N_SEL, ns))
    b_i = jnp.arange(B)[:, None, None, None]
    h_i = jnp.arange(NSA_KV_HEADS)[None, :, None, None]
    kg = jnp.moveaxis(ks_blk, 3, 1)[b_i, h_i, idx]
    vg = jnp.moveaxis(vs_blk, 3, 1)[b_i, h_i, idx]
    kpos_s = idx[..., None] * SEL_BLOCK + jnp.arange(SEL_BLOCK, dtype=jnp.int32)
    mask_s = (kpos_s <= qpos[None, None, :, None, None])
    s_s = jnp.einsum('btkgd,bktnld->bkgtnl', qs, kg) * scale
    sh = s_s.shape
    p_s = masked_softmax(s_s.reshape(sh[:4] + (-1,)), mask_s.reshape(B, NSA_KV_HEADS, 1, Tq, -1)).reshape(sh)
    o_s = jnp.einsum('bkgtnl,bktnld->btkgd', p_s.astype(vg.dtype), vg)
    dist = qpos[:, None] - kpos_w[None, :]
    mask_w = (dist >= 0) & (dist < WINDOW) & (kpos_w[None, :] >= 0)
    s_w = jnp.einsum('btkgd,bjkd->bkgtj', qs, kw) * scale
    p_w = masked_softmax(s_w, mask_w)
    o_w = jnp.einsum('bkgtj,bjkd->btkgd', p_w.astype(vw.dtype), vw)
    g = gates.reshape(B, Tq, NSA_KV_HEADS, NSA_GROUP, 3).astype(o_c.dtype)
    o = g[..., 0:1] * o_c + g[..., 1:2] * o_s + g[..., 2:3] * o_w
    return o.reshape(B, Tq, NSA_Q_W)


def nsa_prompt(q, kv, gate, pos, cmp_params):
    B, T = q.shape[:2]
    q_raw, q_rot, rows, win_rows, gates = nsa_prepare(q, kv, gate, pos)
    kc, vc, cmp_end, ks_blk, vs_blk = nsa_views(rows, cmp_params)
    win_pad = jnp.pad(win_rows, ((0, 0), (WINDOW, 0), (0, 0), (0, 0), (0, 0)))

    def block(c):
        s0 = c * Q_BLOCK
        qpos = s0 + jnp.arange(Q_BLOCK, dtype=jnp.int32)
        sl = lambda a: lax.dynamic_slice_in_dim(a, s0, Q_BLOCK, axis=1)
        wk = lax.dynamic_slice_in_dim(win_pad, s0, Q_BLOCK + WINDOW, axis=1)
        kpos = s0 - WINDOW + jnp.arange(Q_BLOCK + WINDOW, dtype=jnp.int32)
        return nsa_attend(sl(q_raw), sl(q_rot), qpos, kc, vc, cmp_end, ks_blk, vs_blk,
                          wk[:, :, 0], wk[:, :, 1], kpos, sl(gates))

    out = lax.map(block, jnp.arange(T // Q_BLOCK, dtype=jnp.int32))
    out = jnp.moveaxis(out, 0, 1).reshape(B, T, NSA_Q_W)
    return out, rows, win_rows[:, -min(WINDOW, T):]


def nsa_sample(q, kv, gate, pos, cmp_params, cache_l, page_table, win_state):
    B, T = q.shape[:2]
    q_raw, q_rot, rows, win_rows, gates = nsa_prepare(q, kv, gate, pos)
    n_pages = page_table.shape[1]
    past = cache_l[page_table].reshape(B, n_pages * PAGE_SIZE, 4, NSA_KV_HEADS, HEAD_DIM)
    all_rows = jnp.concatenate([past.astype(rows.dtype), rows], axis=1)
    kc, vc, cmp_end, ks_blk, vs_blk = nsa_views(all_rows, cmp_params)
    w_buf = win_state.shape[1]
    win_all = jnp.concatenate([win_state.astype(win_rows.dtype), win_rows], axis=1)
    kpos = pos[0] - w_buf + jnp.arange(w_buf + T, dtype=jnp.int32)
    out = nsa_attend(q_raw, q_rot, pos, kc, vc, cmp_end, ks_blk, vs_blk,
                     win_all[:, :, 0], win_all[:, :, 1], kpos, gates)
    return out, rows, win_all[:, -w_buf:]


def gated_delta_chunked(q, k, v, g, beta, S0, chunk):
    B, T, H, DK = q.shape
    DV = v.shape[-1]
    N = T // chunk

    def to_chunks(x):
        x = x.astype(jnp.float32).reshape((B, N, chunk, H) + x.shape[3:])
        return jnp.moveaxis(jnp.moveaxis(x, 3, 2), 1, 0)

    qc, kc, vc, gc, bc = [to_chunks(t) for t in (q, k, v, g, beta)]
    G = jnp.cumsum(gc, axis=-1)
    idx = jnp.arange(chunk)
    incl = idx[:, None] >= idx[None, :]
    strict = idx[:, None] > idx[None, :]
    L = jnp.exp(jnp.where(incl, G[..., :, None] - G[..., None, :], -jnp.inf))
    kb = kc * bc[..., None]
    A = jnp.where(strict, jnp.einsum('nbhid,nbhjd->nbhij', kb, kc) * L, 0.0)
    IA = A + jnp.eye(chunk, dtype=jnp.float32)
    u = lax.linalg.triangular_solve(IA, vc * bc[..., None], left_side=True, lower=True, unit_diagonal=True)
    w = lax.linalg.triangular_solve(IA, kb * jnp.exp(G)[..., None], left_side=True, lower=True, unit_diagonal=True)
    qk = jnp.einsum('nbhid,nbhjd->nbhij', qc, kc) * L
    q_dec = qc * jnp.exp(G)[..., None]
    k_dec = kc * jnp.exp(G[..., -1:] - G)[..., None]
    g_last = jnp.exp(G[..., -1])

    def step(S, xs):
        u_n, w_n, qk_n, qd_n, kd_n, gl_n = xs
        v_new = u_n - jnp.einsum('bhck,bhkv->bhcv', w_n, S)
        o = jnp.einsum('bhck,bhkv->bhcv', qd_n, S) + jnp.einsum('bhij,bhjv->bhiv', qk_n, v_new)
        S = S * gl_n[..., None, None] + jnp.einsum('bhck,bhcv->bhkv', kd_n, v_new)
        return S, o

    S, o = lax.scan(step, S0.astype(jnp.float32), (u, w, qk, q_dec, k_dec, g_last))
    o = jnp.moveaxis(jnp.moveaxis(o, 0, 1), 2, 3).reshape(B, T, H, DV)
    return o, S


def deltanet(qkv, a, b, zg, prefix, S0, dn_params, chunk):
    conv_w, A_log, dt_bias, norm_w = dn_params
    B, T = qkv.shape[:2]
    xpad = jnp.concatenate([prefix.astype(qkv.dtype), qkv], axis=1)
    conv = jax.nn.silu(sum(xpad[:, i:i + T] * conv_w[i] for i in range(CONV_W)))
    q, k, v = jnp.split(conv, [DN_QK_W, 2 * DN_QK_W], axis=-1)
    q = l2norm(q.reshape(B, T, DN_HEADS, DN_DK)) * DN_DK ** -0.5
    k = l2norm(k.reshape(B, T, DN_HEADS, DN_DK))
    v = v.reshape(B, T, DN_HEADS, DN_DV)
    g = -jnp.exp(A_log.astype(jnp.float32)) * jax.nn.softplus(a.astype(jnp.float32) + dt_bias.astype(jnp.float32))
    beta = jax.nn.sigmoid(b.astype(jnp.float32))
    o, S = gated_delta_chunked(q, k, v, g, beta, S0, chunk)
    o = rmsnorm(o.astype(qkv.dtype), norm_w) * jax.nn.silu(zg.reshape(B, T, DN_HEADS, DN_DV))
    return o.reshape(B, T, DN_W), S.astype(S0.dtype), xpad[:, T:]


def trunk_layer(x, pos, lp, nsa_fn, dn_prefix, dn_S0, dn_chunk):
    (f1n, f1i, f1o, mn, w_in, cmp_params, dn_params, wa, wb, wo, f2n, f2i, f2o) = lp
    B, T = x.shape[:2]
    x = half_ffn(x, f1n, f1i, f1o)
    q, kv, gate, qkv, a, b, zg, merge = split_in(rmsnorm(x, mn) @ w_in)
    o_a, nsa_rows, nsa_win = nsa_fn(q, kv, gate, pos, cmp_params)
    o_b, S, conv = deltanet(qkv, a, b, zg, dn_prefix, dn_S0, dn_params, dn_chunk)
    gm = jax.nn.sigmoid(merge).reshape(B, T, 2, D_MODEL)
    x = x + (gm[:, :, 0] * (o_a @ wa) + gm[:, :, 1] * (o_b @ wb)) @ wo
    x = half_ffn(x, f2n, f2i, f2o)
    return x, nsa_rows, nsa_win, S, conv


def setup_inputs(seed: int = 0) -> dict:
    key = jax.random.key(seed)
    ks = jax.random.split(key, 32)
    f32 = jnp.float32
    n_pages = PAST_LEN // PAGE_SIZE
    n_used = DEC_BATCH * n_pages
    n_phys = n_used + n_used // 4
    w_buf = min(WINDOW, PAST_LEN)

    def nrm(k, shape, scale=1.0):
        return jax.random.normal(k, shape, f32) * scale

    def gain(k, shape):
        return 1.0 + nrm(k, shape, 0.01)

    page_table = jax.random.permutation(ks[4], n_phys)[:n_used].reshape(DEC_BATCH, n_pages).astype(jnp.int32)
    dt = jnp.exp(jax.random.uniform(ks[14], (DEPTH, DN_HEADS), f32, math.log(1e-3), math.log(1e-1)))
    return {
        'x_prompt': nrm(ks[0], (BATCH, SEQ, D_MODEL)),
        'x_sample': nrm(ks[1], (DEC_BATCH, DEC_SEQ, D_MODEL)),
        'cache_nsa_kv': nrm(ks[2], (DEPTH, n_phys, PAGE_SIZE, 4, NSA_KV_HEADS, HEAD_DIM)),
        'state_nsa_win': nrm(ks[3], (DEPTH, DEC_BATCH, w_buf, 2, NSA_KV_HEADS, HEAD_DIM)),
        'state_dn_S': nrm(ks[5], (DEPTH, DEC_BATCH, DN_HEADS, DN_DK, DN_DV), 0.1),
        'state_dn_conv': nrm(ks[6], (DEPTH, DEC_BATCH, CONV_W - 1, DN_QKV_W)),
        'page_table': page_table,
        'ffn1_norm': gain(ks[7], (DEPTH, D_MODEL)),
        'ffn1_w_in': nrm(ks[8], (DEPTH, D_MODEL, 2 * D_FF), D_MODEL ** -0.5),
        'ffn1_w_out': nrm(ks[9], (DEPTH, D_FF, D_MODEL), D_FF ** -0.5),
        'mix_norm': gain(ks[10], (DEPTH, D_MODEL)),
        'w_in': nrm(ks[11], (DEPTH, D_MODEL, IN_W), D_MODEL ** -0.5),
        'nsa_cmp_pos': nrm(ks[12], (DEPTH, 2, CMP_BLOCK, HEAD_DIM), 0.1),
        'nsa_cmp_w1': nrm(ks[13], (DEPTH, 2, CMP_BLOCK * HEAD_DIM, CMP_HIDDEN), (CMP_BLOCK * HEAD_DIM) ** -0.5),
        'nsa_cmp_w2': nrm(ks[15], (DEPTH, 2, CMP_HIDDEN, HEAD_DIM), CMP_HIDDEN ** -0.5),
        'dn_conv_w': nrm(ks[16], (DEPTH, CONV_W, DN_QKV_W), CONV_W ** -0.5),
        'dn_A_log': jnp.log(jax.random.uniform(ks[17], (DEPTH, DN_HEADS), f32, 1.0, 16.0)),
        'dn_dt_bias': dt + jnp.log(-jnp.expm1(-dt)),
        'dn_out_norm': gain(ks[18], (DEPTH, DN_DV)),
        'w_branch_a': nrm(ks[19], (DEPTH, NSA_Q_W, D_MODEL), NSA_Q_W ** -0.5),
        'w_branch_b': nrm(ks[20], (DEPTH, DN_W, D_MODEL), DN_W ** -0.5),
        'w_out': nrm(ks[21], (DEPTH, D_MODEL, D_MODEL), D_MODEL ** -0.5),
        'ffn2_norm': gain(ks[22], (DEPTH, D_MODEL)),
        'ffn2_w_in': nrm(ks[23], (DEPTH, D_MODEL, 2 * D_FF), D_MODEL ** -0.5),
        'ffn2_w_out': nrm(ks[24], (DEPTH, D_FF, D_MODEL), D_FF ** -0.5),
        'final_norm': gain(ks[25], (D_MODEL,)),
    }


def reference(x_prompt, x_sample, cache_nsa_kv, state_nsa_win, state_dn_S, state_dn_conv, page_table,
              ffn1_norm, ffn1_w_in, ffn1_w_out, mix_norm, w_in, nsa_cmp_pos, nsa_cmp_w1, nsa_cmp_w2,
              dn_conv_w, dn_A_log, dn_dt_bias, dn_out_norm, w_branch_a, w_branch_b, w_out,
              ffn2_norm, ffn2_w_in, ffn2_w_out, final_norm):
    B, T = x_prompt.shape[:2]
    TS = x_sample.shape[1]
    past_len = page_table.shape[1] * PAGE_SIZE
    pos_p = jnp.arange(T, dtype=jnp.int32)
    pos_s = past_len + jnp.arange(TS, dtype=jnp.int32)
    xp, xs = x_prompt, x_sample
    p_rows, p_win, p_S, p_conv = [], [], [], []
    s_rows, s_win, s_S, s_conv = [], [], [], []
    for l in range(DEPTH):
        lp = (ffn1_norm[l], ffn1_w_in[l], ffn1_w_out[l], mix_norm[l], w_in[l],
              (nsa_cmp_pos[l], nsa_cmp_w1[l], nsa_cmp_w2[l]),
              (dn_conv_w[l], dn_A_log[l], dn_dt_bias[l], dn_out_norm[l]),
              w_branch_a[l], w_branch_b[l], w_out[l], ffn2_norm[l], ffn2_w_in[l], ffn2_w_out[l])
        xp, r, w, S, c = trunk_layer(xp, pos_p, lp, nsa_prompt,
                                     jnp.zeros((B, CONV_W - 1, DN_QKV_W), xp.dtype),
                                     jnp.zeros((B, DN_HEADS, DN_DK, DN_DV), jnp.float32), DN_CHUNK)
        p_rows.append(r); p_win.append(w); p_S.append(S); p_conv.append(c)
        nsa_fn = functools.partial(nsa_sample, cache_l=cache_nsa_kv[l], page_table=page_table,
                                   win_state=state_nsa_win[l])
        xs, r, w, S, c = trunk_layer(xs, pos_s, lp, nsa_fn, state_dn_conv[l], state_dn_S[l], TS)
        s_rows.append(r); s_win.append(w); s_S.append(S); s_conv.append(c)
    y_prompt = rmsnorm(xp, final_norm)
    y_sample = rmsnorm(xs, final_norm)
    return (y_prompt, y_sample, jnp.stack(p_rows), jnp.stack(p_win), jnp.stack(p_S), jnp.stack(p_conv),
            jnp.stack(s_rows), jnp.stack(s_win), jnp.stack(s_S), jnp.stack(s_conv))
```

```python
import functools
import math

import jax
import jax.numpy as jnp
import numpy as np
from jax import lax
from jax.experimental import pallas as pl
from jax.experimental.pallas import tpu as pltpu

F32 = jnp.float32
BF16 = jnp.bfloat16

D_MODEL = 1024
DEPTH = 4
PAGE_SIZE = 128
NSA_HEADS = 8
NSA_KV_HEADS = 2
HEAD_DIM = 64
NSA_GROUP = NSA_HEADS // NSA_KV_HEADS
NSA_Q_W = NSA_HEADS * HEAD_DIM
NSA_KV_W = NSA_KV_HEADS * HEAD_DIM
CMP_BLOCK = 32
CMP_STRIDE = 16
SEL_BLOCK = 64
N_SEL = 16
WINDOW = 512
FORCE_BONUS = 1e6
ROPE_THETA = 10000.0
DN_HEADS = 4
DN_DK = 128
DN_DV = 128
DN_QK_W = DN_HEADS * DN_DK
DN_W = DN_HEADS * DN_DV
DN_QKV_W = 2 * DN_QK_W + DN_W
CONV_W = 4
D_FF = 2816
EPS = 1e-6

LANES = 128
VMEM_LIMIT = 56 * 1024 * 1024
NEG_BIG = -1e30

QPAD_W = NSA_HEADS * LANES
C_Q = 0
C_KV = C_Q + QPAD_W
C_SMALL = C_KV + 6 * NSA_KV_W
C_QKV = C_SMALL + LANES
C_ZG = C_QKV + DN_QKV_W
C_MERGE = C_ZG + DN_W
MIX_W = C_MERGE + 2 * D_MODEL
SM_GATE = 0
SM_A = 3 * NSA_HEADS
SM_B = SM_A + DN_HEADS


def _dot(a, b):
    return jnp.dot(a, b, preferred_element_type=F32)


def _dot_nt(a, b):
    return lax.dot_general(a, b, (((1,), (1,)), ((), ())), preferred_element_type=F32)


def _split3(a):
    hi = a.astype(BF16)
    r1 = a - hi.astype(F32)
    mid = r1.astype(BF16)
    lo = (r1 - mid.astype(F32)).astype(BF16)
    return hi, mid, lo


def _dot_exact_rhs(a, b01):
    hi, mid, lo = _split3(a)
    return _dot(hi, b01) + _dot(mid, b01) + _dot(lo, b01)


def _dot_exact_lhs(a01, b):
    hi, mid, lo = _split3(b)
    return _dot(a01, hi) + _dot(a01, mid) + _dot(a01, lo)


def _dot_exact_lhs_nt(a01, b):
    hi, mid, lo = _split3(b)
    return _dot_nt(a01, hi) + _dot_nt(a01, mid) + _dot_nt(a01, lo)


def _dot3(a, b):
    ah = a.astype(BF16)
    al = (a - ah.astype(F32)).astype(BF16)
    bh = b.astype(BF16)
    bl = (b - bh.astype(F32)).astype(BF16)
    return _dot(ah, bh) + (_dot(ah, bl) + _dot(al, bh))


def _rms(x, w):
    return x * lax.rsqrt(jnp.mean(x * x, axis=-1, keepdims=True) + EPS) * w


def _sigmoid(x):
    return 1.0 / (1.0 + jnp.exp(-x))


def _silu(x):
    return x * _sigmoid(x)


def _params(*sem):
    return pltpu.CompilerParams(dimension_semantics=sem, vmem_limit_bytes=VMEM_LIMIT)


def _const_spec(shape):
    nd = len(shape)
    return pl.BlockSpec(shape, lambda *_: (0,) * nd)


FF_TILE = 256


def _ffn_kernel(x_ref, nw_ref, wi_ref, wo_ref, o_ref, act_ref):
    x = x_ref[...]
    hb = _rms(x, nw_ref[...]).astype(BF16)
    for j in range(D_FF // FF_TILE):
        g = _dot(hb, wi_ref[:, j * FF_TILE:(j + 1) * FF_TILE])
        u = _dot(hb, wi_ref[:, D_FF + j * FF_TILE:D_FF + (j + 1) * FF_TILE])
        act_ref[:, j * FF_TILE:(j + 1) * FF_TILE] = (_silu(g) * u).astype(BF16)
    o_ref[...] = x + 0.5 * _dot(act_ref[...], wo_ref[...])


def _ffn(x, nw, wi, wo, tm):
    m = x.shape[0]
    return pl.pallas_call(
        _ffn_kernel,
        grid=(m // tm,),
        in_specs=[pl.BlockSpec((tm, D_MODEL), lambda i: (i, 0)),
                  _const_spec((1, D_MODEL)),
                  _const_spec((D_MODEL, 2 * D_FF)),
                  _const_spec((D_FF, D_MODEL))],
        out_specs=pl.BlockSpec((tm, D_MODEL), lambda i: (i, 0)),
        out_shape=jax.ShapeDtypeStruct((m, D_MODEL), F32),
        scratch_shapes=[pltpu.VMEM((tm, D_FF), BF16)],
        compiler_params=_params("parallel"), name="half_ffn",
    )(x, nw, wi, wo)


def _rope128(x, cos, sin):
    lane = lax.broadcasted_iota(jnp.int32, x.shape, 1)
    first = (lane % HEAD_DIM) < (HEAD_DIM // 2)
    swapped = jnp.where(first, pltpu.roll(x, LANES - HEAD_DIM // 2, 1), pltpu.roll(x, HEAD_DIM // 2, 1))
    return x * cos + swapped * sin


def _mix_kernel(seq_tiles, x_ref, nw_ref, w_ref, cos_ref, sin_ref,
                qraw_ref, qrot_ref, rows_ref, win_ref, ksa_ref, vs_ref, win16_ref,
                small_ref, qkv_ref, zg_ref, merge_ref):
    tm = x_ref.shape[0]
    hb = _rms(x_ref[...], nw_ref[...]).astype(BF16)
    cos = cos_ref[...]
    sin = sin_ref[...]
    scale = HEAD_DIM ** -0.5
    for j in range(NSA_HEADS):
        q = _dot(hb, w_ref[:, C_Q + j * LANES:C_Q + (j + 1) * LANES])
        qraw_ref[:, j * LANES:(j + 1) * LANES] = (q * scale).astype(BF16)
        qrot_ref[:, j * LANES:(j + 1) * LANES] = (_rope128(q, cos, sin) * scale).astype(BF16)
    kv = _dot(hb, w_ref[:, C_KV:C_KV + 6 * NSA_KV_W])
    k_sel = _rope128(kv[:, 2 * LANES:3 * LANES], cos, sin)
    v_sel = kv[:, 3 * LANES:4 * LANES]
    k_win = _rope128(kv[:, 4 * LANES:5 * LANES], cos, sin)
    v_win = kv[:, 5 * LANES:6 * LANES]
    rows_ref[:, 0:2 * LANES] = kv[:, 0:2 * LANES]
    rows_ref[:, 2 * LANES:3 * LANES] = k_sel
    rows_ref[:, 3 * LANES:4 * LANES] = v_sel
    win_ref[:, 0:LANES] = k_win
    win_ref[:, LANES:2 * LANES] = v_win
    pos = (pl.program_id(0) % seq_tiles) * tm + lax.broadcasted_iota(jnp.int32, (tm, LANES), 0)
    lane = lax.broadcasted_iota(jnp.int32, (tm, LANES), 1)
    ksa_ref[:, 0:LANES] = k_sel.astype(BF16)
    ksa_ref[:, LANES:2 * LANES] = jnp.where(pos // SEL_BLOCK == lane, 1.0, 0.0).astype(BF16)
    vs_ref[...] = v_sel.astype(BF16)
    win16_ref[:, 0:LANES] = k_win.astype(BF16)
    win16_ref[:, LANES:2 * LANES] = v_win.astype(BF16)
    small_ref[...] = _dot(hb, w_ref[:, C_SMALL:C_SMALL + LANES])
    qkv_ref[...] = _dot(hb, w_ref[:, C_QKV:C_QKV + DN_QKV_W])
    zg_ref[...] = _dot(hb, w_ref[:, C_ZG:C_ZG + DN_W])
    merge_ref[...] = _dot(hb, w_ref[:, C_MERGE:C_MERGE + 2 * D_MODEL])


def _mix(x, nw, w, cos, sin, tm, seq_len):
    m = x.shape[0]
    seq_tiles = max(seq_len // tm, 1)
    row = lambda width: pl.BlockSpec((tm, width), lambda i: (i, 0))
    tab = pl.BlockSpec((tm, LANES), lambda i: (i % seq_tiles, 0))
    widths = [(QPAD_W, BF16), (QPAD_W, BF16), (4 * NSA_KV_W, F32), (2 * NSA_KV_W, F32),
              (2 * LANES, BF16), (LANES, BF16), (2 * LANES, BF16), (LANES, F32),
              (DN_QKV_W, F32), (DN_W, F32), (2 * D_MODEL, F32)]
    return pl.pallas_call(
        functools.partial(_mix_kernel, seq_tiles),
        grid=(m // tm,),
        in_specs=[row(D_MODEL), _const_spec((1, D_MODEL)), _const_spec((D_MODEL, MIX_W)), tab, tab],
        out_specs=[row(wd) for wd, _ in widths],
        out_shape=[jax.ShapeDtypeStruct((m, wd), dt) for wd, dt in widths],
        compiler_params=_params("parallel"), name="mix_in",
    )(x, nw, w, cos, sin)


def _merge_kernel(x_ref, oa_ref, ob_ref, mg_ref, wa_ref, wb_ref, wo_ref, o_ref):
    mg = mg_ref[...]
    ya = _dot(oa_ref[...], wa_ref[...])
    yb = _dot(ob_ref[...], wb_ref[...])
    y = _sigmoid(mg[:, :D_MODEL]) * ya + _sigmoid(mg[:, D_MODEL:]) * yb
    o_ref[...] = x_ref[...] + _dot(y.astype(BF16), wo_ref[...])


def _merge(x, oa, ob, mg, wa, wb, wo, tm):
    m = x.shape[0]
    row = lambda width: pl.BlockSpec((tm, width), lambda i: (i, 0))
    return pl.pallas_call(
        _merge_kernel,
        grid=(m // tm,),
        in_specs=[row(D_MODEL), row(NSA_Q_W), row(DN_W), row(2 * D_MODEL),
                  _const_spec((NSA_Q_W, D_MODEL)), _const_spec((DN_W, D_MODEL)),
                  _const_spec((D_MODEL, D_MODEL))],
        out_specs=row(D_MODEL),
        out_shape=jax.ShapeDtypeStruct((m, D_MODEL), F32),
        compiler_params=_params("parallel"), name="branch_merge",
    )(x, oa, ob, mg, wa, wb, wo)


def _norm_kernel(x_ref, w_ref, o_ref):
    o_ref[...] = _rms(x_ref[...], w_ref[...])


def _final_norm(x, w, tm):
    m = x.shape[0]
    row = pl.BlockSpec((tm, D_MODEL), lambda i: (i, 0))
    return pl.pallas_call(
        _norm_kernel, grid=(m // tm,), in_specs=[row, _const_spec((1, D_MODEL))], out_specs=row,
        out_shape=jax.ShapeDtypeStruct((m, D_MODEL), F32), compiler_params=_params("parallel"),
    )(x, w)


def _compress_kernel(seg_ref, pe_ref, w1a0_ref, w1a1_ref, w1b0_ref, w1b1_ref, w2_ref, o_ref):
    ns = seg_ref.shape[2]
    pe_a = pe_ref[0, 0:1, :]
    pe_b = pe_ref[0, 1:2, :]
    s0 = seg_ref[0, 0]
    s1 = seg_ref[0, 1]
    sa = _dot((s0 + pe_a).astype(BF16), w1a0_ref[0]) + _dot((s1 + pe_a).astype(BF16), w1a1_ref[0])
    sb = _dot((s0 + pe_b).astype(BF16), w1b0_ref[0]) + _dot((s1 + pe_b).astype(BF16), w1b1_ref[0])
    pre = sa + pltpu.roll(sb, ns - 1, 0)
    o_ref[0, 0] = _dot(jax.nn.gelu(pre).astype(BF16), w2_ref[0]).astype(BF16)


def _compress_prompt(seg, pe, w1a0, w1a1, w1b0, w1b1, w2):
    b, _, ns, sw = seg.shape
    wspec = pl.BlockSpec((1, sw, LANES), lambda i, c: (c, 0, 0))
    return pl.pallas_call(
        _compress_kernel,
        grid=(b, 2),
        in_specs=[pl.BlockSpec((1, 2, ns, sw), lambda i, c: (i, c, 0, 0)),
                  pl.BlockSpec((1, 8, sw), lambda i, c: (c, 0, 0)),
                  wspec, wspec, wspec, wspec,
                  pl.BlockSpec((1, LANES, LANES), lambda i, c: (c, 0, 0))],
        out_specs=pl.BlockSpec((1, 1, ns, LANES), lambda i, c: (i, c, 0, 0)),
        out_shape=jax.ShapeDtypeStruct((b, 2, ns, LANES), BF16),
        compiler_params=_params("parallel", "parallel"), name="nsa_compress",
    )(seg, pe, w1a0, w1a1, w1b0, w1b1, w2)


TQ = 128
KT = 512
WIN_KEYS = WINDOW + TQ


def _softmax_rows(s, mask):
    s = jnp.where(mask, s, NEG_BIG)
    e = jnp.where(mask, jnp.exp(s - jnp.max(s, axis=-1, keepdims=True)), 0.0)
    return e * (1.0 / jnp.maximum(jnp.sum(e, axis=-1, keepdims=True), 1e-30))


def _topk_bias_t(imp_t, n_cand, n_sel):
    nr = imp_t.shape[0] // 8
    xs = [imp_t[8 * r:8 * r + 8, :] for r in range(nr)]
    cnt = [jnp.zeros_like(xs[0]) for _ in range(nr)]
    sub = lax.broadcasted_iota(jnp.int32, xs[0].shape, 0)
    for j in range(n_cand):
        rj = j // 8
        row = xs[rj][j % 8:j % 8 + 1, :]
        for r in range(nr):
            if r < rj:
                inc = jnp.where(row > xs[r], 1.0, 0.0)
            elif r > rj:
                inc = jnp.where(row >= xs[r], 1.0, 0.0)
            else:
                inc = jnp.where(sub > j % 8, jnp.where(row >= xs[r], 1.0, 0.0),
                                jnp.where(row > xs[r], 1.0, 0.0))
            cnt[r] = cnt[r] + inc
    return jnp.concatenate([jnp.where(c < n_sel, 0.0, NEG_BIG) for c in cnt], axis=0)


def _nsa_prompt_kernel(qraw_ref, qrot_ref, small_ref, kc_ref, vc_ref, ksa_ref, vs_ref, win_ref,
                       ovt_ref, rexp_ref, o_ref):
    seq = ksa_ref.shape[1]
    nc = kc_ref.shape[2]
    ns = seq // SEL_BLOCK
    s0 = pl.multiple_of(pl.program_id(1) * TQ, TQ)
    rows = NSA_GROUP * TQ
    qpos = s0 + lax.broadcasted_iota(jnp.int32, (rows, 1), 0) % TQ
    gexp = _dot_exact_rhs(_sigmoid(small_ref[...]), rexp_ref[...])

    blk_t = lax.broadcasted_iota(jnp.int32, (ns, TQ), 0)
    qpos_t = s0 + lax.broadcasted_iota(jnp.int32, (ns, TQ), 1)
    cur_t = qpos_t // SEL_BLOCK
    valid_t = blk_t * SEL_BLOCK <= qpos_t
    forced_t = (blk_t == 0) | (blk_t == cur_t) | (blk_t == cur_t - 1)

    res = []
    for k in range(NSA_KV_HEADS):
        def stack(ref):
            return jnp.concatenate(
                [ref[:, (k * NSA_GROUP + g) * LANES:(k * NSA_GROUP + g + 1) * LANES]
                 for g in range(NSA_GROUP)], axis=0)
        q_raw = stack(qraw_ref)
        q_rot = stack(qrot_ref)
        s_c = _dot_nt(q_raw, kc_ref[0, 0])
        cend = lax.broadcasted_iota(jnp.int32, (rows, nc), 1) * CMP_STRIDE + (CMP_BLOCK - 1)
        p_c = _softmax_rows(s_c, cend <= qpos)
        o_c = _dot(p_c.astype(BF16), vc_ref[0, 1])
        p_sum = p_c[0:TQ] + p_c[TQ:2 * TQ] + p_c[2 * TQ:3 * TQ] + p_c[3 * TQ:4 * TQ]
        imp_t = _dot_exact_lhs_nt(ovt_ref[...], p_sum)[0:ns]
        imp_t = jnp.where(valid_t, imp_t + jnp.where(forced_t, FORCE_BONUS, 0.0), -jnp.inf)
        bias_t = _topk_bias_t(imp_t, ns, min(N_SEL, ns))
        if ns < LANES:
            bias_t = jnp.concatenate([bias_t, jnp.zeros((LANES - ns, TQ), F32)], axis=0)
        bias = bias_t.T.astype(BF16)
        q_sel = jnp.concatenate([q_rot, jnp.concatenate([bias] * NSA_GROUP, axis=0)], axis=1)

        def sel_step(c, carry):
            m, l, acc = carry
            k0 = pl.multiple_of(c * KT, KT)
            s = _dot_nt(q_sel, ksa_ref[0, pl.ds(k0, KT), :])
            kpos = k0 + lax.broadcasted_iota(jnp.int32, (rows, KT), 1)
            s = jnp.where(kpos <= qpos, s, NEG_BIG)
            m_new = jnp.maximum(m, jnp.max(s, axis=-1, keepdims=True))
            alpha = jnp.exp(m - m_new)
            p = jnp.exp(s - m_new)
            l = alpha * l + jnp.sum(p, axis=-1, keepdims=True)
            acc = alpha * acc + _dot(p.astype(BF16), vs_ref[0, pl.ds(k0, KT), :])
            return m_new, l, acc
        init = (jnp.full((rows, 1), NEG_BIG, F32), jnp.zeros((rows, 1), F32), jnp.zeros((rows, LANES), F32))
        _, l_s, acc_s = lax.fori_loop(0, s0 // KT + 1, sel_step, init)
        o_s = acc_s * (1.0 / l_s)

        w0 = pl.multiple_of(jnp.maximum(s0 - WINDOW, 0), TQ)
        kw = win_ref[0, pl.ds(w0, WIN_KEYS), 0:LANES]
        vw = win_ref[0, pl.ds(w0, WIN_KEYS), LANES:2 * LANES]
        s_w = _dot_nt(q_rot, kw)
        kpos_w = w0 + lax.broadcasted_iota(jnp.int32, (rows, WIN_KEYS), 1)
        p_w = _softmax_rows(s_w, (kpos_w <= qpos) & (kpos_w > qpos - WINDOW))
        o_w = _dot(p_w.astype(BF16), vw)
        res.append((o_c, o_s, o_w))

    low = lax.broadcasted_iota(jnp.int32, (TQ, LANES), 1) < HEAD_DIM
    for g in range(NSA_GROUP):
        out = jnp.zeros((TQ, LANES), F32)
        for br in range(3):
            both = jnp.where(low, res[0][br][g * TQ:(g + 1) * TQ], res[1][br][g * TQ:(g + 1) * TQ])
            out = out + gexp[:, br * NSA_Q_W + g * LANES:br * NSA_Q_W + (g + 1) * LANES] * both
        o_ref[:, g * LANES:(g + 1) * LANES] = out.astype(BF16)


def _nsa_prompt(qraw, qrot, small, kcvc, ksa, vs, win16, ovt, rexp, b, seq):
    nc = kcvc.shape[2]
    qspec = pl.BlockSpec((TQ, QPAD_W), lambda i, j: (i * (seq // TQ) + j, 0))
    per_b = lambda width: pl.BlockSpec((1, seq, width), lambda i, j: (i, 0, 0))
    return pl.pallas_call(
        _nsa_prompt_kernel,
        grid=(b, seq // TQ),
        in_specs=[qspec, qspec,
                  pl.BlockSpec((TQ, LANES), lambda i, j: (i * (seq // TQ) + j, 0)),
                  pl.BlockSpec((1, 2, nc, LANES), lambda i, j: (i, 0, 0, 0)),
                  pl.BlockSpec((1, 2, nc, LANES), lambda i, j: (i, 0, 0, 0)),
                  per_b(2 * LANES), per_b(LANES), per_b(2 * LANES),
                  _const_spec(ovt.shape), _const_spec(rexp.shape)],
        out_specs=pl.BlockSpec((TQ, NSA_Q_W), lambda i, j: (i * (seq // TQ) + j, 0)),
        out_shape=jax.ShapeDtypeStruct((b * seq, NSA_Q_W), BF16),
        compiler_params=_params("parallel", "parallel"), name="nsa_prompt",
    )(qraw, qrot, small, kcvc, kcvc, ksa.reshape(b, seq, 2 * LANES), vs.reshape(b, seq, LANES),
      win16.reshape(b, seq, 2 * LANES), ovt, rexp)


PAGES_PER_STEP = 8
SEG_PER_PAGE = PAGE_SIZE // CMP_STRIDE
SEG_W = CMP_STRIDE * 2 * NSA_KV_W


def _nsa_sample_kernel(ts, past, pt_ref, *refs):
    pages = refs[:PAGES_PER_STEP]
    (qraw_ref, qrot_ref, small_ref, rows_ref, winnew_ref, winst_ref, pea_ref, peb_ref, wa_ref, wb_ref, w2_ref,
     ovt_ref, rexp_ref, o_ref, sega_ref, segb_ref, ksa_ref, vs_ref, sm_ref) = refs[PAGES_PER_STEP:]
    del pt_ref
    pg = pl.program_id(1)
    nseg = sega_ref.shape[0]
    nrows = NSA_HEADS * ts
    kt = NSA_KV_HEADS * ts

    for pair in range(PAGES_PER_STEP // 2):
        r0 = pl.multiple_of(pg * (PAGES_PER_STEP * SEG_PER_PAGE) + pair * 2 * SEG_PER_PAGE, 2 * SEG_PER_PAGE)
        for p in range(CMP_STRIDE):
            for part in range(2):
                x = jnp.concatenate(
                    [pages[2 * pair + e][0, 0, pl.ds(p * 4 + part, SEG_PER_PAGE, stride=CMP_STRIDE * 4), :]
                     for e in range(2)], axis=0)
                cs = slice((2 * p + part) * LANES, (2 * p + part + 1) * LANES)
                sega_ref[pl.ds(r0, 2 * SEG_PER_PAGE), cs] = (x + pea_ref[:, cs]).astype(BF16)
                segb_ref[pl.ds(r0, 2 * SEG_PER_PAGE), cs] = (x + peb_ref[:, cs]).astype(BF16)
    for e in range(PAGES_PER_STEP):
        k0 = pl.multiple_of((pg * PAGES_PER_STEP + e) * PAGE_SIZE, PAGE_SIZE)
        ksa_ref[pl.ds(k0, PAGE_SIZE), 0:LANES] = pages[e][0, 0, pl.ds(2, PAGE_SIZE, stride=4), :].astype(BF16)
        kpos = k0 + lax.broadcasted_iota(jnp.int32, (PAGE_SIZE, LANES), 0)
        lane = lax.broadcasted_iota(jnp.int32, (PAGE_SIZE, LANES), 1)
        ksa_ref[pl.ds(k0, PAGE_SIZE), LANES:2 * LANES] = jnp.where(kpos // SEL_BLOCK == lane, 1.0, 0.0).astype(BF16)
        vs_ref[pl.ds(k0, PAGE_SIZE), :] = pages[e][0, 0, pl.ds(3, PAGE_SIZE, stride=4), :].astype(BF16)

    @pl.when(pg == pl.num_programs(1) - 1)
    def _():
        ns_past = past // SEL_BLOCK
        nc = nseg - 1
        q_raw = qraw_ref[0]
        q_rot = qrot_ref[0]
        q_rot_f = q_rot.astype(F32)
        row = lax.broadcasted_iota(jnp.int32, (nrows, 1), 0)
        t_row = row % ts
        pre = _dot(sega_ref[...], wa_ref[...]) + pltpu.roll(_dot(segb_ref[...], wb_ref[...]), nseg - 1, 0)
        kcvc = _dot(jax.nn.gelu(pre).astype(BF16), w2_ref[...]).astype(BF16)
        s_c = _dot_nt(q_raw, kcvc[:, 0:LANES])
        p_c = _softmax_rows(s_c, lax.broadcasted_iota(jnp.int32, (nrows, nseg), 1) < nc)
        o_c = _dot(p_c.astype(BF16), kcvc[:, LANES:2 * LANES])
        p_sum = p_c[0:kt] + p_c[kt:2 * kt] + p_c[2 * kt:3 * kt] + p_c[3 * kt:4 * kt]
        p_sum = jnp.concatenate([p_sum, jnp.zeros((LANES - kt, nseg), F32)], axis=0)
        nb = ovt_ref.shape[0]
        imp_t = _dot_exact_lhs_nt(ovt_ref[...], p_sum)
        blk_t = lax.broadcasted_iota(jnp.int32, (nb, LANES), 0)
        qpos_t = past + lax.broadcasted_iota(jnp.int32, (nb, LANES), 1) % ts
        cur_t = qpos_t // SEL_BLOCK
        forced_t = (blk_t == 0) | (blk_t == cur_t) | (blk_t == cur_t - 1)
        imp_t = jnp.where(blk_t * SEL_BLOCK <= qpos_t, imp_t + jnp.where(forced_t, FORCE_BONUS, 0.0), -jnp.inf)
        bias_t = _topk_bias_t(imp_t, -(-(past + ts) // SEL_BLOCK), N_SEL)[0:ns_past]
        bias = bias_t.T[0:kt]
        q_sel = jnp.concatenate([q_rot, jnp.concatenate([bias] * NSA_GROUP, axis=0).astype(BF16)], axis=1)

        def attend(s_past, v_past16, k_new, v_new, mask_past):
            if mask_past is not None:
                s_past = jnp.where(mask_past, s_past, NEG_BIG)
            s_new = [jnp.where(t_row >= j, jnp.sum(q_rot_f * k_new[j:j + 1, :], axis=-1, keepdims=True), NEG_BIG)
                     for j in range(ts)]
            m = jnp.max(s_past, axis=-1, keepdims=True)
            for s in s_new:
                m = jnp.maximum(m, s)
            p_past = jnp.exp(s_past - m)
            if mask_past is not None:
                p_past = jnp.where(mask_past, p_past, 0.0)
            acc = _dot(p_past.astype(BF16), v_past16)
            l = jnp.sum(p_past, axis=-1, keepdims=True)
            for j, s in enumerate(s_new):
                p = jnp.where(t_row >= j, jnp.exp(s - m), 0.0)
                acc = acc + p * v_new[j:j + 1, :]
                l = l + p
            return acc * (1.0 / l)

        new = rows_ref[0]
        o_s = attend(_dot_nt(q_sel, ksa_ref[...]), vs_ref[...], new[:, 2 * NSA_KV_W:3 * NSA_KV_W],
                     new[:, 3 * NSA_KV_W:4 * NSA_KV_W], None)
        wbuf = winst_ref.shape[2]
        wst = winst_ref[0, 0]
        wnew = winnew_ref[0]
        i_w = lax.broadcasted_iota(jnp.int32, (nrows, wbuf), 1)
        dist = t_row + wbuf - i_w
        o_w = attend(_dot_nt(q_rot, wst[:, 0:LANES].astype(BF16)), wst[:, LANES:2 * LANES].astype(BF16),
                     wnew[:, 0:LANES], wnew[:, LANES:2 * LANES],
                     (dist < WINDOW) & (past - wbuf + i_w >= 0))

        sm_ref[...] = jnp.zeros_like(sm_ref)
        sm_ref[0:ts, :] = small_ref[0]
        gexp = _dot_exact_rhs(_sigmoid(sm_ref[...]), rexp_ref[...])[0:kt]
        low = lax.broadcasted_iota(jnp.int32, (kt, LANES), 1) < HEAD_DIM
        for g in range(NSA_GROUP):
            out = jnp.zeros((kt, LANES), F32)
            for br, o_br in enumerate((o_c, o_s, o_w)):
                og = o_br[g * kt:(g + 1) * kt]
                both = jnp.where(low, og, pltpu.roll(og, kt - ts, 0))
                out = out + gexp[:, br * NSA_Q_W + g * LANES:br * NSA_Q_W + (g + 1) * LANES] * both
            o_ref[0, :, g * LANES:(g + 1) * LANES] = out


def _nsa_sample(layer, cache, page_table, qraw_s, qrot_s, small_s, rows_s, winnew_s, win_state,
                pea, peb, wa, wb, w2, ovt, rexp, past):
    b, ts = small_s.shape[:2]
    n_pages = page_table.shape[1]
    assert n_pages % PAGES_PER_STEP == 0 and ts * NSA_KV_HEADS == 8 and past == n_pages * PAGE_SIZE
    nseg = past // CMP_STRIDE
    wbuf = win_state.shape[2]
    kw = 4 * NSA_KV_W

    def page_spec(e):
        return pl.BlockSpec((1, 1, 4 * PAGE_SIZE, LANES),
                            lambda i, g, pt: (layer, pt[i, g * PAGES_PER_STEP + e], 0, 0))
    per_b = lambda shape: pl.BlockSpec((1,) + shape, lambda i, g, pt: (i,) + (0,) * len(shape))
    const = lambda shape: pl.BlockSpec(shape, lambda i, g, pt: (0,) * len(shape))
    grid_spec = pltpu.PrefetchScalarGridSpec(
        num_scalar_prefetch=1,
        grid=(b, n_pages // PAGES_PER_STEP),
        in_specs=[page_spec(e) for e in range(PAGES_PER_STEP)] + [
            per_b((NSA_HEADS * ts, LANES)), per_b((NSA_HEADS * ts, LANES)), per_b((ts, LANES)),
            per_b((ts, kw)), per_b((ts, 2 * NSA_KV_W)),
            pl.BlockSpec((1, 1, wbuf, 2 * NSA_KV_W), lambda i, g, pt: (layer, i, 0, 0)),
            const(pea.shape), const(peb.shape), const(wa.shape), const(wb.shape), const(w2.shape),
            const(ovt.shape), const(rexp.shape)],
        out_specs=per_b((NSA_KV_HEADS * ts, NSA_Q_W)),
        scratch_shapes=[pltpu.VMEM((nseg, CMP_STRIDE * 2 * NSA_KV_W), BF16),
                        pltpu.VMEM((nseg, CMP_STRIDE * 2 * NSA_KV_W), BF16),
                        pltpu.VMEM((past, 2 * LANES), BF16), pltpu.VMEM((past, LANES), BF16),
                        pltpu.VMEM((2 * NSA_KV_HEADS * ts, LANES), F32)])
    return pl.pallas_call(
        functools.partial(_nsa_sample_kernel, ts, past),
        grid_spec=grid_spec,
        out_shape=jax.ShapeDtypeStruct((b, NSA_KV_HEADS * ts, NSA_Q_W), F32),
        compiler_params=_params("parallel", "arbitrary"), name="nsa_sample",
    )(page_table, *([cache] * PAGES_PER_STEP), qraw_s, qrot_s, small_s, rows_s, winnew_s, win_state,
      pea, peb, wa, wb, w2, ovt, rexp)


DN_C = 128
HALO = 8


def _softplus(x):
    return jnp.maximum(x, 0.0) + jnp.log(1.0 + jnp.exp(-jnp.abs(x)))


def _l2n(x):
    return x * lax.rsqrt(jnp.sum(x * x, axis=-1, keepdims=True) + EPS)


def _dn_prep_kernel(n_valid, qkv_ref, halo_ref, prefix_ref, small_ref, cw_ref, alog_ref, dtb_ref,
                    tri_ref, trit_ref, u_ref, w_ref, qd_ref, qk_ref, kdt_ref, gl_ref, xcat_ref, sm_ref):
    n = pl.program_id(1)
    if n_valid == DN_C:
        xcat_ref[HALO:HALO + DN_C, :] = qkv_ref[...]
        small = small_ref[...]

        @pl.when(n == 0)
        def _():
            xcat_ref[0:HALO, :] = prefix_ref[0]

        @pl.when(n > 0)
        def _():
            xcat_ref[0:HALO, :] = halo_ref[...]
    else:
        xcat_ref[...] = jnp.zeros_like(xcat_ref)
        xcat_ref[0:HALO, :] = prefix_ref[0]
        xcat_ref[HALO:HALO + n_valid, :] = qkv_ref[0]
        sm_ref[...] = jnp.zeros_like(sm_ref)
        sm_ref[0:n_valid, :] = small_ref[0]
        small = sm_ref[...]

    conv = jnp.zeros((DN_C, DN_QKV_W), F32)
    for i in range(CONV_W):
        conv = conv + xcat_ref[HALO - (CONV_W - 1) + i:HALO - (CONV_W - 1) + i + DN_C, :] * cw_ref[i:i + 1, :]
    conv = _silu(conv)

    ii = lax.broadcasted_iota(jnp.int32, (DN_C, DN_C), 0)
    jj = lax.broadcasted_iota(jnp.int32, (DN_C, DN_C), 1)
    g_all = -jnp.exp(alog_ref[...]) * _softplus(small + dtb_ref[...])
    beta_all = _sigmoid(small)
    if n_valid != DN_C:
        g_all = jnp.where(ii < n_valid, g_all, 0.0)
        beta_all = jnp.where(ii < n_valid, beta_all, 0.0)
    gc_all = _dot_exact_lhs(tri_ref[...], g_all)
    gr_all = _dot_exact_rhs(g_all.T, trit_ref[...])
    eye = jnp.where(ii == jj, 1.0, 0.0)

    for h in range(DN_HEADS):
        q = _l2n(conv[:, h * DN_DK:(h + 1) * DN_DK]) * DN_DK ** -0.5
        k = _l2n(conv[:, DN_QK_W + h * DN_DK:DN_QK_W + (h + 1) * DN_DK])
        v = conv[:, 2 * DN_QK_W + h * DN_DV:2 * DN_QK_W + (h + 1) * DN_DV]
        gcol = gc_all[:, SM_A + h:SM_A + h + 1]
        grow = gr_all[SM_A + h:SM_A + h + 1, :]
        beta = beta_all[:, SM_B + h:SM_B + h + 1]
        decay = jnp.where(ii >= jj, jnp.exp(gcol - grow), 0.0)
        kb = k * beta
        k16 = k.astype(BF16)
        a = jnp.where(ii > jj, _dot_nt(kb.astype(BF16), k16) * decay, 0.0)
        t = eye - a
        apow = a
        for _ in range(int(math.log2(DN_C)) - 1):
            apow = _dot3(apow, apow)
            t = t + _dot3(t, apow)
        eg = jnp.exp(gcol)
        u_ref[:, h * DN_DV:(h + 1) * DN_DV] = _dot3(t, v * beta)
        w_ref[:, h * DN_DK:(h + 1) * DN_DK] = _dot3(t, kb * eg).astype(BF16)
        qk = jnp.where(ii >= jj, _dot_nt(q.astype(BF16), k16) * decay, 0.0)
        qk_ref[:, h * DN_C:(h + 1) * DN_C] = qk.astype(BF16)
        qd_ref[:, h * DN_DK:(h + 1) * DN_DK] = (q * eg).astype(BF16)
        glast = gcol[DN_C - 1:DN_C, :]
        kdt_ref[0, h * DN_DK:(h + 1) * DN_DK, :] = (k * jnp.exp(glast - gcol)).T.astype(BF16)
        gl_ref[0, h:h + 1, :] = jnp.broadcast_to(jnp.exp(glast), (1, LANES))
    gl_ref[0, DN_HEADS:, :] = jnp.zeros((HALO - DN_HEADS, LANES), F32)


def _dn_prep(qkv, prefix8, small, cw, alog_row, dtb_row, tri, trit, b, nch, n_valid):
    rows = b * nch * DN_C
    if n_valid == DN_C:
        qkv_spec = pl.BlockSpec((DN_C, DN_QKV_W), lambda i, n: (i * nch + n, 0))
        halo_spec = pl.BlockSpec((HALO, DN_QKV_W),
                                 lambda i, n: (jnp.maximum((i * nch + n) * (DN_C // HALO) - 1, 0), 0))
        small_spec = pl.BlockSpec((DN_C, LANES), lambda i, n: (i * nch + n, 0))
        qkv_in, halo_in, small_in = qkv, qkv, small
    else:
        qkv_in = qkv.reshape(b, n_valid, DN_QKV_W)
        small_in = small.reshape(b, n_valid, LANES)
        halo_in = prefix8
        qkv_spec = pl.BlockSpec((1, n_valid, DN_QKV_W), lambda i, n: (i, 0, 0))
        halo_spec = pl.BlockSpec((1, HALO, DN_QKV_W), lambda i, n: (i, 0, 0))
        small_spec = pl.BlockSpec((1, n_valid, LANES), lambda i, n: (i, 0, 0))
    tok = lambda width: pl.BlockSpec((DN_C, width), lambda i, n: (i * nch + n, 0))
    return pl.pallas_call(
        functools.partial(_dn_prep_kernel, n_valid),
        grid=(b, nch),
        in_specs=[qkv_spec, halo_spec, pl.BlockSpec((1, HALO, DN_QKV_W), lambda i, n: (i, 0, 0)), small_spec,
                  _const_spec((CONV_W, DN_QKV_W)), _const_spec((1, LANES)), _const_spec((1, LANES)),
                  _const_spec((DN_C, DN_C)), _const_spec((DN_C, DN_C))],
        out_specs=[tok(DN_W), tok(DN_QK_W), tok(DN_QK_W), tok(DN_HEADS * DN_C),
                   pl.BlockSpec((1, DN_QK_W, DN_C), lambda i, n: (i * nch + n, 0, 0)),
                   pl.BlockSpec((1, HALO, LANES), lambda i, n: (i * nch + n, 0, 0))],
        out_shape=[jax.ShapeDtypeStruct((rows, DN_W), F32), jax.ShapeDtypeStruct((rows, DN_QK_W), BF16),
                   jax.ShapeDtypeStruct((rows, DN_QK_W), BF16), jax.ShapeDtypeStruct((rows, DN_HEADS * DN_C), BF16),
                   jax.ShapeDtypeStruct((b * nch, DN_QK_W, DN_C), BF16),
                   jax.ShapeDtypeStruct((b * nch, HALO, LANES), F32)],
        scratch_shapes=[pltpu.VMEM((HALO + DN_C, DN_QKV_W), F32), pltpu.VMEM((DN_C, LANES), F32)],
        compiler_params=_params("parallel", "parallel"), name="dn_prep",
    )(qkv_in, halo_in, prefix8, small_in, cw, alog_row, dtb_row, tri, trit)


def _dn_scan_kernel(n_valid, u_ref, w_ref, qd_ref, qk_ref, kdt_ref, gl_ref, zg_ref, s0_ref, nw_ref,
                    o_ref, s_ref):
    @pl.when(pl.program_id(1) == 0)
    def _():
        s_ref[...] = s0_ref[...]

    for b in range(u_ref.shape[0]):
        for h in range(DN_HEADS):
            cs = slice(h * DN_DK, (h + 1) * DN_DK)
            s = s_ref[b, h]
            s16 = s.astype(BF16)
            v_new = u_ref[b, :, cs] - _dot(w_ref[b, :, cs], s16)
            v16 = v_new.astype(BF16)
            o = _dot(qd_ref[b, :, cs], s16) + _dot(qk_ref[b, :, cs], v16)
            s_ref[b, h] = s * gl_ref[b, h:h + 1, :] + _dot(kdt_ref[b, cs, :], v16)
            on = o * lax.rsqrt(jnp.mean(o * o, axis=-1, keepdims=True) + EPS) * nw_ref[...]
            o_ref[b, :, cs] = (on[0:n_valid] * _silu(zg_ref[b, :, cs])).astype(BF16)


def _dn_scan(u, w, qd, qk, kdt, gl, zg, s0, nw, b, nch, bb, n_valid):
    tok = lambda width: pl.BlockSpec((bb, DN_C, width), lambda i, n: (i, n, 0))
    r3 = lambda a: a.reshape(b, nch * DN_C, a.shape[-1])
    st = pl.BlockSpec((bb, DN_HEADS, DN_DK, DN_DV), lambda i, n: (i, 0, 0, 0))
    return pl.pallas_call(
        functools.partial(_dn_scan_kernel, n_valid),
        grid=(b // bb, nch),
        in_specs=[tok(DN_W), tok(DN_QK_W), tok(DN_QK_W), tok(DN_HEADS * DN_C),
                  pl.BlockSpec((bb, DN_QK_W, DN_C), lambda i, n: (i, n, 0)),
                  pl.BlockSpec((bb, HALO, LANES), lambda i, n: (i, n, 0)),
                  pl.BlockSpec((bb, n_valid, DN_W), lambda i, n: (i, n, 0)), st, _const_spec((1, DN_DV))],
        out_specs=[pl.BlockSpec((bb, n_valid, DN_W), lambda i, n: (i, n, 0)), st],
        out_shape=[jax.ShapeDtypeStruct((b, nch * n_valid, DN_W), BF16),
                   jax.ShapeDtypeStruct((b, DN_HEADS, DN_DK, DN_DV), F32)],
        compiler_params=_params("parallel", "arbitrary"), name="dn_scan",
    )(r3(u), r3(w), r3(qd), r3(qk), kdt.reshape(b, nch * DN_QK_W, DN_C), gl.reshape(b, nch * HALO, LANES),
      zg.reshape(b, nch * n_valid, DN_W), s0, nw)


IN_WIDTHS = (NSA_Q_W, 6 * NSA_KV_W, 3 * NSA_HEADS, DN_QKV_W, DN_HEADS, DN_HEADS, DN_W, 2 * D_MODEL)


def _pack_mix_weight(w_in):
    offs = np.cumsum(IN_WIDTHS)[:-1].tolist()
    q, kv, gate, qkv, a, b, zg, merge = jnp.split(w_in, offs, axis=-1)
    d = w_in.shape[0]
    qh = q.reshape(d, NSA_KV_HEADS, NSA_GROUP, 1, HEAD_DIM)
    eye = jnp.eye(NSA_KV_HEADS, dtype=w_in.dtype).reshape(1, NSA_KV_HEADS, 1, NSA_KV_HEADS, 1)
    qpad = (qh * eye).reshape(d, QPAD_W)
    small = jnp.concatenate([gate, a, b, jnp.zeros((d, LANES - SM_B - DN_HEADS), w_in.dtype)], axis=1)
    return jnp.concatenate([qpad, kv, small, qkv, zg, merge], axis=1).astype(BF16)


def _rope_tables(pos):
    half = HEAD_DIM // 2
    freq = ROPE_THETA ** (-jnp.arange(half, dtype=F32) / half)
    ang = pos.astype(F32)[:, None] * freq[None, :]
    cos, sin = jnp.cos(ang), jnp.sin(ang)
    cos = jnp.concatenate([cos, cos] * (LANES // HEAD_DIM), axis=1)
    sin = jnp.concatenate([-sin, sin] * (LANES // HEAD_DIM), axis=1)
    return cos, sin


def _gate_expand_matrix():
    r = np.zeros((LANES, 3 * NSA_Q_W), np.float32)
    for h in range(NSA_HEADS):
        k, g = divmod(h, NSA_GROUP)
        for br in range(3):
            c0 = br * NSA_Q_W + g * LANES + k * HEAD_DIM
            r[SM_GATE + h * 3 + br, c0:c0 + HEAD_DIM] = 1.0
    return jnp.asarray(r, BF16)


def _overlap_t(nc_pad, nc, ns, rows):
    i = np.arange(nc_pad)[None, :]
    j = np.arange(rows)[:, None]
    ov = (i * CMP_STRIDE < (j + 1) * SEL_BLOCK) & (i * CMP_STRIDE + CMP_BLOCK > j * SEL_BLOCK)
    ov = ov & (i < nc) & (j < ns)
    return jnp.asarray(ov.astype(np.float32), BF16)


def _pack_compress_sample(pos_emb, w1, w2):
    comp = np.array([0, 0, 1, 1])
    eye4 = jnp.eye(4, dtype=w1.dtype)

    def seg_weight(w1_half):
        w = w1_half.reshape(2, CMP_STRIDE, HEAD_DIM, HEAD_DIM)[comp]
        w = jnp.einsum('cpde,cf->pcdfe', w, eye4)
        return w.reshape(SEG_W, 4 * HEAD_DIM).astype(BF16)

    def seg_pe(pe_half):
        return pe_half[comp].transpose(1, 0, 2).reshape(1, SEG_W)
    half = CMP_STRIDE * HEAD_DIM
    w2c = w2[comp]
    w2d = jnp.einsum('cde,cf->cdfe', w2c, eye4).reshape(4 * HEAD_DIM, 4 * HEAD_DIM).astype(BF16)
    return (seg_pe(pos_emb[:, :CMP_STRIDE]), seg_pe(pos_emb[:, CMP_STRIDE:]),
            seg_weight(w1[:, :half]), seg_weight(w1[:, half:]), w2d)


def _sample_rows(q_pad, b, ts):
    q = q_pad.reshape(b, ts, NSA_KV_HEADS, NSA_GROUP, LANES).transpose(0, 3, 2, 1, 4)
    return q.reshape(b, NSA_HEADS * ts, LANES)


def _dn_lane_row(vals):
    return jnp.zeros((1, LANES), F32).at[0, SM_A:SM_A + DN_HEADS].set(vals.astype(F32))


def _tri_mats():
    tri = np.tril(np.ones((DN_C, DN_C), np.float32))
    return jnp.asarray(tri, BF16), jnp.asarray(tri.T, BF16)


def _deltanet(qkv, small, zg, prefix, s0, dn_params, tri, trit, b, seq_tokens):
    conv_w, a_log, dt_bias, norm_w = dn_params
    if seq_tokens % DN_C == 0:
        nch, n_valid, bb = seq_tokens // DN_C, DN_C, min(b, 4)
    else:
        nch, n_valid, bb = 1, seq_tokens, 4
    prefix8 = jnp.concatenate([jnp.zeros((b, HALO - (CONV_W - 1), DN_QKV_W), F32), prefix], axis=1)
    u, w, qd, qk, kdt, gl = _dn_prep(qkv, prefix8, small, conv_w, _dn_lane_row(a_log), _dn_lane_row(dt_bias),
                                     tri, trit, b, nch, n_valid)
    ob, s_out = _dn_scan(u, w, qd, qk, kdt, gl, zg, s0, norm_w[None], b, nch, bb, n_valid)
    return ob.reshape(b * seq_tokens, DN_W), s_out


def _head_perm_rows(wa):
    d = wa.shape[1]
    return wa.reshape(NSA_KV_HEADS, NSA_GROUP, HEAD_DIM, d).transpose(1, 0, 2, 3).reshape(NSA_Q_W, d)


def _pack_compress(pos_emb, w1, w2):
    half = CMP_STRIDE * HEAD_DIM
    pe = pos_emb.reshape(2, 2, half)
    pe = jnp.concatenate([pe, jnp.zeros((2, 6, half), pe.dtype)], axis=1)
    z = jnp.zeros((2, half, HEAD_DIM), w1.dtype)
    w1a, w1b = w1[:, :half], w1[:, half:]
    pad0 = lambda w: jnp.concatenate([w, z], axis=2).astype(BF16)
    pad1 = lambda w: jnp.concatenate([z, w], axis=2).astype(BF16)
    zz = jnp.zeros_like(w2)
    w2d = jnp.concatenate([jnp.concatenate([w2, zz], axis=2), jnp.concatenate([zz, w2], axis=2)], axis=1)
    return pe, pad0(w1a), pad1(w1a), pad0(w1b), pad1(w1b), w2d.astype(BF16)


def kernel(x_prompt, x_sample, cache_nsa_kv, state_nsa_win, state_dn_S, state_dn_conv, page_table,
           ffn1_norm, ffn1_w_in, ffn1_w_out, mix_norm, w_in, nsa_cmp_pos, nsa_cmp_w1, nsa_cmp_w2,
           dn_conv_w, dn_A_log, dn_dt_bias, dn_out_norm, w_branch_a, w_branch_b, w_out,
           ffn2_norm, ffn2_w_in, ffn2_w_out, final_norm):
    b, seq = x_prompt.shape[:2]
    bs, ts = x_sample.shape[:2]
    depth = w_in.shape[0]
    n_phys = cache_nsa_kv.shape[1]
    past = page_table.shape[1] * PAGE_SIZE
    wbuf = state_nsa_win.shape[2]
    kvw = 4 * NSA_KV_W

    xp = x_prompt.reshape(b * seq, D_MODEL)
    xs = x_sample.reshape(bs * ts, D_MODEL)
    cos_p, sin_p = _rope_tables(jnp.arange(seq, dtype=jnp.int32))
    cos_s, sin_s = _rope_tables(jnp.tile(past + jnp.arange(ts, dtype=jnp.int32), bs))
    nc = (seq - CMP_BLOCK) // CMP_STRIDE + 1
    ovt_p = _overlap_t(seq // CMP_STRIDE, nc, seq // SEL_BLOCK, LANES)
    ns_s = -(-(past + ts) // SEL_BLOCK)
    ovt_s = _overlap_t(past // CMP_STRIDE, past // CMP_STRIDE - 1, ns_s, -(-ns_s // 16) * 16)
    rexp = _gate_expand_matrix()
    tri, trit = _tri_mats()
    cache = cache_nsa_kv.reshape(depth, n_phys, 4 * PAGE_SIZE, LANES)
    win_state = state_nsa_win.reshape(depth, bs, wbuf, 2 * NSA_KV_W)
    tm_p, tm_s = 512, bs * ts

    outs = [[] for _ in range(8)]
    for l in range(depth):
        f1 = (ffn1_norm[l][None], ffn1_w_in[l].astype(BF16), ffn1_w_out[l].astype(BF16))
        f2 = (ffn2_norm[l][None], ffn2_w_in[l].astype(BF16), ffn2_w_out[l].astype(BF16))
        w_mix = _pack_mix_weight(w_in[l])
        mrg = (_head_perm_rows(w_branch_a[l]).astype(BF16), w_branch_b[l].astype(BF16), w_out[l].astype(BF16))
        dn_params = (dn_conv_w[l], dn_A_log[l], dn_dt_bias[l], dn_out_norm[l])
        cmp_l = (nsa_cmp_pos[l], nsa_cmp_w1[l], nsa_cmp_w2[l])

        xp = _ffn(xp, *f1, tm_p)
        (qraw, qrot, rows, win, ksa, vs, win16, small, qkv, zg, mg) = _mix(
            xp, mix_norm[l][None], w_mix, cos_p, sin_p, 256, seq)
        seg = rows.reshape(b, seq // CMP_STRIDE, CMP_STRIDE, 4 * NSA_KV_HEADS, HEAD_DIM)[:, :, :, 0:4]
        seg = seg.transpose(0, 3, 1, 2, 4).reshape(b, 4, seq // CMP_STRIDE, CMP_STRIDE * HEAD_DIM)
        kcvc = _compress_prompt(seg, *_pack_compress(*cmp_l))
        oa = _nsa_prompt(qraw, qrot, small, kcvc, ksa, vs, win16, ovt_p, rexp, b, seq)
        ob, s_fin = _deltanet(qkv, small, zg, jnp.zeros((b, CONV_W - 1, DN_QKV_W), F32),
                              jnp.zeros((b, DN_HEADS, DN_DK, DN_DV), F32), dn_params, tri, trit, b, seq)
        xp = _merge(xp, oa, ob, mg, *mrg, tm_p)
        xp = _ffn(xp, *f2, tm_p)
        outs[0].append(rows.reshape(b, seq, 4, NSA_KV_HEADS, HEAD_DIM))
        outs[1].append(win.reshape(b, seq, 2, NSA_KV_HEADS, HEAD_DIM)[:, -min(WINDOW, seq):])
        outs[2].append(s_fin)
        outs[3].append(qkv.reshape(b, seq, DN_QKV_W)[:, seq - (CONV_W - 1):])

        xs = _ffn(xs, *f1, tm_s)
        (qraw, qrot, rows, win, _, _, _, small, qkv, zg, mg) = _mix(
            xs, mix_norm[l][None], w_mix, cos_s, sin_s, tm_s, tm_s)
        oa = _nsa_sample(l, cache, page_table, _sample_rows(qraw, bs, ts), _sample_rows(qrot, bs, ts),
                         small.reshape(bs, ts, LANES), rows.reshape(bs, ts, kvw), win.reshape(bs, ts, 2 * NSA_KV_W),
                         win_state, *_pack_compress_sample(*cmp_l), ovt_s, rexp, past)
        oa = oa[:, :ts].reshape(bs * ts, NSA_Q_W).astype(BF16)
        ob, s_fin = _deltanet(qkv, small, zg, state_dn_conv[l], state_dn_S[l], dn_params, tri, trit, bs, ts)
        xs = _merge(xs, oa, ob, mg, *mrg, tm_s)
        xs = _ffn(xs, *f2, tm_s)
        outs[4].append(rows.reshape(bs, ts, 4, NSA_KV_HEADS, HEAD_DIM))
        win_all = jnp.concatenate([state_nsa_win[l], win.reshape(bs, ts, 2, NSA_KV_HEADS, HEAD_DIM)], axis=1)
        outs[5].append(win_all[:, ts:])
        outs[6].append(s_fin)
        conv_all = jnp.concatenate([state_dn_conv[l], qkv.reshape(bs, ts, DN_QKV_W)], axis=1)
        outs[7].append(conv_all[:, ts:])

    y_prompt = _final_norm(xp, final_norm[None], tm_p).reshape(b, seq, D_MODEL)
    y_sample = _final_norm(xs, final_norm[None], tm_s).reshape(bs, ts, D_MODEL)
    return (y_prompt, y_sample) + tuple(jnp.stack(o) for o in outs)
```

```python
import functools
import math

import jax
import jax.numpy as jnp
import numpy as np
from jax import lax
from jax.experimental import pallas as pl
from jax.experimental.pallas import tpu as pltpu

F32 = jnp.float32
BF16 = jnp.bfloat16

D_MODEL = 1024
DEPTH = 4
PAGE_SIZE = 128
NSA_HEADS = 8
NSA_KV_HEADS = 2
HEAD_DIM = 64
NSA_GROUP = NSA_HEADS // NSA_KV_HEADS
NSA_Q_W = NSA_HEADS * HEAD_DIM
NSA_KV_W = NSA_KV_HEADS * HEAD_DIM
CMP_BLOCK = 32
CMP_STRIDE = 16
SEL_BLOCK = 64
N_SEL = 16
WINDOW = 512
FORCE_BONUS = 1e6
ROPE_THETA = 10000.0
DN_HEADS = 4
DN_DK = 128
DN_DV = 128
DN_QK_W = DN_HEADS * DN_DK
DN_W = DN_HEADS * DN_DV
DN_QKV_W = 2 * DN_QK_W + DN_W
CONV_W = 4
D_FF = 2816
EPS = 1e-6

LANES = 128
VMEM_LIMIT = 56 * 1024 * 1024
NEG_BIG = -1e30

QPAD_W = NSA_HEADS * LANES
C_Q = 0
C_KV = C_Q + QPAD_W
C_SMALL = C_KV + 6 * NSA_KV_W
C_QKV = C_SMALL + LANES
C_ZG = C_QKV + DN_QKV_W
C_MERGE = C_ZG + DN_W
MIX_W = C_MERGE + 2 * D_MODEL
SM_GATE = 0
SM_A = 3 * NSA_HEADS
SM_B = SM_A + DN_HEADS


def _dot(a, b):
    return jnp.dot(a, b, preferred_element_type=F32)


def _dot_nt(a, b):
    return lax.dot_general(a, b, (((1,), (1,)), ((), ())), preferred_element_type=F32)


def _split3(a):
    hi = a.astype(BF16)
    r1 = a - hi.astype(F32)
    mid = r1.astype(BF16)
    lo = (r1 - mid.astype(F32)).astype(BF16)
    return hi, mid, lo


def _dot_exact_rhs(a, b01):
    hi, mid, lo = _split3(a)
    return _dot(hi, b01) + _dot(mid, b01) + _dot(lo, b01)


def _dot_exact_lhs(a01, b):
    hi, mid, lo = _split3(b)
    return _dot(a01, hi) + _dot(a01, mid) + _dot(a01, lo)


def _dot_exact_lhs_nt(a01, b):
    hi, mid, lo = _split3(b)
    return _dot_nt(a01, hi) + _dot_nt(a01, mid) + _dot_nt(a01, lo)


def _dot3(a, b):
    ah = a.astype(BF16)
    al = (a - ah.astype(F32)).astype(BF16)
    bh = b.astype(BF16)
    bl = (b - bh.astype(F32)).astype(BF16)
    return _dot(ah, bh) + (_dot(ah, bl) + _dot(al, bh))


def _rms(x, w):
    return x * lax.rsqrt(jnp.mean(x * x, axis=-1, keepdims=True) + EPS) * w


def _sigmoid(x):
    return 1.0 / (1.0 + jnp.exp(-x))


def _silu(x):
    return x * _sigmoid(x)


def _params(*sem):
    return pltpu.CompilerParams(dimension_semantics=sem, vmem_limit_bytes=VMEM_LIMIT)


def _const_spec(shape):
    nd = len(shape)
    return pl.BlockSpec(shape, lambda *_: (0,) * nd)


FF_TILE = 256


def _ffn_kernel(x_ref, nw_ref, wi_ref, wo_ref, o_ref, act_ref):
    x = x_ref[...]
    hb = _rms(x, nw_ref[...]).astype(BF16)
    for j in range(D_FF // FF_TILE):
        g = _dot(hb, wi_ref[:, j * FF_TILE:(j + 1) * FF_TILE])
        u = _dot(hb, wi_ref[:, D_FF + j * FF_TILE:D_FF + (j + 1) * FF_TILE])
        act_ref[:, j * FF_TILE:(j + 1) * FF_TILE] = (_silu(g) * u).astype(BF16)
    o_ref[...] = x + 0.5 * _dot(act_ref[...], wo_ref[...])


def _ffn(x, nw, wi, wo, tm):
    m = x.shape[0]
    return pl.pallas_call(
        _ffn_kernel,
        grid=(m // tm,),
        in_specs=[pl.BlockSpec((tm, D_MODEL), lambda i: (i, 0)),
                  _const_spec((1, D_MODEL)),
                  _const_spec((D_MODEL, 2 * D_FF)),
                  _const_spec((D_FF, D_MODEL))],
        out_specs=pl.BlockSpec((tm, D_MODEL), lambda i: (i, 0)),
        out_shape=jax.ShapeDtypeStruct((m, D_MODEL), F32),
        scratch_shapes=[pltpu.VMEM((tm, D_FF), BF16)],
        compiler_params=_params("parallel"), name="half_ffn",
    )(x, nw, wi, wo)


def _rope128(x, cos, sin):
    lane = lax.broadcasted_iota(jnp.int32, x.shape, 1)
    first = (lane % HEAD_DIM) < (HEAD_DIM // 2)
    swapped = jnp.where(first, pltpu.roll(x, LANES - HEAD_DIM // 2, 1), pltpu.roll(x, HEAD_DIM // 2, 1))
    return x * cos + swapped * sin


def _mix_kernel(seq_tiles, x_ref, nw_ref, w_ref, cos_ref, sin_ref,
                qraw_ref, qrot_ref, rows_ref, win_ref, ksa_ref, vs_ref, win16_ref,
                small_ref, qkv_ref, zg_ref, merge_ref):
    tm = x_ref.shape[0]
    hb = _rms(x_ref[...], nw_ref[...]).astype(BF16)
    cos = cos_ref[...]
    sin = sin_ref[...]
    scale = HEAD_DIM ** -0.5
    for j in range(NSA_HEADS):
        q = _dot(hb, w_ref[:, C_Q + j * LANES:C_Q + (j + 1) * LANES])
        qraw_ref[:, j * LANES:(j + 1) * LANES] = (q * scale).astype(BF16)
        qrot_ref[:, j * LANES:(j + 1) * LANES] = (_rope128(q, cos, sin) * scale).astype(BF16)
    kv = _dot(hb, w_ref[:, C_KV:C_KV + 6 * NSA_KV_W])
    k_sel = _rope128(kv[:, 2 * LANES:3 * LANES], cos, sin)
    v_sel = kv[:, 3 * LANES:4 * LANES]
    k_win = _rope128(kv[:, 4 * LANES:5 * LANES], cos, sin)
    v_win = kv[:, 5 * LANES:6 * LANES]
    rows_ref[:, 0:2 * LANES] = kv[:, 0:2 * LANES]
    rows_ref[:, 2 * LANES:3 * LANES] = k_sel
    rows_ref[:, 3 * LANES:4 * LANES] = v_sel
    win_ref[:, 0:LANES] = k_win
    win_ref[:, LANES:2 * LANES] = v_win
    pos = (pl.program_id(0) % seq_tiles) * tm + lax.broadcasted_iota(jnp.int32, (tm, LANES), 0)
    lane = lax.broadcasted_iota(jnp.int32, (tm, LANES), 1)
    ksa_ref[:, 0:LANES] = k_sel.astype(BF16)
    ksa_ref[:, LANES:2 * LANES] = jnp.where(pos // SEL_BLOCK == lane, 1.0, 0.0).astype(BF16)
    vs_ref[...] = v_sel.astype(BF16)
    win16_ref[:, 0:LANES] = k_win.astype(BF16)
    win16_ref[:, LANES:2 * LANES] = v_win.astype(BF16)
    small_ref[...] = _dot(hb, w_ref[:, C_SMALL:C_SMALL + LANES])
    qkv_ref[...] = _dot(hb, w_ref[:, C_QKV:C_QKV + DN_QKV_W])
    zg_ref[...] = _dot(hb, w_ref[:, C_ZG:C_ZG + DN_W])
    merge_ref[...] = _dot(hb, w_ref[:, C_MERGE:C_MERGE + 2 * D_MODEL])


def _mix(x, nw, w, cos, sin, tm, seq_len):
    m = x.shape[0]
    seq_tiles = max(seq_len // tm, 1)
    row = lambda width: pl.BlockSpec((tm, width), lambda i: (i, 0))
    tab = pl.BlockSpec((tm, LANES), lambda i: (i % seq_tiles, 0))
    widths = [(QPAD_W, BF16), (QPAD_W, BF16), (4 * NSA_KV_W, F32), (2 * NSA_KV_W, F32),
              (2 * LANES, BF16), (LANES, BF16), (2 * LANES, BF16), (LANES, F32),
              (DN_QKV_W, F32), (DN_W, F32), (2 * D_MODEL, F32)]
    return pl.pallas_call(
        functools.partial(_mix_kernel, seq_tiles),
        grid=(m // tm,),
        in_specs=[row(D_MODEL), _const_spec((1, D_MODEL)), _const_spec((D_MODEL, MIX_W)), tab, tab],
        out_specs=[row(wd) for wd, _ in widths],
        out_shape=[jax.ShapeDtypeStruct((m, wd), dt) for wd, dt in widths],
        compiler_params=_params("parallel"), name="mix_in",
    )(x, nw, w, cos, sin)


def _merge_kernel(x_ref, oa_ref, ob_ref, mg_ref, wa_ref, wb_ref, wo_ref, o_ref):
    mg = mg_ref[...]
    ya = _dot(oa_ref[...], wa_ref[...])
    yb = _dot(ob_ref[...], wb_ref[...])
    y = _sigmoid(mg[:, :D_MODEL]) * ya + _sigmoid(mg[:, D_MODEL:]) * yb
    o_ref[...] = x_ref[...] + _dot(y.astype(BF16), wo_ref[...])


def _merge(x, oa, ob, mg, wa, wb, wo, tm):
    m = x.shape[0]
    row = lambda width: pl.BlockSpec((tm, width), lambda i: (i, 0))
    return pl.pallas_call(
        _merge_kernel,
        grid=(m // tm,),
        in_specs=[row(D_MODEL), row(NSA_Q_W), row(DN_W), row(2 * D_MODEL),
                  _const_spec((NSA_Q_W, D_MODEL)), _const_spec((DN_W, D_MODEL)),
                  _const_spec((D_MODEL, D_MODEL))],
        out_specs=row(D_MODEL),
        out_shape=jax.ShapeDtypeStruct((m, D_MODEL), F32),
        compiler_params=_params("parallel"), name="branch_merge",
    )(x, oa, ob, mg, wa, wb, wo)


def _norm_kernel(x_ref, w_ref, o_ref):
    o_ref[...] = _rms(x_ref[...], w_ref[...])


def _final_norm(x, w, tm):
    m = x.shape[0]
    row = pl.BlockSpec((tm, D_MODEL), lambda i: (i, 0))
    return pl.pallas_call(
        _norm_kernel, grid=(m // tm,), in_specs=[row, _const_spec((1, D_MODEL))], out_specs=row,
        out_shape=jax.ShapeDtypeStruct((m, D_MODEL), F32), compiler_params=_params("parallel"),
    )(x, w)


def _compress_kernel(seg_ref, pe_ref, w1a0_ref, w1a1_ref, w1b0_ref, w1b1_ref, w2_ref, o_ref):
    ns = seg_ref.shape[2]
    pe_a = pe_ref[0, 0:1, :]
    pe_b = pe_ref[0, 1:2, :]
    s0 = seg_ref[0, 0]
    s1 = seg_ref[0, 1]
    sa = _dot((s0 + pe_a).astype(BF16), w1a0_ref[0]) + _dot((s1 + pe_a).astype(BF16), w1a1_ref[0])
    sb = _dot((s0 + pe_b).astype(BF16), w1b0_ref[0]) + _dot((s1 + pe_b).astype(BF16), w1b1_ref[0])
    pre = sa + pltpu.roll(sb, ns - 1, 0)
    o_ref[0, 0] = _dot(jax.nn.gelu(pre).astype(BF16), w2_ref[0]).astype(BF16)


def _compress_prompt(seg, pe, w1a0, w1a1, w1b0, w1b1, w2):
    b, _, ns, sw = seg.shape
    wspec = pl.BlockSpec((1, sw, LANES), lambda i, c: (c, 0, 0))
    return pl.pallas_call(
        _compress_kernel,
        grid=(b, 2),
        in_specs=[pl.BlockSpec((1, 2, ns, sw), lambda i, c: (i, c, 0, 0)),
                  pl.BlockSpec((1, 8, sw), lambda i, c: (c, 0, 0)),
                  wspec, wspec, wspec, wspec,
                  pl.BlockSpec((1, LANES, LANES), lambda i, c: (c, 0, 0))],
        out_specs=pl.BlockSpec((1, 1, ns, LANES), lambda i, c: (i, c, 0, 0)),
        out_shape=jax.ShapeDtypeStruct((b, 2, ns, LANES), BF16),
        compiler_params=_params("parallel", "parallel"), name="nsa_compress",
    )(seg, pe, w1a0, w1a1, w1b0, w1b1, w2)


TQ = 128
KT = 512
WIN_KEYS = WINDOW + TQ


def _softmax_rows(s, mask):
    s = jnp.where(mask, s, NEG_BIG)
    e = jnp.where(mask, jnp.exp(s - jnp.max(s, axis=-1, keepdims=True)), 0.0)
    return e * (1.0 / jnp.maximum(jnp.sum(e, axis=-1, keepdims=True), 1e-30))


def _topk_bias_t(imp_t, n_cand, n_sel):
    nr = imp_t.shape[0] // 8
    xs = [imp_t[8 * r:8 * r + 8, :] for r in range(nr)]
    cnt = [jnp.zeros_like(xs[0]) for _ in range(nr)]
    sub = lax.broadcasted_iota(jnp.int32, xs[0].shape, 0)
    for j in range(n_cand):
        rj = j // 8
        row = xs[rj][j % 8:j % 8 + 1, :]
        for r in range(nr):
            if r < rj:
                inc = jnp.where(row > xs[r], 1.0, 0.0)
            elif r > rj:
                inc = jnp.where(row >= xs[r], 1.0, 0.0)
            else:
                inc = jnp.where(sub > j % 8, jnp.where(row >= xs[r], 1.0, 0.0),
                                jnp.where(row > xs[r], 1.0, 0.0))
            cnt[r] = cnt[r] + inc
    return jnp.concatenate([jnp.where(c < n_sel, 0.0, NEG_BIG) for c in cnt], axis=0)


def _nsa_prompt_kernel(qraw_ref, qrot_ref, small_ref, kc_ref, vc_ref, ksa_ref, vs_ref, win_ref,
                       ovt_ref, rexp_ref, o_ref):
    seq = ksa_ref.shape[1]
    nc = kc_ref.shape[2]
    ns = seq // SEL_BLOCK
    s0 = pl.multiple_of(pl.program_id(1) * TQ, TQ)
    rows = NSA_GROUP * TQ
    qpos = s0 + lax.broadcasted_iota(jnp.int32, (rows, 1), 0) % TQ
    gexp = _dot_exact_rhs(_sigmoid(small_ref[...]), rexp_ref[...])

    blk_t = lax.broadcasted_iota(jnp.int32, (ns, TQ), 0)
    qpos_t = s0 + lax.broadcasted_iota(jnp.int32, (ns, TQ), 1)
    cur_t = qpos_t // SEL_BLOCK
    valid_t = blk_t * SEL_BLOCK <= qpos_t
    forced_t = (blk_t == 0) | (blk_t == cur_t) | (blk_t == cur_t - 1)

    res = []
    for k in range(NSA_KV_HEADS):
        def stack(ref):
            return jnp.concatenate(
                [ref[:, (k * NSA_GROUP + g) * LANES:(k * NSA_GROUP + g + 1) * LANES]
                 for g in range(NSA_GROUP)], axis=0)
        q_raw = stack(qraw_ref)
        q_rot = stack(qrot_ref)
        s_c = _dot_nt(q_raw, kc_ref[0, 0])
        cend = lax.broadcasted_iota(jnp.int32, (rows, nc), 1) * CMP_STRIDE + (CMP_BLOCK - 1)
        p_c = _softmax_rows(s_c, cend <= qpos)
        o_c = _dot(p_c.astype(BF16), vc_ref[0, 1])
        p_sum = p_c[0:TQ] + p_c[TQ:2 * TQ] + p_c[2 * TQ:3 * TQ] + p_c[3 * TQ:4 * TQ]
        imp_t = _dot_exact_lhs_nt(ovt_ref[...], p_sum)[0:ns]
        imp_t = jnp.where(valid_t, imp_t + jnp.where(forced_t, FORCE_BONUS, 0.0), -jnp.inf)
        bias_t = _topk_bias_t(imp_t, ns, min(N_SEL, ns))
        if ns < LANES:
            bias_t = jnp.concatenate([bias_t, jnp.zeros((LANES - ns, TQ), F32)], axis=0)
        bias = bias_t.T.astype(BF16)
        q_sel = jnp.concatenate([q_rot, jnp.concatenate([bias] * NSA_GROUP, axis=0)], axis=1)

        def sel_step(c, carry, causal):
            m, l, acc = carry
            k0 = pl.multiple_of(c * KT, KT)
            s = _dot_nt(q_sel, ksa_ref[0, pl.ds(k0, KT), :])
            if causal:
                kpos = k0 + lax.broadcasted_iota(jnp.int32, (rows, KT), 1)
                s = jnp.where(kpos <= qpos, s, NEG_BIG)
            m_new = jnp.maximum(m, jnp.max(s, axis=-1, keepdims=True))
            alpha = jnp.exp(m - m_new)
            p = jnp.exp(s - m_new)
            l = alpha * l + jnp.sum(p, axis=-1, keepdims=True)
            acc = alpha * acc + _dot(p.astype(BF16), vs_ref[0, pl.ds(k0, KT), :])
            return m_new, l, acc
        init = (jnp.full((rows, 1), NEG_BIG, F32), jnp.zeros((rows, 1), F32), jnp.zeros((rows, LANES), F32))
        n_full = s0 // KT
        carry = lax.fori_loop(0, n_full, functools.partial(sel_step, causal=False), init)
        _, l_s, acc_s = sel_step(n_full, carry, True)
        o_s = acc_s * (1.0 / l_s)

        w0 = pl.multiple_of(jnp.maximum(s0 - WINDOW, 0), TQ)
        kw = win_ref[0, pl.ds(w0, WIN_KEYS), 0:LANES]
        vw = win_ref[0, pl.ds(w0, WIN_KEYS), LANES:2 * LANES]
        s_w = _dot_nt(q_rot, kw)
        kpos_w = w0 + lax.broadcasted_iota(jnp.int32, (rows, WIN_KEYS), 1)
        p_w = _softmax_rows(s_w, (kpos_w <= qpos) & (kpos_w > qpos - WINDOW))
        o_w = _dot(p_w.astype(BF16), vw)
        res.append((o_c, o_s, o_w))

    low = lax.broadcasted_iota(jnp.int32, (TQ, LANES), 1) < HEAD_DIM
    for g in range(NSA_GROUP):
        out = jnp.zeros((TQ, LANES), F32)
        for br in range(3):
            both = jnp.where(low, res[0][br][g * TQ:(g + 1) * TQ], res[1][br][g * TQ:(g + 1) * TQ])
            out = out + gexp[:, br * NSA_Q_W + g * LANES:br * NSA_Q_W + (g + 1) * LANES] * both
        o_ref[:, g * LANES:(g + 1) * LANES] = out.astype(BF16)


def _nsa_prompt(qraw, qrot, small, kcvc, ksa, vs, win16, ovt, rexp, b, seq):
    nc = kcvc.shape[2]
    qspec = pl.BlockSpec((TQ, QPAD_W), lambda i, j: (i * (seq // TQ) + j, 0))
    per_b = lambda width: pl.BlockSpec((1, seq, width), lambda i, j: (i, 0, 0))
    return pl.pallas_call(
        _nsa_prompt_kernel,
        grid=(b, seq // TQ),
        in_specs=[qspec, qspec,
                  pl.BlockSpec((TQ, LANES), lambda i, j: (i * (seq // TQ) + j, 0)),
                  pl.BlockSpec((1, 2, nc, LANES), lambda i, j: (i, 0, 0, 0)),
                  pl.BlockSpec((1, 2, nc, LANES), lambda i, j: (i, 0, 0, 0)),
                  per_b(2 * LANES), per_b(LANES), per_b(2 * LANES),
                  _const_spec(ovt.shape), _const_spec(rexp.shape)],
        out_specs=pl.BlockSpec((TQ, NSA_Q_W), lambda i, j: (i * (seq // TQ) + j, 0)),
        out_shape=jax.ShapeDtypeStruct((b * seq, NSA_Q_W), BF16),
        compiler_params=_params("parallel", "parallel"), name="nsa_prompt",
    )(qraw, qrot, small, kcvc, kcvc, ksa.reshape(b, seq, 2 * LANES), vs.reshape(b, seq, LANES),
      win16.reshape(b, seq, 2 * LANES), ovt, rexp)


PAGES_PER_STEP = 8
SEG_PER_PAGE = PAGE_SIZE // CMP_STRIDE
SEG_W = CMP_STRIDE * 2 * NSA_KV_W


def _nsa_sample_kernel(ts, past, pt_ref, *refs):
    pages = refs[:PAGES_PER_STEP]
    (qraw_ref, qrot_ref, small_ref, rows_ref, winnew_ref, winst_ref, pea_ref, peb_ref, wa_ref, wb_ref, w2_ref,
     ovt_ref, rexp_ref, o_ref, sega_ref, segb_ref, ksat_ref, vst_ref, sm_ref, stage_ref) = refs[PAGES_PER_STEP:]
    del pt_ref
    pg = pl.program_id(1)
    nseg = sega_ref.shape[0]
    nrows = NSA_HEADS * ts
    kt = NSA_KV_HEADS * ts

    for e in range(PAGES_PER_STEP):
        for part in range(2):
            stage_ref[e, part] = pages[e][0, 0, part * LANES:(part + 1) * LANES, :].T
        k0 = pl.multiple_of((pg * PAGES_PER_STEP + e) * PAGE_SIZE, PAGE_SIZE)
        ksat_ref[0:LANES, pl.ds(k0, PAGE_SIZE)] = pages[e][0, 0, 2 * LANES:3 * LANES, :].astype(BF16)
        kpos = k0 + lax.broadcasted_iota(jnp.int32, (LANES, PAGE_SIZE), 1)
        blk = lax.broadcasted_iota(jnp.int32, (LANES, PAGE_SIZE), 0)
        ksat_ref[LANES:2 * LANES, pl.ds(k0, PAGE_SIZE)] = jnp.where(kpos // SEL_BLOCK == blk, 1.0, 0.0).astype(BF16)
        vst_ref[:, pl.ds(k0, PAGE_SIZE)] = pages[e][0, 0, 3 * LANES:4 * LANES, :].astype(BF16)
    for pair in range(PAGES_PER_STEP // 2):
        r0 = pl.multiple_of(pg * (PAGES_PER_STEP * SEG_PER_PAGE) + pair * 2 * SEG_PER_PAGE, 2 * SEG_PER_PAGE)
        for p in range(CMP_STRIDE):
            for part in range(2):
                x = jnp.concatenate(
                    [stage_ref[2 * pair + e, part, pl.ds(p, SEG_PER_PAGE, stride=CMP_STRIDE), :] for e in range(2)],
                    axis=0)
                cs = slice((2 * p + part) * LANES, (2 * p + part + 1) * LANES)
                sega_ref[pl.ds(r0, 2 * SEG_PER_PAGE), cs] = (x + pea_ref[:, cs]).astype(BF16)
                segb_ref[pl.ds(r0, 2 * SEG_PER_PAGE), cs] = (x + peb_ref[:, cs]).astype(BF16)

    @pl.when(pg == pl.num_programs(1) - 1)
    def _():
        ns_past = past // SEL_BLOCK
        nc = nseg - 1
        q_raw = qraw_ref[0]
        q_rot = qrot_ref[0]
        q_rot_f = q_rot.astype(F32)
        row = lax.broadcasted_iota(jnp.int32, (nrows, 1), 0)
        t_row = row % ts
        pre = _dot(sega_ref[...], wa_ref[...]) + pltpu.roll(_dot(segb_ref[...], wb_ref[...]), nseg - 1, 0)
        kcvc = _dot(jax.nn.gelu(pre).astype(BF16), w2_ref[...]).astype(BF16)
        s_c = _dot_nt(q_raw, kcvc[:, 0:LANES])
        p_c = _softmax_rows(s_c, lax.broadcasted_iota(jnp.int32, (nrows, nseg), 1) < nc)
        o_c = _dot(p_c.astype(BF16), kcvc[:, LANES:2 * LANES])
        p_sum = p_c[0:kt] + p_c[kt:2 * kt] + p_c[2 * kt:3 * kt] + p_c[3 * kt:4 * kt]
        p_sum = jnp.concatenate([p_sum, jnp.zeros((LANES - kt, nseg), F32)], axis=0)
        nb = ovt_ref.shape[0]
        imp_t = _dot_exact_lhs_nt(ovt_ref[...], p_sum)
        blk_t = lax.broadcasted_iota(jnp.int32, (nb, LANES), 0)
        qpos_t = past + lax.broadcasted_iota(jnp.int32, (nb, LANES), 1) % ts
        cur_t = qpos_t // SEL_BLOCK
        forced_t = (blk_t == 0) | (blk_t == cur_t) | (blk_t == cur_t - 1)
        imp_t = jnp.where(blk_t * SEL_BLOCK <= qpos_t, imp_t + jnp.where(forced_t, FORCE_BONUS, 0.0), -jnp.inf)
        bias_t = _topk_bias_t(imp_t, -(-(past + ts) // SEL_BLOCK), N_SEL)[0:ns_past]
        bias = bias_t.T[0:kt]
        q_sel = jnp.concatenate([q_rot, jnp.concatenate([bias] * NSA_GROUP, axis=0).astype(BF16)], axis=1)

        def attend(s_past, v_past16_t, k_new, v_new, mask_past):
            if mask_past is not None:
                s_past = jnp.where(mask_past, s_past, NEG_BIG)
            s_new = [jnp.where(t_row >= j, jnp.sum(q_rot_f * k_new[j:j + 1, :], axis=-1, keepdims=True), NEG_BIG)
                     for j in range(ts)]
            m = jnp.max(s_past, axis=-1, keepdims=True)
            for s in s_new:
                m = jnp.maximum(m, s)
            p_past = jnp.exp(s_past - m)
            if mask_past is not None:
                p_past = jnp.where(mask_past, p_past, 0.0)
            acc = _dot_nt(p_past.astype(BF16), v_past16_t)
            l = jnp.sum(p_past, axis=-1, keepdims=True)
            for j, s in enumerate(s_new):
                p = jnp.where(t_row >= j, jnp.exp(s - m), 0.0)
                acc = acc + p * v_new[j:j + 1, :]
                l = l + p
            return acc * (1.0 / l)

        new = rows_ref[0]
        o_s = attend(_dot(q_sel, ksat_ref[...]), vst_ref[...], new[:, 2 * NSA_KV_W:3 * NSA_KV_W],
                     new[:, 3 * NSA_KV_W:4 * NSA_KV_W], None)
        wbuf = winst_ref.shape[3]
        wnew = winnew_ref[0]
        i_w = lax.broadcasted_iota(jnp.int32, (nrows, wbuf), 1)
        dist = t_row + wbuf - i_w
        o_w = attend(_dot(q_rot, winst_ref[0, 0, 0:LANES, :].astype(BF16)),
                     winst_ref[0, 0, LANES:2 * LANES, :].astype(BF16),
                     wnew[:, 0:LANES], wnew[:, LANES:2 * LANES],
                     (dist < WINDOW) & (past - wbuf + i_w >= 0))

        sm_ref[...] = jnp.zeros_like(sm_ref)
        sm_ref[0:ts, :] = small_ref[0]
        gexp = _dot_exact_rhs(_sigmoid(sm_ref[...]), rexp_ref[...])[0:kt]
        low = lax.broadcasted_iota(jnp.int32, (kt, LANES), 1) < HEAD_DIM
        for g in range(NSA_GROUP):
            out = jnp.zeros((kt, LANES), F32)
            for br, o_br in enumerate((o_c, o_s, o_w)):
                og = o_br[g * kt:(g + 1) * kt]
                both = jnp.where(low, og, pltpu.roll(og, kt - ts, 0))
                out = out + gexp[:, br * NSA_Q_W + g * LANES:br * NSA_Q_W + (g + 1) * LANES] * both
            o_ref[0, :, g * LANES:(g + 1) * LANES] = out


def _nsa_sample(layer, cache, page_table, qraw_s, qrot_s, small_s, rows_s, winnew_s, win_state,
                pea, peb, wa, wb, w2, ovt, rexp, past):
    b, ts = small_s.shape[:2]
    n_pages = page_table.shape[1]
    assert n_pages % PAGES_PER_STEP == 0 and ts * NSA_KV_HEADS == 8 and past == n_pages * PAGE_SIZE
    nseg = past // CMP_STRIDE
    wbuf = win_state.shape[3]
    kw = 4 * NSA_KV_W

    def page_spec(e):
        return pl.BlockSpec((1, 1, 4 * PAGE_SIZE, LANES),
                            lambda i, g, pt: (layer, pt[i, g * PAGES_PER_STEP + e], 0, 0))
    per_b = lambda shape: pl.BlockSpec((1,) + shape, lambda i, g, pt: (i,) + (0,) * len(shape))
    const = lambda shape: pl.BlockSpec(shape, lambda i, g, pt: (0,) * len(shape))
    grid_spec = pltpu.PrefetchScalarGridSpec(
        num_scalar_prefetch=1,
        grid=(b, n_pages // PAGES_PER_STEP),
        in_specs=[page_spec(e) for e in range(PAGES_PER_STEP)] + [
            per_b((NSA_HEADS * ts, LANES)), per_b((NSA_HEADS * ts, LANES)), per_b((ts, LANES)),
            per_b((ts, kw)), per_b((ts, 2 * NSA_KV_W)),
            pl.BlockSpec((1, 1, 2 * NSA_KV_W, wbuf), lambda i, g, pt: (layer, i, 0, 0)),
            const(pea.shape), const(peb.shape), const(wa.shape), const(wb.shape), const(w2.shape),
            const(ovt.shape), const(rexp.shape)],
        out_specs=per_b((NSA_KV_HEADS * ts, NSA_Q_W)),
        scratch_shapes=[pltpu.VMEM((nseg, CMP_STRIDE * 2 * NSA_KV_W), BF16),
                        pltpu.VMEM((nseg, CMP_STRIDE * 2 * NSA_KV_W), BF16),
                        pltpu.VMEM((2 * LANES, past), BF16), pltpu.VMEM((LANES, past), BF16),
                        pltpu.VMEM((2 * NSA_KV_HEADS * ts, LANES), F32),
                        pltpu.VMEM((PAGES_PER_STEP, 2, PAGE_SIZE, LANES), F32)])
    return pl.pallas_call(
        functools.partial(_nsa_sample_kernel, ts, past),
        grid_spec=grid_spec,
        out_shape=jax.ShapeDtypeStruct((b, NSA_KV_HEADS * ts, NSA_Q_W), F32),
        compiler_params=_params("parallel", "arbitrary"), name="nsa_sample",
    )(page_table, *([cache] * PAGES_PER_STEP), qraw_s, qrot_s, small_s, rows_s, winnew_s, win_state,
      pea, peb, wa, wb, w2, ovt, rexp)


DN_C = 128
HALO = 8


def _softplus(x):
    return jnp.maximum(x, 0.0) + jnp.log(1.0 + jnp.exp(-jnp.abs(x)))


def _l2n(x):
    return x * lax.rsqrt(jnp.sum(x * x, axis=-1, keepdims=True) + EPS)


def _dn_apply(p, x):
    return _dot3(p, x)


def _dn_square(p):
    return _dot3(p, p)


def _dn_prep_kernel(n_valid, qkv_ref, halo_ref, prefix_ref, small_ref, cw_ref, alog_ref, dtb_ref,
                    tri_ref, trit_ref, u_ref, w_ref, qd_ref, qk_ref, kdt_ref, gl_ref, xcat_ref, sm_ref):
    n = pl.program_id(1)
    if n_valid == DN_C:
        xcat_ref[HALO:HALO + DN_C, :] = qkv_ref[...]
        small = small_ref[...]

        @pl.when(n == 0)
        def _():
            xcat_ref[0:HALO, :] = prefix_ref[0]

        @pl.when(n > 0)
        def _():
            xcat_ref[0:HALO, :] = halo_ref[...]
    else:
        xcat_ref[...] = jnp.zeros_like(xcat_ref)
        xcat_ref[0:HALO, :] = prefix_ref[0]
        xcat_ref[HALO:HALO + n_valid, :] = qkv_ref[0]
        sm_ref[...] = jnp.zeros_like(sm_ref)
        sm_ref[0:n_valid, :] = small_ref[0]
        small = sm_ref[...]

    conv = jnp.zeros((DN_C, DN_QKV_W), F32)
    for i in range(CONV_W):
        conv = conv + xcat_ref[HALO - (CONV_W - 1) + i:HALO - (CONV_W - 1) + i + DN_C, :] * cw_ref[i:i + 1, :]
    conv = _silu(conv)

    ii = lax.broadcasted_iota(jnp.int32, (DN_C, DN_C), 0)
    jj = lax.broadcasted_iota(jnp.int32, (DN_C, DN_C), 1)
    g_all = -jnp.exp(alog_ref[...]) * _softplus(small + dtb_ref[...])
    beta_all = _sigmoid(small)
    if n_valid != DN_C:
        g_all = jnp.where(ii < n_valid, g_all, 0.0)
        beta_all = jnp.where(ii < n_valid, beta_all, 0.0)
    gc_all = _dot_exact_lhs(tri_ref[...], g_all)
    gr_all = _dot_exact_rhs(g_all.T, trit_ref[...])
    heads = range(DN_HEADS)
    q = [_l2n(conv[:, h * DN_DK:(h + 1) * DN_DK]) * DN_DK ** -0.5 for h in heads]
    k = [_l2n(conv[:, DN_QK_W + h * DN_DK:DN_QK_W + (h + 1) * DN_DK]) for h in heads]
    v = [conv[:, 2 * DN_QK_W + h * DN_DV:2 * DN_QK_W + (h + 1) * DN_DV] for h in heads]
    gcol = [gc_all[:, SM_A + h:SM_A + h + 1] for h in heads]
    beta = [beta_all[:, SM_B + h:SM_B + h + 1] for h in heads]
    decay = [jnp.where(ii >= jj, jnp.exp(gcol[h] - gr_all[SM_A + h:SM_A + h + 1, :]), 0.0) for h in heads]
    kb = [k[h] * beta[h] for h in heads]
    k16 = [k[h].astype(BF16) for h in heads]
    eg = [jnp.exp(gcol[h]) for h in heads]
    p = [jnp.where(ii > jj, -(_dot_nt(kb[h].astype(BF16), k16[h]) * decay[h]), 0.0) for h in heads]
    x = [jnp.concatenate([v[h] * beta[h], kb[h] * eg[h]], axis=1) for h in heads]
    levels = int(math.log2(DN_C))
    for lvl in range(levels):
        x = [x[h] + _dn_apply(p[h], x[h]) for h in heads]
        if lvl + 1 < levels:
            p = [_dn_square(p[h]) for h in heads]
    for h in heads:
        u_ref[:, h * DN_DV:(h + 1) * DN_DV] = x[h][:, 0:DN_DV]
        w_ref[:, h * DN_DK:(h + 1) * DN_DK] = x[h][:, DN_DV:DN_DV + DN_DK].astype(BF16)
        qk = jnp.where(ii >= jj, _dot_nt(q[h].astype(BF16), k16[h]) * decay[h], 0.0)
        qk_ref[:, h * DN_C:(h + 1) * DN_C] = qk.astype(BF16)
        qd_ref[:, h * DN_DK:(h + 1) * DN_DK] = (q[h] * eg[h]).astype(BF16)
        glast = gcol[h][DN_C - 1:DN_C, :]
        kdt_ref[0, h * DN_DK:(h + 1) * DN_DK, :] = (k[h] * jnp.exp(glast - gcol[h])).T.astype(BF16)
        gl_ref[0, h:h + 1, :] = jnp.broadcast_to(jnp.exp(glast), (1, LANES))
    gl_ref[0, DN_HEADS:, :] = jnp.zeros((HALO - DN_HEADS, LANES), F32)


def _dn_prep(qkv, prefix8, small, cw, alog_row, dtb_row, tri, trit, b, nch, n_valid):
    rows = b * nch * DN_C
    if n_valid == DN_C:
        qkv_spec = pl.BlockSpec((DN_C, DN_QKV_W), lambda i, n: (i * nch + n, 0))
        halo_spec = pl.BlockSpec((HALO, DN_QKV_W),
                                 lambda i, n: (jnp.maximum((i * nch + n) * (DN_C // HALO) - 1, 0), 0))
        small_spec = pl.BlockSpec((DN_C, LANES), lambda i, n: (i * nch + n, 0))
        qkv_in, halo_in, small_in = qkv, qkv, small
    else:
        qkv_in = qkv.reshape(b, n_valid, DN_QKV_W)
        small_in = small.reshape(b, n_valid, LANES)
        halo_in = prefix8
        qkv_spec = pl.BlockSpec((1, n_valid, DN_QKV_W), lambda i, n: (i, 0, 0))
        halo_spec = pl.BlockSpec((1, HALO, DN_QKV_W), lambda i, n: (i, 0, 0))
        small_spec = pl.BlockSpec((1, n_valid, LANES), lambda i, n: (i, 0, 0))
    tok = lambda width: pl.BlockSpec((DN_C, width), lambda i, n: (i * nch + n, 0))
    return pl.pallas_call(
        functools.partial(_dn_prep_kernel, n_valid),
        grid=(b, nch),
        in_specs=[qkv_spec, halo_spec, pl.BlockSpec((1, HALO, DN_QKV_W), lambda i, n: (i, 0, 0)), small_spec,
                  _const_spec((CONV_W, DN_QKV_W)), _const_spec((1, LANES)), _const_spec((1, LANES)),
                  _const_spec((DN_C, DN_C)), _const_spec((DN_C, DN_C))],
        out_specs=[tok(DN_W), tok(DN_QK_W), tok(DN_QK_W), tok(DN_HEADS * DN_C),
                   pl.BlockSpec((1, DN_QK_W, DN_C), lambda i, n: (i * nch + n, 0, 0)),
                   pl.BlockSpec((1, HALO, LANES), lambda i, n: (i * nch + n, 0, 0))],
        out_shape=[jax.ShapeDtypeStruct((rows, DN_W), F32), jax.ShapeDtypeStruct((rows, DN_QK_W), BF16),
                   jax.ShapeDtypeStruct((rows, DN_QK_W), BF16), jax.ShapeDtypeStruct((rows, DN_HEADS * DN_C), BF16),
                   jax.ShapeDtypeStruct((b * nch, DN_QK_W, DN_C), BF16),
                   jax.ShapeDtypeStruct((b * nch, HALO, LANES), F32)],
        scratch_shapes=[pltpu.VMEM((HALO + DN_C, DN_QKV_W), F32), pltpu.VMEM((DN_C, LANES), F32)],
        compiler_params=_params("parallel", "parallel"), name="dn_prep",
    )(qkv_in, halo_in, prefix8, small_in, cw, alog_row, dtb_row, tri, trit)


def _dn_scan_kernel(n_valid, u_ref, w_ref, qd_ref, qk_ref, kdt_ref, gl_ref, zg_ref, s0_ref, nw_ref,
                    o_ref, s_ref):
    @pl.when(pl.program_id(1) == 0)
    def _():
        s_ref[...] = s0_ref[...]

    for b in range(u_ref.shape[0]):
        for h in range(DN_HEADS):
            cs = slice(h * DN_DK, (h + 1) * DN_DK)
            s = s_ref[b, h]
            s16 = s.astype(BF16)
            v_new = u_ref[b, :, cs] - _dot(w_ref[b, :, cs], s16)
            v16 = v_new.astype(BF16)
            o = _dot(qd_ref[b, :, cs], s16) + _dot(qk_ref[b, :, cs], v16)
            s_ref[b, h] = s * gl_ref[b, h:h + 1, :] + _dot(kdt_ref[b, cs, :], v16)
            on = o * lax.rsqrt(jnp.mean(o * o, axis=-1, keepdims=True) + EPS) * nw_ref[...]
            o_ref[b, :, cs] = (on[0:n_valid] * _silu(zg_ref[b, :, cs])).astype(BF16)


def _dn_scan(u, w, qd, qk, kdt, gl, zg, s0, nw, b, nch, bb, n_valid):
    tok = lambda width: pl.BlockSpec((bb, DN_C, width), lambda i, n: (i, n, 0))
    r3 = lambda a: a.reshape(b, nch * DN_C, a.shape[-1])
    st = pl.BlockSpec((bb, DN_HEADS, DN_DK, DN_DV), lambda i, n: (i, 0, 0, 0))
    return pl.pallas_call(
        functools.partial(_dn_scan_kernel, n_valid),
        grid=(b // bb, nch),
        in_specs=[tok(DN_W), tok(DN_QK_W), tok(DN_QK_W), tok(DN_HEADS * DN_C),
                  pl.BlockSpec((bb, DN_QK_W, DN_C), lambda i, n: (i, n, 0)),
                  pl.BlockSpec((bb, HALO, LANES), lambda i, n: (i, n, 0)),
                  pl.BlockSpec((bb, n_valid, DN_W), lambda i, n: (i, n, 0)), st, _const_spec((1, DN_DV))],
        out_specs=[pl.BlockSpec((bb, n_valid, DN_W), lambda i, n: (i, n, 0)), st],
        out_shape=[jax.ShapeDtypeStruct((b, nch * n_valid, DN_W), BF16),
                   jax.ShapeDtypeStruct((b, DN_HEADS, DN_DK, DN_DV), F32)],
        compiler_params=_params("parallel", "arbitrary"), name="dn_scan",
    )(r3(u), r3(w), r3(qd), r3(qk), kdt.reshape(b, nch * DN_QK_W, DN_C), gl.reshape(b, nch * HALO, LANES),
      zg.reshape(b, nch * n_valid, DN_W), s0, nw)


IN_WIDTHS = (NSA_Q_W, 6 * NSA_KV_W, 3 * NSA_HEADS, DN_QKV_W, DN_HEADS, DN_HEADS, DN_W, 2 * D_MODEL)


def _pack_mix_weight(w_in):
    offs = np.cumsum(IN_WIDTHS)[:-1].tolist()
    q, kv, gate, qkv, a, b, zg, merge = jnp.split(w_in, offs, axis=-1)
    d = w_in.shape[0]
    qh = q.reshape(d, NSA_KV_HEADS, NSA_GROUP, 1, HEAD_DIM)
    eye = jnp.eye(NSA_KV_HEADS, dtype=w_in.dtype).reshape(1, NSA_KV_HEADS, 1, NSA_KV_HEADS, 1)
    qpad = (qh * eye).reshape(d, QPAD_W)
    small = jnp.concatenate([gate, a, b, jnp.zeros((d, LANES - SM_B - DN_HEADS), w_in.dtype)], axis=1)
    return jnp.concatenate([qpad, kv, small, qkv, zg, merge], axis=1).astype(BF16)


def _rope_tables(pos):
    half = HEAD_DIM // 2
    freq = ROPE_THETA ** (-jnp.arange(half, dtype=F32) / half)
    ang = pos.astype(F32)[:, None] * freq[None, :]
    cos, sin = jnp.cos(ang), jnp.sin(ang)
    cos = jnp.concatenate([cos, cos] * (LANES // HEAD_DIM), axis=1)
    sin = jnp.concatenate([-sin, sin] * (LANES // HEAD_DIM), axis=1)
    return cos, sin


def _gate_expand_matrix():
    r = np.zeros((LANES, 3 * NSA_Q_W), np.float32)
    for h in range(NSA_HEADS):
        k, g = divmod(h, NSA_GROUP)
        for br in range(3):
            c0 = br * NSA_Q_W + g * LANES + k * HEAD_DIM
            r[SM_GATE + h * 3 + br, c0:c0 + HEAD_DIM] = 1.0
    return jnp.asarray(r, BF16)


def _overlap_t(nc_pad, nc, ns, rows):
    i = np.arange(nc_pad)[None, :]
    j = np.arange(rows)[:, None]
    ov = (i * CMP_STRIDE < (j + 1) * SEL_BLOCK) & (i * CMP_STRIDE + CMP_BLOCK > j * SEL_BLOCK)
    ov = ov & (i < nc) & (j < ns)
    return jnp.asarray(ov.astype(np.float32), BF16)


def _pack_compress_sample(pos_emb, w1, w2):
    comp = np.array([0, 0, 1, 1])
    eye4 = jnp.eye(4, dtype=w1.dtype)

    def seg_weight(w1_half):
        w = w1_half.reshape(2, CMP_STRIDE, HEAD_DIM, HEAD_DIM)[comp]
        w = jnp.einsum('cpde,cf->pcdfe', w, eye4)
        return w.reshape(SEG_W, 4 * HEAD_DIM).astype(BF16)

    def seg_pe(pe_half):
        return pe_half[comp].transpose(1, 0, 2).reshape(1, SEG_W)
    half = CMP_STRIDE * HEAD_DIM
    w2c = w2[comp]
    w2d = jnp.einsum('cde,cf->cdfe', w2c, eye4).reshape(4 * HEAD_DIM, 4 * HEAD_DIM).astype(BF16)
    return (seg_pe(pos_emb[:, :CMP_STRIDE]), seg_pe(pos_emb[:, CMP_STRIDE:]),
            seg_weight(w1[:, :half]), seg_weight(w1[:, half:]), w2d)


def _sample_rows(q_pad, b, ts):
    q = q_pad.reshape(b, ts, NSA_KV_HEADS, NSA_GROUP, LANES).transpose(0, 3, 2, 1, 4)
    return q.reshape(b, NSA_HEADS * ts, LANES)


def _dn_lane_row(vals):
    return jnp.zeros((1, LANES), F32).at[0, SM_A:SM_A + DN_HEADS].set(vals.astype(F32))


def _tri_mats():
    tri = np.tril(np.ones((DN_C, DN_C), np.float32))
    return jnp.asarray(tri, BF16), jnp.asarray(tri.T, BF16)


def _deltanet(qkv, small, zg, prefix, s0, dn_params, tri, trit, b, seq_tokens):
    conv_w, a_log, dt_bias, norm_w = dn_params
    if seq_tokens % DN_C == 0:
        nch, n_valid, bb = seq_tokens // DN_C, DN_C, min(b, 4)
    else:
        nch, n_valid, bb = 1, seq_tokens, 4
    prefix8 = jnp.concatenate([jnp.zeros((b, HALO - (CONV_W - 1), DN_QKV_W), F32), prefix], axis=1)
    u, w, qd, qk, kdt, gl = _dn_prep(qkv, prefix8, small, conv_w, _dn_lane_row(a_log), _dn_lane_row(dt_bias),
                                     tri, trit, b, nch, n_valid)
    ob, s_out = _dn_scan(u, w, qd, qk, kdt, gl, zg, s0, norm_w[None], b, nch, bb, n_valid)
    return ob.reshape(b * seq_tokens, DN_W), s_out


def _head_perm_rows(wa):
    d = wa.shape[1]
    return wa.reshape(NSA_KV_HEADS, NSA_GROUP, HEAD_DIM, d).transpose(1, 0, 2, 3).reshape(NSA_Q_W, d)


def _pack_compress(pos_emb, w1, w2):
    half = CMP_STRIDE * HEAD_DIM
    pe = pos_emb.reshape(2, 2, half)
    pe = jnp.concatenate([pe, jnp.zeros((2, 6, half), pe.dtype)], axis=1)
    z = jnp.zeros((2, half, HEAD_DIM), w1.dtype)
    w1a, w1b = w1[:, :half], w1[:, half:]
    pad0 = lambda w: jnp.concatenate([w, z], axis=2).astype(BF16)
    pad1 = lambda w: jnp.concatenate([z, w], axis=2).astype(BF16)
    zz = jnp.zeros_like(w2)
    w2d = jnp.concatenate([jnp.concatenate([w2, zz], axis=2), jnp.concatenate([zz, w2], axis=2)], axis=1)
    return pe, pad0(w1a), pad1(w1a), pad0(w1b), pad1(w1b), w2d.astype(BF16)


def kernel(x_prompt, x_sample, cache_nsa_kv, state_nsa_win, state_dn_S, state_dn_conv, page_table,
           ffn1_norm, ffn1_w_in, ffn1_w_out, mix_norm, w_in, nsa_cmp_pos, nsa_cmp_w1, nsa_cmp_w2,
           dn_conv_w, dn_A_log, dn_dt_bias, dn_out_norm, w_branch_a, w_branch_b, w_out,
           ffn2_norm, ffn2_w_in, ffn2_w_out, final_norm):
    b, seq = x_prompt.shape[:2]
    bs, ts = x_sample.shape[:2]
    depth = w_in.shape[0]
    n_phys = cache_nsa_kv.shape[1]
    past = page_table.shape[1] * PAGE_SIZE
    wbuf = state_nsa_win.shape[2]
    kvw = 4 * NSA_KV_W

    xp = x_prompt.reshape(b * seq, D_MODEL)
    xs = x_sample.reshape(bs * ts, D_MODEL)
    cos_p, sin_p = _rope_tables(jnp.arange(seq, dtype=jnp.int32))
    cos_s, sin_s = _rope_tables(jnp.tile(past + jnp.arange(ts, dtype=jnp.int32), bs))
    nc = (seq - CMP_BLOCK) // CMP_STRIDE + 1
    ovt_p = _overlap_t(seq // CMP_STRIDE, nc, seq // SEL_BLOCK, LANES)
    ns_s = -(-(past + ts) // SEL_BLOCK)
    ovt_s = _overlap_t(past // CMP_STRIDE, past // CMP_STRIDE - 1, ns_s, -(-ns_s // 16) * 16)
    rexp = _gate_expand_matrix()
    tri, trit = _tri_mats()
    cache = cache_nsa_kv.transpose(0, 1, 3, 4, 5, 2).reshape(depth, n_phys, kvw, PAGE_SIZE)
    win_state = state_nsa_win.transpose(0, 1, 3, 4, 5, 2).reshape(depth, bs, 2 * NSA_KV_W, wbuf)
    tm_p, tm_s = 512, bs * ts

    outs = [[] for _ in range(8)]
    for l in range(depth):
        f1 = (ffn1_norm[l][None], ffn1_w_in[l].astype(BF16), ffn1_w_out[l].astype(BF16))
        f2 = (ffn2_norm[l][None], ffn2_w_in[l].astype(BF16), ffn2_w_out[l].astype(BF16))
        w_mix = _pack_mix_weight(w_in[l])
        mrg = (_head_perm_rows(w_branch_a[l]).astype(BF16), w_branch_b[l].astype(BF16), w_out[l].astype(BF16))
        dn_params = (dn_conv_w[l], dn_A_log[l], dn_dt_bias[l], dn_out_norm[l])
        cmp_l = (nsa_cmp_pos[l], nsa_cmp_w1[l], nsa_cmp_w2[l])

        xp = _ffn(xp, *f1, tm_p)
        (qraw, qrot, rows, win, ksa, vs, win16, small, qkv, zg, mg) = _mix(
            xp, mix_norm[l][None], w_mix, cos_p, sin_p, 256, seq)
        seg = rows.reshape(b, seq // CMP_STRIDE, CMP_STRIDE, 4 * NSA_KV_HEADS, HEAD_DIM)[:, :, :, 0:4]
        seg = seg.transpose(0, 3, 1, 2, 4).reshape(b, 4, seq // CMP_STRIDE, CMP_STRIDE * HEAD_DIM)
        kcvc = _compress_prompt(seg, *_pack_compress(*cmp_l))
        oa = _nsa_prompt(qraw, qrot, small, kcvc, ksa, vs, win16, ovt_p, rexp, b, seq)
        ob, s_fin = _deltanet(qkv, small, zg, jnp.zeros((b, CONV_W - 1, DN_QKV_W), F32),
                              jnp.zeros((b, DN_HEADS, DN_DK, DN_DV), F32), dn_params, tri, trit, b, seq)
        xp = _merge(xp, oa, ob, mg, *mrg, tm_p)
        xp = _ffn(xp, *f2, tm_p)
        outs[0].append(rows.reshape(b, seq, 4, NSA_KV_HEADS, HEAD_DIM))
        outs[1].append(win.reshape(b, seq, 2, NSA_KV_HEADS, HEAD_DIM)[:, -min(WINDOW, seq):])
        outs[2].append(s_fin)
        outs[3].append(qkv.reshape(b, seq, DN_QKV_W)[:, seq - (CONV_W - 1):])

        xs = _ffn(xs, *f1, tm_s)
        (qraw, qrot, rows, win, _, _, _, small, qkv, zg, mg) = _mix(
            xs, mix_norm[l][None], w_mix, cos_s, sin_s, tm_s, tm_s)
        oa = _nsa_sample(l, cache, page_table, _sample_rows(qraw, bs, ts), _sample_rows(qrot, bs, ts),
                         small.reshape(bs, ts, LANES), rows.reshape(bs, ts, kvw), win.reshape(bs, ts, 2 * NSA_KV_W),
                         win_state, *_pack_compress_sample(*cmp_l), ovt_s, rexp, past)
        oa = oa[:, :ts].reshape(bs * ts, NSA_Q_W).astype(BF16)
        ob, s_fin = _deltanet(qkv, small, zg, state_dn_conv[l], state_dn_S[l], dn_params, tri, trit, bs, ts)
        xs = _merge(xs, oa, ob, mg, *mrg, tm_s)
        xs = _ffn(xs, *f2, tm_s)
        outs[4].append(rows.reshape(bs, ts, 4, NSA_KV_HEADS, HEAD_DIM))
        win_all = jnp.concatenate([state_nsa_win[l], win.reshape(bs, ts, 2, NSA_KV_HEADS, HEAD_DIM)], axis=1)
        outs[5].append(win_all[:, ts:])
        outs[6].append(s_fin)
        conv_all = jnp.concatenate([state_dn_conv[l], qkv.reshape(bs, ts, DN_QKV_W)], axis=1)
        outs[7].append(conv_all[:, ts:])

    y_prompt = _final_norm(xp, final_norm[None], tm_p).reshape(b, seq, D_MODEL)
    y_sample = _final_norm(xs, final_norm[None], tm_s).reshape(bs, ts, D_MODEL)
    return (y_prompt, y_sample) + tuple(jnp.stack(o) for o in outs)
```

```python
import functools
import math

import jax
import jax.numpy as jnp
import numpy as np
from jax import lax
from jax.experimental import pallas as pl
from jax.experimental.pallas import tpu as pltpu

F32 = jnp.float32
BF16 = jnp.bfloat16

D_MODEL = 1024
DEPTH = 4
PAGE_SIZE = 128
NSA_HEADS = 8
NSA_KV_HEADS = 2
HEAD_DIM = 64
NSA_GROUP = NSA_HEADS // NSA_KV_HEADS
NSA_Q_W = NSA_HEADS * HEAD_DIM
NSA_KV_W = NSA_KV_HEADS * HEAD_DIM
CMP_BLOCK = 32
CMP_STRIDE = 16
SEL_BLOCK = 64
N_SEL = 16
WINDOW = 512
FORCE_BONUS = 1e6
ROPE_THETA = 10000.0
DN_HEADS = 4
DN_DK = 128
DN_DV = 128
DN_QK_W = DN_HEADS * DN_DK
DN_W = DN_HEADS * DN_DV
DN_QKV_W = 2 * DN_QK_W + DN_W
CONV_W = 4
D_FF = 2816
EPS = 1e-6

LANES = 128
VT_ROWS = HEAD_DIM + 16
VMEM_LIMIT = 56 * 1024 * 1024
NEG_BIG = -1e30

QPAD_W = NSA_HEADS * LANES
C_Q = 0
C_KV = C_Q + QPAD_W
C_SMALL = C_KV + 6 * NSA_KV_W
C_QKV = C_SMALL + LANES
C_ZG = C_QKV + DN_QKV_W
C_MERGE = C_ZG + DN_W
MIX_W = C_MERGE + 2 * D_MODEL
SM_GATE = 0
SM_A = 3 * NSA_HEADS
SM_B = SM_A + DN_HEADS


def _dot(a, b):
    return jnp.dot(a, b, preferred_element_type=F32)


def _dot_nt(a, b):
    return lax.dot_general(a, b, (((1,), (1,)), ((), ())), preferred_element_type=F32)


def _split3(a):
    hi = a.astype(BF16)
    r1 = a - hi.astype(F32)
    mid = r1.astype(BF16)
    lo = (r1 - mid.astype(F32)).astype(BF16)
    return hi, mid, lo


def _dot_exact_rhs(a, b01):
    hi, mid, lo = _split3(a)
    return _dot(hi, b01) + _dot(mid, b01) + _dot(lo, b01)


def _dot_exact_lhs(a01, b):
    hi, mid, lo = _split3(b)
    return _dot(a01, hi) + _dot(a01, mid) + _dot(a01, lo)


def _dot_exact_lhs_nt(a01, b):
    hi, mid, lo = _split3(b)
    return _dot_nt(a01, hi) + _dot_nt(a01, mid) + _dot_nt(a01, lo)


def _dot3(a, b):
    ah = a.astype(BF16)
    al = (a - ah.astype(F32)).astype(BF16)
    bh = b.astype(BF16)
    bl = (b - bh.astype(F32)).astype(BF16)
    return _dot(ah, bh) + (_dot(ah, bl) + _dot(al, bh))


def _rms(x, w):
    return x * lax.rsqrt(jnp.mean(x * x, axis=-1, keepdims=True) + EPS) * w


def _sigmoid(x):
    return 1.0 / (1.0 + jnp.exp(-x))


def _silu(x):
    return x * _sigmoid(x)


def _params(*sem):
    return pltpu.CompilerParams(dimension_semantics=sem, vmem_limit_bytes=VMEM_LIMIT)


def _const_spec(shape):
    nd = len(shape)
    return pl.BlockSpec(shape, lambda *_: (0,) * nd)


FF_TILE = 256


def _ffn_kernel(x_ref, nw_ref, wi_ref, wo_ref, o_ref, act_ref):
    x = x_ref[...]
    hb = _rms(x, nw_ref[...]).astype(BF16)
    for j in range(D_FF // FF_TILE):
        g = _dot(hb, wi_ref[:, j * FF_TILE:(j + 1) * FF_TILE])
        u = _dot(hb, wi_ref[:, D_FF + j * FF_TILE:D_FF + (j + 1) * FF_TILE])
        act_ref[:, j * FF_TILE:(j + 1) * FF_TILE] = (_silu(g) * u).astype(BF16)
    o_ref[...] = x + 0.5 * _dot(act_ref[...], wo_ref[...])


def _ffn(x, nw, wi, wo, tm):
    m = x.shape[0]
    return pl.pallas_call(
        _ffn_kernel,
        grid=(m // tm,),
        in_specs=[pl.BlockSpec((tm, D_MODEL), lambda i: (i, 0)),
                  _const_spec((1, D_MODEL)),
                  _const_spec((D_MODEL, 2 * D_FF)),
                  _const_spec((D_FF, D_MODEL))],
        out_specs=pl.BlockSpec((tm, D_MODEL), lambda i: (i, 0)),
        out_shape=jax.ShapeDtypeStruct((m, D_MODEL), F32),
        scratch_shapes=[pltpu.VMEM((tm, D_FF), BF16)],
        compiler_params=_params("parallel"), name="half_ffn",
    )(x, nw, wi, wo)


def _rope128(x, cos, sin):
    lane = lax.broadcasted_iota(jnp.int32, x.shape, 1)
    first = (lane % HEAD_DIM) < (HEAD_DIM // 2)
    swapped = jnp.where(first, pltpu.roll(x, LANES - HEAD_DIM // 2, 1), pltpu.roll(x, HEAD_DIM // 2, 1))
    return x * cos + swapped * sin


def _mix_kernel(seq_tiles, x_ref, nw_ref, w_ref, cos_ref, sin_ref,
                qraw_ref, qrot_ref, rows_ref, win_ref, ksa_ref, vs_ref, win16_ref,
                small_ref, qkv_ref, zg_ref, merge_ref, vst_ref, vwt_ref):
    tm = x_ref.shape[0]
    hb = _rms(x_ref[...], nw_ref[...]).astype(BF16)
    cos = cos_ref[...]
    sin = sin_ref[...]
    scale = HEAD_DIM ** -0.5
    for j in range(NSA_HEADS):
        q = _dot(hb, w_ref[:, C_Q + j * LANES:C_Q + (j + 1) * LANES])
        qraw_ref[:, j * LANES:(j + 1) * LANES] = (q * scale).astype(BF16)
        qrot_ref[:, j * LANES:(j + 1) * LANES] = (_rope128(q, cos, sin) * scale).astype(BF16)
    kv = _dot(hb, w_ref[:, C_KV:C_KV + 6 * NSA_KV_W])
    k_sel = _rope128(kv[:, 2 * LANES:3 * LANES], cos, sin)
    v_sel = kv[:, 3 * LANES:4 * LANES]
    k_win = _rope128(kv[:, 4 * LANES:5 * LANES], cos, sin)
    v_win = kv[:, 5 * LANES:6 * LANES]
    rows_ref[:, 0:2 * LANES] = kv[:, 0:2 * LANES]
    rows_ref[:, 2 * LANES:3 * LANES] = k_sel
    rows_ref[:, 3 * LANES:4 * LANES] = v_sel
    win_ref[:, 0:LANES] = k_win
    win_ref[:, LANES:2 * LANES] = v_win
    pos = (pl.program_id(0) % seq_tiles) * tm + lax.broadcasted_iota(jnp.int32, (tm, LANES), 0)
    lane = lax.broadcasted_iota(jnp.int32, (tm, LANES), 1)
    ksa_ref[:, 0:LANES] = k_sel.astype(BF16)
    ksa_ref[:, LANES:2 * LANES] = jnp.where(pos // SEL_BLOCK == lane, 1.0, 0.0).astype(BF16)
    vs_ref[...] = v_sel.astype(BF16)
    win16_ref[:, 0:LANES] = k_win.astype(BF16)
    win16_ref[:, LANES:2 * LANES] = v_win.astype(BF16)
    ones = jnp.ones((VT_ROWS - HEAD_DIM, tm), F32)
    for v, vt_ref in ((v_sel, vst_ref), (v_win, vwt_ref)):
        vt = v.T
        for k in range(NSA_KV_HEADS):
            vt_ref[k] = jnp.concatenate([vt[k * HEAD_DIM:(k + 1) * HEAD_DIM], ones], axis=0).astype(BF16)
    small_ref[...] = _dot(hb, w_ref[:, C_SMALL:C_SMALL + LANES])
    qkv_ref[...] = _dot(hb, w_ref[:, C_QKV:C_QKV + DN_QKV_W])
    zg_ref[...] = _dot(hb, w_ref[:, C_ZG:C_ZG + DN_W])
    merge_ref[...] = _dot(hb, w_ref[:, C_MERGE:C_MERGE + 2 * D_MODEL])


def _mix(x, nw, w, cos, sin, tm, seq_len):
    m = x.shape[0]
    seq_tiles = max(seq_len // tm, 1)
    row = lambda width: pl.BlockSpec((tm, width), lambda i: (i, 0))
    tab = pl.BlockSpec((tm, LANES), lambda i: (i % seq_tiles, 0))
    widths = [(QPAD_W, BF16), (QPAD_W, BF16), (4 * NSA_KV_W, F32), (2 * NSA_KV_W, F32),
              (2 * LANES, BF16), (LANES, BF16), (2 * LANES, BF16), (LANES, F32),
              (DN_QKV_W, F32), (DN_W, F32), (2 * D_MODEL, F32)]
    vt_spec = pl.BlockSpec((NSA_KV_HEADS, VT_ROWS, tm), lambda i: (0, 0, i))
    vt_shape = jax.ShapeDtypeStruct((NSA_KV_HEADS, VT_ROWS, m), BF16)
    return pl.pallas_call(
        functools.partial(_mix_kernel, seq_tiles),
        grid=(m // tm,),
        in_specs=[row(D_MODEL), _const_spec((1, D_MODEL)), _const_spec((D_MODEL, MIX_W)), tab, tab],
        out_specs=[row(wd) for wd, _ in widths] + [vt_spec, vt_spec],
        out_shape=[jax.ShapeDtypeStruct((m, wd), dt) for wd, dt in widths] + [vt_shape, vt_shape],
        compiler_params=_params("parallel"), name="mix_in",
    )(x, nw, w, cos, sin)


def _merge_kernel(x_ref, oa_ref, ob_ref, mg_ref, wa_ref, wb_ref, wo_ref, o_ref):
    mg = mg_ref[...]
    ya = _dot(oa_ref[...], wa_ref[...])
    yb = _dot(ob_ref[...], wb_ref[...])
    y = _sigmoid(mg[:, :D_MODEL]) * ya + _sigmoid(mg[:, D_MODEL:]) * yb
    o_ref[...] = x_ref[...] + _dot(y.astype(BF16), wo_ref[...])


def _merge(x, oa, ob, mg, wa, wb, wo, tm):
    m = x.shape[0]
    row = lambda width: pl.BlockSpec((tm, width), lambda i: (i, 0))
    return pl.pallas_call(
        _merge_kernel,
        grid=(m // tm,),
        in_specs=[row(D_MODEL), row(NSA_Q_W), row(DN_W), row(2 * D_MODEL),
                  _const_spec((NSA_Q_W, D_MODEL)), _const_spec((DN_W, D_MODEL)),
                  _const_spec((D_MODEL, D_MODEL))],
        out_specs=row(D_MODEL),
        out_shape=jax.ShapeDtypeStruct((m, D_MODEL), F32),
        compiler_params=_params("parallel"), name="branch_merge",
    )(x, oa, ob, mg, wa, wb, wo)


def _norm_kernel(x_ref, w_ref, o_ref):
    o_ref[...] = _rms(x_ref[...], w_ref[...])


def _final_norm(x, w, tm):
    m = x.shape[0]
    row = pl.BlockSpec((tm, D_MODEL), lambda i: (i, 0))
    return pl.pallas_call(
        _norm_kernel, grid=(m // tm,), in_specs=[row, _const_spec((1, D_MODEL))], out_specs=row,
        out_shape=jax.ShapeDtypeStruct((m, D_MODEL), F32), compiler_params=_params("parallel"),
    )(x, w)


def _compress_kernel(rows_ref, pe_ref, w1_ref, w2_ref, kc_ref, vct_ref):
    ns = kc_ref.shape[1]
    res = []
    for part in range(2):
        sa = jnp.zeros((ns, LANES), F32)
        sb = jnp.zeros((ns, LANES), F32)
        for p in range(CMP_STRIDE):
            x = rows_ref[pl.ds(p * 4 + part, ns, stride=CMP_STRIDE * 4), :]
            sa = sa + _dot((x + pe_ref[part, p:p + 1, :]).astype(BF16), w1_ref[part, p])
            sb = sb + _dot((x + pe_ref[part, CMP_STRIDE + p:CMP_STRIDE + p + 1, :]).astype(BF16),
                           w1_ref[part, CMP_STRIDE + p])
        pre = sa + pltpu.roll(sb, ns - 1, 0)
        res.append(_dot(jax.nn.gelu(pre).astype(BF16), w2_ref[part]))
    kc_ref[0] = res[0].astype(BF16)
    vt = res[1].T
    ones = jnp.ones((VT_ROWS - HEAD_DIM, ns), F32)
    for k in range(NSA_KV_HEADS):
        vct_ref[0, k] = jnp.concatenate([vt[k * HEAD_DIM:(k + 1) * HEAD_DIM], ones], axis=0).astype(BF16)


def _compress_prompt(rows, pe, w1, w2, b, seq):
    ns = seq // CMP_STRIDE
    return pl.pallas_call(
        _compress_kernel,
        grid=(b,),
        in_specs=[pl.BlockSpec((seq * 4, LANES), lambda i: (i, 0)),
                  _const_spec(pe.shape), _const_spec(w1.shape), _const_spec(w2.shape)],
        out_specs=[pl.BlockSpec((1, ns, LANES), lambda i: (i, 0, 0)),
                   pl.BlockSpec((1, NSA_KV_HEADS, VT_ROWS, ns), lambda i: (i, 0, 0, 0))],
        out_shape=[jax.ShapeDtypeStruct((b, ns, LANES), BF16),
                   jax.ShapeDtypeStruct((b, NSA_KV_HEADS, VT_ROWS, ns), BF16)],
        compiler_params=_params("parallel"), name="nsa_compress",
    )(rows.reshape(b * seq * 4, LANES), pe, w1, w2)


TQ = 128
KT = 512
SEL_SPLIT = 1
WIN_KEYS = WINDOW + TQ


def _softmax_rows(s, mask):
    s = jnp.where(mask, s, NEG_BIG)
    e = jnp.where(mask, jnp.exp(s - jnp.max(s, axis=-1, keepdims=True)), 0.0)
    return e * (1.0 / jnp.maximum(jnp.sum(e, axis=-1, keepdims=True), 1e-30))


def _topk_bias_t(imp_t, n_cand, n_sel):
    nr = imp_t.shape[0] // 8
    xs = [imp_t[8 * r:8 * r + 8, :] for r in range(nr)]
    cnt = [jnp.zeros_like(xs[0]) for _ in range(nr)]
    sub = lax.broadcasted_iota(jnp.int32, xs[0].shape, 0)
    for j in range(n_cand):
        rj = j // 8
        row = xs[rj][j % 8:j % 8 + 1, :]
        for r in range(nr):
            if r < rj:
                inc = jnp.where(row > xs[r], 1.0, 0.0)
            elif r > rj:
                inc = jnp.where(row >= xs[r], 1.0, 0.0)
            else:
                inc = jnp.where(sub > j % 8, jnp.where(row >= xs[r], 1.0, 0.0),
                                jnp.where(row > xs[r], 1.0, 0.0))
            cnt[r] = cnt[r] + inc
    return jnp.concatenate([jnp.where(c < n_sel, 0.0, NEG_BIG) for c in cnt], axis=0)


def _nsa_prompt_kernel(qraw_ref, qrot_ref, small_ref, kc_ref, vct_ref, ksa_ref, vst_ref, win_ref, vwt_ref,
                       ovt_ref, cbias_ref, dbias_ref, wbias_ref, o_ref):
    seq = ksa_ref.shape[1]
    ns = seq // SEL_BLOCK
    s0 = pl.multiple_of(pl.program_id(1) * TQ, TQ)
    cols = NSA_GROUP * TQ
    tile = lambda a: jnp.concatenate([a] * NSA_GROUP, axis=1)
    qpos_row = s0 + lax.broadcasted_iota(jnp.int32, (1, cols), 1) % TQ
    gates_t = _sigmoid(small_ref[...]).T

    blk_t = lax.broadcasted_iota(jnp.int32, (ns, TQ), 0)
    qpos_t = s0 + lax.broadcasted_iota(jnp.int32, (ns, TQ), 1)
    cur_t = qpos_t // SEL_BLOCK
    valid_t = blk_t * SEL_BLOCK <= qpos_t
    forced_t = (blk_t == 0) | (blk_t == cur_t) | (blk_t == cur_t - 1)
    cbias = tile(cbias_ref[0])
    dbias = tile(dbias_ref[0])
    wbias = tile(wbias_ref[0])

    def split_sum(aug):
        return aug[0:HEAD_DIM], aug[HEAD_DIM:HEAD_DIM + 1]

    kv_heads = range(NSA_KV_HEADS)
    w0 = pl.multiple_of(jnp.maximum(s0 - WINDOW, 0), TQ)
    o_c, o_w, q_sel = [], [], []
    for k in kv_heads:
        def q_t(ref):
            return jnp.concatenate(
                [ref[:, (k * NSA_GROUP + g) * LANES:(k * NSA_GROUP + g + 1) * LANES].astype(F32).T.astype(BF16)
                 for g in range(NSA_GROUP)], axis=1)
        qt_raw = q_t(qraw_ref)
        qt_rot = q_t(qrot_ref)

        s = _dot(kc_ref[0], qt_raw) + cbias
        e = jnp.exp(s - jnp.max(s, axis=0, keepdims=True))
        acc, l = split_sum(_dot(vct_ref[0, k], e.astype(BF16)))
        inv = jnp.where(qpos_row >= CMP_BLOCK - 1, 1.0 / l, 0.0)
        o_c.append(acc * inv)
        p = e * inv
        p_sum = p[:, 0:TQ] + p[:, TQ:2 * TQ] + p[:, 2 * TQ:3 * TQ] + p[:, 3 * TQ:4 * TQ]
        imp_t = _dot_exact_lhs(ovt_ref[...], p_sum)[0:ns]
        imp_t = jnp.where(valid_t, imp_t + jnp.where(forced_t, FORCE_BONUS, 0.0), -jnp.inf)
        bias_t = _topk_bias_t(imp_t, ns, min(N_SEL, ns))
        if ns < LANES:
            bias_t = jnp.concatenate([bias_t, jnp.zeros((LANES - ns, TQ), F32)], axis=0)
        q_sel.append(jnp.concatenate([qt_rot, tile(bias_t).astype(BF16)], axis=0))

        s = _dot(win_ref[0, pl.ds(w0, WIN_KEYS), 0:LANES], qt_rot) + wbias
        e = jnp.exp(s - jnp.max(s, axis=0, keepdims=True)).astype(BF16)
        acc, l = split_sum(_dot(vwt_ref[k, :, pl.ds(w0, WIN_KEYS)], e))
        o_w.append(acc * (1.0 / l))

    def sel_step(c, carry, diagonal):
        k0 = pl.multiple_of(c * KT, KT)
        keys = ksa_ref[0, pl.ds(k0, KT), :]
        out = []
        for k in kv_heads:
            vals = vst_ref[k, :, pl.ds(k0, KT)]
            for half in range(SEL_SPLIT):
                cs = slice(half * (cols // SEL_SPLIT), (half + 1) * (cols // SEL_SPLIT))
                m, acc = carry[k * SEL_SPLIT + half]
                s = _dot(keys, q_sel[k][:, cs])
                if diagonal:
                    s = s + dbias[:, cs]
                m_new = jnp.maximum(m, jnp.max(s, axis=0, keepdims=True))
                p = jnp.exp(s - m_new).astype(BF16)
                out.append((m_new, jnp.exp(m - m_new) * acc + _dot(vals, p)))
        return tuple(out)
    init = tuple((jnp.full((1, cols // SEL_SPLIT), NEG_BIG, F32), jnp.zeros((VT_ROWS, cols // SEL_SPLIT), F32))
                 for _ in range(NSA_KV_HEADS * SEL_SPLIT))
    n_full = s0 // KT
    carry = lax.fori_loop(0, n_full, functools.partial(sel_step, diagonal=False), init)
    carry = sel_step(n_full, carry, True)
    res = []
    for k in kv_heads:
        aug = jnp.concatenate([carry[k * SEL_SPLIT + half][1] for half in range(SEL_SPLIT)], axis=1)
        acc, l = split_sum(aug)
        res.append((o_c[k], acc * (1.0 / l), o_w[k]))

    for g in range(NSA_GROUP):
        halves = []
        for k in range(NSA_KV_HEADS):
            r = SM_GATE + (k * NSA_GROUP + g) * 3
            halves.append(sum(gates_t[r + br:r + br + 1, :] * res[k][br][:, g * TQ:(g + 1) * TQ]
                              for br in range(3)))
        o_ref[:, g * LANES:(g + 1) * LANES] = jnp.concatenate(halves, axis=0).T.astype(BF16)


def _attn_bias_tables(seq):
    r = np.arange(TQ)[None, None, :]
    blocks = np.arange(seq // TQ)[:, None, None]
    c = np.arange(seq // CMP_STRIDE)[None, :, None]
    cb = c * CMP_STRIDE + CMP_BLOCK - 1 <= blocks * TQ + r
    kk = np.arange(KT)[None, :, None]
    off = np.arange(KT // TQ)[:, None, None] * TQ
    db = kk <= off + r
    kw = np.arange(WIN_KEYS)[None, :, None]
    nw = WINDOW // TQ
    j = np.arange(nw + 1)[:, None, None]
    qpos = j * TQ + r
    kpos = np.where(j < nw, kw, qpos - r - WINDOW + kw)
    wb = (kpos <= qpos) & (kpos > qpos - WINDOW)
    f = lambda m: jnp.asarray(np.where(m, 0.0, NEG_BIG).astype(np.float32))
    return f(cb), f(db), f(wb)


def _nsa_prompt(qraw, qrot, small, kc, vct, ksa, vst, win16, vwt, ovt, tables, b, seq):
    nc = kc.shape[1]
    nq = seq // TQ
    cb, db, wb = tables
    tok = lambda width: pl.BlockSpec((TQ, width), lambda i, j: (i * nq + j, 0))
    per_b = lambda width: pl.BlockSpec((1, seq, width), lambda i, j: (i, 0, 0))
    vt_spec = pl.BlockSpec((NSA_KV_HEADS, VT_ROWS, seq), lambda i, j: (0, 0, i))
    return pl.pallas_call(
        _nsa_prompt_kernel,
        grid=(b, nq),
        in_specs=[tok(QPAD_W), tok(QPAD_W), tok(LANES),
                  pl.BlockSpec((1, nc, LANES), lambda i, j: (i, 0, 0)),
                  pl.BlockSpec((1, NSA_KV_HEADS, VT_ROWS, nc), lambda i, j: (i, 0, 0, 0)),
                  per_b(2 * LANES), vt_spec, per_b(2 * LANES), vt_spec,
                  _const_spec(ovt.shape),
                  pl.BlockSpec((1, nc, TQ), lambda i, j: (j, 0, 0)),
                  pl.BlockSpec((1, KT, TQ), lambda i, j: (j % (KT // TQ), 0, 0)),
                  pl.BlockSpec((1, WIN_KEYS, TQ), lambda i, j: (jnp.minimum(j, WINDOW // TQ), 0, 0))],
        out_specs=tok(NSA_Q_W),
        out_shape=jax.ShapeDtypeStruct((b * seq, NSA_Q_W), BF16),
        compiler_params=_params("parallel", "parallel"), name="nsa_prompt",
    )(qraw, qrot, small, kc, vct, ksa.reshape(b, seq, 2 * LANES), vst, win16.reshape(b, seq, 2 * LANES), vwt,
      ovt, cb, db, wb)


PAGES_PER_STEP = 16
SEG_PER_PAGE = PAGE_SIZE // CMP_STRIDE
SEG_W = CMP_STRIDE * 2 * NSA_KV_W


def _nsa_sample_kernel(ts, past, pt_ref, *refs):
    pages = refs[:PAGES_PER_STEP]
    (qraw_ref, qrot_ref, small_ref, rows_ref, winnew_ref, winst_ref, pea_ref, peb_ref, wa_ref, wb_ref, w2_ref,
     ovt_ref, rexp_ref, o_ref, sega_ref, segb_ref, ksat_ref, vst_ref, sm_ref, stage_ref) = refs[PAGES_PER_STEP:]
    del pt_ref
    pg = pl.program_id(1)
    nseg = sega_ref.shape[0]
    nrows = NSA_HEADS * ts
    kt = NSA_KV_HEADS * ts

    for e in range(PAGES_PER_STEP):
        for part in range(2):
            stage_ref[e, part] = pages[e][0, 0, part * LANES:(part + 1) * LANES, :].T
        k0 = pl.multiple_of((pg * PAGES_PER_STEP + e) * PAGE_SIZE, PAGE_SIZE)
        ksat_ref[0:LANES, pl.ds(k0, PAGE_SIZE)] = pages[e][0, 0, 2 * LANES:3 * LANES, :].astype(BF16)
        vst_ref[:, pl.ds(k0, PAGE_SIZE)] = pages[e][0, 0, 3 * LANES:4 * LANES, :].astype(BF16)

        @pl.when(pl.program_id(0) == 0)
        def _():
            kpos = k0 + lax.broadcasted_iota(jnp.int32, (LANES, PAGE_SIZE), 1)
            blk = lax.broadcasted_iota(jnp.int32, (LANES, PAGE_SIZE), 0)
            ksat_ref[LANES:2 * LANES, pl.ds(k0, PAGE_SIZE)] = jnp.where(
                kpos // SEL_BLOCK == blk, 1.0, 0.0).astype(BF16)
    for pair in range(PAGES_PER_STEP // 2):
        r0 = pl.multiple_of(pg * (PAGES_PER_STEP * SEG_PER_PAGE) + pair * 2 * SEG_PER_PAGE, 2 * SEG_PER_PAGE)
        for p in range(CMP_STRIDE):
            for part in range(2):
                x = jnp.concatenate(
                    [stage_ref[2 * pair + e, part, pl.ds(p, SEG_PER_PAGE, stride=CMP_STRIDE), :] for e in range(2)],
                    axis=0)
                cs = slice((2 * p + part) * LANES, (2 * p + part + 1) * LANES)
                sega_ref[pl.ds(r0, 2 * SEG_PER_PAGE), cs] = (x + pea_ref[:, cs]).astype(BF16)
                segb_ref[pl.ds(r0, 2 * SEG_PER_PAGE), cs] = (x + peb_ref[:, cs]).astype(BF16)

    @pl.when(pg == pl.num_programs(1) - 1)
    def _():
        ns_past = past // SEL_BLOCK
        nc = nseg - 1
        q_raw = qraw_ref[0]
        q_rot = qrot_ref[0]
        q_rot_f = q_rot.astype(F32)
        row = lax.broadcasted_iota(jnp.int32, (nrows, 1), 0)
        t_row = row % ts
        pre = _dot(sega_ref[...], wa_ref[...]) + pltpu.roll(_dot(segb_ref[...], wb_ref[...]), nseg - 1, 0)
        kcvc = _dot(jax.nn.gelu(pre).astype(BF16), w2_ref[...]).astype(BF16)
        s_c = _dot_nt(q_raw, kcvc[:, 0:LANES])
        p_c = _softmax_rows(s_c, lax.broadcasted_iota(jnp.int32, (nrows, nseg), 1) < nc)
        o_c = _dot(p_c.astype(BF16), kcvc[:, LANES:2 * LANES])
        p_sum = p_c[0:kt] + p_c[kt:2 * kt] + p_c[2 * kt:3 * kt] + p_c[3 * kt:4 * kt]
        p_sum = jnp.concatenate([p_sum, jnp.zeros((LANES - kt, nseg), F32)], axis=0)
        nb = ovt_ref.shape[0]
        imp_t = _dot_exact_lhs_nt(ovt_ref[...], p_sum)
        blk_t = lax.broadcasted_iota(jnp.int32, (nb, LANES), 0)
        qpos_t = past + lax.broadcasted_iota(jnp.int32, (nb, LANES), 1) % ts
        cur_t = qpos_t // SEL_BLOCK
        forced_t = (blk_t == 0) | (blk_t == cur_t) | (blk_t == cur_t - 1)
        imp_t = jnp.where(blk_t * SEL_BLOCK <= qpos_t, imp_t + jnp.where(forced_t, FORCE_BONUS, 0.0), -jnp.inf)
        bias_t = _topk_bias_t(imp_t, -(-(past + ts) // SEL_BLOCK), N_SEL)[0:ns_past]
        bias = bias_t.T[0:kt]
        q_sel = jnp.concatenate([q_rot, jnp.concatenate([bias] * NSA_GROUP, axis=0).astype(BF16)], axis=1)

        def attend(s_past, v_past16_t, k_new, v_new, mask_past):
            if mask_past is not None:
                s_past = jnp.where(mask_past, s_past, NEG_BIG)
            s_new = [jnp.where(t_row >= j, jnp.sum(q_rot_f * k_new[j:j + 1, :], axis=-1, keepdims=True), NEG_BIG)
                     for j in range(ts)]
            m = jnp.max(s_past, axis=-1, keepdims=True)
            for s in s_new:
                m = jnp.maximum(m, s)
            p_past = jnp.exp(s_past - m)
            if mask_past is not None:
                p_past = jnp.where(mask_past, p_past, 0.0)
            acc = _dot_nt(p_past.astype(BF16), v_past16_t)
            l = jnp.sum(p_past, axis=-1, keepdims=True)
            for j, s in enumerate(s_new):
                p = jnp.where(t_row >= j, jnp.exp(s - m), 0.0)
                acc = acc + p * v_new[j:j + 1, :]
                l = l + p
            return acc * (1.0 / l)

        new = rows_ref[0]
        o_s = attend(_dot(q_sel, ksat_ref[...]), vst_ref[...], new[:, 2 * NSA_KV_W:3 * NSA_KV_W],
                     new[:, 3 * NSA_KV_W:4 * NSA_KV_W], None)
        wbuf = winst_ref.shape[3]
        wnew = winnew_ref[0]
        i_w = lax.broadcasted_iota(jnp.int32, (nrows, wbuf), 1)
        dist = t_row + wbuf - i_w
        o_w = attend(_dot(q_rot, winst_ref[0, 0, 0:LANES, :].astype(BF16)),
                     winst_ref[0, 0, LANES:2 * LANES, :].astype(BF16),
                     wnew[:, 0:LANES], wnew[:, LANES:2 * LANES],
                     (dist < WINDOW) & (past - wbuf + i_w >= 0))

        sm_ref[...] = jnp.zeros_like(sm_ref)
        sm_ref[0:ts, :] = small_ref[0]
        gexp = _dot_exact_rhs(_sigmoid(sm_ref[...]), rexp_ref[...])[0:kt]
        low = lax.broadcasted_iota(jnp.int32, (kt, LANES), 1) < HEAD_DIM
        for g in range(NSA_GROUP):
            out = jnp.zeros((kt, LANES), F32)
            for br, o_br in enumerate((o_c, o_s, o_w)):
                og = o_br[g * kt:(g + 1) * kt]
                both = jnp.where(low, og, pltpu.roll(og, kt - ts, 0))
                out = out + gexp[:, br * NSA_Q_W + g * LANES:br * NSA_Q_W + (g + 1) * LANES] * both
            o_ref[0, :, g * LANES:(g + 1) * LANES] = out


def _nsa_sample(layer, cache, page_table, qraw_s, qrot_s, small_s, rows_s, winnew_s, win_state,
                pea, peb, wa, wb, w2, ovt, rexp, past):
    b, ts = small_s.shape[:2]
    n_pages = page_table.shape[1]
    assert n_pages % PAGES_PER_STEP == 0 and ts * NSA_KV_HEADS == 8 and past == n_pages * PAGE_SIZE
    nseg = past // CMP_STRIDE
    wbuf = win_state.shape[3]
    kw = 4 * NSA_KV_W

    def page_spec(e):
        return pl.BlockSpec((1, 1, 4 * PAGE_SIZE, LANES),
                            lambda i, g, pt: (layer, pt[i, g * PAGES_PER_STEP + e], 0, 0))
    per_b = lambda shape: pl.BlockSpec((1,) + shape, lambda i, g, pt: (i,) + (0,) * len(shape))
    const = lambda shape: pl.BlockSpec(shape, lambda i, g, pt: (0,) * len(shape))
    grid_spec = pltpu.PrefetchScalarGridSpec(
        num_scalar_prefetch=1,
        grid=(b, n_pages // PAGES_PER_STEP),
        in_specs=[page_spec(e) for e in range(PAGES_PER_STEP)] + [
            per_b((NSA_HEADS * ts, LANES)), per_b((NSA_HEADS * ts, LANES)), per_b((ts, LANES)),
            per_b((ts, kw)), per_b((ts, 2 * NSA_KV_W)),
            pl.BlockSpec((1, 1, 2 * NSA_KV_W, wbuf), lambda i, g, pt: (layer, i, 0, 0)),
            const(pea.shape), const(peb.shape), const(wa.shape), const(wb.shape), const(w2.shape),
            const(ovt.shape), const(rexp.shape)],
        out_specs=per_b((NSA_KV_HEADS * ts, NSA_Q_W)),
        scratch_shapes=[pltpu.VMEM((nseg, CMP_STRIDE * 2 * NSA_KV_W), BF16),
                        pltpu.VMEM((nseg, CMP_STRIDE * 2 * NSA_KV_W), BF16),
                        pltpu.VMEM((2 * LANES, past), BF16), pltpu.VMEM((LANES, past), BF16),
                        pltpu.VMEM((2 * NSA_KV_HEADS * ts, LANES), F32),
                        pltpu.VMEM((PAGES_PER_STEP, 2, PAGE_SIZE, LANES), F32)])
    return pl.pallas_call(
        functools.partial(_nsa_sample_kernel, ts, past),
        grid_spec=grid_spec,
        out_shape=jax.ShapeDtypeStruct((b, NSA_KV_HEADS * ts, NSA_Q_W), F32),
        compiler_params=_params("arbitrary", "arbitrary"), name="nsa_sample",
    )(page_table, *([cache] * PAGES_PER_STEP), qraw_s, qrot_s, small_s, rows_s, winnew_s, win_state,
      pea, peb, wa, wb, w2, ovt, rexp)


DN_C = 128
HALO = 8


def _softplus(x):
    return jnp.maximum(x, 0.0) + jnp.log(1.0 + jnp.exp(-jnp.abs(x)))


def _l2n(x):
    return x * lax.rsqrt(jnp.sum(x * x, axis=-1, keepdims=True) + EPS)


def _dn_apply(p, x):
    return _dot3(p, x)


def _dn_square(p):
    return _dot3(p, p)


def _dn_prep_kernel(n_valid, qkv_ref, halo_ref, prefix_ref, small_ref, cw_ref, alog_ref, dtb_ref,
                    tri_ref, trit_ref, u_ref, w_ref, qd_ref, qk_ref, kdt_ref, gl_ref, xcat_ref, sm_ref):
    n = pl.program_id(1)
    if n_valid == DN_C:
        xcat_ref[HALO:HALO + DN_C, :] = qkv_ref[...]
        small = small_ref[...]

        @pl.when(n == 0)
        def _():
            xcat_ref[0:HALO, :] = prefix_ref[0]

        @pl.when(n > 0)
        def _():
            xcat_ref[0:HALO, :] = halo_ref[...]
    else:
        xcat_ref[...] = jnp.zeros_like(xcat_ref)
        xcat_ref[0:HALO, :] = prefix_ref[0]
        xcat_ref[HALO:HALO + n_valid, :] = qkv_ref[0]
        sm_ref[...] = jnp.zeros_like(sm_ref)
        sm_ref[0:n_valid, :] = small_ref[0]
        small = sm_ref[...]

    conv = jnp.zeros((DN_C, DN_QKV_W), F32)
    for i in range(CONV_W):
        conv = conv + xcat_ref[HALO - (CONV_W - 1) + i:HALO - (CONV_W - 1) + i + DN_C, :] * cw_ref[i:i + 1, :]
    conv = _silu(conv)

    ii = lax.broadcasted_iota(jnp.int32, (DN_C, DN_C), 0)
    jj = lax.broadcasted_iota(jnp.int32, (DN_C, DN_C), 1)
    g_all = -jnp.exp(alog_ref[...]) * _softplus(small + dtb_ref[...])
    beta_all = _sigmoid(small)
    if n_valid != DN_C:
        g_all = jnp.where(ii < n_valid, g_all, 0.0)
        beta_all = jnp.where(ii < n_valid, beta_all, 0.0)
    gc_all = _dot_exact_lhs(tri_ref[...], g_all)
    gr_all = _dot_exact_rhs(g_all.T, trit_ref[...])
    heads = range(DN_HEADS)
    q = [_l2n(conv[:, h * DN_DK:(h + 1) * DN_DK]) * DN_DK ** -0.5 for h in heads]
    k = [_l2n(conv[:, DN_QK_W + h * DN_DK:DN_QK_W + (h + 1) * DN_DK]) for h in heads]
    v = [conv[:, 2 * DN_QK_W + h * DN_DV:2 * DN_QK_W + (h + 1) * DN_DV] for h in heads]
    gcol = [gc_all[:, SM_A + h:SM_A + h + 1] for h in heads]
    beta = [beta_all[:, SM_B + h:SM_B + h + 1] for h in heads]
    decay = [jnp.where(ii >= jj, jnp.exp(gcol[h] - gr_all[SM_A + h:SM_A + h + 1, :]), 0.0) for h in heads]
    kb = [k[h] * beta[h] for h in heads]
    k16 = [k[h].astype(BF16) for h in heads]
    eg = [jnp.exp(gcol[h]) for h in heads]
    p = [jnp.where(ii > jj, -(_dot_nt(kb[h].astype(BF16), k16[h]) * decay[h]), 0.0) for h in heads]
    x = [jnp.concatenate([v[h] * beta[h], kb[h] * eg[h]], axis=1) for h in heads]
    levels = int(math.log2(DN_C))
    for lvl in range(levels):
        x = [x[h] + _dn_apply(p[h], x[h]) for h in heads]
        if lvl + 1 < levels:
            p = [_dn_square(p[h]) for h in heads]
    for h in heads:
        u_ref[:, h * DN_DV:(h + 1) * DN_DV] = x[h][:, 0:DN_DV]
        w_ref[:, h * DN_DK:(h + 1) * DN_DK] = x[h][:, DN_DV:DN_DV + DN_DK].astype(BF16)
        qk = jnp.where(ii >= jj, _dot_nt(q[h].astype(BF16), k16[h]) * decay[h], 0.0)
        qk_ref[:, h * DN_C:(h + 1) * DN_C] = qk.astype(BF16)
        qd_ref[:, h * DN_DK:(h + 1) * DN_DK] = (q[h] * eg[h]).astype(BF16)
        glast = gcol[h][DN_C - 1:DN_C, :]
        kdt_ref[0, h * DN_DK:(h + 1) * DN_DK, :] = (k[h] * jnp.exp(glast - gcol[h])).T.astype(BF16)
        gl_ref[0, h:h + 1, :] = jnp.broadcast_to(jnp.exp(glast), (1, LANES))
    gl_ref[0, DN_HEADS:, :] = jnp.zeros((HALO - DN_HEADS, LANES), F32)


def _dn_prep(qkv, prefix8, small, cw, alog_row, dtb_row, tri, trit, b, nch, n_valid):
    rows = b * nch * DN_C
    if n_valid == DN_C:
        qkv_spec = pl.BlockSpec((DN_C, DN_QKV_W), lambda i, n: (i * nch + n, 0))
        halo_spec = pl.BlockSpec((HALO, DN_QKV_W),
                                 lambda i, n: (jnp.maximum((i * nch + n) * (DN_C // HALO) - 1, 0), 0))
        small_spec = pl.BlockSpec((DN_C, LANES), lambda i, n: (i * nch + n, 0))
        qkv_in, halo_in, small_in = qkv, qkv, small
    else:
        qkv_in = qkv.reshape(b, n_valid, DN_QKV_W)
        small_in = small.reshape(b, n_valid, LANES)
        halo_in = prefix8
        qkv_spec = pl.BlockSpec((1, n_valid, DN_QKV_W), lambda i, n: (i, 0, 0))
        halo_spec = pl.BlockSpec((1, HALO, DN_QKV_W), lambda i, n: (i, 0, 0))
        small_spec = pl.BlockSpec((1, n_valid, LANES), lambda i, n: (i, 0, 0))
    tok = lambda width: pl.BlockSpec((DN_C, width), lambda i, n: (i * nch + n, 0))
    return pl.pallas_call(
        functools.partial(_dn_prep_kernel, n_valid),
        grid=(b, nch),
        in_specs=[qkv_spec, halo_spec, pl.BlockSpec((1, HALO, DN_QKV_W), lambda i, n: (i, 0, 0)), small_spec,
                  _const_spec((CONV_W, DN_QKV_W)), _const_spec((1, LANES)), _const_spec((1, LANES)),
                  _const_spec((DN_C, DN_C)), _const_spec((DN_C, DN_C))],
        out_specs=[tok(DN_W), tok(DN_QK_W), tok(DN_QK_W), tok(DN_HEADS * DN_C),
                   pl.BlockSpec((1, DN_QK_W, DN_C), lambda i, n: (i * nch + n, 0, 0)),
                   pl.BlockSpec((1, HALO, LANES), lambda i, n: (i * nch + n, 0, 0))],
        out_shape=[jax.ShapeDtypeStruct((rows, DN_W), F32), jax.ShapeDtypeStruct((rows, DN_QK_W), BF16),
                   jax.ShapeDtypeStruct((rows, DN_QK_W), BF16), jax.ShapeDtypeStruct((rows, DN_HEADS * DN_C), BF16),
                   jax.ShapeDtypeStruct((b * nch, DN_QK_W, DN_C), BF16),
                   jax.ShapeDtypeStruct((b * nch, HALO, LANES), F32)],
        scratch_shapes=[pltpu.VMEM((HALO + DN_C, DN_QKV_W), F32), pltpu.VMEM((DN_C, LANES), F32)],
        compiler_params=_params("parallel", "parallel"), name="dn_prep",
    )(qkv_in, halo_in, prefix8, small_in, cw, alog_row, dtb_row, tri, trit)


def _dn_scan_kernel(n_valid, u_ref, w_ref, qd_ref, qk_ref, kdt_ref, gl_ref, zg_ref, s0_ref, nw_ref,
                    o_ref, s_ref):
    @pl.when(pl.program_id(1) == 0)
    def _():
        s_ref[...] = s0_ref[...]

    for b in range(u_ref.shape[0]):
        for h in range(DN_HEADS):
            cs = slice(h * DN_DK, (h + 1) * DN_DK)
            s = s_ref[b, h]
            s16 = s.astype(BF16)
            v_new = u_ref[b, :, cs] - _dot(w_ref[b, :, cs], s16)
            v16 = v_new.astype(BF16)
            o = _dot(qd_ref[b, :, cs], s16) + _dot(qk_ref[b, :, cs], v16)
            s_ref[b, h] = s * gl_ref[b, h:h + 1, :] + _dot(kdt_ref[b, cs, :], v16)
            on = o * lax.rsqrt(jnp.mean(o * o, axis=-1, keepdims=True) + EPS) * nw_ref[...]
            o_ref[b, :, cs] = (on[0:n_valid] * _silu(zg_ref[b, :, cs])).astype(BF16)


def _dn_scan(u, w, qd, qk, kdt, gl, zg, s0, nw, b, nch, bb, n_valid):
    tok = lambda width: pl.BlockSpec((bb, DN_C, width), lambda i, n: (i, n, 0))
    r3 = lambda a: a.reshape(b, nch * DN_C, a.shape[-1])
    st = pl.BlockSpec((bb, DN_HEADS, DN_DK, DN_DV), lambda i, n: (i, 0, 0, 0))
    return pl.pallas_call(
        functools.partial(_dn_scan_kernel, n_valid),
        grid=(b // bb, nch),
        in_specs=[tok(DN_W), tok(DN_QK_W), tok(DN_QK_W), tok(DN_HEADS * DN_C),
                  pl.BlockSpec((bb, DN_QK_W, DN_C), lambda i, n: (i, n, 0)),
                  pl.BlockSpec((bb, HALO, LANES), lambda i, n: (i, n, 0)),
                  pl.BlockSpec((bb, n_valid, DN_W), lambda i, n: (i, n, 0)), st, _const_spec((1, DN_DV))],
        out_specs=[pl.BlockSpec((bb, n_valid, DN_W), lambda i, n: (i, n, 0)), st],
        out_shape=[jax.ShapeDtypeStruct((b, nch * n_valid, DN_W), BF16),
                   jax.ShapeDtypeStruct((b, DN_HEADS, DN_DK, DN_DV), F32)],
        compiler_params=_params("parallel", "arbitrary"), name="dn_scan",
    )(r3(u), r3(w), r3(qd), r3(qk), kdt.reshape(b, nch * DN_QK_W, DN_C), gl.reshape(b, nch * HALO, LANES),
      zg.reshape(b, nch * n_valid, DN_W), s0, nw)


IN_WIDTHS = (NSA_Q_W, 6 * NSA_KV_W, 3 * NSA_HEADS, DN_QKV_W, DN_HEADS, DN_HEADS, DN_W, 2 * D_MODEL)


def _pack_mix_weight(w_in):
    offs = np.cumsum(IN_WIDTHS)[:-1].tolist()
    q, kv, gate, qkv, a, b, zg, merge = jnp.split(w_in, offs, axis=-1)
    d = w_in.shape[0]
    qh = q.reshape(d, NSA_KV_HEADS, NSA_GROUP, 1, HEAD_DIM)
    eye = jnp.eye(NSA_KV_HEADS, dtype=w_in.dtype).reshape(1, NSA_KV_HEADS, 1, NSA_KV_HEADS, 1)
    qpad = (qh * eye).reshape(d, QPAD_W)
    small = jnp.concatenate([gate, a, b, jnp.zeros((d, LANES - SM_B - DN_HEADS), w_in.dtype)], axis=1)
    return jnp.concatenate([qpad, kv, small, qkv, zg, merge], axis=1).astype(BF16)


def _rope_tables(pos):
    half = HEAD_DIM // 2
    freq = ROPE_THETA ** (-jnp.arange(half, dtype=F32) / half)
    ang = pos.astype(F32)[:, None] * freq[None, :]
    cos, sin = jnp.cos(ang), jnp.sin(ang)
    cos = jnp.concatenate([cos, cos] * (LANES // HEAD_DIM), axis=1)
    sin = jnp.concatenate([-sin, sin] * (LANES // HEAD_DIM), axis=1)
    return cos, sin


def _gate_expand_matrix():
    r = np.zeros((LANES, 3 * NSA_Q_W), np.float32)
    for h in range(NSA_HEADS):
        k, g = divmod(h, NSA_GROUP)
        for br in range(3):
            c0 = br * NSA_Q_W + g * LANES + k * HEAD_DIM
            r[SM_GATE + h * 3 + br, c0:c0 + HEAD_DIM] = 1.0
    return jnp.asarray(r, BF16)


def _overlap_t(nc_pad, nc, ns, rows):
    i = np.arange(nc_pad)[None, :]
    j = np.arange(rows)[:, None]
    ov = (i * CMP_STRIDE < (j + 1) * SEL_BLOCK) & (i * CMP_STRIDE + CMP_BLOCK > j * SEL_BLOCK)
    ov = ov & (i < nc) & (j < ns)
    return jnp.asarray(ov.astype(np.float32), BF16)


def _pack_compress_sample(pos_emb, w1, w2):
    comp = np.array([0, 0, 1, 1])
    eye4 = jnp.eye(4, dtype=w1.dtype)

    def seg_weight(w1_half):
        w = w1_half.reshape(2, CMP_STRIDE, HEAD_DIM, HEAD_DIM)[comp]
        w = jnp.einsum('cpde,cf->pcdfe', w, eye4)
        return w.reshape(SEG_W, 4 * HEAD_DIM).astype(BF16)

    def seg_pe(pe_half):
        return pe_half[comp].transpose(1, 0, 2).reshape(1, SEG_W)
    half = CMP_STRIDE * HEAD_DIM
    w2c = w2[comp]
    w2d = jnp.einsum('cde,cf->cdfe', w2c, eye4).reshape(4 * HEAD_DIM, 4 * HEAD_DIM).astype(BF16)
    return (seg_pe(pos_emb[:, :CMP_STRIDE]), seg_pe(pos_emb[:, CMP_STRIDE:]),
            seg_weight(w1[:, :half]), seg_weight(w1[:, half:]), w2d)


def _sample_rows(q_pad, b, ts):
    q = q_pad.reshape(b, ts, NSA_KV_HEADS, NSA_GROUP, LANES).transpose(0, 3, 2, 1, 4)
    return q.reshape(b, NSA_HEADS * ts, LANES)


def _dn_lane_row(vals):
    return jnp.zeros((1, LANES), F32).at[0, SM_A:SM_A + DN_HEADS].set(vals.astype(F32))


def _tri_mats():
    tri = np.tril(np.ones((DN_C, DN_C), np.float32))
    return jnp.asarray(tri, BF16), jnp.asarray(tri.T, BF16)


def _deltanet(qkv, small, zg, prefix, s0, dn_params, tri, trit, b, seq_tokens):
    conv_w, a_log, dt_bias, norm_w = dn_params
    if seq_tokens % DN_C == 0:
        nch, n_valid, bb = seq_tokens // DN_C, DN_C, min(b, 4)
    else:
        nch, n_valid, bb = 1, seq_tokens, 4
    prefix8 = jnp.concatenate([jnp.zeros((b, HALO - (CONV_W - 1), DN_QKV_W), F32), prefix], axis=1)
    u, w, qd, qk, kdt, gl = _dn_prep(qkv, prefix8, small, conv_w, _dn_lane_row(a_log), _dn_lane_row(dt_bias),
                                     tri, trit, b, nch, n_valid)
    ob, s_out = _dn_scan(u, w, qd, qk, kdt, gl, zg, s0, norm_w[None], b, nch, bb, n_valid)
    return ob.reshape(b * seq_tokens, DN_W), s_out


def _head_perm_rows(wa):
    d = wa.shape[1]
    return wa.reshape(NSA_KV_HEADS, NSA_GROUP, HEAD_DIM, d).transpose(1, 0, 2, 3).reshape(NSA_Q_W, d)


def _pack_compress(pos_emb, w1, w2):
    eye = jnp.eye(NSA_KV_HEADS, dtype=w1.dtype)
    pe = jnp.concatenate([pos_emb] * NSA_KV_HEADS, axis=2)
    w1p = w1.reshape(2, CMP_BLOCK, HEAD_DIM, HEAD_DIM)
    w1d = jnp.einsum('kpde,hg->kphdge', w1p, eye).reshape(2, CMP_BLOCK, LANES, LANES)
    w2d = jnp.einsum('kde,hg->khdge', w2, eye).reshape(2, LANES, LANES)
    return pe, w1d.astype(BF16), w2d.astype(BF16)


def kernel(x_prompt, x_sample, cache_nsa_kv, state_nsa_win, state_dn_S, state_dn_conv, page_table,
           ffn1_norm, ffn1_w_in, ffn1_w_out, mix_norm, w_in, nsa_cmp_pos, nsa_cmp_w1, nsa_cmp_w2,
           dn_conv_w, dn_A_log, dn_dt_bias, dn_out_norm, w_branch_a, w_branch_b, w_out,
           ffn2_norm, ffn2_w_in, ffn2_w_out, final_norm):
    b, seq = x_prompt.shape[:2]
    bs, ts = x_sample.shape[:2]
    depth = w_in.shape[0]
    n_phys = cache_nsa_kv.shape[1]
    past = page_table.shape[1] * PAGE_SIZE
    wbuf = state_nsa_win.shape[2]
    kvw = 4 * NSA_KV_W

    xp = x_prompt.reshape(b * seq, D_MODEL)
    xs = x_sample.reshape(bs * ts, D_MODEL)
    cos_p, sin_p = _rope_tables(jnp.arange(seq, dtype=jnp.int32))
    cos_s, sin_s = _rope_tables(jnp.tile(past + jnp.arange(ts, dtype=jnp.int32), bs))
    nc = (seq - CMP_BLOCK) // CMP_STRIDE + 1
    ovt_p = _overlap_t(seq // CMP_STRIDE, nc, seq // SEL_BLOCK, LANES)
    tables_p = _attn_bias_tables(seq)
    ns_s = -(-(past + ts) // SEL_BLOCK)
    ovt_s = _overlap_t(past // CMP_STRIDE, past // CMP_STRIDE - 1, ns_s, -(-ns_s // 16) * 16)
    rexp = _gate_expand_matrix()
    tri, trit = _tri_mats()
    cache = cache_nsa_kv.transpose(0, 1, 3, 4, 5, 2).reshape(depth, n_phys, kvw, PAGE_SIZE)
    win_state = state_nsa_win.transpose(0, 1, 3, 4, 5, 2).reshape(depth, bs, 2 * NSA_KV_W, wbuf)
    tm_p, tm_s = 512, bs * ts

    outs = [[] for _ in range(8)]
    for l in range(depth):
        f1 = (ffn1_norm[l][None], ffn1_w_in[l].astype(BF16), ffn1_w_out[l].astype(BF16))
        f2 = (ffn2_norm[l][None], ffn2_w_in[l].astype(BF16), ffn2_w_out[l].astype(BF16))
        w_mix = _pack_mix_weight(w_in[l])
        mrg = (_head_perm_rows(w_branch_a[l]).astype(BF16), w_branch_b[l].astype(BF16), w_out[l].astype(BF16))
        dn_params = (dn_conv_w[l], dn_A_log[l], dn_dt_bias[l], dn_out_norm[l])
        cmp_l = (nsa_cmp_pos[l], nsa_cmp_w1[l], nsa_cmp_w2[l])

        xp = _ffn(xp, *f1, tm_p)
        (qraw, qrot, rows, win, ksa, _, win16, small, qkv, zg, mg, vst, vwt) = _mix(
            xp, mix_norm[l][None], w_mix, cos_p, sin_p, 256, seq)
        kc, vct = _compress_prompt(rows, *_pack_compress(*cmp_l), b, seq)
        oa = _nsa_prompt(qraw, qrot, small, kc, vct, ksa, vst, win16, vwt, ovt_p, tables_p, b, seq)
        ob, s_fin = _deltanet(qkv, small, zg, jnp.zeros((b, CONV_W - 1, DN_QKV_W), F32),
                              jnp.zeros((b, DN_HEADS, DN_DK, DN_DV), F32), dn_params, tri, trit, b, seq)
        xp = _merge(xp, oa, ob, mg, *mrg, tm_p)
        xp = _ffn(xp, *f2, tm_p)
        outs[0].append(rows.reshape(b, seq, 4, NSA_KV_HEADS, HEAD_DIM))
        outs[1].append(win.reshape(b, seq, 2, NSA_KV_HEADS, HEAD_DIM)[:, -min(WINDOW, seq):])
        outs[2].append(s_fin)
        outs[3].append(qkv.reshape(b, seq, DN_QKV_W)[:, seq - (CONV_W - 1):])

        xs = _ffn(xs, *f1, tm_s)
        (qraw, qrot, rows, win, _, _, _, small, qkv, zg, mg, _, _) = _mix(
            xs, mix_norm[l][None], w_mix, cos_s, sin_s, tm_s, tm_s)
        oa = _nsa_sample(l, cache, page_table, _sample_rows(qraw, bs, ts), _sample_rows(qrot, bs, ts),
                         small.reshape(bs, ts, LANES), rows.reshape(bs, ts, kvw), win.reshape(bs, ts, 2 * NSA_KV_W),
                         win_state, *_pack_compress_sample(*cmp_l), ovt_s, rexp, past)
        oa = oa[:, :ts].reshape(bs * ts, NSA_Q_W).astype(BF16)
        ob, s_fin = _deltanet(qkv, small, zg, state_dn_conv[l], state_dn_S[l], dn_params, tri, trit, bs, ts)
        xs = _merge(xs, oa, ob, mg, *mrg, tm_s)
        xs = _ffn(xs, *f2, tm_s)
        outs[4].append(rows.reshape(bs, ts, 4, NSA_KV_HEADS, HEAD_DIM))
        win_all = jnp.concatenate([state_nsa_win[l], win.reshape(bs, ts, 2, NSA_KV_HEADS, HEAD_DIM)], axis=1)
        outs[5].append(win_all[:, ts:])
        outs[6].append(s_fin)
        conv_all = jnp.concatenate([state_dn_conv[l], qkv.reshape(bs, ts, DN_QKV_W)], axis=1)
        outs[7].append(conv_all[:, ts:])

    y_prompt = _final_norm(xp, final_norm[None], tm_p).reshape(b, seq, D_MODEL)
    y_sample = _final_norm(xs, final_norm[None], tm_s).reshape(bs, ts, D_MODEL)
    return (y_prompt, y_sample) + tuple(jnp.stack(o) for o in outs)
```

```python
import functools
import math

import jax
import jax.numpy as jnp
import numpy as np
from jax import lax
from jax.experimental import pallas as pl
from jax.experimental.pallas import tpu as pltpu

F32 = jnp.float32
BF16 = jnp.bfloat16

D_MODEL = 1024
DEPTH = 4
PAGE_SIZE = 128
NSA_HEADS = 8
NSA_KV_HEADS = 2
HEAD_DIM = 64
NSA_GROUP = NSA_HEADS // NSA_KV_HEADS
NSA_Q_W = NSA_HEADS * HEAD_DIM
NSA_KV_W = NSA_KV_HEADS * HEAD_DIM
CMP_BLOCK = 32
CMP_STRIDE = 16
SEL_BLOCK = 64
N_SEL = 16
WINDOW = 512
FORCE_BONUS = 1e6
ROPE_THETA = 10000.0
DN_HEADS = 4
DN_DK = 128
DN_DV = 128
DN_QK_W = DN_HEADS * DN_DK
DN_W = DN_HEADS * DN_DV
DN_QKV_W = 2 * DN_QK_W + DN_W
CONV_W = 4
D_FF = 2816
EPS = 1e-6

LANES = 128
VT_ROWS = HEAD_DIM + 16
VMEM_LIMIT = 56 * 1024 * 1024
NEG_BIG = -1e30

QPAD_W = NSA_HEADS * LANES
C_Q = 0
C_KV = C_Q + QPAD_W
C_SMALL = C_KV + 6 * NSA_KV_W
C_QKV = C_SMALL + LANES
C_ZG = C_QKV + DN_QKV_W
C_MERGE = C_ZG + DN_W
MIX_W = C_MERGE + 2 * D_MODEL
SM_GATE = 0
SM_A = 3 * NSA_HEADS
SM_B = SM_A + DN_HEADS


def _dot(a, b):
    return jnp.dot(a, b, preferred_element_type=F32)


def _dot_nt(a, b):
    return lax.dot_general(a, b, (((1,), (1,)), ((), ())), preferred_element_type=F32)


def _split3(a):
    hi = a.astype(BF16)
    r1 = a - hi.astype(F32)
    mid = r1.astype(BF16)
    lo = (r1 - mid.astype(F32)).astype(BF16)
    return hi, mid, lo


def _dot_exact_rhs(a, b01):
    hi, mid, lo = _split3(a)
    return _dot(hi, b01) + _dot(mid, b01) + _dot(lo, b01)


def _dot_exact_lhs(a01, b):
    hi, mid, lo = _split3(b)
    return _dot(a01, hi) + _dot(a01, mid) + _dot(a01, lo)


def _dot_exact_lhs_nt(a01, b):
    hi, mid, lo = _split3(b)
    return _dot_nt(a01, hi) + _dot_nt(a01, mid) + _dot_nt(a01, lo)


def _dot3(a, b):
    ah = a.astype(BF16)
    al = (a - ah.astype(F32)).astype(BF16)
    bh = b.astype(BF16)
    bl = (b - bh.astype(F32)).astype(BF16)
    return _dot(ah, bh) + (_dot(ah, bl) + _dot(al, bh))


def _rms(x, w):
    return x * lax.rsqrt(jnp.mean(x * x, axis=-1, keepdims=True) + EPS) * w


def _sigmoid(x):
    return 1.0 / (1.0 + jnp.exp(-x))


def _silu(x):
    return x * _sigmoid(x)


def _params(*sem):
    return pltpu.CompilerParams(dimension_semantics=sem, vmem_limit_bytes=VMEM_LIMIT)


def _const_spec(shape):
    nd = len(shape)
    return pl.BlockSpec(shape, lambda *_: (0,) * nd)


FF_TILE = 256


def _ffn_kernel(x_ref, nw_ref, wi_ref, wo_ref, o_ref, act_ref):
    x = x_ref[...]
    hb = _rms(x, nw_ref[...]).astype(BF16)
    for j in range(D_FF // FF_TILE):
        g = _dot(hb, wi_ref[:, j * FF_TILE:(j + 1) * FF_TILE])
        u = _dot(hb, wi_ref[:, D_FF + j * FF_TILE:D_FF + (j + 1) * FF_TILE])
        act_ref[:, j * FF_TILE:(j + 1) * FF_TILE] = (_silu(g) * u).astype(BF16)
    o_ref[...] = x + 0.5 * _dot(act_ref[...], wo_ref[...])


def _ffn(x, nw, wi, wo, tm):
    m = x.shape[0]
    return pl.pallas_call(
        _ffn_kernel,
        grid=(m // tm,),
        in_specs=[pl.BlockSpec((tm, D_MODEL), lambda i: (i, 0)),
                  _const_spec((1, D_MODEL)),
                  _const_spec((D_MODEL, 2 * D_FF)),
                  _const_spec((D_FF, D_MODEL))],
        out_specs=pl.BlockSpec((tm, D_MODEL), lambda i: (i, 0)),
        out_shape=jax.ShapeDtypeStruct((m, D_MODEL), F32),
        scratch_shapes=[pltpu.VMEM((tm, D_FF), BF16)],
        compiler_params=_params("parallel"), name="half_ffn",
    )(x, nw, wi, wo)


def _rope128(x, cos, sin):
    lane = lax.broadcasted_iota(jnp.int32, x.shape, 1)
    first = (lane % HEAD_DIM) < (HEAD_DIM // 2)
    swapped = jnp.where(first, pltpu.roll(x, LANES - HEAD_DIM // 2, 1), pltpu.roll(x, HEAD_DIM // 2, 1))
    return x * cos + swapped * sin


def _mix_kernel(seq_tiles, prompt, x_ref, nw_ref, w_ref, cos_ref, sin_ref,
                qraw_ref, qrot_ref, small_ref, qkv_ref, zg_ref, merge_ref, *group_refs):
    tm = x_ref.shape[0]
    hb = _rms(x_ref[...], nw_ref[...]).astype(BF16)
    cos = cos_ref[...]
    sin = sin_ref[...]
    scale = HEAD_DIM ** -0.5
    for j in range(NSA_HEADS):
        q = _dot(hb, w_ref[:, C_Q + j * LANES:C_Q + (j + 1) * LANES])
        qraw_ref[:, j * LANES:(j + 1) * LANES] = (q * scale).astype(BF16)
        qrot_ref[:, j * LANES:(j + 1) * LANES] = (_rope128(q, cos, sin) * scale).astype(BF16)
    kv = _dot(hb, w_ref[:, C_KV:C_KV + 6 * NSA_KV_W])
    k_cmp = kv[:, 0:LANES]
    v_cmp = kv[:, LANES:2 * LANES]
    k_sel = _rope128(kv[:, 2 * LANES:3 * LANES], cos, sin)
    v_sel = kv[:, 3 * LANES:4 * LANES]
    k_win = _rope128(kv[:, 4 * LANES:5 * LANES], cos, sin)
    v_win = kv[:, 5 * LANES:6 * LANES]
    if prompt:
        rows_t_ref, win_t_ref, kcmp_ref, vcmp_ref, ksa_ref, kw16_ref, vst_ref, vwt_ref = group_refs
        ones = jnp.ones((VT_ROWS - HEAD_DIM, tm), F32)
        for part, a in enumerate((k_cmp, v_cmp, k_sel, v_sel)):
            rows_t_ref[0, part * LANES:(part + 1) * LANES, :] = a.T
        for part, a in enumerate((k_win, v_win)):
            win_t_ref[0, part * LANES:(part + 1) * LANES, :] = a.T
        for v, vt_ref in ((v_sel, vst_ref), (v_win, vwt_ref)):
            vt = v.T
            for k in range(NSA_KV_HEADS):
                vt_ref[k] = jnp.concatenate([vt[k * HEAD_DIM:(k + 1) * HEAD_DIM], ones], axis=0).astype(BF16)
        kcmp_ref[...] = k_cmp
        vcmp_ref[...] = v_cmp
        pos = (pl.program_id(0) % seq_tiles) * tm + lax.broadcasted_iota(jnp.int32, (tm, LANES), 0)
        lane = lax.broadcasted_iota(jnp.int32, (tm, LANES), 1)
        ksa_ref[:, 0:LANES] = k_sel.astype(BF16)
        ksa_ref[:, LANES:2 * LANES] = jnp.where(pos // SEL_BLOCK == lane, 1.0, 0.0).astype(BF16)
        kw16_ref[...] = k_win.astype(BF16)
    else:
        rows_ref, win_ref = group_refs
        for part, a in enumerate((k_cmp, v_cmp, k_sel, v_sel)):
            rows_ref[:, part * LANES:(part + 1) * LANES] = a
        win_ref[:, 0:LANES] = k_win
        win_ref[:, LANES:2 * LANES] = v_win
    small_ref[...] = _dot(hb, w_ref[:, C_SMALL:C_SMALL + LANES])
    qkv_ref[...] = _dot(hb, w_ref[:, C_QKV:C_QKV + DN_QKV_W])
    zg_ref[...] = _dot(hb, w_ref[:, C_ZG:C_ZG + DN_W])
    merge_ref[...] = _dot(hb, w_ref[:, C_MERGE:C_MERGE + 2 * D_MODEL])


def _mix(x, nw, w, cos, sin, tm, seq_len, prompt):
    m = x.shape[0]
    seq_tiles = max(seq_len // tm, 1)
    row = lambda width: pl.BlockSpec((tm, width), lambda i: (i, 0))
    tab = pl.BlockSpec((tm, LANES), lambda i: (i % seq_tiles, 0))
    widths = [(QPAD_W, BF16), (QPAD_W, BF16), (LANES, F32), (DN_QKV_W, F32), (DN_W, F32), (2 * D_MODEL, F32)]
    if prompt:
        nb = m // seq_len
        t_spec = lambda rows: pl.BlockSpec((1, rows, tm), lambda i: (i // seq_tiles, 0, i % seq_tiles))
        vt_spec = pl.BlockSpec((NSA_KV_HEADS, VT_ROWS, tm), lambda i: (0, 0, i))
        vt_shape = jax.ShapeDtypeStruct((NSA_KV_HEADS, VT_ROWS, m), BF16)
        g_widths = [(LANES, F32), (LANES, F32), (2 * LANES, BF16), (LANES, BF16)]
        g_specs = ([t_spec(4 * NSA_KV_W), t_spec(2 * NSA_KV_W)] + [row(wd) for wd, _ in g_widths]
                   + [vt_spec, vt_spec])
        g_shapes = ([jax.ShapeDtypeStruct((nb, 4 * NSA_KV_W, seq_len), F32),
                     jax.ShapeDtypeStruct((nb, 2 * NSA_KV_W, seq_len), F32)]
                    + [jax.ShapeDtypeStruct((m, wd), dt) for wd, dt in g_widths] + [vt_shape, vt_shape])
    else:
        g_widths = [(4 * NSA_KV_W, F32), (2 * NSA_KV_W, F32)]
        g_specs = [row(wd) for wd, _ in g_widths]
        g_shapes = [jax.ShapeDtypeStruct((m, wd), dt) for wd, dt in g_widths]
    return pl.pallas_call(
        functools.partial(_mix_kernel, seq_tiles, prompt),
        grid=(m // tm,),
        in_specs=[row(D_MODEL), _const_spec((1, D_MODEL)), _const_spec((D_MODEL, MIX_W)), tab, tab],
        out_specs=[row(wd) for wd, _ in widths] + g_specs,
        out_shape=[jax.ShapeDtypeStruct((m, wd), dt) for wd, dt in widths] + g_shapes,
        compiler_params=_params("parallel"), name="mix_in",
    )(x, nw, w, cos, sin)


def _merge_kernel(x_ref, oa_ref, ob_ref, mg_ref, wa_ref, wb_ref, wo_ref, o_ref):
    mg = mg_ref[...]
    ya = _dot(oa_ref[...], wa_ref[...])
    yb = _dot(ob_ref[...], wb_ref[...])
    y = _sigmoid(mg[:, :D_MODEL]) * ya + _sigmoid(mg[:, D_MODEL:]) * yb
    o_ref[...] = x_ref[...] + _dot(y.astype(BF16), wo_ref[...])


def _merge(x, oa, ob, mg, wa, wb, wo, tm):
    m = x.shape[0]
    row = lambda width: pl.BlockSpec((tm, width), lambda i: (i, 0))
    return pl.pallas_call(
        _merge_kernel,
        grid=(m // tm,),
        in_specs=[row(D_MODEL), row(NSA_Q_W), row(DN_W), row(2 * D_MODEL),
                  _const_spec((NSA_Q_W, D_MODEL)), _const_spec((DN_W, D_MODEL)),
                  _const_spec((D_MODEL, D_MODEL))],
        out_specs=row(D_MODEL),
        out_shape=jax.ShapeDtypeStruct((m, D_MODEL), F32),
        compiler_params=_params("parallel"), name="branch_merge",
    )(x, oa, ob, mg, wa, wb, wo)


def _norm_kernel(x_ref, w_ref, o_ref):
    o_ref[...] = _rms(x_ref[...], w_ref[...])


def _final_norm(x, w, tm):
    m = x.shape[0]
    row = pl.BlockSpec((tm, D_MODEL), lambda i: (i, 0))
    return pl.pallas_call(
        _norm_kernel, grid=(m // tm,), in_specs=[row, _const_spec((1, D_MODEL))], out_specs=row,
        out_shape=jax.ShapeDtypeStruct((m, D_MODEL), F32), compiler_params=_params("parallel"),
    )(x, w)


def _compress_kernel(kcmp_ref, vcmp_ref, pe_ref, w1_ref, w2_ref, kc_ref, vct_ref):
    ns = kc_ref.shape[1]
    res = []
    for part, rows_ref in enumerate((kcmp_ref, vcmp_ref)):
        sa = jnp.zeros((ns, LANES), F32)
        sb = jnp.zeros((ns, LANES), F32)
        for p in range(CMP_STRIDE):
            x = rows_ref[pl.ds(p, ns, stride=CMP_STRIDE), :]
            sa = sa + _dot((x + pe_ref[part, p:p + 1, :]).astype(BF16), w1_ref[part, p])
            sb = sb + _dot((x + pe_ref[part, CMP_STRIDE + p:CMP_STRIDE + p + 1, :]).astype(BF16),
                           w1_ref[part, CMP_STRIDE + p])
        pre = sa + pltpu.roll(sb, ns - 1, 0)
        res.append(_dot(jax.nn.gelu(pre).astype(BF16), w2_ref[part]))
    kc_ref[0] = res[0].astype(BF16)
    vt = res[1].T
    ones = jnp.ones((VT_ROWS - HEAD_DIM, ns), F32)
    for k in range(NSA_KV_HEADS):
        vct_ref[0, k] = jnp.concatenate([vt[k * HEAD_DIM:(k + 1) * HEAD_DIM], ones], axis=0).astype(BF16)


def _compress_prompt(kcmp, vcmp, pe, w1, w2, b, seq):
    ns = seq // CMP_STRIDE
    tok = pl.BlockSpec((seq, LANES), lambda i: (i, 0))
    return pl.pallas_call(
        _compress_kernel,
        grid=(b,),
        in_specs=[tok, tok, _const_spec(pe.shape), _const_spec(w1.shape), _const_spec(w2.shape)],
        out_specs=[pl.BlockSpec((1, ns, LANES), lambda i: (i, 0, 0)),
                   pl.BlockSpec((1, NSA_KV_HEADS, VT_ROWS, ns), lambda i: (i, 0, 0, 0))],
        out_shape=[jax.ShapeDtypeStruct((b, ns, LANES), BF16),
                   jax.ShapeDtypeStruct((b, NSA_KV_HEADS, VT_ROWS, ns), BF16)],
        compiler_params=_params("parallel"), name="nsa_compress",
    )(kcmp, vcmp, pe, w1, w2)


TQ = 128
KT = 512
SEL_SPLIT = 1
WIN_KEYS = WINDOW + TQ


def _softmax_rows(s, mask):
    s = jnp.where(mask, s, NEG_BIG)
    e = jnp.where(mask, jnp.exp(s - jnp.max(s, axis=-1, keepdims=True)), 0.0)
    return e * (1.0 / jnp.maximum(jnp.sum(e, axis=-1, keepdims=True), 1e-30))


def _topk_bias_t(imp_t, n_cand, n_sel):
    nr = imp_t.shape[0] // 8
    xs = [imp_t[8 * r:8 * r + 8, :] for r in range(nr)]
    cnt = [jnp.zeros_like(xs[0]) for _ in range(nr)]
    sub = lax.broadcasted_iota(jnp.int32, xs[0].shape, 0)
    for j in range(n_cand):
        rj = j // 8
        row = xs[rj][j % 8:j % 8 + 1, :]
        for r in range(nr):
            if r < rj:
                inc = jnp.where(row > xs[r], 1.0, 0.0)
            elif r > rj:
                inc = jnp.where(row >= xs[r], 1.0, 0.0)
            else:
                inc = jnp.where(sub > j % 8, jnp.where(row >= xs[r], 1.0, 0.0),
                                jnp.where(row > xs[r], 1.0, 0.0))
            cnt[r] = cnt[r] + inc
    return jnp.concatenate([jnp.where(c < n_sel, 0.0, NEG_BIG) for c in cnt], axis=0)


def _nsa_prompt_kernel(qraw_ref, qrot_ref, small_ref, kc_ref, vct_ref, ksa_ref, vst_ref, win_ref, vwt_ref,
                       ovt_ref, cbias_ref, dbias_ref, wbias_ref, o_ref):
    seq = ksa_ref.shape[1]
    ns = seq // SEL_BLOCK
    s0 = pl.multiple_of(pl.program_id(1) * TQ, TQ)
    cols = NSA_GROUP * TQ
    tile = lambda a: jnp.concatenate([a] * NSA_GROUP, axis=1)
    qpos_row = s0 + lax.broadcasted_iota(jnp.int32, (1, cols), 1) % TQ
    gates_t = _sigmoid(small_ref[...]).T

    blk_t = lax.broadcasted_iota(jnp.int32, (ns, TQ), 0)
    qpos_t = s0 + lax.broadcasted_iota(jnp.int32, (ns, TQ), 1)
    cur_t = qpos_t // SEL_BLOCK
    valid_t = blk_t * SEL_BLOCK <= qpos_t
    forced_t = (blk_t == 0) | (blk_t == cur_t) | (blk_t == cur_t - 1)
    cbias = tile(cbias_ref[0])
    dbias = tile(dbias_ref[0])
    wbias = tile(wbias_ref[0])

    def split_sum(aug):
        return aug[0:HEAD_DIM], aug[HEAD_DIM:HEAD_DIM + 1]

    kv_heads = range(NSA_KV_HEADS)
    w0 = pl.multiple_of(jnp.maximum(s0 - WINDOW, 0), TQ)
    o_c, o_w, q_sel = [], [], []
    for k in kv_heads:
        def q_t(ref):
            return jnp.concatenate(
                [ref[:, (k * NSA_GROUP + g) * LANES:(k * NSA_GROUP + g + 1) * LANES].astype(F32).T.astype(BF16)
                 for g in range(NSA_GROUP)], axis=1)
        qt_raw = q_t(qraw_ref)
        qt_rot = q_t(qrot_ref)

        s = _dot(kc_ref[0], qt_raw) + cbias
        e = jnp.exp(s - jnp.max(s, axis=0, keepdims=True))
        acc, l = split_sum(_dot(vct_ref[0, k], e.astype(BF16)))
        inv = jnp.where(qpos_row >= CMP_BLOCK - 1, 1.0 / l, 0.0)
        o_c.append(acc * inv)
        p = e * inv
        p_sum = p[:, 0:TQ] + p[:, TQ:2 * TQ] + p[:, 2 * TQ:3 * TQ] + p[:, 3 * TQ:4 * TQ]
        imp_t = _dot_exact_lhs(ovt_ref[...], p_sum)[0:ns]
        imp_t = jnp.where(valid_t, imp_t + jnp.where(forced_t, FORCE_BONUS, 0.0), -jnp.inf)
        bias_t = _topk_bias_t(imp_t, ns, min(N_SEL, ns))
        if ns < LANES:
            bias_t = jnp.concatenate([bias_t, jnp.zeros((LANES - ns, TQ), F32)], axis=0)
        q_sel.append(jnp.concatenate([qt_rot, tile(bias_t).astype(BF16)], axis=0))

        s = _dot(win_ref[0, pl.ds(w0, WIN_KEYS), :], qt_rot) + wbias
        e = jnp.exp(s - jnp.max(s, axis=0, keepdims=True)).astype(BF16)
        acc, l = split_sum(_dot(vwt_ref[k, :, pl.ds(w0, WIN_KEYS)], e))
        o_w.append(acc * (1.0 / l))

    def sel_step(c, carry, diagonal):
        k0 = pl.multiple_of(c * KT, KT)
        keys = ksa_ref[0, pl.ds(k0, KT), :]
        out = []
        for k in kv_heads:
            vals = vst_ref[k, :, pl.ds(k0, KT)]
            for half in range(SEL_SPLIT):
                cs = slice(half * (cols // SEL_SPLIT), (half + 1) * (cols // SEL_SPLIT))
                m, acc = carry[k * SEL_SPLIT + half]
                s = _dot(keys, q_sel[k][:, cs])
                if diagonal:
                    s = s + dbias[:, cs]
                m_new = jnp.maximum(m, jnp.max(s, axis=0, keepdims=True))
                p = jnp.exp(s - m_new).astype(BF16)
                out.append((m_new, jnp.exp(m - m_new) * acc + _dot(vals, p)))
        return tuple(out)
    init = tuple((jnp.full((1, cols // SEL_SPLIT), NEG_BIG, F32), jnp.zeros((VT_ROWS, cols // SEL_SPLIT), F32))
                 for _ in range(NSA_KV_HEADS * SEL_SPLIT))
    n_full = s0 // KT
    carry = lax.fori_loop(0, n_full, functools.partial(sel_step, diagonal=False), init)
    carry = sel_step(n_full, carry, True)
    res = []
    for k in kv_heads:
        aug = jnp.concatenate([carry[k * SEL_SPLIT + half][1] for half in range(SEL_SPLIT)], axis=1)
        acc, l = split_sum(aug)
        res.append((o_c[k], acc * (1.0 / l), o_w[k]))

    for g in range(NSA_GROUP):
        halves = []
        for k in range(NSA_KV_HEADS):
            r = SM_GATE + (k * NSA_GROUP + g) * 3
            halves.append(sum(gates_t[r + br:r + br + 1, :] * res[k][br][:, g * TQ:(g + 1) * TQ]
                              for br in range(3)))
        o_ref[:, g * LANES:(g + 1) * LANES] = jnp.concatenate(halves, axis=0).T.astype(BF16)


def _attn_bias_tables(seq):
    r = np.arange(TQ)[None, None, :]
    blocks = np.arange(seq // TQ)[:, None, None]
    c = np.arange(seq // CMP_STRIDE)[None, :, None]
    cb = c * CMP_STRIDE + CMP_BLOCK - 1 <= blocks * TQ + r
    kk = np.arange(KT)[None, :, None]
    off = np.arange(KT // TQ)[:, None, None] * TQ
    db = kk <= off + r
    kw = np.arange(WIN_KEYS)[None, :, None]
    nw = WINDOW // TQ
    j = np.arange(nw + 1)[:, None, None]
    qpos = j * TQ + r
    kpos = np.where(j < nw, kw, qpos - r - WINDOW + kw)
    wb = (kpos <= qpos) & (kpos > qpos - WINDOW)
    f = lambda m: jnp.asarray(np.where(m, 0.0, NEG_BIG).astype(np.float32))
    return f(cb), f(db), f(wb)


def _nsa_prompt(qraw, qrot, small, kc, vct, ksa, vst, kw16, vwt, ovt, tables, b, seq):
    nc = kc.shape[1]
    nq = seq // TQ
    cb, db, wb = tables
    tok = lambda width: pl.BlockSpec((TQ, width), lambda i, j: (i * nq + j, 0))
    per_b = lambda width: pl.BlockSpec((1, seq, width), lambda i, j: (i, 0, 0))
    vt_spec = pl.BlockSpec((NSA_KV_HEADS, VT_ROWS, seq), lambda i, j: (0, 0, i))
    return pl.pallas_call(
        _nsa_prompt_kernel,
        grid=(b, nq),
        in_specs=[tok(QPAD_W), tok(QPAD_W), tok(LANES),
                  pl.BlockSpec((1, nc, LANES), lambda i, j: (i, 0, 0)),
                  pl.BlockSpec((1, NSA_KV_HEADS, VT_ROWS, nc), lambda i, j: (i, 0, 0, 0)),
                  per_b(2 * LANES), vt_spec, per_b(LANES), vt_spec,
                  _const_spec(ovt.shape),
                  pl.BlockSpec((1, nc, TQ), lambda i, j: (j, 0, 0)),
                  pl.BlockSpec((1, KT, TQ), lambda i, j: (j % (KT // TQ), 0, 0)),
                  pl.BlockSpec((1, WIN_KEYS, TQ), lambda i, j: (jnp.minimum(j, WINDOW // TQ), 0, 0))],
        out_specs=tok(NSA_Q_W),
        out_shape=jax.ShapeDtypeStruct((b * seq, NSA_Q_W), BF16),
        compiler_params=_params("parallel", "parallel"), name="nsa_prompt",
    )(qraw, qrot, small, kc, vct, ksa.reshape(b, seq, 2 * LANES), vst, kw16.reshape(b, seq, LANES), vwt,
      ovt, cb, db, wb)


PAGES_PER_STEP = 16
SEG_PER_PAGE = PAGE_SIZE // CMP_STRIDE
SEG_W = CMP_STRIDE * 2 * NSA_KV_W


def _nsa_sample_kernel(ts, past, pt_ref, *refs):
    pages = refs[:PAGES_PER_STEP]
    (qraw_ref, qrot_ref, small_ref, rows_ref, winnew_ref, winst_ref, pe_ref, w1_ref, w2_ref,
     ovt_ref, rexp_ref, o_ref, seg_ref, ksat_ref, vst_ref, sm_ref, stage_ref, bias_ref) = refs[PAGES_PER_STEP:]
    del pt_ref
    pg = pl.program_id(1)
    nseg = seg_ref.shape[0]
    nrows = NSA_HEADS * ts
    kt = NSA_KV_HEADS * ts
    half_w = NSA_KV_HEADS * CMP_STRIDE * HEAD_DIM

    @pl.when((pl.program_id(0) == 0) & (pg == 0))
    def _():
        for part in range(2):
            hi, mid, lo = _split3(pe_ref[part])
            w = w1_ref[part]
            bias_ref[part] = _dot(hi, w) + _dot(mid, w) + _dot(lo, w)

    for e in range(PAGES_PER_STEP):
        for part in range(2):
            xt = pages[e][0, 0, part * LANES:(part + 1) * LANES, :].T
            for j in range(PAGE_SIZE // 8):
                start = (j % 2) * (PAGE_SIZE // 2) + j // 2
                stage_ref[e, part, pl.ds(start, 8, stride=SEG_PER_PAGE), :] = xt[8 * j:8 * j + 8, :]
        k0 = pl.multiple_of((pg * PAGES_PER_STEP + e) * PAGE_SIZE, PAGE_SIZE)
        ksat_ref[0:LANES, pl.ds(k0, PAGE_SIZE)] = pages[e][0, 0, 2 * LANES:3 * LANES, :].astype(BF16)
        vst_ref[:, pl.ds(k0, PAGE_SIZE)] = pages[e][0, 0, 3 * LANES:4 * LANES, :].astype(BF16)

        @pl.when(pl.program_id(0) == 0)
        def _():
            kpos = k0 + lax.broadcasted_iota(jnp.int32, (LANES, PAGE_SIZE), 1)
            blk = lax.broadcasted_iota(jnp.int32, (LANES, PAGE_SIZE), 0)
            ksat_ref[LANES:2 * LANES, pl.ds(k0, PAGE_SIZE)] = jnp.where(
                kpos // SEL_BLOCK == blk, 1.0, 0.0).astype(BF16)
    low = lax.broadcasted_iota(jnp.int32, (2 * SEG_PER_PAGE, LANES), 1) < HEAD_DIM
    for pair in range(PAGES_PER_STEP // 2):
        r0 = pl.multiple_of(pg * (PAGES_PER_STEP * SEG_PER_PAGE) + pair * 2 * SEG_PER_PAGE, 2 * SEG_PER_PAGE)
        for part in range(2):
            for pp in range(CMP_STRIDE // 2):
                xp, xq = [jnp.concatenate([stage_ref[2 * pair + e, part, p * SEG_PER_PAGE:(p + 1) * SEG_PER_PAGE, :]
                                           for e in range(2)], axis=0) for p in (2 * pp, 2 * pp + 1)]
                heads = (jnp.where(low, xp, pltpu.roll(xq, HEAD_DIM, 1)), jnp.where(low, pltpu.roll(xp, HEAD_DIM, 1), xq))
                for h in range(NSA_KV_HEADS):
                    c0 = part * half_w + h * (CMP_STRIDE * HEAD_DIM) + pp * LANES
                    seg_ref[pl.ds(r0, 2 * SEG_PER_PAGE), c0:c0 + LANES] = heads[h].astype(BF16)

    @pl.when(pg == pl.num_programs(1) - 1)
    def _():
        ns_past = past // SEL_BLOCK
        nc = nseg - 1
        q_raw = qraw_ref[0]
        q_rot = qrot_ref[0]
        q_rot_f = q_rot.astype(F32)
        row = lax.broadcasted_iota(jnp.int32, (nrows, 1), 0)
        t_row = row % ts
        kcvc = []
        for part in range(2):
            res = _dot(seg_ref[:, part * half_w:(part + 1) * half_w], w1_ref[part])
            b8 = bias_ref[part]
            pre = (res[:, 0:LANES] + b8[0:1, 0:LANES]
                   + pltpu.roll(res[:, LANES:2 * LANES] + b8[1:2, LANES:2 * LANES], nseg - 1, 0))
            kcvc.append(_dot(jax.nn.gelu(pre).astype(BF16), w2_ref[part]).astype(BF16))
        s_c = _dot_nt(q_raw, kcvc[0])
        p_c = _softmax_rows(s_c, lax.broadcasted_iota(jnp.int32, (nrows, nseg), 1) < nc)
        o_c = _dot(p_c.astype(BF16), kcvc[1])
        p_sum = p_c[0:kt] + p_c[kt:2 * kt] + p_c[2 * kt:3 * kt] + p_c[3 * kt:4 * kt]
        p_sum = jnp.concatenate([p_sum, jnp.zeros((LANES - kt, nseg), F32)], axis=0)
        nb = ovt_ref.shape[0]
        imp_t = _dot_exact_lhs_nt(ovt_ref[...], p_sum)
        blk_t = lax.broadcasted_iota(jnp.int32, (nb, LANES), 0)
        qpos_t = past + lax.broadcasted_iota(jnp.int32, (nb, LANES), 1) % ts
        cur_t = qpos_t // SEL_BLOCK
        forced_t = (blk_t == 0) | (blk_t == cur_t) | (blk_t == cur_t - 1)
        imp_t = jnp.where(blk_t * SEL_BLOCK <= qpos_t, imp_t + jnp.where(forced_t, FORCE_BONUS, 0.0), -jnp.inf)
        bias_t = _topk_bias_t(imp_t, -(-(past + ts) // SEL_BLOCK), N_SEL)[0:ns_past]
        bias = bias_t.T[0:kt]
        q_sel = jnp.concatenate([q_rot, jnp.concatenate([bias] * NSA_GROUP, axis=0).astype(BF16)], axis=1)

        def attend(s_past, v_past16_t, k_new, v_new, mask_past):
            if mask_past is not None:
                s_past = jnp.where(mask_past, s_past, NEG_BIG)
            s_new = [jnp.where(t_row >= j, jnp.sum(q_rot_f * k_new[j:j + 1, :], axis=-1, keepdims=True), NEG_BIG)
                     for j in range(ts)]
            m = jnp.max(s_past, axis=-1, keepdims=True)
            for s in s_new:
                m = jnp.maximum(m, s)
            p_past = jnp.exp(s_past - m)
            if mask_past is not None:
                p_past = jnp.where(mask_past, p_past, 0.0)
            acc = _dot_nt(p_past.astype(BF16), v_past16_t)
            l = jnp.sum(p_past, axis=-1, keepdims=True)
            for j, s in enumerate(s_new):
                p = jnp.where(t_row >= j, jnp.exp(s - m), 0.0)
                acc = acc + p * v_new[j:j + 1, :]
                l = l + p
            return acc * (1.0 / l)

        new = rows_ref[0]
        o_s = attend(_dot(q_sel, ksat_ref[...]), vst_ref[...], new[:, 2 * NSA_KV_W:3 * NSA_KV_W],
                     new[:, 3 * NSA_KV_W:4 * NSA_KV_W], None)
        wbuf = winst_ref.shape[3]
        wnew = winnew_ref[0]
        i_w = lax.broadcasted_iota(jnp.int32, (nrows, wbuf), 1)
        dist = t_row + wbuf - i_w
        o_w = attend(_dot(q_rot, winst_ref[0, 0, 0:LANES, :].astype(BF16)),
                     winst_ref[0, 0, LANES:2 * LANES, :].astype(BF16),
                     wnew[:, 0:LANES], wnew[:, LANES:2 * LANES],
                     (dist < WINDOW) & (past - wbuf + i_w >= 0))

        sm_ref[...] = jnp.zeros_like(sm_ref)
        sm_ref[0:ts, :] = small_ref[0]
        gexp = _dot_exact_rhs(_sigmoid(sm_ref[...]), rexp_ref[...])[0:kt]
        low = lax.broadcasted_iota(jnp.int32, (kt, LANES), 1) < HEAD_DIM
        for g in range(NSA_GROUP):
            out = jnp.zeros((kt, LANES), F32)
            for br, o_br in enumerate((o_c, o_s, o_w)):
                og = o_br[g * kt:(g + 1) * kt]
                both = jnp.where(low, og, pltpu.roll(og, kt - ts, 0))
                out = out + gexp[:, br * NSA_Q_W + g * LANES:br * NSA_Q_W + (g + 1) * LANES] * both
            o_ref[0, :, g * LANES:(g + 1) * LANES] = out


def _nsa_sample(layer, cache, page_table, qraw_s, qrot_s, small_s, rows_s, winnew_s, win_state,
                pe, w1, w2, ovt, rexp, past):
    b, ts = small_s.shape[:2]
    n_pages = page_table.shape[1]
    assert n_pages % PAGES_PER_STEP == 0 and ts * NSA_KV_HEADS == 8 and past == n_pages * PAGE_SIZE
    nseg = past // CMP_STRIDE
    wbuf = win_state.shape[3]
    kw = 4 * NSA_KV_W

    def page_spec(e):
        return pl.BlockSpec((1, 1, 4 * PAGE_SIZE, LANES),
                            lambda i, g, pt: (layer, pt[i, g * PAGES_PER_STEP + e], 0, 0))
    per_b = lambda shape: pl.BlockSpec((1,) + shape, lambda i, g, pt: (i,) + (0,) * len(shape))
    const = lambda shape: pl.BlockSpec(shape, lambda i, g, pt: (0,) * len(shape))
    grid_spec = pltpu.PrefetchScalarGridSpec(
        num_scalar_prefetch=1,
        grid=(b, n_pages // PAGES_PER_STEP),
        in_specs=[page_spec(e) for e in range(PAGES_PER_STEP)] + [
            per_b((NSA_HEADS * ts, LANES)), per_b((NSA_HEADS * ts, LANES)), per_b((ts, LANES)),
            per_b((ts, kw)), per_b((ts, 2 * NSA_KV_W)),
            pl.BlockSpec((1, 1, 2 * NSA_KV_W, wbuf), lambda i, g, pt: (layer, i, 0, 0)),
            const(pe.shape), const(w1.shape), const(w2.shape), const(ovt.shape), const(rexp.shape)],
        out_specs=per_b((NSA_KV_HEADS * ts, NSA_Q_W)),
        scratch_shapes=[pltpu.VMEM((nseg, SEG_W), BF16),
                        pltpu.VMEM((2 * LANES, past), BF16), pltpu.VMEM((LANES, past), BF16),
                        pltpu.VMEM((2 * NSA_KV_HEADS * ts, LANES), F32),
                        pltpu.VMEM((PAGES_PER_STEP, 2, PAGE_SIZE, LANES), F32),
                        pltpu.VMEM((2, 8, 2 * LANES), F32)])
    return pl.pallas_call(
        functools.partial(_nsa_sample_kernel, ts, past),
        grid_spec=grid_spec,
        out_shape=jax.ShapeDtypeStruct((b, NSA_KV_HEADS * ts, NSA_Q_W), F32),
        compiler_params=_params("arbitrary", "arbitrary"), name="nsa_sample",
    )(page_table, *([cache] * PAGES_PER_STEP), qraw_s, qrot_s, small_s, rows_s, winnew_s, win_state,
      pe, w1, w2, ovt, rexp)


DN_C = 128
HALO = 8


def _softplus(x):
    return jnp.maximum(x, 0.0) + jnp.log(1.0 + jnp.exp(-jnp.abs(x)))


def _l2n(x):
    return x * lax.rsqrt(jnp.sum(x * x, axis=-1, keepdims=True) + EPS)


def _dn_apply(p, x):
    return _dot3(p, x)


def _dn_square(p):
    return _dot3(p, p)


def _dn_prep_kernel(n_valid, qkv_ref, halo_ref, prefix_ref, small_ref, cw_ref, alog_ref, dtb_ref,
                    tri_ref, trit_ref, u_ref, w_ref, qd_ref, qk_ref, kdt_ref, gl_ref, xcat_ref, sm_ref):
    n = pl.program_id(1)
    if n_valid == DN_C:
        xcat_ref[HALO:HALO + DN_C, :] = qkv_ref[...]
        small = small_ref[...]

        @pl.when(n == 0)
        def _():
            xcat_ref[0:HALO, :] = prefix_ref[0]

        @pl.when(n > 0)
        def _():
            xcat_ref[0:HALO, :] = halo_ref[...]
    else:
        xcat_ref[...] = jnp.zeros_like(xcat_ref)
        xcat_ref[0:HALO, :] = prefix_ref[0]
        xcat_ref[HALO:HALO + n_valid, :] = qkv_ref[0]
        sm_ref[...] = jnp.zeros_like(sm_ref)
        sm_ref[0:n_valid, :] = small_ref[0]
        small = sm_ref[...]

    conv = jnp.zeros((DN_C, DN_QKV_W), F32)
    for i in range(CONV_W):
        conv = conv + xcat_ref[HALO - (CONV_W - 1) + i:HALO - (CONV_W - 1) + i + DN_C, :] * cw_ref[i:i + 1, :]
    conv = _silu(conv)

    ii = lax.broadcasted_iota(jnp.int32, (DN_C, DN_C), 0)
    jj = lax.broadcasted_iota(jnp.int32, (DN_C, DN_C), 1)
    g_all = -jnp.exp(alog_ref[...]) * _softplus(small + dtb_ref[...])
    beta_all = _sigmoid(small)
    if n_valid != DN_C:
        g_all = jnp.where(ii < n_valid, g_all, 0.0)
        beta_all = jnp.where(ii < n_valid, beta_all, 0.0)
    gc_all = _dot_exact_lhs(tri_ref[...], g_all)
    gr_all = _dot_exact_rhs(g_all.T, trit_ref[...])
    heads = range(DN_HEADS)
    q = [_l2n(conv[:, h * DN_DK:(h + 1) * DN_DK]) * DN_DK ** -0.5 for h in heads]
    k = [_l2n(conv[:, DN_QK_W + h * DN_DK:DN_QK_W + (h + 1) * DN_DK]) for h in heads]
    v = [conv[:, 2 * DN_QK_W + h * DN_DV:2 * DN_QK_W + (h + 1) * DN_DV] for h in heads]
    gcol = [gc_all[:, SM_A + h:SM_A + h + 1] for h in heads]
    beta = [beta_all[:, SM_B + h:SM_B + h + 1] for h in heads]
    decay = [jnp.where(ii >= jj, jnp.exp(gcol[h] - gr_all[SM_A + h:SM_A + h + 1, :]), 0.0) for h in heads]
    kb = [k[h] * beta[h] for h in heads]
    k16 = [k[h].astype(BF16) for h in heads]
    eg = [jnp.exp(gcol[h]) for h in heads]
    p = [jnp.where(ii > jj, -(_dot_nt(kb[h].astype(BF16), k16[h]) * decay[h]), 0.0) for h in heads]
    x = [jnp.concatenate([v[h] * beta[h], kb[h] * eg[h]], axis=1) for h in heads]
    levels = int(math.log2(DN_C))
    for lvl in range(levels):
        x = [x[h] + _dn_apply(p[h], x[h]) for h in heads]
        if lvl + 1 < levels:
            p = [_dn_square(p[h]) for h in heads]
    for h in heads:
        u_ref[:, h * DN_DV:(h + 1) * DN_DV] = x[h][:, 0:DN_DV]
        w_ref[:, h * DN_DK:(h + 1) * DN_DK] = x[h][:, DN_DV:DN_DV + DN_DK].astype(BF16)
        qk = jnp.where(ii >= jj, _dot_nt(q[h].astype(BF16), k16[h]) * decay[h], 0.0)
        qk_ref[:, h * DN_C:(h + 1) * DN_C] = qk.astype(BF16)
        qd_ref[:, h * DN_DK:(h + 1) * DN_DK] = (q[h] * eg[h]).astype(BF16)
        glast = gcol[h][DN_C - 1:DN_C, :]
        kdt_ref[0, h * DN_DK:(h + 1) * DN_DK, :] = (k[h] * jnp.exp(glast - gcol[h])).T.astype(BF16)
        gl_ref[0, h:h + 1, :] = jnp.broadcast_to(jnp.exp(glast), (1, LANES))
    gl_ref[0, DN_HEADS:, :] = jnp.zeros((HALO - DN_HEADS, LANES), F32)


def _dn_prep(qkv, prefix8, small, cw, alog_row, dtb_row, tri, trit, b, nch, n_valid):
    rows = b * nch * DN_C
    if n_valid == DN_C:
        qkv_spec = pl.BlockSpec((DN_C, DN_QKV_W), lambda i, n: (i * nch + n, 0))
        halo_spec = pl.BlockSpec((HALO, DN_QKV_W),
                                 lambda i, n: (jnp.maximum((i * nch + n) * (DN_C // HALO) - 1, 0), 0))
        small_spec = pl.BlockSpec((DN_C, LANES), lambda i, n: (i * nch + n, 0))
        qkv_in, halo_in, small_in = qkv, qkv, small
    else:
        qkv_in = qkv.reshape(b, n_valid, DN_QKV_W)
        small_in = small.reshape(b, n_valid, LANES)
        halo_in = prefix8
        qkv_spec = pl.BlockSpec((1, n_valid, DN_QKV_W), lambda i, n: (i, 0, 0))
        halo_spec = pl.BlockSpec((1, HALO, DN_QKV_W), lambda i, n: (i, 0, 0))
        small_spec = pl.BlockSpec((1, n_valid, LANES), lambda i, n: (i, 0, 0))
    tok = lambda width: pl.BlockSpec((DN_C, width), lambda i, n: (i * nch + n, 0))
    return pl.pallas_call(
        functools.partial(_dn_prep_kernel, n_valid),
        grid=(b, nch),
        in_specs=[qkv_spec, halo_spec, pl.BlockSpec((1, HALO, DN_QKV_W), lambda i, n: (i, 0, 0)), small_spec,
                  _const_spec((CONV_W, DN_QKV_W)), _const_spec((1, LANES)), _const_spec((1, LANES)),
                  _const_spec((DN_C, DN_C)), _const_spec((DN_C, DN_C))],
        out_specs=[tok(DN_W), tok(DN_QK_W), tok(DN_QK_W), tok(DN_HEADS * DN_C),
                   pl.BlockSpec((1, DN_QK_W, DN_C), lambda i, n: (i * nch + n, 0, 0)),
                   pl.BlockSpec((1, HALO, LANES), lambda i, n: (i * nch + n, 0, 0))],
        out_shape=[jax.ShapeDtypeStruct((rows, DN_W), F32), jax.ShapeDtypeStruct((rows, DN_QK_W), BF16),
                   jax.ShapeDtypeStruct((rows, DN_QK_W), BF16), jax.ShapeDtypeStruct((rows, DN_HEADS * DN_C), BF16),
                   jax.ShapeDtypeStruct((b * nch, DN_QK_W, DN_C), BF16),
                   jax.ShapeDtypeStruct((b * nch, HALO, LANES), F32)],
        scratch_shapes=[pltpu.VMEM((HALO + DN_C, DN_QKV_W), F32), pltpu.VMEM((DN_C, LANES), F32)],
        compiler_params=_params("parallel", "parallel"), name="dn_prep",
    )(qkv_in, halo_in, prefix8, small_in, cw, alog_row, dtb_row, tri, trit)


def _dn_scan_kernel(n_valid, u_ref, w_ref, qd_ref, qk_ref, kdt_ref, gl_ref, zg_ref, s0_ref, nw_ref,
                    o_ref, s_ref):
    @pl.when(pl.program_id(1) == 0)
    def _():
        s_ref[...] = s0_ref[...]

    for b in range(u_ref.shape[0]):
        for h in range(DN_HEADS):
            cs = slice(h * DN_DK, (h + 1) * DN_DK)
            s = s_ref[b, h]
            s16 = s.astype(BF16)
            v_new = u_ref[b, :, cs] - _dot(w_ref[b, :, cs], s16)
            v16 = v_new.astype(BF16)
            o = _dot(qd_ref[b, :, cs], s16) + _dot(qk_ref[b, :, cs], v16)
            s_ref[b, h] = s * gl_ref[b, h:h + 1, :] + _dot(kdt_ref[b, cs, :], v16)
            on = o * lax.rsqrt(jnp.mean(o * o, axis=-1, keepdims=True) + EPS) * nw_ref[...]
            o_ref[b, :, cs] = (on[0:n_valid] * _silu(zg_ref[b, :, cs])).astype(BF16)


def _dn_scan(u, w, qd, qk, kdt, gl, zg, s0, nw, b, nch, bb, n_valid):
    tok = lambda width: pl.BlockSpec((bb, DN_C, width), lambda i, n: (i, n, 0))
    r3 = lambda a: a.reshape(b, nch * DN_C, a.shape[-1])
    st = pl.BlockSpec((bb, DN_HEADS, DN_DK, DN_DV), lambda i, n: (i, 0, 0, 0))
    return pl.pallas_call(
        functools.partial(_dn_scan_kernel, n_valid),
        grid=(b // bb, nch),
        in_specs=[tok(DN_W), tok(DN_QK_W), tok(DN_QK_W), tok(DN_HEADS * DN_C),
                  pl.BlockSpec((bb, DN_QK_W, DN_C), lambda i, n: (i, n, 0)),
                  pl.BlockSpec((bb, HALO, LANES), lambda i, n: (i, n, 0)),
                  pl.BlockSpec((bb, n_valid, DN_W), lambda i, n: (i, n, 0)), st, _const_spec((1, DN_DV))],
        out_specs=[pl.BlockSpec((bb, n_valid, DN_W), lambda i, n: (i, n, 0)), st],
        out_shape=[jax.ShapeDtypeStruct((b, nch * n_valid, DN_W), BF16),
                   jax.ShapeDtypeStruct((b, DN_HEADS, DN_DK, DN_DV), F32)],
        compiler_params=_params("parallel", "arbitrary"), name="dn_scan",
    )(r3(u), r3(w), r3(qd), r3(qk), kdt.reshape(b, nch * DN_QK_W, DN_C), gl.reshape(b, nch * HALO, LANES),
      zg.reshape(b, nch * n_valid, DN_W), s0, nw)


IN_WIDTHS = (NSA_Q_W, 6 * NSA_KV_W, 3 * NSA_HEADS, DN_QKV_W, DN_HEADS, DN_HEADS, DN_W, 2 * D_MODEL)


def _pack_mix_weight(w_in):
    offs = np.cumsum(IN_WIDTHS)[:-1].tolist()
    q, kv, gate, qkv, a, b, zg, merge = jnp.split(w_in, offs, axis=-1)
    d = w_in.shape[0]
    qh = q.reshape(d, NSA_KV_HEADS, NSA_GROUP, 1, HEAD_DIM)
    eye = jnp.eye(NSA_KV_HEADS, dtype=w_in.dtype).reshape(1, NSA_KV_HEADS, 1, NSA_KV_HEADS, 1)
    qpad = (qh * eye).reshape(d, QPAD_W)
    small = jnp.concatenate([gate, a, b, jnp.zeros((d, LANES - SM_B - DN_HEADS), w_in.dtype)], axis=1)
    return jnp.concatenate([qpad, kv, small, qkv, zg, merge], axis=1).astype(BF16)


def _rope_tables(pos):
    half = HEAD_DIM // 2
    freq = ROPE_THETA ** (-jnp.arange(half, dtype=F32) / half)
    ang = pos.astype(F32)[:, None] * freq[None, :]
    cos, sin = jnp.cos(ang), jnp.sin(ang)
    cos = jnp.concatenate([cos, cos] * (LANES // HEAD_DIM), axis=1)
    sin = jnp.concatenate([-sin, sin] * (LANES // HEAD_DIM), axis=1)
    return cos, sin


def _gate_expand_matrix():
    r = np.zeros((LANES, 3 * NSA_Q_W), np.float32)
    for h in range(NSA_HEADS):
        k, g = divmod(h, NSA_GROUP)
        for br in range(3):
            c0 = br * NSA_Q_W + g * LANES + k * HEAD_DIM
            r[SM_GATE + h * 3 + br, c0:c0 + HEAD_DIM] = 1.0
    return jnp.asarray(r, BF16)


def _overlap_t(nc_pad, nc, ns, rows):
    i = np.arange(nc_pad)[None, :]
    j = np.arange(rows)[:, None]
    ov = (i * CMP_STRIDE < (j + 1) * SEL_BLOCK) & (i * CMP_STRIDE + CMP_BLOCK > j * SEL_BLOCK)
    ov = ov & (i < nc) & (j < ns)
    return jnp.asarray(ov.astype(np.float32), BF16)


def _pack_compress_sample(pos_emb, w1, w2):
    eye = jnp.eye(NSA_KV_HEADS, dtype=w1.dtype)
    w = w1.reshape(2, 2, CMP_STRIDE, HEAD_DIM, HEAD_DIM)
    w = jnp.einsum('kapde,hg->khpdage', w, eye)
    w = w.reshape(2, NSA_KV_HEADS * CMP_STRIDE * HEAD_DIM, 2 * LANES).astype(BF16)
    pe = pos_emb.reshape(2, 2, 1, CMP_STRIDE * HEAD_DIM)
    pe = jnp.concatenate([pe] * NSA_KV_HEADS, axis=3)
    pe = jnp.concatenate([pe[:, :, 0], jnp.zeros((2, 6, pe.shape[3]), pe.dtype)], axis=1)
    w2d = jnp.einsum('kde,hg->khdge', w2, eye).reshape(2, LANES, LANES).astype(BF16)
    return pe, w, w2d


def _sample_rows(q_pad, b, ts):
    q = q_pad.reshape(b, ts, NSA_KV_HEADS, NSA_GROUP, LANES).transpose(0, 3, 2, 1, 4)
    return q.reshape(b, NSA_HEADS * ts, LANES)


def _dn_lane_row(vals):
    return jnp.zeros((1, LANES), F32).at[0, SM_A:SM_A + DN_HEADS].set(vals.astype(F32))


def _tri_mats():
    tri = np.tril(np.ones((DN_C, DN_C), np.float32))
    return jnp.asarray(tri, BF16), jnp.asarray(tri.T, BF16)


def _deltanet(qkv, small, zg, prefix, s0, dn_params, tri, trit, b, seq_tokens):
    conv_w, a_log, dt_bias, norm_w = dn_params
    if seq_tokens % DN_C == 0:
        nch, n_valid, bb = seq_tokens // DN_C, DN_C, min(b, 4)
    else:
        nch, n_valid, bb = 1, seq_tokens, 4
    prefix8 = jnp.concatenate([jnp.zeros((b, HALO - (CONV_W - 1), DN_QKV_W), F32), prefix], axis=1)
    u, w, qd, qk, kdt, gl = _dn_prep(qkv, prefix8, small, conv_w, _dn_lane_row(a_log), _dn_lane_row(dt_bias),
                                     tri, trit, b, nch, n_valid)
    ob, s_out = _dn_scan(u, w, qd, qk, kdt, gl, zg, s0, norm_w[None], b, nch, bb, n_valid)
    return ob.reshape(b * seq_tokens, DN_W), s_out


def _head_perm_rows(wa):
    d = wa.shape[1]
    return wa.reshape(NSA_KV_HEADS, NSA_GROUP, HEAD_DIM, d).transpose(1, 0, 2, 3).reshape(NSA_Q_W, d)


def _pack_compress(pos_emb, w1, w2):
    eye = jnp.eye(NSA_KV_HEADS, dtype=w1.dtype)
    pe = jnp.concatenate([pos_emb] * NSA_KV_HEADS, axis=2)
    w1p = w1.reshape(2, CMP_BLOCK, HEAD_DIM, HEAD_DIM)
    w1d = jnp.einsum('kpde,hg->kphdge', w1p, eye).reshape(2, CMP_BLOCK, LANES, LANES)
    w2d = jnp.einsum('kde,hg->khdge', w2, eye).reshape(2, LANES, LANES)
    return pe, w1d.astype(BF16), w2d.astype(BF16)


def kernel(x_prompt, x_sample, cache_nsa_kv, state_nsa_win, state_dn_S, state_dn_conv, page_table,
           ffn1_norm, ffn1_w_in, ffn1_w_out, mix_norm, w_in, nsa_cmp_pos, nsa_cmp_w1, nsa_cmp_w2,
           dn_conv_w, dn_A_log, dn_dt_bias, dn_out_norm, w_branch_a, w_branch_b, w_out,
           ffn2_norm, ffn2_w_in, ffn2_w_out, final_norm):
    b, seq = x_prompt.shape[:2]
    bs, ts = x_sample.shape[:2]
    depth = w_in.shape[0]
    n_phys = cache_nsa_kv.shape[1]
    past = page_table.shape[1] * PAGE_SIZE
    wbuf = state_nsa_win.shape[2]
    kvw = 4 * NSA_KV_W

    xp = x_prompt.reshape(b * seq, D_MODEL)
    xs = x_sample.reshape(bs * ts, D_MODEL)
    cos_p, sin_p = _rope_tables(jnp.arange(seq, dtype=jnp.int32))
    cos_s, sin_s = _rope_tables(jnp.tile(past + jnp.arange(ts, dtype=jnp.int32), bs))
    nc = (seq - CMP_BLOCK) // CMP_STRIDE + 1
    ovt_p = _overlap_t(seq // CMP_STRIDE, nc, seq // SEL_BLOCK, LANES)
    tables_p = _attn_bias_tables(seq)
    ns_s = -(-(past + ts) // SEL_BLOCK)
    ovt_s = _overlap_t(past // CMP_STRIDE, past // CMP_STRIDE - 1, ns_s, -(-ns_s // 16) * 16)
    rexp = _gate_expand_matrix()
    tri, trit = _tri_mats()
    cache = cache_nsa_kv.transpose(0, 1, 3, 4, 5, 2).reshape(depth, n_phys, kvw, PAGE_SIZE)
    win_state = state_nsa_win.transpose(0, 1, 3, 4, 5, 2).reshape(depth, bs, 2 * NSA_KV_W, wbuf)
    tm_p, tm_s = 512, bs * ts

    outs = [[] for _ in range(8)]
    for l in range(depth):
        f1 = (ffn1_norm[l][None], ffn1_w_in[l].astype(BF16), ffn1_w_out[l].astype(BF16))
        f2 = (ffn2_norm[l][None], ffn2_w_in[l].astype(BF16), ffn2_w_out[l].astype(BF16))
        w_mix = _pack_mix_weight(w_in[l])
        mrg = (_head_perm_rows(w_branch_a[l]).astype(BF16), w_branch_b[l].astype(BF16), w_out[l].astype(BF16))
        dn_params = (dn_conv_w[l], dn_A_log[l], dn_dt_bias[l], dn_out_norm[l])
        cmp_l = (nsa_cmp_pos[l], nsa_cmp_w1[l], nsa_cmp_w2[l])

        xp = _ffn(xp, *f1, tm_p)
        (qraw, qrot, small, qkv, zg, mg, rows_t, win_t, kcmp, vcmp, ksa, kw16, vst, vwt) = _mix(
            xp, mix_norm[l][None], w_mix, cos_p, sin_p, 256, seq, True)
        kc, vct = _compress_prompt(kcmp, vcmp, *_pack_compress(*cmp_l), b, seq)
        oa = _nsa_prompt(qraw, qrot, small, kc, vct, ksa, vst, kw16, vwt, ovt_p, tables_p, b, seq)
        ob, s_fin = _deltanet(qkv, small, zg, jnp.zeros((b, CONV_W - 1, DN_QKV_W), F32),
                              jnp.zeros((b, DN_HEADS, DN_DK, DN_DV), F32), dn_params, tri, trit, b, seq)
        xp = _merge(xp, oa, ob, mg, *mrg, tm_p)
        xp = _ffn(xp, *f2, tm_p)
        outs[0].append(rows_t)
        outs[1].append(win_t[:, :, seq - min(WINDOW, seq):])
        outs[2].append(s_fin)
        outs[3].append(qkv.reshape(b, seq, DN_QKV_W)[:, seq - (CONV_W - 1):])

        xs = _ffn(xs, *f1, tm_s)
        (qraw, qrot, small, qkv, zg, mg, rows, win) = _mix(
            xs, mix_norm[l][None], w_mix, cos_s, sin_s, tm_s, tm_s, False)
        oa = _nsa_sample(l, cache, page_table, _sample_rows(qraw, bs, ts), _sample_rows(qrot, bs, ts),
                         small.reshape(bs, ts, LANES), rows.reshape(bs, ts, kvw), win.reshape(bs, ts, 2 * NSA_KV_W),
                         win_state, *_pack_compress_sample(*cmp_l), ovt_s, rexp, past)
        oa = oa[:, :ts].reshape(bs * ts, NSA_Q_W).astype(BF16)
        ob, s_fin = _deltanet(qkv, small, zg, state_dn_conv[l], state_dn_S[l], dn_params, tri, trit, bs, ts)
        xs = _merge(xs, oa, ob, mg, *mrg, tm_s)
        xs = _ffn(xs, *f2, tm_s)
        outs[4].append(rows.reshape(bs, ts, 4, NSA_KV_HEADS, HEAD_DIM))
        win_all = jnp.concatenate([state_nsa_win[l], win.reshape(bs, ts, 2, NSA_KV_HEADS, HEAD_DIM)], axis=1)
        outs[5].append(win_all[:, ts:])
        outs[6].append(s_fin)
        conv_all = jnp.concatenate([state_dn_conv[l], qkv.reshape(bs, ts, DN_QKV_W)], axis=1)
        outs[7].append(conv_all[:, ts:])

    y_prompt = _final_norm(xp, final_norm[None], tm_p).reshape(b, seq, D_MODEL)
    y_sample = _final_norm(xs, final_norm[None], tm_s).reshape(bs, ts, D_MODEL)
    res = [jnp.stack(o) for o in outs]
    for i, parts in ((0, 4), (1, 2)):
        r = res[i]
        res[i] = r.reshape(depth, b, parts, NSA_KV_HEADS, HEAD_DIM, r.shape[-1]).transpose(0, 1, 5, 2, 3, 4)
    return (y_prompt, y_sample) + tuple(res)
```

```python
import functools
import math

import jax
import jax.numpy as jnp
import numpy as np
from jax import lax
from jax.experimental import pallas as pl
from jax.experimental.pallas import tpu as pltpu

F32 = jnp.float32
BF16 = jnp.bfloat16

D_MODEL = 1024
DEPTH = 4
PAGE_SIZE = 128
NSA_HEADS = 8
NSA_KV_HEADS = 2
HEAD_DIM = 64
NSA_GROUP = NSA_HEADS // NSA_KV_HEADS
NSA_Q_W = NSA_HEADS * HEAD_DIM
NSA_KV_W = NSA_KV_HEADS * HEAD_DIM
CMP_BLOCK = 32
CMP_STRIDE = 16
SEL_BLOCK = 64
N_SEL = 16
WINDOW = 512
FORCE_BONUS = 1e6
ROPE_THETA = 10000.0
DN_HEADS = 4
DN_DK = 128
DN_DV = 128
DN_QK_W = DN_HEADS * DN_DK
DN_W = DN_HEADS * DN_DV
DN_QKV_W = 2 * DN_QK_W + DN_W
CONV_W = 4
D_FF = 2816
EPS = 1e-6

LANES = 128
VT_ROWS = HEAD_DIM + 16
VMEM_LIMIT = 56 * 1024 * 1024
NEG_BIG = -1e30

QPAD_W = NSA_HEADS * LANES
C_Q = 0
C_KV = C_Q + QPAD_W
C_SMALL = C_KV + 6 * NSA_KV_W
C_QKV = C_SMALL + LANES
C_ZG = C_QKV + DN_QKV_W
C_MERGE = C_ZG + DN_W
MIX_W = C_MERGE + 2 * D_MODEL
SM_GATE = 0
SM_A = 3 * NSA_HEADS
SM_B = SM_A + DN_HEADS


def _dot(a, b):
    return jnp.dot(a, b, preferred_element_type=F32)


def _dot_nt(a, b):
    return lax.dot_general(a, b, (((1,), (1,)), ((), ())), preferred_element_type=F32)


def _split3(a):
    hi = a.astype(BF16)
    r1 = a - hi.astype(F32)
    mid = r1.astype(BF16)
    lo = (r1 - mid.astype(F32)).astype(BF16)
    return hi, mid, lo


def _dot_exact_rhs(a, b01):
    hi, mid, lo = _split3(a)
    return _dot(hi, b01) + _dot(mid, b01) + _dot(lo, b01)


def _dot_exact_lhs(a01, b):
    hi, mid, lo = _split3(b)
    return _dot(a01, hi) + _dot(a01, mid) + _dot(a01, lo)


def _dot_exact_lhs_nt(a01, b):
    hi, mid, lo = _split3(b)
    return _dot_nt(a01, hi) + _dot_nt(a01, mid) + _dot_nt(a01, lo)


def _dot3(a, b):
    ah = a.astype(BF16)
    al = (a - ah.astype(F32)).astype(BF16)
    bh = b.astype(BF16)
    bl = (b - bh.astype(F32)).astype(BF16)
    return _dot(ah, bh) + (_dot(ah, bl) + _dot(al, bh))


def _rms(x, w):
    return x * lax.rsqrt(jnp.mean(x * x, axis=-1, keepdims=True) + EPS) * w


def _sigmoid(x):
    return 1.0 / (1.0 + jnp.exp(-x))


def _silu(x):
    return x * _sigmoid(x)


def _params(*sem):
    return pltpu.CompilerParams(dimension_semantics=sem, vmem_limit_bytes=VMEM_LIMIT)


def _const_spec(shape):
    nd = len(shape)
    return pl.BlockSpec(shape, lambda *_: (0,) * nd)


FF_TILE = 256


def _ffn_kernel(x_ref, nw_ref, wi_ref, wo_ref, o_ref, act_ref):
    x = x_ref[...]
    hb = _rms(x, nw_ref[...]).astype(BF16)
    for j in range(D_FF // FF_TILE):
        g = _dot(hb, wi_ref[:, j * FF_TILE:(j + 1) * FF_TILE])
        u = _dot(hb, wi_ref[:, D_FF + j * FF_TILE:D_FF + (j + 1) * FF_TILE])
        act_ref[:, j * FF_TILE:(j + 1) * FF_TILE] = (_silu(g) * u).astype(BF16)
    o_ref[...] = x + 0.5 * _dot(act_ref[...], wo_ref[...])


def _ffn(x, nw, wi, wo, tm):
    m = x.shape[0]
    return pl.pallas_call(
        _ffn_kernel,
        grid=(m // tm,),
        in_specs=[pl.BlockSpec((tm, D_MODEL), lambda i: (i, 0)),
                  _const_spec((1, D_MODEL)),
                  _const_spec((D_MODEL, 2 * D_FF)),
                  _const_spec((D_FF, D_MODEL))],
        out_specs=pl.BlockSpec((tm, D_MODEL), lambda i: (i, 0)),
        out_shape=jax.ShapeDtypeStruct((m, D_MODEL), F32),
        scratch_shapes=[pltpu.VMEM((tm, D_FF), BF16)],
        compiler_params=_params("parallel"), name="half_ffn",
    )(x, nw, wi, wo)


def _rope128(x, cos, sin):
    lane = lax.broadcasted_iota(jnp.int32, x.shape, 1)
    first = (lane % HEAD_DIM) < (HEAD_DIM // 2)
    swapped = jnp.where(first, pltpu.roll(x, LANES - HEAD_DIM // 2, 1), pltpu.roll(x, HEAD_DIM // 2, 1))
    return x * cos + swapped * sin


def _mix_kernel(seq_tiles, prompt, n_passthrough, x_ref, nw_ref, w_ref, cos_ref, sin_ref, *refs):
    qraw_ref, qrot_ref, small_ref, qkv_ref, zg_ref, merge_ref, *group_refs = refs[n_passthrough:]
    tm = x_ref.shape[0]
    hb = _rms(x_ref[...], nw_ref[...]).astype(BF16)
    cos = cos_ref[...]
    sin = sin_ref[...]
    scale = HEAD_DIM ** -0.5 * (math.log2(math.e) if prompt else 1.0)
    for j in range(NSA_HEADS):
        q = _dot(hb, w_ref[:, C_Q + j * LANES:C_Q + (j + 1) * LANES])
        qraw_ref[:, j * LANES:(j + 1) * LANES] = (q * scale).astype(BF16)
        qrot_ref[:, j * LANES:(j + 1) * LANES] = (_rope128(q, cos, sin) * scale).astype(BF16)
    kv = _dot(hb, w_ref[:, C_KV:C_KV + 6 * NSA_KV_W])
    k_cmp = kv[:, 0:LANES]
    v_cmp = kv[:, LANES:2 * LANES]
    k_sel = _rope128(kv[:, 2 * LANES:3 * LANES], cos, sin)
    v_sel = kv[:, 3 * LANES:4 * LANES]
    k_win = _rope128(kv[:, 4 * LANES:5 * LANES], cos, sin)
    v_win = kv[:, 5 * LANES:6 * LANES]
    if prompt:
        rows_t_ref, win_t_ref, kcmp_ref, vcmp_ref, ksa_ref, kw16_ref, vst_ref, vwt_ref = group_refs
        ones = jnp.ones((VT_ROWS - HEAD_DIM, tm), F32)
        for part, a in enumerate((k_cmp, v_cmp, k_sel, v_sel)):
            rows_t_ref[0, 0, part * LANES:(part + 1) * LANES, :] = a.T
        for part, a in enumerate((k_win, v_win)):
            win_t_ref[0, part * LANES:(part + 1) * LANES, :] = a.T
        for v, vt_ref in ((v_sel, vst_ref), (v_win, vwt_ref)):
            vt = v.T
            for k in range(NSA_KV_HEADS):
                vt_ref[k] = jnp.concatenate([vt[k * HEAD_DIM:(k + 1) * HEAD_DIM], ones], axis=0).astype(BF16)
        kcmp_ref[...] = k_cmp
        vcmp_ref[...] = v_cmp
        pos = (pl.program_id(0) % seq_tiles) * tm + lax.broadcasted_iota(jnp.int32, (tm, LANES), 0)
        lane = lax.broadcasted_iota(jnp.int32, (tm, LANES), 1)
        ksa_ref[:, 0:LANES] = k_sel.astype(BF16)
        ksa_ref[:, LANES:2 * LANES] = jnp.where(pos // SEL_BLOCK == lane, 1.0, 0.0).astype(BF16)
        kw16_ref[...] = k_win.astype(BF16)
    else:
        rows_ref, win_ref = group_refs
        for part, a in enumerate((k_cmp, v_cmp, k_sel, v_sel)):
            rows_ref[:, part * LANES:(part + 1) * LANES] = a
        win_ref[:, 0:LANES] = k_win
        win_ref[:, LANES:2 * LANES] = v_win
    small_ref[...] = _dot(hb, w_ref[:, C_SMALL:C_SMALL + LANES])
    qkv_ref[...] = _dot(hb, w_ref[:, C_QKV:C_QKV + DN_QKV_W])
    zg_ref[...] = _dot(hb, w_ref[:, C_ZG:C_ZG + DN_W])
    merge_ref[...] = _dot(hb, w_ref[:, C_MERGE:C_MERGE + 2 * D_MODEL])


def _mix(x, nw, w, cos, sin, tm, seq_len, prompt, layer=0, depth=1, rows_all=None):
    m = x.shape[0]
    seq_tiles = max(seq_len // tm, 1)
    row = lambda width: pl.BlockSpec((tm, width), lambda i: (i, 0))
    tab = pl.BlockSpec((tm, LANES), lambda i: (i % seq_tiles, 0))
    widths = [(QPAD_W, BF16), (QPAD_W, BF16), (LANES, F32), (DN_QKV_W, F32), (DN_W, F32), (2 * D_MODEL, F32)]
    inputs = [x, nw, w, cos, sin]
    in_specs = [row(D_MODEL), _const_spec((1, D_MODEL)), _const_spec((D_MODEL, MIX_W)), tab, tab]
    aliases = {}
    if prompt:
        nb = m // seq_len
        vt_spec = pl.BlockSpec((NSA_KV_HEADS, VT_ROWS, tm), lambda i: (0, 0, i))
        vt_shape = jax.ShapeDtypeStruct((NSA_KV_HEADS, VT_ROWS, m), BF16)
        g_widths = [(LANES, F32), (LANES, F32), (2 * LANES, BF16), (LANES, BF16)]
        g_specs = ([pl.BlockSpec((1, 1, 4 * NSA_KV_W, tm), lambda i: (layer, i // seq_tiles, 0, i % seq_tiles)),
                    pl.BlockSpec((1, 2 * NSA_KV_W, tm), lambda i: (i // seq_tiles, 0, i % seq_tiles))]
                   + [row(wd) for wd, _ in g_widths] + [vt_spec, vt_spec])
        g_shapes = ([jax.ShapeDtypeStruct((depth, nb, 4 * NSA_KV_W, seq_len), F32),
                     jax.ShapeDtypeStruct((nb, 2 * NSA_KV_W, seq_len), F32)]
                    + [jax.ShapeDtypeStruct((m, wd), dt) for wd, dt in g_widths] + [vt_shape, vt_shape])
        if rows_all is not None:
            aliases = {len(inputs): len(widths)}
            inputs.append(rows_all)
            in_specs.append(pl.BlockSpec(memory_space=pl.ANY))
    else:
        g_widths = [(4 * NSA_KV_W, F32), (2 * NSA_KV_W, F32)]
        g_specs = [row(wd) for wd, _ in g_widths]
        g_shapes = [jax.ShapeDtypeStruct((m, wd), dt) for wd, dt in g_widths]
    return pl.pallas_call(
        functools.partial(_mix_kernel, seq_tiles, prompt, len(aliases)),
        grid=(m // tm,),
        in_specs=in_specs,
        out_specs=[row(wd) for wd, _ in widths] + g_specs,
        out_shape=[jax.ShapeDtypeStruct((m, wd), dt) for wd, dt in widths] + g_shapes,
        input_output_aliases=aliases,
        compiler_params=_params("parallel"), name="mix_in",
    )(*inputs)


def _merge_kernel(x_ref, oa_ref, ob_ref, mg_ref, wa_ref, wb_ref, wo_ref, o_ref):
    mg = mg_ref[...]
    ya = _dot(oa_ref[...], wa_ref[...])
    yb = _dot(ob_ref[...], wb_ref[...])
    y = _sigmoid(mg[:, :D_MODEL]) * ya + _sigmoid(mg[:, D_MODEL:]) * yb
    o_ref[...] = x_ref[...] + _dot(y.astype(BF16), wo_ref[...])


def _merge(x, oa, ob, mg, wa, wb, wo, tm):
    m = x.shape[0]
    row = lambda width: pl.BlockSpec((tm, width), lambda i: (i, 0))
    return pl.pallas_call(
        _merge_kernel,
        grid=(m // tm,),
        in_specs=[row(D_MODEL), row(NSA_Q_W), row(DN_W), row(2 * D_MODEL),
                  _const_spec((NSA_Q_W, D_MODEL)), _const_spec((DN_W, D_MODEL)),
                  _const_spec((D_MODEL, D_MODEL))],
        out_specs=row(D_MODEL),
        out_shape=jax.ShapeDtypeStruct((m, D_MODEL), F32),
        compiler_params=_params("parallel"), name="branch_merge",
    )(x, oa, ob, mg, wa, wb, wo)


def _norm_kernel(x_ref, w_ref, o_ref):
    o_ref[...] = _rms(x_ref[...], w_ref[...])


def _final_norm(x, w, tm):
    m = x.shape[0]
    row = pl.BlockSpec((tm, D_MODEL), lambda i: (i, 0))
    return pl.pallas_call(
        _norm_kernel, grid=(m // tm,), in_specs=[row, _const_spec((1, D_MODEL))], out_specs=row,
        out_shape=jax.ShapeDtypeStruct((m, D_MODEL), F32), compiler_params=_params("parallel"),
    )(x, w)


def _compress_kernel(kcmp_ref, vcmp_ref, pe_ref, w1_ref, w2_ref, kc_ref, vct_ref):
    ns = kc_ref.shape[1]
    res = []
    for part, rows_ref in enumerate((kcmp_ref, vcmp_ref)):
        sa = jnp.zeros((ns, LANES), F32)
        sb = jnp.zeros((ns, LANES), F32)
        for p in range(CMP_STRIDE):
            x = rows_ref[pl.ds(p, ns, stride=CMP_STRIDE), :]
            sa = sa + _dot((x + pe_ref[part, p:p + 1, :]).astype(BF16), w1_ref[part, p])
            sb = sb + _dot((x + pe_ref[part, CMP_STRIDE + p:CMP_STRIDE + p + 1, :]).astype(BF16),
                           w1_ref[part, CMP_STRIDE + p])
        pre = sa + pltpu.roll(sb, ns - 1, 0)
        res.append(_dot(jax.nn.gelu(pre).astype(BF16), w2_ref[part]))
    kc_ref[0] = res[0].astype(BF16)
    vt = res[1].T
    ones = jnp.ones((VT_ROWS - HEAD_DIM, ns), F32)
    for k in range(NSA_KV_HEADS):
        vct_ref[0, k] = jnp.concatenate([vt[k * HEAD_DIM:(k + 1) * HEAD_DIM], ones], axis=0).astype(BF16)


def _compress_prompt(kcmp, vcmp, pe, w1, w2, b, seq):
    ns = seq // CMP_STRIDE
    tok = pl.BlockSpec((seq, LANES), lambda i: (i, 0))
    return pl.pallas_call(
        _compress_kernel,
        grid=(b,),
        in_specs=[tok, tok, _const_spec(pe.shape), _const_spec(w1.shape), _const_spec(w2.shape)],
        out_specs=[pl.BlockSpec((1, ns, LANES), lambda i: (i, 0, 0)),
                   pl.BlockSpec((1, NSA_KV_HEADS, VT_ROWS, ns), lambda i: (i, 0, 0, 0))],
        out_shape=[jax.ShapeDtypeStruct((b, ns, LANES), BF16),
                   jax.ShapeDtypeStruct((b, NSA_KV_HEADS, VT_ROWS, ns), BF16)],
        compiler_params=_params("parallel"), name="nsa_compress",
    )(kcmp, vcmp, pe, w1, w2)


TQ = 256
KT = 512
WIN_KEYS = WINDOW + TQ


def _softmax_rows(s, mask):
    s = jnp.where(mask, s, NEG_BIG)
    e = jnp.where(mask, jnp.exp(s - jnp.max(s, axis=-1, keepdims=True)), 0.0)
    return e * (1.0 / jnp.maximum(jnp.sum(e, axis=-1, keepdims=True), 1e-30))


def _topk_bias_t(imp_t, n_cand, n_sel):
    nr = imp_t.shape[0] // 8
    xs = [imp_t[8 * r:8 * r + 8, :] for r in range(nr)]
    cnt = [jnp.zeros_like(xs[0]) for _ in range(nr)]
    sub = lax.broadcasted_iota(jnp.int32, xs[0].shape, 0)
    for j in range(n_cand):
        rj = j // 8
        row = xs[rj][j % 8:j % 8 + 1, :]
        for r in range(nr):
            if r < rj:
                inc = jnp.where(row > xs[r], 1.0, 0.0)
            elif r > rj:
                inc = jnp.where(row >= xs[r], 1.0, 0.0)
            else:
                inc = jnp.where(sub > j % 8, jnp.where(row >= xs[r], 1.0, 0.0),
                                jnp.where(row > xs[r], 1.0, 0.0))
            cnt[r] = cnt[r] + inc
    return jnp.concatenate([jnp.where(c < n_sel, 0.0, NEG_BIG) for c in cnt], axis=0)


def _nsa_prompt_kernel(qraw_ref, qrot_ref, small_ref, kc_ref, vct_ref, ksa_ref, vst_ref, win_ref, vwt_ref,
                       ovt_ref, cbias_ref, dbias_ref, wbias_ref, o_ref):
    seq = ksa_ref.shape[1]
    ns = seq // SEL_BLOCK
    s0 = pl.multiple_of(pl.program_id(1) * TQ, TQ)
    cols = NSA_GROUP * TQ
    tile = lambda a: jnp.concatenate([a] * NSA_GROUP, axis=1)
    qpos_row = s0 + lax.broadcasted_iota(jnp.int32, (1, cols), 1) % TQ
    gates_t = _sigmoid(small_ref[...]).T

    blk_t = lax.broadcasted_iota(jnp.int32, (ns, TQ), 0)
    qpos_t = s0 + lax.broadcasted_iota(jnp.int32, (ns, TQ), 1)
    cur_t = qpos_t // SEL_BLOCK
    valid_t = blk_t * SEL_BLOCK <= qpos_t
    forced_t = (blk_t == 0) | (blk_t == cur_t) | (blk_t == cur_t - 1)
    cbias = tile(cbias_ref[0])
    dbias = tile(dbias_ref[0])
    wbias = tile(wbias_ref[0])

    def split_sum(aug):
        return aug[0:HEAD_DIM], aug[HEAD_DIM:HEAD_DIM + 1]

    kv_heads = range(NSA_KV_HEADS)
    w0 = pl.multiple_of(jnp.maximum(s0 - WINDOW, 0), TQ)
    o_c, o_w, q_sel = [], [], []
    for k in kv_heads:
        def q_t(ref):
            return jnp.concatenate(
                [ref[:, (k * NSA_GROUP + g) * LANES:(k * NSA_GROUP + g + 1) * LANES].astype(F32).T.astype(BF16)
                 for g in range(NSA_GROUP)], axis=1)
        qt_raw = q_t(qraw_ref)
        qt_rot = q_t(qrot_ref)

        s = _dot(kc_ref[0], qt_raw) + cbias
        e = jnp.exp2(s - jnp.max(s, axis=0, keepdims=True))
        acc, l = split_sum(_dot(vct_ref[0, k], e.astype(BF16)))
        inv = jnp.where(qpos_row >= CMP_BLOCK - 1, 1.0 / l, 0.0)
        o_c.append(acc * inv)
        p = e * inv
        p_sum = p[:, 0:TQ] + p[:, TQ:2 * TQ] + p[:, 2 * TQ:3 * TQ] + p[:, 3 * TQ:4 * TQ]
        imp_t = _dot_exact_lhs(ovt_ref[...], p_sum)[0:ns]
        imp_t = jnp.where(valid_t, imp_t + jnp.where(forced_t, FORCE_BONUS, 0.0), -jnp.inf)
        bias_t = _topk_bias_t(imp_t, ns, min(N_SEL, ns))
        if ns < LANES:
            bias_t = jnp.concatenate([bias_t, jnp.zeros((LANES - ns, TQ), F32)], axis=0)
        q_sel.append(jnp.concatenate([qt_rot, tile(bias_t).astype(BF16)], axis=0))

        s = _dot(win_ref[0, pl.ds(w0, WIN_KEYS), :], qt_rot) + wbias
        e = jnp.exp2(s - jnp.max(s, axis=0, keepdims=True)).astype(BF16)
        acc, l = split_sum(_dot(vwt_ref[k, :, pl.ds(w0, WIN_KEYS)], e))
        o_w.append(acc * (1.0 / l))

    def sel_step(c, carry, diagonal):
        k0 = pl.multiple_of(c * KT, KT)
        keys = ksa_ref[0, pl.ds(k0, KT), :]
        out = []
        for k in kv_heads:
            m, acc = carry[k]
            s = _dot(keys, q_sel[k])
            if diagonal:
                s = s + dbias
            m_new = jnp.maximum(m, jnp.max(s, axis=0, keepdims=True))
            p = jnp.exp2(s - m_new).astype(BF16)
            out.append((m_new, jnp.exp2(m - m_new) * acc + _dot(vst_ref[k, :, pl.ds(k0, KT)], p)))
        return tuple(out)
    init = tuple((jnp.full((1, cols), NEG_BIG, F32), jnp.zeros((VT_ROWS, cols), F32)) for _ in kv_heads)
    n_full = s0 // KT
    carry = lax.fori_loop(0, n_full, functools.partial(sel_step, diagonal=False), init)
    carry = sel_step(n_full, carry, True)
    res = []
    for k in kv_heads:
        acc, l = split_sum(carry[k][1])
        res.append((o_c[k], acc * (1.0 / l), o_w[k]))

    for g in range(NSA_GROUP):
        halves = []
        for k in range(NSA_KV_HEADS):
            r = SM_GATE + (k * NSA_GROUP + g) * 3
            halves.append(sum(gates_t[r + br:r + br + 1, :] * res[k][br][:, g * TQ:(g + 1) * TQ]
                              for br in range(3)))
        o_ref[:, g * LANES:(g + 1) * LANES] = jnp.concatenate(halves, axis=0).T.astype(BF16)


def _attn_bias_tables(seq):
    r = np.arange(TQ)[None, None, :]
    blocks = np.arange(seq // TQ)[:, None, None]
    c = np.arange(seq // CMP_STRIDE)[None, :, None]
    cb = c * CMP_STRIDE + CMP_BLOCK - 1 <= blocks * TQ + r
    kk = np.arange(KT)[None, :, None]
    off = np.arange(KT // TQ)[:, None, None] * TQ
    db = kk <= off + r
    kw = np.arange(WIN_KEYS)[None, :, None]
    nw = WINDOW // TQ
    j = np.arange(nw + 1)[:, None, None]
    qpos = j * TQ + r
    kpos = np.where(j < nw, kw, qpos - r - WINDOW + kw)
    wb = (kpos <= qpos) & (kpos > qpos - WINDOW)
    f = lambda m: jnp.asarray(np.where(m, 0.0, NEG_BIG).astype(np.float32))
    return f(cb), f(db), f(wb)


def _nsa_prompt(qraw, qrot, small, kc, vct, ksa, vst, kw16, vwt, ovt, tables, b, seq):
    nc = kc.shape[1]
    nq = seq // TQ
    cb, db, wb = tables
    tok = lambda width: pl.BlockSpec((TQ, width), lambda i, j: (i * nq + j, 0))
    per_b = lambda width: pl.BlockSpec((1, seq, width), lambda i, j: (i, 0, 0))
    vt_spec = pl.BlockSpec((NSA_KV_HEADS, VT_ROWS, seq), lambda i, j: (0, 0, i))
    return pl.pallas_call(
        _nsa_prompt_kernel,
        grid=(b, nq),
        in_specs=[tok(QPAD_W), tok(QPAD_W), tok(LANES),
                  pl.BlockSpec((1, nc, LANES), lambda i, j: (i, 0, 0)),
                  pl.BlockSpec((1, NSA_KV_HEADS, VT_ROWS, nc), lambda i, j: (i, 0, 0, 0)),
                  per_b(2 * LANES), vt_spec, per_b(LANES), vt_spec,
                  _const_spec(ovt.shape),
                  pl.BlockSpec((1, nc, TQ), lambda i, j: (j, 0, 0)),
                  pl.BlockSpec((1, KT, TQ), lambda i, j: (j % (KT // TQ), 0, 0)),
                  pl.BlockSpec((1, WIN_KEYS, TQ), lambda i, j: (jnp.minimum(j, WINDOW // TQ), 0, 0))],
        out_specs=tok(NSA_Q_W),
        out_shape=jax.ShapeDtypeStruct((b * seq, NSA_Q_W), BF16),
        compiler_params=_params("parallel", "parallel"), name="nsa_prompt",
    )(qraw, qrot, small, kc, vct, ksa.reshape(b, seq, 2 * LANES), vst, kw16.reshape(b, seq, LANES), vwt,
      ovt, cb, db, wb)


PAGES_PER_STEP = 16
SEG_PER_PAGE = PAGE_SIZE // CMP_STRIDE
SEG_W = CMP_STRIDE * 2 * NSA_KV_W


def _nsa_sample_kernel(ts, past, pt_ref, *refs):
    pages = refs[:PAGES_PER_STEP]
    (qraw_ref, qrot_ref, small_ref, rows_ref, winnew_ref, winst_ref, pe_ref, w1_ref, w2_ref,
     ovt_ref, rexp_ref, o_ref, seg_ref, ksat_ref, vst_ref, sm_ref, stage_ref, bias_ref) = refs[PAGES_PER_STEP:]
    del pt_ref
    pg = pl.program_id(1)
    nseg = seg_ref.shape[0]
    nrows = NSA_HEADS * ts
    kt = NSA_KV_HEADS * ts
    half_w = NSA_KV_HEADS * CMP_STRIDE * HEAD_DIM

    @pl.when((pl.program_id(0) == 0) & (pg == 0))
    def _():
        for part in range(2):
            hi, mid, lo = _split3(pe_ref[part])
            w = w1_ref[part]
            bias_ref[part] = _dot(hi, w) + _dot(mid, w) + _dot(lo, w)

    for e in range(PAGES_PER_STEP):
        for part in range(2):
            xt = pages[e][0, 0, part * LANES:(part + 1) * LANES, :].T
            for j in range(PAGE_SIZE // 8):
                start = (j % 2) * (PAGE_SIZE // 2) + j // 2
                stage_ref[e, part, pl.ds(start, 8, stride=SEG_PER_PAGE), :] = xt[8 * j:8 * j + 8, :]
        k0 = pl.multiple_of((pg * PAGES_PER_STEP + e) * PAGE_SIZE, PAGE_SIZE)
        ksat_ref[0:LANES, pl.ds(k0, PAGE_SIZE)] = pages[e][0, 0, 2 * LANES:3 * LANES, :].astype(BF16)
        vst_ref[:, pl.ds(k0, PAGE_SIZE)] = pages[e][0, 0, 3 * LANES:4 * LANES, :].astype(BF16)

        @pl.when(pl.program_id(0) == 0)
        def _():
            kpos = k0 + lax.broadcasted_iota(jnp.int32, (LANES, PAGE_SIZE), 1)
            blk = lax.broadcasted_iota(jnp.int32, (LANES, PAGE_SIZE), 0)
            ksat_ref[LANES:2 * LANES, pl.ds(k0, PAGE_SIZE)] = jnp.where(
                kpos // SEL_BLOCK == blk, 1.0, 0.0).astype(BF16)
    low = lax.broadcasted_iota(jnp.int32, (2 * SEG_PER_PAGE, LANES), 1) < HEAD_DIM
    for pair in range(PAGES_PER_STEP // 2):
        r0 = pl.multiple_of(pg * (PAGES_PER_STEP * SEG_PER_PAGE) + pair * 2 * SEG_PER_PAGE, 2 * SEG_PER_PAGE)
        for part in range(2):
            for pp in range(CMP_STRIDE // 2):
                xp, xq = [jnp.concatenate([stage_ref[2 * pair + e, part, p * SEG_PER_PAGE:(p + 1) * SEG_PER_PAGE, :]
                                           for e in range(2)], axis=0) for p in (2 * pp, 2 * pp + 1)]
                heads = (jnp.where(low, xp, pltpu.roll(xq, HEAD_DIM, 1)), jnp.where(low, pltpu.roll(xp, HEAD_DIM, 1), xq))
                for h in range(NSA_KV_HEADS):
                    c0 = part * half_w + h * (CMP_STRIDE * HEAD_DIM) + pp * LANES
                    seg_ref[pl.ds(r0, 2 * SEG_PER_PAGE), c0:c0 + LANES] = heads[h].astype(BF16)

    @pl.when(pg == pl.num_programs(1) - 1)
    def _():
        ns_past = past // SEL_BLOCK
        nc = nseg - 1
        q_raw = qraw_ref[0]
        q_rot = qrot_ref[0]
        q_rot_f = q_rot.astype(F32)
        row = lax.broadcasted_iota(jnp.int32, (nrows, 1), 0)
        t_row = row % ts
        kcvc = []
        for part in range(2):
            res = _dot(seg_ref[:, part * half_w:(part + 1) * half_w], w1_ref[part])
            b8 = bias_ref[part]
            pre = (res[:, 0:LANES] + b8[0:1, 0:LANES]
                   + pltpu.roll(res[:, LANES:2 * LANES] + b8[1:2, LANES:2 * LANES], nseg - 1, 0))
            kcvc.append(_dot(jax.nn.gelu(pre).astype(BF16), w2_ref[part]).astype(BF16))
        s_c = _dot_nt(q_raw, kcvc[0])
        p_c = _softmax_rows(s_c, lax.broadcasted_iota(jnp.int32, (nrows, nseg), 1) < nc)
        o_c = _dot(p_c.astype(BF16), kcvc[1])
        p_sum = p_c[0:kt] + p_c[kt:2 * kt] + p_c[2 * kt:3 * kt] + p_c[3 * kt:4 * kt]
        p_sum = jnp.concatenate([p_sum, jnp.zeros((LANES - kt, nseg), F32)], axis=0)
        nb = ovt_ref.shape[0]
        imp_t = _dot_exact_lhs_nt(ovt_ref[...], p_sum)
        blk_t = lax.broadcasted_iota(jnp.int32, (nb, LANES), 0)
        qpos_t = past + lax.broadcasted_iota(jnp.int32, (nb, LANES), 1) % ts
        cur_t = qpos_t // SEL_BLOCK
        forced_t = (blk_t == 0) | (blk_t == cur_t) | (blk_t == cur_t - 1)
        imp_t = jnp.where(blk_t * SEL_BLOCK <= qpos_t, imp_t + jnp.where(forced_t, FORCE_BONUS, 0.0), -jnp.inf)
        bias_t = _topk_bias_t(imp_t, -(-(past + ts) // SEL_BLOCK), N_SEL)[0:ns_past]
        bias = bias_t.T[0:kt]
        q_sel = jnp.concatenate([q_rot, jnp.concatenate([bias] * NSA_GROUP, axis=0).astype(BF16)], axis=1)

        def attend(s_past, v_past16_t, k_new, v_new, mask_past):
            if mask_past is not None:
                s_past = jnp.where(mask_past, s_past, NEG_BIG)
            s_new = [jnp.where(t_row >= j, jnp.sum(q_rot_f * k_new[j:j + 1, :], axis=-1, keepdims=True), NEG_BIG)
                     for j in range(ts)]
            m = jnp.max(s_past, axis=-1, keepdims=True)
            for s in s_new:
                m = jnp.maximum(m, s)
            p_past = jnp.exp(s_past - m)
            if mask_past is not None:
                p_past = jnp.where(mask_past, p_past, 0.0)
            acc = _dot_nt(p_past.astype(BF16), v_past16_t)
            l = jnp.sum(p_past, axis=-1, keepdims=True)
            for j, s in enumerate(s_new):
                p = jnp.where(t_row >= j, jnp.exp(s - m), 0.0)
                acc = acc + p * v_new[j:j + 1, :]
                l = l + p
            return acc * (1.0 / l)

        new = rows_ref[0]
        o_s = attend(_dot(q_sel, ksat_ref[...]), vst_ref[...], new[:, 2 * NSA_KV_W:3 * NSA_KV_W],
                     new[:, 3 * NSA_KV_W:4 * NSA_KV_W], None)
        wbuf = winst_ref.shape[3]
        wnew = winnew_ref[0]
        i_w = lax.broadcasted_iota(jnp.int32, (nrows, wbuf), 1)
        dist = t_row + wbuf - i_w
        o_w = attend(_dot(q_rot, winst_ref[0, 0, 0:LANES, :].astype(BF16)),
                     winst_ref[0, 0, LANES:2 * LANES, :].astype(BF16),
                     wnew[:, 0:LANES], wnew[:, LANES:2 * LANES],
                     (dist < WINDOW) & (past - wbuf + i_w >= 0))

        sm_ref[...] = jnp.zeros_like(sm_ref)
        sm_ref[0:ts, :] = small_ref[0]
        gexp = _dot_exact_rhs(_sigmoid(sm_ref[...]), rexp_ref[...])[0:kt]
        low = lax.broadcasted_iota(jnp.int32, (kt, LANES), 1) < HEAD_DIM
        for g in range(NSA_GROUP):
            out = jnp.zeros((kt, LANES), F32)
            for br, o_br in enumerate((o_c, o_s, o_w)):
                og = o_br[g * kt:(g + 1) * kt]
                both = jnp.where(low, og, pltpu.roll(og, kt - ts, 0))
                out = out + gexp[:, br * NSA_Q_W + g * LANES:br * NSA_Q_W + (g + 1) * LANES] * both
            o_ref[0, :, g * LANES:(g + 1) * LANES] = out


def _nsa_sample(layer, cache, page_table, qraw_s, qrot_s, small_s, rows_s, winnew_s, win_state,
                pe, w1, w2, ovt, rexp, past):
    b, ts = small_s.shape[:2]
    n_pages = page_table.shape[1]
    assert n_pages % PAGES_PER_STEP == 0 and ts * NSA_KV_HEADS == 8 and past == n_pages * PAGE_SIZE
    nseg = past // CMP_STRIDE
    wbuf = win_state.shape[3]
    kw = 4 * NSA_KV_W

    def page_spec(e):
        return pl.BlockSpec((1, 1, 4 * PAGE_SIZE, LANES),
                            lambda i, g, pt: (layer, pt[i, g * PAGES_PER_STEP + e], 0, 0))
    per_b = lambda shape: pl.BlockSpec((1,) + shape, lambda i, g, pt: (i,) + (0,) * len(shape))
    const = lambda shape: pl.BlockSpec(shape, lambda i, g, pt: (0,) * len(shape))
    grid_spec = pltpu.PrefetchScalarGridSpec(
        num_scalar_prefetch=1,
        grid=(b, n_pages // PAGES_PER_STEP),
        in_specs=[page_spec(e) for e in range(PAGES_PER_STEP)] + [
            per_b((NSA_HEADS * ts, LANES)), per_b((NSA_HEADS * ts, LANES)), per_b((ts, LANES)),
            per_b((ts, kw)), per_b((ts, 2 * NSA_KV_W)),
            pl.BlockSpec((1, 1, 2 * NSA_KV_W, wbuf), lambda i, g, pt: (layer, i, 0, 0)),
            const(pe.shape), const(w1.shape), const(w2.shape), const(ovt.shape), const(rexp.shape)],
        out_specs=per_b((NSA_KV_HEADS * ts, NSA_Q_W)),
        scratch_shapes=[pltpu.VMEM((nseg, SEG_W), BF16),
                        pltpu.VMEM((2 * LANES, past), BF16), pltpu.VMEM((LANES, past), BF16),
                        pltpu.VMEM((2 * NSA_KV_HEADS * ts, LANES), F32),
                        pltpu.VMEM((PAGES_PER_STEP, 2, PAGE_SIZE, LANES), F32),
                        pltpu.VMEM((2, 8, 2 * LANES), F32)])
    return pl.pallas_call(
        functools.partial(_nsa_sample_kernel, ts, past),
        grid_spec=grid_spec,
        out_shape=jax.ShapeDtypeStruct((b, NSA_KV_HEADS * ts, NSA_Q_W), F32),
        compiler_params=_params("arbitrary", "arbitrary"), name="nsa_sample",
    )(page_table, *([cache] * PAGES_PER_STEP), qraw_s, qrot_s, small_s, rows_s, winnew_s, win_state,
      pe, w1, w2, ovt, rexp)


DN_C = 128
HALO = 8


def _softplus(x):
    return jnp.maximum(x, 0.0) + jnp.log(1.0 + jnp.exp(-jnp.abs(x)))


def _l2n(x):
    return x * lax.rsqrt(jnp.sum(x * x, axis=-1, keepdims=True) + EPS)


def _dn_prep_kernel(n_valid, qkv_ref, halo_ref, prefix_ref, small_ref, cw_ref, alog_ref, dtb_ref,
                    tri_ref, trit_ref, u_ref, w_ref, qd_ref, qk_ref, kdt_ref, gl_ref, xcat_ref, sm_ref):
    n = pl.program_id(1)
    if n_valid == DN_C:
        xcat_ref[HALO:HALO + DN_C, :] = qkv_ref[...]
        small = small_ref[...]

        @pl.when(n == 0)
        def _():
            xcat_ref[0:HALO, :] = prefix_ref[0]

        @pl.when(n > 0)
        def _():
            xcat_ref[0:HALO, :] = halo_ref[...]
    else:
        xcat_ref[...] = jnp.zeros_like(xcat_ref)
        xcat_ref[0:HALO, :] = prefix_ref[0]
        xcat_ref[HALO:HALO + n_valid, :] = qkv_ref[0]
        sm_ref[...] = jnp.zeros_like(sm_ref)
        sm_ref[0:n_valid, :] = small_ref[0]
        small = sm_ref[...]

    conv = jnp.zeros((DN_C, DN_QKV_W), F32)
    for i in range(CONV_W):
        conv = conv + xcat_ref[HALO - (CONV_W - 1) + i:HALO - (CONV_W - 1) + i + DN_C, :] * cw_ref[i:i + 1, :]
    conv = _silu(conv)

    ii = lax.broadcasted_iota(jnp.int32, (DN_C, DN_C), 0)
    jj = lax.broadcasted_iota(jnp.int32, (DN_C, DN_C), 1)
    g_all = -jnp.exp(alog_ref[...]) * _softplus(small + dtb_ref[...])
    beta_all = _sigmoid(small)
    if n_valid != DN_C:
        g_all = jnp.where(ii < n_valid, g_all, 0.0)
        beta_all = jnp.where(ii < n_valid, beta_all, 0.0)
    gc_all = _dot_exact_lhs(tri_ref[...], g_all)
    gr_all = _dot_exact_rhs(g_all.T, trit_ref[...])
    heads = range(DN_HEADS)
    q = [_l2n(conv[:, h * DN_DK:(h + 1) * DN_DK]) * DN_DK ** -0.5 for h in heads]
    k = [_l2n(conv[:, DN_QK_W + h * DN_DK:DN_QK_W + (h + 1) * DN_DK]) for h in heads]
    v = [conv[:, 2 * DN_QK_W + h * DN_DV:2 * DN_QK_W + (h + 1) * DN_DV] for h in heads]
    gcol = [gc_all[:, SM_A + h:SM_A + h + 1] for h in heads]
    beta = [beta_all[:, SM_B + h:SM_B + h + 1] for h in heads]
    decay = [jnp.where(ii >= jj, jnp.exp(gcol[h] - gr_all[SM_A + h:SM_A + h + 1, :]), 0.0) for h in heads]
    kb = [k[h] * beta[h] for h in heads]
    k16 = [k[h].astype(BF16) for h in heads]
    eg = [jnp.exp(gcol[h]) for h in heads]
    a = [jnp.where(ii > jj, _dot_nt(kb[h].astype(BF16), k16[h]) * decay[h], 0.0) for h in heads]
    t = [jnp.where(ii == jj, 1.0, 0.0) - a[h] for h in heads]
    for _ in range(max(1, math.ceil(math.log2(n_valid))) - 1):
        a = [_dot3(a[h], a[h]) for h in heads]
        t = [t[h] + _dot3(t[h], a[h]) for h in heads]
    x = [_dot3(t[h], jnp.concatenate([v[h] * beta[h], kb[h] * eg[h]], axis=1)) for h in heads]
    for h in heads:
        u_ref[:, h * DN_DV:(h + 1) * DN_DV] = x[h][:, 0:DN_DV]
        w_ref[:, h * DN_DK:(h + 1) * DN_DK] = x[h][:, DN_DV:DN_DV + DN_DK].astype(BF16)
        qk = jnp.where(ii >= jj, _dot_nt(q[h].astype(BF16), k16[h]) * decay[h], 0.0)
        qk_ref[:, h * DN_C:(h + 1) * DN_C] = qk.astype(BF16)
        qd_ref[:, h * DN_DK:(h + 1) * DN_DK] = (q[h] * eg[h]).astype(BF16)
        glast = gcol[h][DN_C - 1:DN_C, :]
        kdt_ref[0, h * DN_DK:(h + 1) * DN_DK, :] = (k[h] * jnp.exp(glast - gcol[h])).T.astype(BF16)
        gl_ref[0, h:h + 1, :] = jnp.broadcast_to(jnp.exp(glast), (1, LANES))
    gl_ref[0, DN_HEADS:, :] = jnp.zeros((HALO - DN_HEADS, LANES), F32)


def _dn_prep(qkv, prefix8, small, cw, alog_row, dtb_row, tri, trit, b, nch, n_valid):
    rows = b * nch * DN_C
    if n_valid == DN_C:
        qkv_spec = pl.BlockSpec((DN_C, DN_QKV_W), lambda i, n: (i * nch + n, 0))
        halo_spec = pl.BlockSpec((HALO, DN_QKV_W),
                                 lambda i, n: (jnp.maximum((i * nch + n) * (DN_C // HALO) - 1, 0), 0))
        small_spec = pl.BlockSpec((DN_C, LANES), lambda i, n: (i * nch + n, 0))
        qkv_in, halo_in, small_in = qkv, qkv, small
    else:
        qkv_in = qkv.reshape(b, n_valid, DN_QKV_W)
        small_in = small.reshape(b, n_valid, LANES)
        halo_in = prefix8
        qkv_spec = pl.BlockSpec((1, n_valid, DN_QKV_W), lambda i, n: (i, 0, 0))
        halo_spec = pl.BlockSpec((1, HALO, DN_QKV_W), lambda i, n: (i, 0, 0))
        small_spec = pl.BlockSpec((1, n_valid, LANES), lambda i, n: (i, 0, 0))
    tok = lambda width: pl.BlockSpec((DN_C, width), lambda i, n: (i * nch + n, 0))
    return pl.pallas_call(
        functools.partial(_dn_prep_kernel, n_valid),
        grid=(b, nch),
        in_specs=[qkv_spec, halo_spec, pl.BlockSpec((1, HALO, DN_QKV_W), lambda i, n: (i, 0, 0)), small_spec,
                  _const_spec((CONV_W, DN_QKV_W)), _const_spec((1, LANES)), _const_spec((1, LANES)),
                  _const_spec((DN_C, DN_C)), _const_spec((DN_C, DN_C))],
        out_specs=[tok(DN_W), tok(DN_QK_W), tok(DN_QK_W), tok(DN_HEADS * DN_C),
                   pl.BlockSpec((1, DN_QK_W, DN_C), lambda i, n: (i * nch + n, 0, 0)),
                   pl.BlockSpec((1, HALO, LANES), lambda i, n: (i * nch + n, 0, 0))],
        out_shape=[jax.ShapeDtypeStruct((rows, DN_W), F32), jax.ShapeDtypeStruct((rows, DN_QK_W), BF16),
                   jax.ShapeDtypeStruct((rows, DN_QK_W), BF16), jax.ShapeDtypeStruct((rows, DN_HEADS * DN_C), BF16),
                   jax.ShapeDtypeStruct((b * nch, DN_QK_W, DN_C), BF16),
                   jax.ShapeDtypeStruct((b * nch, HALO, LANES), F32)],
        scratch_shapes=[pltpu.VMEM((HALO + DN_C, DN_QKV_W), F32), pltpu.VMEM((DN_C, LANES), F32)],
        compiler_params=_params("parallel", "parallel"), name="dn_prep",
    )(qkv_in, halo_in, prefix8, small_in, cw, alog_row, dtb_row, tri, trit)


def _dn_scan_kernel(n_valid, u_ref, w_ref, qd_ref, qk_ref, kdt_ref, gl_ref, zg_ref, s0_ref, nw_ref,
                    o_ref, s_ref):
    @pl.when(pl.program_id(1) == 0)
    def _():
        s_ref[...] = s0_ref[...]

    for b in range(u_ref.shape[0]):
        for h in range(DN_HEADS):
            cs = slice(h * DN_DK, (h + 1) * DN_DK)
            s = s_ref[b, h]
            s16 = s.astype(BF16)
            v_new = u_ref[b, :, cs] - _dot(w_ref[b, :, cs], s16)
            v16 = v_new.astype(BF16)
            o = _dot(qd_ref[b, :, cs], s16) + _dot(qk_ref[b, :, cs], v16)
            s_ref[b, h] = s * gl_ref[b, h:h + 1, :] + _dot(kdt_ref[b, cs, :], v16)
            on = o * lax.rsqrt(jnp.mean(o * o, axis=-1, keepdims=True) + EPS) * nw_ref[...]
            o_ref[b, :, cs] = (on[0:n_valid] * _silu(zg_ref[b, :, cs])).astype(BF16)


def _dn_scan(u, w, qd, qk, kdt, gl, zg, s0, nw, b, nch, bb, n_valid):
    tok = lambda width: pl.BlockSpec((bb, DN_C, width), lambda i, n: (i, n, 0))
    r3 = lambda a: a.reshape(b, nch * DN_C, a.shape[-1])
    st = pl.BlockSpec((bb, DN_HEADS, DN_DK, DN_DV), lambda i, n: (i, 0, 0, 0))
    return pl.pallas_call(
        functools.partial(_dn_scan_kernel, n_valid),
        grid=(b // bb, nch),
        in_specs=[tok(DN_W), tok(DN_QK_W), tok(DN_QK_W), tok(DN_HEADS * DN_C),
                  pl.BlockSpec((bb, DN_QK_W, DN_C), lambda i, n: (i, n, 0)),
                  pl.BlockSpec((bb, HALO, LANES), lambda i, n: (i, n, 0)),
                  pl.BlockSpec((bb, n_valid, DN_W), lambda i, n: (i, n, 0)), st, _const_spec((1, DN_DV))],
        out_specs=[pl.BlockSpec((bb, n_valid, DN_W), lambda i, n: (i, n, 0)), st],
        out_shape=[jax.ShapeDtypeStruct((b, nch * n_valid, DN_W), BF16),
                   jax.ShapeDtypeStruct((b, DN_HEADS, DN_DK, DN_DV), F32)],
        compiler_params=_params("parallel", "arbitrary"), name="dn_scan",
    )(r3(u), r3(w), r3(qd), r3(qk), kdt.reshape(b, nch * DN_QK_W, DN_C), gl.reshape(b, nch * HALO, LANES),
      zg.reshape(b, nch * n_valid, DN_W), s0, nw)


IN_WIDTHS = (NSA_Q_W, 6 * NSA_KV_W, 3 * NSA_HEADS, DN_QKV_W, DN_HEADS, DN_HEADS, DN_W, 2 * D_MODEL)


def _pack_mix_weight(w_in):
    offs = np.cumsum(IN_WIDTHS)[:-1].tolist()
    q, kv, gate, qkv, a, b, zg, merge = jnp.split(w_in, offs, axis=-1)
    d = w_in.shape[0]
    qh = q.reshape(d, NSA_KV_HEADS, NSA_GROUP, 1, HEAD_DIM)
    eye = jnp.eye(NSA_KV_HEADS, dtype=w_in.dtype).reshape(1, NSA_KV_HEADS, 1, NSA_KV_HEADS, 1)
    qpad = (qh * eye).reshape(d, QPAD_W)
    small = jnp.concatenate([gate, a, b, jnp.zeros((d, LANES - SM_B - DN_HEADS), w_in.dtype)], axis=1)
    return jnp.concatenate([qpad, kv, small, qkv, zg, merge], axis=1).astype(BF16)


def _rope_tables(pos):
    half = HEAD_DIM // 2
    freq = ROPE_THETA ** (-jnp.arange(half, dtype=F32) / half)
    ang = pos.astype(F32)[:, None] * freq[None, :]
    cos, sin = jnp.cos(ang), jnp.sin(ang)
    cos = jnp.concatenate([cos, cos] * (LANES // HEAD_DIM), axis=1)
    sin = jnp.concatenate([-sin, sin] * (LANES // HEAD_DIM), axis=1)
    return cos, sin


def _gate_expand_matrix():
    r = np.zeros((LANES, 3 * NSA_Q_W), np.float32)
    for h in range(NSA_HEADS):
        k, g = divmod(h, NSA_GROUP)
        for br in range(3):
            c0 = br * NSA_Q_W + g * LANES + k * HEAD_DIM
            r[SM_GATE + h * 3 + br, c0:c0 + HEAD_DIM] = 1.0
    return jnp.asarray(r, BF16)


def _overlap_t(nc_pad, nc, ns, rows):
    i = np.arange(nc_pad)[None, :]
    j = np.arange(rows)[:, None]
    ov = (i * CMP_STRIDE < (j + 1) * SEL_BLOCK) & (i * CMP_STRIDE + CMP_BLOCK > j * SEL_BLOCK)
    ov = ov & (i < nc) & (j < ns)
    return jnp.asarray(ov.astype(np.float32), BF16)


def _pack_compress_sample(pos_emb, w1, w2):
    eye = jnp.eye(NSA_KV_HEADS, dtype=w1.dtype)
    w = w1.reshape(2, 2, CMP_STRIDE, HEAD_DIM, HEAD_DIM)
    w = jnp.einsum('kapde,hg->khpdage', w, eye)
    w = w.reshape(2, NSA_KV_HEADS * CMP_STRIDE * HEAD_DIM, 2 * LANES).astype(BF16)
    pe = pos_emb.reshape(2, 2, 1, CMP_STRIDE * HEAD_DIM)
    pe = jnp.concatenate([pe] * NSA_KV_HEADS, axis=3)
    pe = jnp.concatenate([pe[:, :, 0], jnp.zeros((2, 6, pe.shape[3]), pe.dtype)], axis=1)
    w2d = jnp.einsum('kde,hg->khdge', w2, eye).reshape(2, LANES, LANES).astype(BF16)
    return pe, w, w2d


def _sample_rows(q_pad, b, ts):
    q = q_pad.reshape(b, ts, NSA_KV_HEADS, NSA_GROUP, LANES).transpose(0, 3, 2, 1, 4)
    return q.reshape(b, NSA_HEADS * ts, LANES)


def _dn_lane_row(vals):
    return jnp.zeros((1, LANES), F32).at[0, SM_A:SM_A + DN_HEADS].set(vals.astype(F32))


def _tri_mats():
    tri = np.tril(np.ones((DN_C, DN_C), np.float32))
    return jnp.asarray(tri, BF16), jnp.asarray(tri.T, BF16)


def _deltanet(qkv, small, zg, prefix, s0, dn_params, tri, trit, b, seq_tokens):
    conv_w, a_log, dt_bias, norm_w = dn_params
    if seq_tokens % DN_C == 0:
        nch, n_valid, bb = seq_tokens // DN_C, DN_C, min(b, 4)
    else:
        nch, n_valid, bb = 1, seq_tokens, 4
    prefix8 = jnp.concatenate([jnp.zeros((b, HALO - (CONV_W - 1), DN_QKV_W), F32), prefix], axis=1)
    u, w, qd, qk, kdt, gl = _dn_prep(qkv, prefix8, small, conv_w, _dn_lane_row(a_log), _dn_lane_row(dt_bias),
                                     tri, trit, b, nch, n_valid)
    ob, s_out = _dn_scan(u, w, qd, qk, kdt, gl, zg, s0, norm_w[None], b, nch, bb, n_valid)
    return ob.reshape(b * seq_tokens, DN_W), s_out


def _head_perm_rows(wa):
    d = wa.shape[1]
    return wa.reshape(NSA_KV_HEADS, NSA_GROUP, HEAD_DIM, d).transpose(1, 0, 2, 3).reshape(NSA_Q_W, d)


def _pack_compress(pos_emb, w1, w2):
    eye = jnp.eye(NSA_KV_HEADS, dtype=w1.dtype)
    pe = jnp.concatenate([pos_emb] * NSA_KV_HEADS, axis=2)
    w1p = w1.reshape(2, CMP_BLOCK, HEAD_DIM, HEAD_DIM)
    w1d = jnp.einsum('kpde,hg->kphdge', w1p, eye).reshape(2, CMP_BLOCK, LANES, LANES)
    w2d = jnp.einsum('kde,hg->khdge', w2, eye).reshape(2, LANES, LANES)
    return pe, w1d.astype(BF16), w2d.astype(BF16)


def kernel(x_prompt, x_sample, cache_nsa_kv, state_nsa_win, state_dn_S, state_dn_conv, page_table,
           ffn1_norm, ffn1_w_in, ffn1_w_out, mix_norm, w_in, nsa_cmp_pos, nsa_cmp_w1, nsa_cmp_w2,
           dn_conv_w, dn_A_log, dn_dt_bias, dn_out_norm, w_branch_a, w_branch_b, w_out,
           ffn2_norm, ffn2_w_in, ffn2_w_out, final_norm):
    b, seq = x_prompt.shape[:2]
    bs, ts = x_sample.shape[:2]
    depth = w_in.shape[0]
    n_phys = cache_nsa_kv.shape[1]
    past = page_table.shape[1] * PAGE_SIZE
    wbuf = state_nsa_win.shape[2]
    kvw = 4 * NSA_KV_W

    xp = x_prompt.reshape(b * seq, D_MODEL)
    xs = x_sample.reshape(bs * ts, D_MODEL)
    cos_p, sin_p = _rope_tables(jnp.arange(seq, dtype=jnp.int32))
    cos_s, sin_s = _rope_tables(jnp.tile(past + jnp.arange(ts, dtype=jnp.int32), bs))
    nc = (seq - CMP_BLOCK) // CMP_STRIDE + 1
    ovt_p = _overlap_t(seq // CMP_STRIDE, nc, seq // SEL_BLOCK, LANES)
    tables_p = _attn_bias_tables(seq)
    ns_s = -(-(past + ts) // SEL_BLOCK)
    ovt_s = _overlap_t(past // CMP_STRIDE, past // CMP_STRIDE - 1, ns_s, -(-ns_s // 16) * 16)
    rexp = _gate_expand_matrix()
    tri, trit = _tri_mats()
    cache = cache_nsa_kv.transpose(0, 1, 3, 4, 5, 2).reshape(depth, n_phys, kvw, PAGE_SIZE)
    win_state = state_nsa_win.transpose(0, 1, 3, 4, 5, 2).reshape(depth, bs, 2 * NSA_KV_W, wbuf)
    tm_p, tm_s = 512, bs * ts

    outs = [[] for _ in range(8)]
    rows_all = None
    for l in range(depth):
        f1 = (ffn1_norm[l][None], ffn1_w_in[l].astype(BF16), ffn1_w_out[l].astype(BF16))
        f2 = (ffn2_norm[l][None], ffn2_w_in[l].astype(BF16), ffn2_w_out[l].astype(BF16))
        w_mix = _pack_mix_weight(w_in[l])
        mrg = (_head_perm_rows(w_branch_a[l]).astype(BF16), w_branch_b[l].astype(BF16), w_out[l].astype(BF16))
        dn_params = (dn_conv_w[l], dn_A_log[l], dn_dt_bias[l], dn_out_norm[l])
        cmp_l = (nsa_cmp_pos[l], nsa_cmp_w1[l], nsa_cmp_w2[l])

        xp = _ffn(xp, *f1, tm_p)
        (qraw, qrot, small, qkv, zg, mg, rows_all, win_t, kcmp, vcmp, ksa, kw16, vst, vwt) = _mix(
            xp, mix_norm[l][None], w_mix, cos_p, sin_p, 256, seq, True, l, depth, rows_all)
        kc, vct = _compress_prompt(kcmp, vcmp, *_pack_compress(*cmp_l), b, seq)
        oa = _nsa_prompt(qraw, qrot, small, kc, vct, ksa, vst, kw16, vwt, ovt_p, tables_p, b, seq)
        ob, s_fin = _deltanet(qkv, small, zg, jnp.zeros((b, CONV_W - 1, DN_QKV_W), F32),
                              jnp.zeros((b, DN_HEADS, DN_DK, DN_DV), F32), dn_params, tri, trit, b, seq)
        xp = _merge(xp, oa, ob, mg, *mrg, tm_p)
        xp = _ffn(xp, *f2, tm_p)
        outs[1].append(win_t[:, :, seq - min(WINDOW, seq):])
        outs[2].append(s_fin)
        outs[3].append(qkv.reshape(b, seq, DN_QKV_W)[:, seq - (CONV_W - 1):])

        xs = _ffn(xs, *f1, tm_s)
        (qraw, qrot, small, qkv, zg, mg, rows, win) = _mix(
            xs, mix_norm[l][None], w_mix, cos_s, sin_s, tm_s, tm_s, False)
        oa = _nsa_sample(l, cache, page_table, _sample_rows(qraw, bs, ts), _sample_rows(qrot, bs, ts),
                         small.reshape(bs, ts, LANES), rows.reshape(bs, ts, kvw), win.reshape(bs, ts, 2 * NSA_KV_W),
                         win_state, *_pack_compress_sample(*cmp_l), ovt_s, rexp, past)
        oa = oa[:, :ts].reshape(bs * ts, NSA_Q_W).astype(BF16)
        ob, s_fin = _deltanet(qkv, small, zg, state_dn_conv[l], state_dn_S[l], dn_params, tri, trit, bs, ts)
        xs = _merge(xs, oa, ob, mg, *mrg, tm_s)
        xs = _ffn(xs, *f2, tm_s)
        outs[4].append(rows.reshape(bs, ts, 4, NSA_KV_HEADS, HEAD_DIM))
        win_all = jnp.concatenate([state_nsa_win[l], win.reshape(bs, ts, 2, NSA_KV_HEADS, HEAD_DIM)], axis=1)
        outs[5].append(win_all[:, ts:])
        outs[6].append(s_fin)
        conv_all = jnp.concatenate([state_dn_conv[l], qkv.reshape(bs, ts, DN_QKV_W)], axis=1)
        outs[7].append(conv_all[:, ts:])

    y_prompt = _final_norm(xp, final_norm[None], tm_p).reshape(b, seq, D_MODEL)
    y_sample = _final_norm(xs, final_norm[None], tm_s).reshape(bs, ts, D_MODEL)
    res = [rows_all] + [jnp.stack(o) for o in outs[1:]]
    for i, parts in ((0, 4), (1, 2)):
        r = res[i]
        res[i] = r.reshape(depth, b, parts, NSA_KV_HEADS, HEAD_DIM, r.shape[-1]).transpose(0, 1, 5, 2, 3, 4)
    return (y_prompt, y_sample) + tuple(res)
```

```python
import functools
import math

import jax
import jax.numpy as jnp
import numpy as np
from jax import lax
from jax.experimental import pallas as pl
from jax.experimental.pallas import tpu as pltpu

F32 = jnp.float32
BF16 = jnp.bfloat16

D_MODEL = 1024
DEPTH = 4
PAGE_SIZE = 128
NSA_HEADS = 8
NSA_KV_HEADS = 2
HEAD_DIM = 64
NSA_GROUP = NSA_HEADS // NSA_KV_HEADS
NSA_Q_W = NSA_HEADS * HEAD_DIM
NSA_KV_W = NSA_KV_HEADS * HEAD_DIM
CMP_BLOCK = 32
CMP_STRIDE = 16
SEL_BLOCK = 64
N_SEL = 16
WINDOW = 512
FORCE_BONUS = 1e6
ROPE_THETA = 10000.0
DN_HEADS = 4
DN_DK = 128
DN_DV = 128
DN_QK_W = DN_HEADS * DN_DK
DN_W = DN_HEADS * DN_DV
DN_QKV_W = 2 * DN_QK_W + DN_W
CONV_W = 4
D_FF = 2816
EPS = 1e-6

LANES = 128
VT_ROWS = HEAD_DIM + 16
VMEM_LIMIT = 56 * 1024 * 1024
NEG_BIG = -1e30

QPAD_W = NSA_HEADS * LANES
C_Q = 0
C_KV = C_Q + QPAD_W
C_SMALL = C_KV + 6 * NSA_KV_W
C_QKV = C_SMALL + LANES
C_ZG = C_QKV + DN_QKV_W
C_MERGE = C_ZG + DN_W
MIX_W = C_MERGE + 2 * D_MODEL
SM_GATE = 0
SM_A = 3 * NSA_HEADS
SM_B = SM_A + DN_HEADS


def _dot(a, b):
    return jnp.dot(a, b, preferred_element_type=F32)


def _dot_nt(a, b):
    return lax.dot_general(a, b, (((1,), (1,)), ((), ())), preferred_element_type=F32)


def _split3(a):
    hi = a.astype(BF16)
    r1 = a - hi.astype(F32)
    mid = r1.astype(BF16)
    lo = (r1 - mid.astype(F32)).astype(BF16)
    return hi, mid, lo


def _dot_exact_rhs(a, b01):
    hi, mid, lo = _split3(a)
    return _dot(hi, b01) + _dot(mid, b01) + _dot(lo, b01)


def _dot_exact_lhs(a01, b):
    hi, mid, lo = _split3(b)
    return _dot(a01, hi) + _dot(a01, mid) + _dot(a01, lo)


def _dot_exact_lhs_nt(a01, b):
    hi, mid, lo = _split3(b)
    return _dot_nt(a01, hi) + _dot_nt(a01, mid) + _dot_nt(a01, lo)


def _dot3(a, b):
    ah = a.astype(BF16)
    al = (a - ah.astype(F32)).astype(BF16)
    bh = b.astype(BF16)
    bl = (b - bh.astype(F32)).astype(BF16)
    return _dot(ah, bh) + (_dot(ah, bl) + _dot(al, bh))


def _rms(x, w):
    return x * lax.rsqrt(jnp.mean(x * x, axis=-1, keepdims=True) + EPS) * w


def _sigmoid(x):
    return 1.0 / (1.0 + jnp.exp(-x))


def _silu(x):
    return x * _sigmoid(x)


def _params(*sem):
    return pltpu.CompilerParams(dimension_semantics=sem, vmem_limit_bytes=VMEM_LIMIT)


def _const_spec(shape):
    nd = len(shape)
    return pl.BlockSpec(shape, lambda *_: (0,) * nd)


FF_TILE = 256


def _ffn_kernel(x_ref, nw_ref, wi_ref, wo_ref, o_ref, act_ref):
    x = x_ref[...]
    hb = _rms(x, nw_ref[...]).astype(BF16)
    for j in range(D_FF // FF_TILE):
        g = _dot(hb, wi_ref[:, j * FF_TILE:(j + 1) * FF_TILE])
        u = _dot(hb, wi_ref[:, D_FF + j * FF_TILE:D_FF + (j + 1) * FF_TILE])
        act_ref[:, j * FF_TILE:(j + 1) * FF_TILE] = (_silu(g) * u).astype(BF16)
    o_ref[...] = x + 0.5 * _dot(act_ref[...], wo_ref[...])


def _ffn(x, nw, wi, wo, tm):
    m = x.shape[0]
    return pl.pallas_call(
        _ffn_kernel,
        grid=(m // tm,),
        in_specs=[pl.BlockSpec((tm, D_MODEL), lambda i: (i, 0)),
                  _const_spec((1, D_MODEL)),
                  _const_spec((D_MODEL, 2 * D_FF)),
                  _const_spec((D_FF, D_MODEL))],
        out_specs=pl.BlockSpec((tm, D_MODEL), lambda i: (i, 0)),
        out_shape=jax.ShapeDtypeStruct((m, D_MODEL), F32),
        scratch_shapes=[pltpu.VMEM((tm, D_FF), BF16)],
        compiler_params=_params("parallel"), name="half_ffn",
    )(x, nw, wi, wo)


def _rope128(x, cos, sin):
    lane = lax.broadcasted_iota(jnp.int32, x.shape, 1)
    first = (lane % HEAD_DIM) < (HEAD_DIM // 2)
    swapped = jnp.where(first, pltpu.roll(x, LANES - HEAD_DIM // 2, 1), pltpu.roll(x, HEAD_DIM // 2, 1))
    return x * cos + swapped * sin


def _mix_kernel(seq_tiles, prompt, n_passthrough, x_ref, nw_ref, w_ref, cos_ref, sin_ref, *refs):
    qraw_ref, qrot_ref, small_ref, qkv_ref, zg_ref, merge_ref, *group_refs = refs[n_passthrough:]
    tm = x_ref.shape[0]
    hb = _rms(x_ref[...], nw_ref[...]).astype(BF16)
    cos = cos_ref[...]
    sin = sin_ref[...]
    scale = HEAD_DIM ** -0.5 * (math.log2(math.e) if prompt else 1.0)
    for j in range(NSA_HEADS):
        q = _dot(hb, w_ref[:, C_Q + j * LANES:C_Q + (j + 1) * LANES])
        qraw_ref[:, j * LANES:(j + 1) * LANES] = (q * scale).astype(BF16)
        qrot_ref[:, j * LANES:(j + 1) * LANES] = (_rope128(q, cos, sin) * scale).astype(BF16)
    kv = _dot(hb, w_ref[:, C_KV:C_KV + 6 * NSA_KV_W])
    k_cmp = kv[:, 0:LANES]
    v_cmp = kv[:, LANES:2 * LANES]
    k_sel = _rope128(kv[:, 2 * LANES:3 * LANES], cos, sin)
    v_sel = kv[:, 3 * LANES:4 * LANES]
    k_win = _rope128(kv[:, 4 * LANES:5 * LANES], cos, sin)
    v_win = kv[:, 5 * LANES:6 * LANES]
    if prompt:
        rows_t_ref, win_t_ref, kcmp_ref, vcmp_ref, ksa_ref, kw16_ref, vst_ref, vwt_ref = group_refs
        ones = jnp.ones((VT_ROWS - HEAD_DIM, tm), F32)
        for part, a in enumerate((k_cmp, v_cmp, k_sel, v_sel)):
            rows_t_ref[0, 0, part * LANES:(part + 1) * LANES, :] = a.T
        for part, a in enumerate((k_win, v_win)):
            win_t_ref[0, part * LANES:(part + 1) * LANES, :] = a.T
        for v, vt_ref in ((v_sel, vst_ref), (v_win, vwt_ref)):
            vt = v.T
            for k in range(NSA_KV_HEADS):
                vt_ref[k] = jnp.concatenate([vt[k * HEAD_DIM:(k + 1) * HEAD_DIM], ones], axis=0).astype(BF16)
        kcmp_ref[...] = k_cmp
        vcmp_ref[...] = v_cmp
        pos = (pl.program_id(0) % seq_tiles) * tm + lax.broadcasted_iota(jnp.int32, (tm, LANES), 0)
        lane = lax.broadcasted_iota(jnp.int32, (tm, LANES), 1)
        ksa_ref[:, 0:LANES] = k_sel.astype(BF16)
        ksa_ref[:, LANES:2 * LANES] = jnp.where(pos // SEL_BLOCK == lane, 1.0, 0.0).astype(BF16)
        kw16_ref[...] = k_win.astype(BF16)
    else:
        rows_ref, win_ref = group_refs
        for part, a in enumerate((k_cmp, v_cmp, k_sel, v_sel)):
            rows_ref[:, part * LANES:(part + 1) * LANES] = a
        win_ref[:, 0:LANES] = k_win
        win_ref[:, LANES:2 * LANES] = v_win
    small_ref[...] = _dot(hb, w_ref[:, C_SMALL:C_SMALL + LANES])
    qkv_ref[...] = _dot(hb, w_ref[:, C_QKV:C_QKV + DN_QKV_W])
    zg_ref[...] = _dot(hb, w_ref[:, C_ZG:C_ZG + DN_W])
    merge_ref[...] = _dot(hb, w_ref[:, C_MERGE:C_MERGE + 2 * D_MODEL])


def _mix(x, nw, w, cos, sin, tm, seq_len, prompt, layer=0, depth=1, rows_all=None):
    m = x.shape[0]
    seq_tiles = max(seq_len // tm, 1)
    row = lambda width: pl.BlockSpec((tm, width), lambda i: (i, 0))
    tab = pl.BlockSpec((tm, LANES), lambda i: (i % seq_tiles, 0))
    widths = [(QPAD_W, BF16), (QPAD_W, BF16), (LANES, F32), (DN_QKV_W, F32), (DN_W, F32), (2 * D_MODEL, F32)]
    inputs = [x, nw, w, cos, sin]
    in_specs = [row(D_MODEL), _const_spec((1, D_MODEL)), _const_spec((D_MODEL, MIX_W)), tab, tab]
    aliases = {}
    if prompt:
        nb = m // seq_len
        vt_spec = pl.BlockSpec((NSA_KV_HEADS, VT_ROWS, tm), lambda i: (0, 0, i))
        vt_shape = jax.ShapeDtypeStruct((NSA_KV_HEADS, VT_ROWS, m), BF16)
        g_widths = [(LANES, F32), (LANES, F32), (2 * LANES, BF16), (LANES, BF16)]
        g_specs = ([pl.BlockSpec((1, 1, 4 * NSA_KV_W, tm), lambda i: (layer, i // seq_tiles, 0, i % seq_tiles)),
                    pl.BlockSpec((1, 2 * NSA_KV_W, tm), lambda i: (i // seq_tiles, 0, i % seq_tiles))]
                   + [row(wd) for wd, _ in g_widths] + [vt_spec, vt_spec])
        g_shapes = ([jax.ShapeDtypeStruct((depth, nb, 4 * NSA_KV_W, seq_len), F32),
                     jax.ShapeDtypeStruct((nb, 2 * NSA_KV_W, seq_len), F32)]
                    + [jax.ShapeDtypeStruct((m, wd), dt) for wd, dt in g_widths] + [vt_shape, vt_shape])
        if rows_all is not None:
            aliases = {len(inputs): len(widths)}
            inputs.append(rows_all)
            in_specs.append(pl.BlockSpec(memory_space=pl.ANY))
    else:
        g_widths = [(4 * NSA_KV_W, F32), (2 * NSA_KV_W, F32)]
        g_specs = [row(wd) for wd, _ in g_widths]
        g_shapes = [jax.ShapeDtypeStruct((m, wd), dt) for wd, dt in g_widths]
    return pl.pallas_call(
        functools.partial(_mix_kernel, seq_tiles, prompt, len(aliases)),
        grid=(m // tm,),
        in_specs=in_specs,
        out_specs=[row(wd) for wd, _ in widths] + g_specs,
        out_shape=[jax.ShapeDtypeStruct((m, wd), dt) for wd, dt in widths] + g_shapes,
        input_output_aliases=aliases,
        compiler_params=_params("parallel"), name="mix_in",
    )(*inputs)


def _merge_kernel(x_ref, oa_ref, ob_ref, mg_ref, wa_ref, wb_ref, wo_ref, o_ref):
    mg = mg_ref[...]
    ya = _dot(oa_ref[...], wa_ref[...])
    yb = _dot(ob_ref[...], wb_ref[...])
    y = _sigmoid(mg[:, :D_MODEL]) * ya + _sigmoid(mg[:, D_MODEL:]) * yb
    o_ref[...] = x_ref[...] + _dot(y.astype(BF16), wo_ref[...])


def _merge(x, oa, ob, mg, wa, wb, wo, tm):
    m = x.shape[0]
    row = lambda width: pl.BlockSpec((tm, width), lambda i: (i, 0))
    return pl.pallas_call(
        _merge_kernel,
        grid=(m // tm,),
        in_specs=[row(D_MODEL), row(NSA_Q_W), row(DN_W), row(2 * D_MODEL),
                  _const_spec((NSA_Q_W, D_MODEL)), _const_spec((DN_W, D_MODEL)),
                  _const_spec((D_MODEL, D_MODEL))],
        out_specs=row(D_MODEL),
        out_shape=jax.ShapeDtypeStruct((m, D_MODEL), F32),
        compiler_params=_params("parallel"), name="branch_merge",
    )(x, oa, ob, mg, wa, wb, wo)


def _norm_kernel(x_ref, w_ref, o_ref):
    o_ref[...] = _rms(x_ref[...], w_ref[...])


def _final_norm(x, w, tm):
    m = x.shape[0]
    row = pl.BlockSpec((tm, D_MODEL), lambda i: (i, 0))
    return pl.pallas_call(
        _norm_kernel, grid=(m // tm,), in_specs=[row, _const_spec((1, D_MODEL))], out_specs=row,
        out_shape=jax.ShapeDtypeStruct((m, D_MODEL), F32), compiler_params=_params("parallel"),
    )(x, w)


def _compress_kernel(kcmp_ref, vcmp_ref, pe_ref, w1_ref, w2_ref, kc_ref, vct_ref):
    ns = kc_ref.shape[1]
    res = []
    for part, rows_ref in enumerate((kcmp_ref, vcmp_ref)):
        sa = jnp.zeros((ns, LANES), F32)
        sb = jnp.zeros((ns, LANES), F32)
        for p in range(CMP_STRIDE):
            x = rows_ref[pl.ds(p, ns, stride=CMP_STRIDE), :]
            sa = sa + _dot((x + pe_ref[part, p:p + 1, :]).astype(BF16), w1_ref[part, p])
            sb = sb + _dot((x + pe_ref[part, CMP_STRIDE + p:CMP_STRIDE + p + 1, :]).astype(BF16),
                           w1_ref[part, CMP_STRIDE + p])
        pre = sa + pltpu.roll(sb, ns - 1, 0)
        res.append(_dot(jax.nn.gelu(pre).astype(BF16), w2_ref[part]))
    kc_ref[0] = res[0].astype(BF16)
    vt = res[1].T
    ones = jnp.ones((VT_ROWS - HEAD_DIM, ns), F32)
    for k in range(NSA_KV_HEADS):
        vct_ref[0, k] = jnp.concatenate([vt[k * HEAD_DIM:(k + 1) * HEAD_DIM], ones], axis=0).astype(BF16)


def _compress_prompt(kcmp, vcmp, pe, w1, w2, b, seq):
    ns = seq // CMP_STRIDE
    tok = pl.BlockSpec((seq, LANES), lambda i: (i, 0))
    return pl.pallas_call(
        _compress_kernel,
        grid=(b,),
        in_specs=[tok, tok, _const_spec(pe.shape), _const_spec(w1.shape), _const_spec(w2.shape)],
        out_specs=[pl.BlockSpec((1, ns, LANES), lambda i: (i, 0, 0)),
                   pl.BlockSpec((1, NSA_KV_HEADS, VT_ROWS, ns), lambda i: (i, 0, 0, 0))],
        out_shape=[jax.ShapeDtypeStruct((b, ns, LANES), BF16),
                   jax.ShapeDtypeStruct((b, NSA_KV_HEADS, VT_ROWS, ns), BF16)],
        compiler_params=_params("parallel"), name="nsa_compress",
    )(kcmp, vcmp, pe, w1, w2)


TQ = 256
KT = 512
WIN_KEYS = WINDOW + TQ


def _softmax_rows(s, mask):
    s = jnp.where(mask, s, NEG_BIG)
    e = jnp.where(mask, jnp.exp(s - jnp.max(s, axis=-1, keepdims=True)), 0.0)
    return e * (1.0 / jnp.maximum(jnp.sum(e, axis=-1, keepdims=True), 1e-30))


def _topk_bias_t(imp_t, n_cand, n_sel):
    nr = imp_t.shape[0] // 8
    xs = [imp_t[8 * r:8 * r + 8, :] for r in range(nr)]
    cnt = [jnp.zeros_like(xs[0]) for _ in range(nr)]
    sub = lax.broadcasted_iota(jnp.int32, xs[0].shape, 0)
    for j in range(n_cand):
        rj = j // 8
        row = xs[rj][j % 8:j % 8 + 1, :]
        for r in range(nr):
            if r < rj:
                inc = jnp.where(row > xs[r], 1.0, 0.0)
            elif r > rj:
                inc = jnp.where(row >= xs[r], 1.0, 0.0)
            else:
                inc = jnp.where(sub > j % 8, jnp.where(row >= xs[r], 1.0, 0.0),
                                jnp.where(row > xs[r], 1.0, 0.0))
            cnt[r] = cnt[r] + inc
    return jnp.concatenate([jnp.where(c < n_sel, 0.0, NEG_BIG) for c in cnt], axis=0)


def _nsa_prompt_kernel(qraw_ref, qrot_ref, small_ref, kc_ref, vct_ref, ksa_ref, vst_ref, win_ref, vwt_ref,
                       ovt_ref, cbias_ref, dbias_ref, wbias_ref, o_ref):
    seq = ksa_ref.shape[1]
    ns = seq // SEL_BLOCK
    s0 = pl.multiple_of(pl.program_id(1) * TQ, TQ)
    cols = NSA_GROUP * TQ
    tile = lambda a: jnp.concatenate([a] * NSA_GROUP, axis=1)
    qpos_row = s0 + lax.broadcasted_iota(jnp.int32, (1, cols), 1) % TQ
    gates_t = _sigmoid(small_ref[...]).T

    blk_t = lax.broadcasted_iota(jnp.int32, (ns, TQ), 0)
    qpos_t = s0 + lax.broadcasted_iota(jnp.int32, (ns, TQ), 1)
    cur_t = qpos_t // SEL_BLOCK
    valid_t = blk_t * SEL_BLOCK <= qpos_t
    forced_t = (blk_t == 0) | (blk_t == cur_t) | (blk_t == cur_t - 1)
    cbias = tile(cbias_ref[0])
    dbias = tile(dbias_ref[0])
    wbias = tile(wbias_ref[0])

    def split_sum(aug):
        return aug[0:HEAD_DIM], aug[HEAD_DIM:HEAD_DIM + 1]

    kv_heads = range(NSA_KV_HEADS)
    w0 = pl.multiple_of(jnp.maximum(s0 - WINDOW, 0), TQ)
    o_c, o_w, q_sel = [], [], []
    for k in kv_heads:
        def q_t(ref):
            return jnp.concatenate(
                [ref[:, (k * NSA_GROUP + g) * LANES:(k * NSA_GROUP + g + 1) * LANES].astype(F32).T.astype(BF16)
                 for g in range(NSA_GROUP)], axis=1)
        qt_raw = q_t(qraw_ref)
        qt_rot = q_t(qrot_ref)

        s = _dot(kc_ref[0], qt_raw) + cbias
        e = jnp.exp2(s - jnp.max(s, axis=0, keepdims=True))
        acc, l = split_sum(_dot(vct_ref[0, k], e.astype(BF16)))
        inv = jnp.where(qpos_row >= CMP_BLOCK - 1, 1.0 / l, 0.0)
        o_c.append(acc * inv)
        p = e * inv
        p_sum = p[:, 0:TQ] + p[:, TQ:2 * TQ] + p[:, 2 * TQ:3 * TQ] + p[:, 3 * TQ:4 * TQ]
        imp_t = _dot_exact_lhs(ovt_ref[...], p_sum)[0:ns]
        imp_t = jnp.where(valid_t, imp_t + jnp.where(forced_t, FORCE_BONUS, 0.0), -jnp.inf)
        bias_t = _topk_bias_t(imp_t, ns, min(N_SEL, ns))
        if ns < LANES:
            bias_t = jnp.concatenate([bias_t, jnp.zeros((LANES - ns, TQ), F32)], axis=0)
        q_sel.append(jnp.concatenate([qt_rot, tile(bias_t).astype(BF16)], axis=0))

        s = _dot(win_ref[0, pl.ds(w0, WIN_KEYS), :], qt_rot) + wbias
        e = jnp.exp2(s - jnp.max(s, axis=0, keepdims=True)).astype(BF16)
        acc, l = split_sum(_dot(vwt_ref[k, :, pl.ds(w0, WIN_KEYS)], e))
        o_w.append(acc * (1.0 / l))

    def sel_step(c, carry, diagonal):
        k0 = pl.multiple_of(c * KT, KT)
        keys = ksa_ref[0, pl.ds(k0, KT), :]
        out = []
        for k in kv_heads:
            m, acc = carry[k]
            s = _dot(keys, q_sel[k])
            if diagonal:
                s = s + dbias
            m_new = jnp.maximum(m, jnp.max(s, axis=0, keepdims=True))
            p = jnp.exp2(s - m_new).astype(BF16)
            out.append((m_new, jnp.exp2(m - m_new) * acc + _dot(vst_ref[k, :, pl.ds(k0, KT)], p)))
        return tuple(out)
    init = tuple((jnp.full((1, cols), NEG_BIG, F32), jnp.zeros((VT_ROWS, cols), F32)) for _ in kv_heads)
    n_full = s0 // KT
    carry = lax.fori_loop(0, n_full, functools.partial(sel_step, diagonal=False), init)
    carry = sel_step(n_full, carry, True)
    res = []
    for k in kv_heads:
        acc, l = split_sum(carry[k][1])
        res.append((o_c[k], acc * (1.0 / l), o_w[k]))

    for g in range(NSA_GROUP):
        halves = []
        for k in range(NSA_KV_HEADS):
            r = SM_GATE + (k * NSA_GROUP + g) * 3
            halves.append(sum(gates_t[r + br:r + br + 1, :] * res[k][br][:, g * TQ:(g + 1) * TQ]
                              for br in range(3)))
        o_ref[:, g * LANES:(g + 1) * LANES] = jnp.concatenate(halves, axis=0).T.astype(BF16)


def _attn_bias_tables(seq):
    r = np.arange(TQ)[None, None, :]
    blocks = np.arange(seq // TQ)[:, None, None]
    c = np.arange(seq // CMP_STRIDE)[None, :, None]
    cb = c * CMP_STRIDE + CMP_BLOCK - 1 <= blocks * TQ + r
    kk = np.arange(KT)[None, :, None]
    off = np.arange(KT // TQ)[:, None, None] * TQ
    db = kk <= off + r
    kw = np.arange(WIN_KEYS)[None, :, None]
    nw = WINDOW // TQ
    j = np.arange(nw + 1)[:, None, None]
    qpos = j * TQ + r
    kpos = np.where(j < nw, kw, qpos - r - WINDOW + kw)
    wb = (kpos <= qpos) & (kpos > qpos - WINDOW)
    f = lambda m: jnp.asarray(np.where(m, 0.0, NEG_BIG).astype(np.float32))
    return f(cb), f(db), f(wb)


def _nsa_prompt(qraw, qrot, small, kc, vct, ksa, vst, kw16, vwt, ovt, tables, b, seq):
    nc = kc.shape[1]
    nq = seq // TQ
    cb, db, wb = tables
    tok = lambda width: pl.BlockSpec((TQ, width), lambda i, j: (i * nq + j, 0))
    per_b = lambda width: pl.BlockSpec((1, seq, width), lambda i, j: (i, 0, 0))
    vt_spec = pl.BlockSpec((NSA_KV_HEADS, VT_ROWS, seq), lambda i, j: (0, 0, i))
    return pl.pallas_call(
        _nsa_prompt_kernel,
        grid=(b, nq),
        in_specs=[tok(QPAD_W), tok(QPAD_W), tok(LANES),
                  pl.BlockSpec((1, nc, LANES), lambda i, j: (i, 0, 0)),
                  pl.BlockSpec((1, NSA_KV_HEADS, VT_ROWS, nc), lambda i, j: (i, 0, 0, 0)),
                  per_b(2 * LANES), vt_spec, per_b(LANES), vt_spec,
                  _const_spec(ovt.shape),
                  pl.BlockSpec((1, nc, TQ), lambda i, j: (j, 0, 0)),
                  pl.BlockSpec((1, KT, TQ), lambda i, j: (j % (KT // TQ), 0, 0)),
                  pl.BlockSpec((1, WIN_KEYS, TQ), lambda i, j: (jnp.minimum(j, WINDOW // TQ), 0, 0))],
        out_specs=tok(NSA_Q_W),
        out_shape=jax.ShapeDtypeStruct((b * seq, NSA_Q_W), BF16),
        compiler_params=_params("parallel", "parallel"), name="nsa_prompt",
    )(qraw, qrot, small, kc, vct, ksa.reshape(b, seq, 2 * LANES), vst, kw16.reshape(b, seq, LANES), vwt,
      ovt, cb, db, wb)


PAGES_PER_STEP = 16
SEG_PER_PAGE = PAGE_SIZE // CMP_STRIDE
SEG_W = CMP_STRIDE * 2 * NSA_KV_W


def _nsa_sample_kernel(ts, past, pt_ref, *refs):
    pages = refs[:PAGES_PER_STEP]
    (qraw_ref, qrot_ref, small_ref, rows_ref, winnew_ref, winst_ref, pe_ref, w1_ref, w2_ref,
     ovt_ref, rexp_ref, o_ref, seg_ref, ksat_ref, vst_ref, sm_ref, stage_ref, bias_ref) = refs[PAGES_PER_STEP:]
    del pt_ref
    pg = pl.program_id(1)
    nseg = seg_ref.shape[0]
    nrows = NSA_HEADS * ts
    kt = NSA_KV_HEADS * ts
    half_w = NSA_KV_HEADS * CMP_STRIDE * HEAD_DIM

    @pl.when((pl.program_id(0) == 0) & (pg == 0))
    def _():
        for part in range(2):
            hi, mid, lo = _split3(pe_ref[part])
            w = w1_ref[part]
            bias_ref[part] = _dot(hi, w) + _dot(mid, w) + _dot(lo, w)

    for e in range(PAGES_PER_STEP):
        for part in range(2):
            xt = pages[e][0, 0, part * LANES:(part + 1) * LANES, :].T
            for j in range(PAGE_SIZE // 8):
                start = (j % 2) * (PAGE_SIZE // 2) + j // 2
                stage_ref[e, part, pl.ds(start, 8, stride=SEG_PER_PAGE), :] = xt[8 * j:8 * j + 8, :]
        k0 = pl.multiple_of((pg * PAGES_PER_STEP + e) * PAGE_SIZE, PAGE_SIZE)
        ksat_ref[0:LANES, pl.ds(k0, PAGE_SIZE)] = pages[e][0, 0, 2 * LANES:3 * LANES, :].astype(BF16)
        vst_ref[:, pl.ds(k0, PAGE_SIZE)] = pages[e][0, 0, 3 * LANES:4 * LANES, :].astype(BF16)

        @pl.when(pl.program_id(0) == 0)
        def _():
            kpos = k0 + lax.broadcasted_iota(jnp.int32, (LANES, PAGE_SIZE), 1)
            blk = lax.broadcasted_iota(jnp.int32, (LANES, PAGE_SIZE), 0)
            ksat_ref[LANES:2 * LANES, pl.ds(k0, PAGE_SIZE)] = jnp.where(
                kpos // SEL_BLOCK == blk, 1.0, 0.0).astype(BF16)
    for pair in range(PAGES_PER_STEP // 2):
        r0 = pl.multiple_of(pg * (PAGES_PER_STEP * SEG_PER_PAGE) + pair * 2 * SEG_PER_PAGE, 2 * SEG_PER_PAGE)
        for part in range(2):
            for p in range(CMP_STRIDE):
                x = jnp.concatenate([stage_ref[2 * pair + e, part, p * SEG_PER_PAGE:(p + 1) * SEG_PER_PAGE, :]
                                     for e in range(2)], axis=0)
                c0 = part * half_w + p * LANES
                seg_ref[pl.ds(r0, 2 * SEG_PER_PAGE), c0:c0 + LANES] = x.astype(BF16)

    @pl.when(pg == pl.num_programs(1) - 1)
    def _():
        ns_past = past // SEL_BLOCK
        nc = nseg - 1
        q_raw = qraw_ref[0]
        q_rot = qrot_ref[0]
        q_rot_f = q_rot.astype(F32)
        row = lax.broadcasted_iota(jnp.int32, (nrows, 1), 0)
        t_row = row % ts
        kcvc = []
        for part in range(2):
            res = _dot(seg_ref[:, part * half_w:(part + 1) * half_w], w1_ref[part])
            b8 = bias_ref[part]
            pre = (res[:, 0:LANES] + b8[0:1, 0:LANES]
                   + pltpu.roll(res[:, LANES:2 * LANES] + b8[1:2, LANES:2 * LANES], nseg - 1, 0))
            kcvc.append(_dot(jax.nn.gelu(pre).astype(BF16), w2_ref[part]).astype(BF16))
        s_c = _dot_nt(q_raw, kcvc[0])
        p_c = _softmax_rows(s_c, lax.broadcasted_iota(jnp.int32, (nrows, nseg), 1) < nc)
        o_c = _dot(p_c.astype(BF16), kcvc[1])
        p_sum = p_c[0:kt] + p_c[kt:2 * kt] + p_c[2 * kt:3 * kt] + p_c[3 * kt:4 * kt]
        p_sum = jnp.concatenate([p_sum, jnp.zeros((LANES - kt, nseg), F32)], axis=0)
        nb = ovt_ref.shape[0]
        imp_t = _dot_exact_lhs_nt(ovt_ref[...], p_sum)
        blk_t = lax.broadcasted_iota(jnp.int32, (nb, LANES), 0)
        qpos_t = past + lax.broadcasted_iota(jnp.int32, (nb, LANES), 1) % ts
        cur_t = qpos_t // SEL_BLOCK
        forced_t = (blk_t == 0) | (blk_t == cur_t) | (blk_t == cur_t - 1)
        imp_t = jnp.where(blk_t * SEL_BLOCK <= qpos_t, imp_t + jnp.where(forced_t, FORCE_BONUS, 0.0), -jnp.inf)
        bias_t = _topk_bias_t(imp_t, -(-(past + ts) // SEL_BLOCK), N_SEL)[0:ns_past]
        bias = bias_t.T[0:kt]
        q_sel = jnp.concatenate([q_rot, jnp.concatenate([bias] * NSA_GROUP, axis=0).astype(BF16)], axis=1)

        def attend(s_past, v_past16_t, k_new, v_new, mask_past):
            if mask_past is not None:
                s_past = jnp.where(mask_past, s_past, NEG_BIG)
            s_new = [jnp.where(t_row >= j, jnp.sum(q_rot_f * k_new[j:j + 1, :], axis=-1, keepdims=True), NEG_BIG)
                     for j in range(ts)]
            m = jnp.max(s_past, axis=-1, keepdims=True)
            for s in s_new:
                m = jnp.maximum(m, s)
            p_past = jnp.exp(s_past - m)
            if mask_past is not None:
                p_past = jnp.where(mask_past, p_past, 0.0)
            acc = _dot_nt(p_past.astype(BF16), v_past16_t)
            l = jnp.sum(p_past, axis=-1, keepdims=True)
            for j, s in enumerate(s_new):
                p = jnp.where(t_row >= j, jnp.exp(s - m), 0.0)
                acc = acc + p * v_new[j:j + 1, :]
                l = l + p
            return acc * (1.0 / l)

        new = rows_ref[0]
        o_s = attend(_dot(q_sel, ksat_ref[...]), vst_ref[...], new[:, 2 * NSA_KV_W:3 * NSA_KV_W],
                     new[:, 3 * NSA_KV_W:4 * NSA_KV_W], None)
        wbuf = winst_ref.shape[3]
        wnew = winnew_ref[0]
        i_w = lax.broadcasted_iota(jnp.int32, (nrows, wbuf), 1)
        dist = t_row + wbuf - i_w
        o_w = attend(_dot(q_rot, winst_ref[0, 0, 0:LANES, :].astype(BF16)),
                     winst_ref[0, 0, LANES:2 * LANES, :].astype(BF16),
                     wnew[:, 0:LANES], wnew[:, LANES:2 * LANES],
                     (dist < WINDOW) & (past - wbuf + i_w >= 0))

        sm_ref[...] = jnp.zeros_like(sm_ref)
        sm_ref[0:ts, :] = small_ref[0]
        gexp = _dot_exact_rhs(_sigmoid(sm_ref[...]), rexp_ref[...])[0:kt]
        low = lax.broadcasted_iota(jnp.int32, (kt, LANES), 1) < HEAD_DIM
        for g in range(NSA_GROUP):
            out = jnp.zeros((kt, LANES), F32)
            for br, o_br in enumerate((o_c, o_s, o_w)):
                og = o_br[g * kt:(g + 1) * kt]
                both = jnp.where(low, og, pltpu.roll(og, kt - ts, 0))
                out = out + gexp[:, br * NSA_Q_W + g * LANES:br * NSA_Q_W + (g + 1) * LANES] * both
            o_ref[0, :, g * LANES:(g + 1) * LANES] = out


def _nsa_sample(layer, cache, page_table, qraw_s, qrot_s, small_s, rows_s, winnew_s, win_state,
                pe, w1, w2, ovt, rexp, past):
    b, ts = small_s.shape[:2]
    n_pages = page_table.shape[1]
    assert n_pages % PAGES_PER_STEP == 0 and ts * NSA_KV_HEADS == 8 and past == n_pages * PAGE_SIZE
    nseg = past // CMP_STRIDE
    wbuf = win_state.shape[3]
    kw = 4 * NSA_KV_W

    def page_spec(e):
        return pl.BlockSpec((1, 1, 4 * PAGE_SIZE, LANES),
                            lambda i, g, pt: (layer, pt[i, g * PAGES_PER_STEP + e], 0, 0))
    per_b = lambda shape: pl.BlockSpec((1,) + shape, lambda i, g, pt: (i,) + (0,) * len(shape))
    const = lambda shape: pl.BlockSpec(shape, lambda i, g, pt: (0,) * len(shape))
    grid_spec = pltpu.PrefetchScalarGridSpec(
        num_scalar_prefetch=1,
        grid=(b, n_pages // PAGES_PER_STEP),
        in_specs=[page_spec(e) for e in range(PAGES_PER_STEP)] + [
            per_b((NSA_HEADS * ts, LANES)), per_b((NSA_HEADS * ts, LANES)), per_b((ts, LANES)),
            per_b((ts, kw)), per_b((ts, 2 * NSA_KV_W)),
            pl.BlockSpec((1, 1, 2 * NSA_KV_W, wbuf), lambda i, g, pt: (layer, i, 0, 0)),
            const(pe.shape), const(w1.shape), const(w2.shape), const(ovt.shape), const(rexp.shape)],
        out_specs=per_b((NSA_KV_HEADS * ts, NSA_Q_W)),
        scratch_shapes=[pltpu.VMEM((nseg, SEG_W), BF16),
                        pltpu.VMEM((2 * LANES, past), BF16), pltpu.VMEM((LANES, past), BF16),
                        pltpu.VMEM((2 * NSA_KV_HEADS * ts, LANES), F32),
                        pltpu.VMEM((PAGES_PER_STEP, 2, PAGE_SIZE, LANES), F32),
                        pltpu.VMEM((2, 8, 2 * LANES), F32)])
    return pl.pallas_call(
        functools.partial(_nsa_sample_kernel, ts, past),
        grid_spec=grid_spec,
        out_shape=jax.ShapeDtypeStruct((b, NSA_KV_HEADS * ts, NSA_Q_W), F32),
        compiler_params=_params("arbitrary", "arbitrary"), name="nsa_sample",
    )(page_table, *([cache] * PAGES_PER_STEP), qraw_s, qrot_s, small_s, rows_s, winnew_s, win_state,
      pe, w1, w2, ovt, rexp)


DN_C = 128
DN_CHUNKS_PER_STEP = 2
HALO = 8


def _softplus(x):
    return jnp.maximum(x, 0.0) + jnp.log(1.0 + jnp.exp(-jnp.abs(x)))


def _l2n(x):
    return x * lax.rsqrt(jnp.sum(x * x, axis=-1, keepdims=True) + EPS)


def _dn_prep_kernel(n_valid, qkv_ref, halo_ref, prefix_ref, small_ref, cw_ref, alog_ref, dtb_ref,
                    tri_ref, u_ref, w_ref, qd_ref, qk_ref, kdt_ref, gl_ref, xcat_ref, sm_ref):
    n = pl.program_id(1)
    cps = kdt_ref.shape[0]
    rows = cps * DN_C
    if n_valid == DN_C:
        xcat_ref[HALO:HALO + rows, :] = qkv_ref[...]
        small = small_ref[...]

        @pl.when(n == 0)
        def _():
            xcat_ref[0:HALO, :] = prefix_ref[0]

        @pl.when(n > 0)
        def _():
            xcat_ref[0:HALO, :] = halo_ref[...]
    else:
        xcat_ref[...] = jnp.zeros_like(xcat_ref)
        xcat_ref[0:HALO, :] = prefix_ref[0]
        xcat_ref[HALO:HALO + n_valid, :] = qkv_ref[0]
        sm_ref[...] = jnp.zeros_like(sm_ref)
        sm_ref[0:n_valid, :] = small_ref[0]
        small = sm_ref[...]

    conv = jnp.zeros((rows, DN_QKV_W), F32)
    for i in range(CONV_W):
        conv = conv + xcat_ref[HALO - (CONV_W - 1) + i:HALO - (CONV_W - 1) + i + rows, :] * cw_ref[i:i + 1, :]
    conv = _silu(conv)

    ii = lax.broadcasted_iota(jnp.int32, (DN_C, DN_C), 0)
    jj = lax.broadcasted_iota(jnp.int32, (DN_C, DN_C), 1)
    g_rows = -jnp.exp(alog_ref[...]) * _softplus(small + dtb_ref[...])
    beta_rows = _sigmoid(small)
    chains = [(c, h) for c in range(cps) for h in range(DN_HEADS)]
    gcol, beta, decay = {}, {}, {}
    for c in range(cps):
        g_all = g_rows[c * DN_C:(c + 1) * DN_C]
        beta_all = beta_rows[c * DN_C:(c + 1) * DN_C]
        if n_valid != DN_C:
            g_all = jnp.where(ii < n_valid, g_all, 0.0)
            beta_all = jnp.where(ii < n_valid, beta_all, 0.0)
        gc_all = _dot_exact_lhs(tri_ref[...], g_all)
        gr_all = gc_all.T
        for h in range(DN_HEADS):
            gcol[c, h] = gc_all[:, SM_A + h:SM_A + h + 1]
            beta[c, h] = beta_all[:, SM_B + h:SM_B + h + 1]
            decay[c, h] = jnp.where(ii >= jj, jnp.exp(gcol[c, h] - gr_all[SM_A + h:SM_A + h + 1, :]), 0.0)
    tok = lambda c: slice(c * DN_C, (c + 1) * DN_C)
    q = {(c, h): _l2n(conv[tok(c), h * DN_DK:(h + 1) * DN_DK]) * DN_DK ** -0.5 for c, h in chains}
    k = {(c, h): _l2n(conv[tok(c), DN_QK_W + h * DN_DK:DN_QK_W + (h + 1) * DN_DK]) for c, h in chains}
    v = {(c, h): conv[tok(c), 2 * DN_QK_W + h * DN_DV:2 * DN_QK_W + (h + 1) * DN_DV] for c, h in chains}
    kb = {i: k[i] * beta[i] for i in chains}
    k16 = {i: k[i].astype(BF16) for i in chains}
    eg = {i: jnp.exp(gcol[i]) for i in chains}
    a = {i: jnp.where(ii > jj, _dot_nt(kb[i].astype(BF16), k16[i]) * decay[i], 0.0) for i in chains}
    t = {i: jnp.where(ii == jj, 1.0, 0.0) - a[i] for i in chains}
    for _ in range(max(1, math.ceil(math.log2(n_valid))) - 1):
        a = {i: _dot3(a[i], a[i]) for i in chains}
        t = {i: t[i] + _dot3(t[i], a[i]) for i in chains}
    x = {i: _dot3(t[i], jnp.concatenate([v[i] * beta[i], kb[i] * eg[i]], axis=1)) for i in chains}
    for c, h in chains:
        i = (c, h)
        u_ref[tok(c), h * DN_DV:(h + 1) * DN_DV] = x[i][:, 0:DN_DV]
        w_ref[tok(c), h * DN_DK:(h + 1) * DN_DK] = x[i][:, DN_DV:DN_DV + DN_DK].astype(BF16)
        qk = jnp.where(ii >= jj, _dot_nt(q[i].astype(BF16), k16[i]) * decay[i], 0.0)
        qk_ref[tok(c), h * DN_C:(h + 1) * DN_C] = qk.astype(BF16)
        qd_ref[tok(c), h * DN_DK:(h + 1) * DN_DK] = (q[i] * eg[i]).astype(BF16)
        glast = gcol[i][DN_C - 1:DN_C, :]
        kdt_ref[c, h * DN_DK:(h + 1) * DN_DK, :] = (k[i] * jnp.exp(glast - gcol[i])).T.astype(BF16)
        gl_ref[c, h:h + 1, :] = jnp.broadcast_to(jnp.exp(glast), (1, LANES))
    for c in range(cps):
        gl_ref[c, DN_HEADS:, :] = jnp.zeros((HALO - DN_HEADS, LANES), F32)


def _dn_prep(qkv, prefix8, small, cw, alog_row, dtb_row, tri, b, nch, n_valid):
    rows = b * nch * DN_C
    cps = DN_CHUNKS_PER_STEP if (n_valid == DN_C and nch % DN_CHUNKS_PER_STEP == 0) else 1
    steps = nch // cps
    blk = cps * DN_C
    if n_valid == DN_C:
        qkv_spec = pl.BlockSpec((blk, DN_QKV_W), lambda i, n: (i * steps + n, 0))
        halo_spec = pl.BlockSpec((HALO, DN_QKV_W),
                                 lambda i, n: (jnp.maximum((i * steps + n) * (blk // HALO) - 1, 0), 0))
        small_spec = pl.BlockSpec((blk, LANES), lambda i, n: (i * steps + n, 0))
        qkv_in, halo_in, small_in = qkv, qkv, small
    else:
        qkv_in = qkv.reshape(b, n_valid, DN_QKV_W)
        small_in = small.reshape(b, n_valid, LANES)
        halo_in = prefix8
        qkv_spec = pl.BlockSpec((1, n_valid, DN_QKV_W), lambda i, n: (i, 0, 0))
        halo_spec = pl.BlockSpec((1, HALO, DN_QKV_W), lambda i, n: (i, 0, 0))
        small_spec = pl.BlockSpec((1, n_valid, LANES), lambda i, n: (i, 0, 0))
    tok = lambda width: pl.BlockSpec((blk, width), lambda i, n: (i * steps + n, 0))
    return pl.pallas_call(
        functools.partial(_dn_prep_kernel, n_valid),
        grid=(b, steps),
        in_specs=[qkv_spec, halo_spec, pl.BlockSpec((1, HALO, DN_QKV_W), lambda i, n: (i, 0, 0)), small_spec,
                  _const_spec((CONV_W, DN_QKV_W)), _const_spec((1, LANES)), _const_spec((1, LANES)),
                  _const_spec((DN_C, DN_C))],
        out_specs=[tok(DN_W), tok(DN_QK_W), tok(DN_QK_W), tok(DN_HEADS * DN_C),
                   pl.BlockSpec((cps, DN_QK_W, DN_C), lambda i, n: (i * steps + n, 0, 0)),
                   pl.BlockSpec((cps, HALO, LANES), lambda i, n: (i * steps + n, 0, 0))],
        out_shape=[jax.ShapeDtypeStruct((rows, DN_W), F32), jax.ShapeDtypeStruct((rows, DN_QK_W), BF16),
                   jax.ShapeDtypeStruct((rows, DN_QK_W), BF16), jax.ShapeDtypeStruct((rows, DN_HEADS * DN_C), BF16),
                   jax.ShapeDtypeStruct((b * nch, DN_QK_W, DN_C), BF16),
                   jax.ShapeDtypeStruct((b * nch, HALO, LANES), F32)],
        scratch_shapes=[pltpu.VMEM((HALO + blk, DN_QKV_W), F32), pltpu.VMEM((DN_C, LANES), F32)],
        compiler_params=_params("parallel", "parallel"), name="dn_prep",
    )(qkv_in, halo_in, prefix8, small_in, cw, alog_row, dtb_row, tri)


def _dn_scan_kernel(n_valid, u_ref, w_ref, qd_ref, qk_ref, kdt_ref, gl_ref, zg_ref, s0_ref, nw_ref,
                    o_ref, s_ref):
    @pl.when(pl.program_id(1) == 0)
    def _():
        s_ref[...] = s0_ref[...]

    for b in range(u_ref.shape[0]):
        for h in range(DN_HEADS):
            cs = slice(h * DN_DK, (h + 1) * DN_DK)
            s = s_ref[b, h]
            s16 = s.astype(BF16)
            v_new = u_ref[b, :, cs] - _dot(w_ref[b, :, cs], s16)
            v16 = v_new.astype(BF16)
            o = _dot(qd_ref[b, :, cs], s16) + _dot(qk_ref[b, :, cs], v16)
            s_ref[b, h] = s * gl_ref[b, h:h + 1, :] + _dot(kdt_ref[b, cs, :], v16)
            on = o * lax.rsqrt(jnp.mean(o * o, axis=-1, keepdims=True) + EPS) * nw_ref[...]
            o_ref[b, :, cs] = (on[0:n_valid] * _silu(zg_ref[b, :, cs])).astype(BF16)


def _dn_scan(u, w, qd, qk, kdt, gl, zg, s0, nw, b, nch, bb, n_valid):
    tok = lambda width: pl.BlockSpec((bb, DN_C, width), lambda i, n: (i, n, 0))
    r3 = lambda a: a.reshape(b, nch * DN_C, a.shape[-1])
    st = pl.BlockSpec((bb, DN_HEADS, DN_DK, DN_DV), lambda i, n: (i, 0, 0, 0))
    return pl.pallas_call(
        functools.partial(_dn_scan_kernel, n_valid),
        grid=(b // bb, nch),
        in_specs=[tok(DN_W), tok(DN_QK_W), tok(DN_QK_W), tok(DN_HEADS * DN_C),
                  pl.BlockSpec((bb, DN_QK_W, DN_C), lambda i, n: (i, n, 0)),
                  pl.BlockSpec((bb, HALO, LANES), lambda i, n: (i, n, 0)),
                  pl.BlockSpec((bb, n_valid, DN_W), lambda i, n: (i, n, 0)), st, _const_spec((1, DN_DV))],
        out_specs=[pl.BlockSpec((bb, n_valid, DN_W), lambda i, n: (i, n, 0)), st],
        out_shape=[jax.ShapeDtypeStruct((b, nch * n_valid, DN_W), BF16),
                   jax.ShapeDtypeStruct((b, DN_HEADS, DN_DK, DN_DV), F32)],
        compiler_params=_params("parallel", "arbitrary"), name="dn_scan",
    )(r3(u), r3(w), r3(qd), r3(qk), kdt.reshape(b, nch * DN_QK_W, DN_C), gl.reshape(b, nch * HALO, LANES),
      zg.reshape(b, nch * n_valid, DN_W), s0, nw)


IN_WIDTHS = (NSA_Q_W, 6 * NSA_KV_W, 3 * NSA_HEADS, DN_QKV_W, DN_HEADS, DN_HEADS, DN_W, 2 * D_MODEL)


def _pack_mix_weight(w_in):
    offs = np.cumsum(IN_WIDTHS)[:-1].tolist()
    q, kv, gate, qkv, a, b, zg, merge = jnp.split(w_in, offs, axis=-1)
    lead = w_in.shape[:-1]
    qh = q.reshape(lead + (NSA_KV_HEADS, NSA_GROUP, 1, HEAD_DIM))
    eye = jnp.eye(NSA_KV_HEADS, dtype=w_in.dtype).reshape(NSA_KV_HEADS, 1, NSA_KV_HEADS, 1)
    qpad = (qh * eye).reshape(lead + (QPAD_W,))
    small = jnp.concatenate([gate, a, b, jnp.zeros(lead + (LANES - SM_B - DN_HEADS,), w_in.dtype)], axis=-1)
    return jnp.concatenate([qpad, kv, small, qkv, zg, merge], axis=-1).astype(BF16)


def _rope_tables(pos):
    half = HEAD_DIM // 2
    freq = ROPE_THETA ** (-jnp.arange(half, dtype=F32) / half)
    ang = pos.astype(F32)[:, None] * freq[None, :]
    cos, sin = jnp.cos(ang), jnp.sin(ang)
    cos = jnp.concatenate([cos, cos] * (LANES // HEAD_DIM), axis=1)
    sin = jnp.concatenate([-sin, sin] * (LANES // HEAD_DIM), axis=1)
    return cos, sin


def _gate_expand_matrix():
    r = np.zeros((LANES, 3 * NSA_Q_W), np.float32)
    for h in range(NSA_HEADS):
        k, g = divmod(h, NSA_GROUP)
        for br in range(3):
            c0 = br * NSA_Q_W + g * LANES + k * HEAD_DIM
            r[SM_GATE + h * 3 + br, c0:c0 + HEAD_DIM] = 1.0
    return jnp.asarray(r, BF16)


def _overlap_t(nc_pad, nc, ns, rows):
    i = np.arange(nc_pad)[None, :]
    j = np.arange(rows)[:, None]
    ov = (i * CMP_STRIDE < (j + 1) * SEL_BLOCK) & (i * CMP_STRIDE + CMP_BLOCK > j * SEL_BLOCK)
    ov = ov & (i < nc) & (j < ns)
    return jnp.asarray(ov.astype(np.float32), BF16)


def _pack_compress_sample(pos_emb, w1, w2):
    eye = jnp.eye(NSA_KV_HEADS, dtype=w1.dtype)
    w = w1.reshape(2, 2, CMP_STRIDE, HEAD_DIM, HEAD_DIM)
    w = jnp.einsum('kapde,hg->kphdage', w, eye)
    w = w.reshape(2, NSA_KV_HEADS * CMP_STRIDE * HEAD_DIM, 2 * LANES).astype(BF16)
    pe = pos_emb.reshape(2, 2, CMP_STRIDE, 1, HEAD_DIM)
    pe = jnp.concatenate([pe] * NSA_KV_HEADS, axis=3).reshape(2, 2, -1)
    pe = jnp.concatenate([pe, jnp.zeros((2, 6, pe.shape[2]), pe.dtype)], axis=1)
    w2d = jnp.einsum('kde,hg->khdge', w2, eye).reshape(2, LANES, LANES).astype(BF16)
    return pe, w, w2d


def _sample_rows(q_pad, b, ts):
    q = q_pad.reshape(b, ts, NSA_KV_HEADS, NSA_GROUP, LANES).transpose(0, 3, 2, 1, 4)
    return q.reshape(b, NSA_HEADS * ts, LANES)


def _dn_lane_row(vals):
    return jnp.zeros((1, LANES), F32).at[0, SM_A:SM_A + DN_HEADS].set(vals.astype(F32))


def _deltanet(qkv, small, zg, prefix, s0, dn_params, tri, b, seq_tokens):
    conv_w, a_log, dt_bias, norm_w = dn_params
    if seq_tokens % DN_C == 0:
        nch, n_valid, bb = seq_tokens // DN_C, DN_C, min(b, 4)
    else:
        nch, n_valid, bb = 1, seq_tokens, 4
    prefix8 = jnp.concatenate([jnp.zeros((b, HALO - (CONV_W - 1), DN_QKV_W), F32), prefix], axis=1)
    u, w, qd, qk, kdt, gl = _dn_prep(qkv, prefix8, small, conv_w, _dn_lane_row(a_log), _dn_lane_row(dt_bias),
                                     tri, b, nch, n_valid)
    ob, s_out = _dn_scan(u, w, qd, qk, kdt, gl, zg, s0, norm_w[None], b, nch, bb, n_valid)
    return ob.reshape(b * seq_tokens, DN_W), s_out


def _head_perm_rows(wa):
    d = wa.shape[1]
    return wa.reshape(NSA_KV_HEADS, NSA_GROUP, HEAD_DIM, d).transpose(1, 0, 2, 3).reshape(NSA_Q_W, d)


def _pack_compress(pos_emb, w1, w2):
    eye = jnp.eye(NSA_KV_HEADS, dtype=w1.dtype)
    pe = jnp.concatenate([pos_emb] * NSA_KV_HEADS, axis=2)
    w1p = w1.reshape(2, CMP_BLOCK, HEAD_DIM, HEAD_DIM)
    w1d = jnp.einsum('kpde,hg->kphdge', w1p, eye).reshape(2, CMP_BLOCK, LANES, LANES)
    w2d = jnp.einsum('kde,hg->khdge', w2, eye).reshape(2, LANES, LANES)
    return pe, w1d.astype(BF16), w2d.astype(BF16)


def kernel(x_prompt, x_sample, cache_nsa_kv, state_nsa_win, state_dn_S, state_dn_conv, page_table,
           ffn1_norm, ffn1_w_in, ffn1_w_out, mix_norm, w_in, nsa_cmp_pos, nsa_cmp_w1, nsa_cmp_w2,
           dn_conv_w, dn_A_log, dn_dt_bias, dn_out_norm, w_branch_a, w_branch_b, w_out,
           ffn2_norm, ffn2_w_in, ffn2_w_out, final_norm):
    b, seq = x_prompt.shape[:2]
    bs, ts = x_sample.shape[:2]
    depth = w_in.shape[0]
    n_phys = cache_nsa_kv.shape[1]
    past = page_table.shape[1] * PAGE_SIZE
    wbuf = state_nsa_win.shape[2]
    kvw = 4 * NSA_KV_W

    xp = x_prompt.reshape(b * seq, D_MODEL)
    xs = x_sample.reshape(bs * ts, D_MODEL)
    cos_p, sin_p = _rope_tables(jnp.arange(seq, dtype=jnp.int32))
    cos_s, sin_s = _rope_tables(jnp.tile(past + jnp.arange(ts, dtype=jnp.int32), bs))
    nc = (seq - CMP_BLOCK) // CMP_STRIDE + 1
    ovt_p = _overlap_t(seq // CMP_STRIDE, nc, seq // SEL_BLOCK, LANES)
    tables_p = _attn_bias_tables(seq)
    ns_s = -(-(past + ts) // SEL_BLOCK)
    ovt_s = _overlap_t(past // CMP_STRIDE, past // CMP_STRIDE - 1, ns_s, -(-ns_s // 16) * 16)
    rexp = _gate_expand_matrix()
    tri = jnp.asarray(np.tril(np.ones((DN_C, DN_C), np.float32)), BF16)
    cache = cache_nsa_kv.transpose(0, 1, 3, 4, 5, 2).reshape(depth, n_phys, kvw, PAGE_SIZE)
    win_state = state_nsa_win.transpose(0, 1, 3, 4, 5, 2).reshape(depth, bs, 2 * NSA_KV_W, wbuf)
    tm_p, tm_s = 512, bs * ts

    outs = [[] for _ in range(8)]
    rows_all = None
    w_mix_all = _pack_mix_weight(w_in)
    for l in range(depth):
        f1 = (ffn1_norm[l][None], ffn1_w_in[l].astype(BF16), ffn1_w_out[l].astype(BF16))
        f2 = (ffn2_norm[l][None], ffn2_w_in[l].astype(BF16), ffn2_w_out[l].astype(BF16))
        w_mix = w_mix_all[l]
        mrg = (_head_perm_rows(w_branch_a[l]).astype(BF16), w_branch_b[l].astype(BF16), w_out[l].astype(BF16))
        dn_params = (dn_conv_w[l], dn_A_log[l], dn_dt_bias[l], dn_out_norm[l])
        cmp_l = (nsa_cmp_pos[l], nsa_cmp_w1[l], nsa_cmp_w2[l])

        xp = _ffn(xp, *f1, tm_p)
        (qraw, qrot, small, qkv, zg, mg, rows_all, win_t, kcmp, vcmp, ksa, kw16, vst, vwt) = _mix(
            xp, mix_norm[l][None], w_mix, cos_p, sin_p, 256, seq, True, l, depth, rows_all)
        kc, vct = _compress_prompt(kcmp, vcmp, *_pack_compress(*cmp_l), b, seq)
        oa = _nsa_prompt(qraw, qrot, small, kc, vct, ksa, vst, kw16, vwt, ovt_p, tables_p, b, seq)
        ob, s_fin = _deltanet(qkv, small, zg, jnp.zeros((b, CONV_W - 1, DN_QKV_W), F32),
                              jnp.zeros((b, DN_HEADS, DN_DK, DN_DV), F32), dn_params, tri, b, seq)
        xp = _merge(xp, oa, ob, mg, *mrg, tm_p)
        xp = _ffn(xp, *f2, tm_p)
        outs[1].append(win_t[:, :, seq - min(WINDOW, seq):])
        outs[2].append(s_fin)
        outs[3].append(qkv.reshape(b, seq, DN_QKV_W)[:, seq - (CONV_W - 1):])

        xs = _ffn(xs, *f1, tm_s)
        (qraw, qrot, small, qkv, zg, mg, rows, win) = _mix(
            xs, mix_norm[l][None], w_mix, cos_s, sin_s, tm_s, tm_s, False)
        oa = _nsa_sample(l, cache, page_table, _sample_rows(qraw, bs, ts), _sample_rows(qrot, bs, ts),
                         small.reshape(bs, ts, LANES), rows.reshape(bs, ts, kvw), win.reshape(bs, ts, 2 * NSA_KV_W),
                         win_state, *_pack_compress_sample(*cmp_l), ovt_s, rexp, past)
        oa = oa[:, :ts].reshape(bs * ts, NSA_Q_W).astype(BF16)
        ob, s_fin = _deltanet(qkv, small, zg, state_dn_conv[l], state_dn_S[l], dn_params, tri, bs, ts)
        xs = _merge(xs, oa, ob, mg, *mrg, tm_s)
        xs = _ffn(xs, *f2, tm_s)
        outs[4].append(rows.reshape(bs, ts, 4, NSA_KV_HEADS, HEAD_DIM))
        outs[5].append(win.reshape(bs, ts, 2, NSA_KV_HEADS, HEAD_DIM))
        outs[6].append(s_fin)
        outs[7].append(qkv.reshape(bs, ts, DN_QKV_W))

    y_prompt = _final_norm(xp, final_norm[None], tm_p).reshape(b, seq, D_MODEL)
    y_sample = _final_norm(xs, final_norm[None], tm_s).reshape(bs, ts, D_MODEL)
    res = [rows_all] + [jnp.stack(o) for o in outs[1:]]
    res[5] = jnp.concatenate([state_nsa_win, res[5]], axis=2)[:, :, ts:]
    res[7] = jnp.concatenate([state_dn_conv, res[7]], axis=2)[:, :, ts:]
    for i, parts in ((0, 4), (1, 2)):
        r = res[i]
        res[i] = r.reshape(depth, b, parts, NSA_KV_HEADS, HEAD_DIM, r.shape[-1]).transpose(0, 1, 5, 2, 3, 4)
    return (y_prompt, y_sample) + tuple(res)
```

```python
import functools
import math

import jax
import jax.numpy as jnp
import numpy as np
from jax import lax
from jax.experimental import pallas as pl
from jax.experimental.pallas import tpu as pltpu

F32 = jnp.float32
BF16 = jnp.bfloat16

D_MODEL = 1024
DEPTH = 4
PAGE_SIZE = 128
NSA_HEADS = 8
NSA_KV_HEADS = 2
HEAD_DIM = 64
NSA_GROUP = NSA_HEADS // NSA_KV_HEADS
NSA_Q_W = NSA_HEADS * HEAD_DIM
NSA_KV_W = NSA_KV_HEADS * HEAD_DIM
CMP_BLOCK = 32
CMP_STRIDE = 16
SEL_BLOCK = 64
N_SEL = 16
WINDOW = 512
FORCE_BONUS = 1e6
ROPE_THETA = 10000.0
DN_HEADS = 4
DN_DK = 128
DN_DV = 128
DN_QK_W = DN_HEADS * DN_DK
DN_W = DN_HEADS * DN_DV
DN_QKV_W = 2 * DN_QK_W + DN_W
CONV_W = 4
D_FF = 2816
EPS = 1e-6

LANES = 128
VT_ROWS = HEAD_DIM + 16
VMEM_LIMIT = 56 * 1024 * 1024
NEG_BIG = -1e30

C_Q = 0
C_KV = C_Q + NSA_Q_W
C_SMALL = C_KV + 6 * NSA_KV_W
C_QKV = C_SMALL + LANES
C_ZG = C_QKV + DN_QKV_W
C_MERGE = C_ZG + DN_W
MIX_W = C_MERGE + 2 * D_MODEL
SM_GATE = 0
SM_A = 3 * NSA_HEADS
SM_B = SM_A + DN_HEADS


def _dot(a, b):
    return jnp.dot(a, b, preferred_element_type=F32)


def _dot_nt(a, b):
    return lax.dot_general(a, b, (((1,), (1,)), ((), ())), preferred_element_type=F32)


def _split3(a):
    hi = a.astype(BF16)
    r1 = a - hi.astype(F32)
    mid = r1.astype(BF16)
    lo = (r1 - mid.astype(F32)).astype(BF16)
    return hi, mid, lo


def _dot_exact_rhs(a, b01):
    hi, mid, lo = _split3(a)
    return _dot(hi, b01) + _dot(mid, b01) + _dot(lo, b01)


def _dot_exact_lhs(a01, b):
    hi, mid, lo = _split3(b)
    return _dot(a01, hi) + _dot(a01, mid) + _dot(a01, lo)


def _dot_exact_lhs_nt(a01, b):
    hi, mid, lo = _split3(b)
    return _dot_nt(a01, hi) + _dot_nt(a01, mid) + _dot_nt(a01, lo)


def _dot3(a, b):
    ah = a.astype(BF16)
    al = (a - ah.astype(F32)).astype(BF16)
    bh = b.astype(BF16)
    bl = (b - bh.astype(F32)).astype(BF16)
    return _dot(ah, bh) + (_dot(ah, bl) + _dot(al, bh))


def _rms(x, w):
    return x * lax.rsqrt(jnp.mean(x * x, axis=-1, keepdims=True) + EPS) * w


def _sigmoid(x):
    return 1.0 / (1.0 + jnp.exp(-x))


def _silu(x):
    return x * _sigmoid(x)


def _params(*sem):
    return pltpu.CompilerParams(dimension_semantics=sem, vmem_limit_bytes=VMEM_LIMIT)


def _const_spec(shape):
    nd = len(shape)
    return pl.BlockSpec(shape, lambda *_: (0,) * nd)


FF_TILE = 256


def _ffn_kernel(x_ref, nw_ref, wi_ref, wo_ref, o_ref, act_ref):
    x = x_ref[...]
    hb = _rms(x, nw_ref[...]).astype(BF16)
    for j in range(D_FF // FF_TILE):
        g = _dot(hb, wi_ref[:, j * FF_TILE:(j + 1) * FF_TILE])
        u = _dot(hb, wi_ref[:, D_FF + j * FF_TILE:D_FF + (j + 1) * FF_TILE])
        act_ref[:, j * FF_TILE:(j + 1) * FF_TILE] = (_silu(g) * u).astype(BF16)
    o_ref[...] = x + 0.5 * _dot(act_ref[...], wo_ref[...])


def _ffn(x, nw, wi, wo, tm):
    m = x.shape[0]
    return pl.pallas_call(
        _ffn_kernel,
        grid=(m // tm,),
        in_specs=[pl.BlockSpec((tm, D_MODEL), lambda i: (i, 0)),
                  _const_spec((1, D_MODEL)),
                  _const_spec((D_MODEL, 2 * D_FF)),
                  _const_spec((D_FF, D_MODEL))],
        out_specs=pl.BlockSpec((tm, D_MODEL), lambda i: (i, 0)),
        out_shape=jax.ShapeDtypeStruct((m, D_MODEL), F32),
        scratch_shapes=[pltpu.VMEM((tm, D_FF), BF16)],
        compiler_params=_params("parallel"), name="half_ffn",
    )(x, nw, wi, wo)


def _rope128(x, cos, sin):
    lane = lax.broadcasted_iota(jnp.int32, x.shape, 1)
    first = (lane % HEAD_DIM) < (HEAD_DIM // 2)
    swapped = jnp.where(first, pltpu.roll(x, LANES - HEAD_DIM // 2, 1), pltpu.roll(x, HEAD_DIM // 2, 1))
    return x * cos + swapped * sin


def _mix_kernel(seq_tiles, prompt, n_passthrough, x_ref, nw_ref, w_ref, cos_ref, sin_ref, *refs):
    qraw_ref, qrot_ref, small_ref, qkv_ref, zg_ref, merge_ref, *group_refs = refs[n_passthrough:]
    tm = x_ref.shape[0]
    hb = _rms(x_ref[...], nw_ref[...]).astype(BF16)
    cos = cos_ref[...]
    sin = sin_ref[...]
    scale = HEAD_DIM ** -0.5 * (math.log2(math.e) if prompt else 1.0)
    for j in range(NSA_Q_W // LANES):
        q = _dot(hb, w_ref[:, C_Q + j * LANES:C_Q + (j + 1) * LANES])
        qraw_ref[:, j * LANES:(j + 1) * LANES] = (q * scale).astype(BF16)
        qrot_ref[:, j * LANES:(j + 1) * LANES] = (_rope128(q, cos, sin) * scale).astype(BF16)
    kv = _dot(hb, w_ref[:, C_KV:C_KV + 6 * NSA_KV_W])
    k_cmp = kv[:, 0:LANES]
    v_cmp = kv[:, LANES:2 * LANES]
    k_sel = _rope128(kv[:, 2 * LANES:3 * LANES], cos, sin)
    v_sel = kv[:, 3 * LANES:4 * LANES]
    k_win = _rope128(kv[:, 4 * LANES:5 * LANES], cos, sin)
    v_win = kv[:, 5 * LANES:6 * LANES]
    if prompt:
        rows_t_ref, win_t_ref, kcmp_ref, vcmp_ref, ksa_ref, kw16_ref, vst_ref, vwt_ref = group_refs
        ones = jnp.ones((VT_ROWS - HEAD_DIM, tm), F32)
        for part, a in enumerate((k_cmp, v_cmp, k_sel, v_sel)):
            rows_t_ref[0, 0, part * LANES:(part + 1) * LANES, :] = a.T
        for part, a in enumerate((k_win, v_win)):
            win_t_ref[0, part * LANES:(part + 1) * LANES, :] = a.T
        for v, vt_ref in ((v_sel, vst_ref), (v_win, vwt_ref)):
            vt = v.T
            for k in range(NSA_KV_HEADS):
                vt_ref[k] = jnp.concatenate([vt[k * HEAD_DIM:(k + 1) * HEAD_DIM], ones], axis=0).astype(BF16)
        kcmp_ref[...] = k_cmp
        vcmp_ref[...] = v_cmp
        pos = (pl.program_id(0) % seq_tiles) * tm + lax.broadcasted_iota(jnp.int32, (tm, LANES), 0)
        lane = lax.broadcasted_iota(jnp.int32, (tm, LANES), 1)
        ksa_ref[:, 0:LANES] = k_sel.astype(BF16)
        ksa_ref[:, LANES:2 * LANES] = jnp.where(pos // SEL_BLOCK == lane, 1.0, 0.0).astype(BF16)
        kw16_ref[...] = k_win.astype(BF16)
    else:
        rows_ref, win_ref = group_refs
        for part, a in enumerate((k_cmp, v_cmp, k_sel, v_sel)):
            rows_ref[:, part * LANES:(part + 1) * LANES] = a
        win_ref[:, 0:LANES] = k_win
        win_ref[:, LANES:2 * LANES] = v_win
    small_ref[...] = _dot(hb, w_ref[:, C_SMALL:C_SMALL + LANES])
    qkv_ref[...] = _dot(hb, w_ref[:, C_QKV:C_QKV + DN_QKV_W])
    zg_ref[...] = _dot(hb, w_ref[:, C_ZG:C_ZG + DN_W])
    merge_ref[...] = _dot(hb, w_ref[:, C_MERGE:C_MERGE + 2 * D_MODEL])


def _mix(x, nw, w, cos, sin, tm, seq_len, prompt, layer=0, depth=1, rows_all=None):
    m = x.shape[0]
    seq_tiles = max(seq_len // tm, 1)
    row = lambda width: pl.BlockSpec((tm, width), lambda i: (i, 0))
    tab = pl.BlockSpec((tm, LANES), lambda i: (i % seq_tiles, 0))
    widths = [(NSA_Q_W, BF16), (NSA_Q_W, BF16), (LANES, F32), (DN_QKV_W, F32), (DN_W, F32), (2 * D_MODEL, F32)]
    inputs = [x, nw, w, cos, sin]
    in_specs = [row(D_MODEL), _const_spec((1, D_MODEL)), _const_spec((D_MODEL, MIX_W)), tab, tab]
    aliases = {}
    if prompt:
        nb = m // seq_len
        vt_spec = pl.BlockSpec((NSA_KV_HEADS, VT_ROWS, tm), lambda i: (0, 0, i))
        vt_shape = jax.ShapeDtypeStruct((NSA_KV_HEADS, VT_ROWS, m), BF16)
        g_widths = [(LANES, F32), (LANES, F32), (2 * LANES, BF16), (LANES, BF16)]
        g_specs = ([pl.BlockSpec((1, 1, 4 * NSA_KV_W, tm), lambda i: (layer, i // seq_tiles, 0, i % seq_tiles)),
                    pl.BlockSpec((1, 2 * NSA_KV_W, tm), lambda i: (i // seq_tiles, 0, i % seq_tiles))]
                   + [row(wd) for wd, _ in g_widths] + [vt_spec, vt_spec])
        g_shapes = ([jax.ShapeDtypeStruct((depth, nb, 4 * NSA_KV_W, seq_len), F32),
                     jax.ShapeDtypeStruct((nb, 2 * NSA_KV_W, seq_len), F32)]
                    + [jax.ShapeDtypeStruct((m, wd), dt) for wd, dt in g_widths] + [vt_shape, vt_shape])
        if rows_all is not None:
            aliases = {len(inputs): len(widths)}
            inputs.append(rows_all)
            in_specs.append(pl.BlockSpec(memory_space=pl.ANY))
    else:
        g_widths = [(4 * NSA_KV_W, F32), (2 * NSA_KV_W, F32)]
        g_specs = [row(wd) for wd, _ in g_widths]
        g_shapes = [jax.ShapeDtypeStruct((m, wd), dt) for wd, dt in g_widths]
    return pl.pallas_call(
        functools.partial(_mix_kernel, seq_tiles, prompt, len(aliases)),
        grid=(m // tm,),
        in_specs=in_specs,
        out_specs=[row(wd) for wd, _ in widths] + g_specs,
        out_shape=[jax.ShapeDtypeStruct((m, wd), dt) for wd, dt in widths] + g_shapes,
        input_output_aliases=aliases,
        compiler_params=_params("parallel"), name="mix_in",
    )(*inputs)


def _merge_kernel(x_ref, oa_ref, ob_ref, mg_ref, wa_ref, wb_ref, wo_ref, o_ref):
    mg = mg_ref[...]
    ya = _dot(oa_ref[...], wa_ref[...])
    yb = _dot(ob_ref[...], wb_ref[...])
    y = _sigmoid(mg[:, :D_MODEL]) * ya + _sigmoid(mg[:, D_MODEL:]) * yb
    o_ref[...] = x_ref[...] + _dot(y.astype(BF16), wo_ref[...])


def _merge(x, oa, ob, mg, wa, wb, wo, tm):
    m = x.shape[0]
    row = lambda width: pl.BlockSpec((tm, width), lambda i: (i, 0))
    return pl.pallas_call(
        _merge_kernel,
        grid=(m // tm,),
        in_specs=[row(D_MODEL), row(NSA_Q_W), row(DN_W), row(2 * D_MODEL),
                  _const_spec((NSA_Q_W, D_MODEL)), _const_spec((DN_W, D_MODEL)),
                  _const_spec((D_MODEL, D_MODEL))],
        out_specs=row(D_MODEL),
        out_shape=jax.ShapeDtypeStruct((m, D_MODEL), F32),
        compiler_params=_params("parallel"), name="branch_merge",
    )(x, oa, ob, mg, wa, wb, wo)


def _norm_kernel(x_ref, w_ref, o_ref):
    o_ref[...] = _rms(x_ref[...], w_ref[...])


def _final_norm(x, w, tm):
    m = x.shape[0]
    row = pl.BlockSpec((tm, D_MODEL), lambda i: (i, 0))
    return pl.pallas_call(
        _norm_kernel, grid=(m // tm,), in_specs=[row, _const_spec((1, D_MODEL))], out_specs=row,
        out_shape=jax.ShapeDtypeStruct((m, D_MODEL), F32), compiler_params=_params("parallel"),
    )(x, w)


def _compress_kernel(kcmp_ref, vcmp_ref, pe_ref, w1_ref, w2_ref, kc_ref, vct_ref):
    ns = kc_ref.shape[1]
    res = []
    for part, rows_ref in enumerate((kcmp_ref, vcmp_ref)):
        sa = jnp.zeros((ns, LANES), F32)
        sb = jnp.zeros((ns, LANES), F32)
        for p in range(CMP_STRIDE):
            x = rows_ref[pl.ds(p, ns, stride=CMP_STRIDE), :]
            sa = sa + _dot((x + pe_ref[part, p:p + 1, :]).astype(BF16), w1_ref[part, p])
            sb = sb + _dot((x + pe_ref[part, CMP_STRIDE + p:CMP_STRIDE + p + 1, :]).astype(BF16),
                           w1_ref[part, CMP_STRIDE + p])
        pre = sa + pltpu.roll(sb, ns - 1, 0)
        res.append(_dot(jax.nn.gelu(pre).astype(BF16), w2_ref[part]))
    kc_ref[0] = res[0].astype(BF16)
    vt = res[1].T
    ones = jnp.ones((VT_ROWS - HEAD_DIM, ns), F32)
    for k in range(NSA_KV_HEADS):
        vct_ref[0, k] = jnp.concatenate([vt[k * HEAD_DIM:(k + 1) * HEAD_DIM], ones], axis=0).astype(BF16)


def _compress_prompt(kcmp, vcmp, pe, w1, w2, b, seq):
    ns = seq // CMP_STRIDE
    tok = pl.BlockSpec((seq, LANES), lambda i: (i, 0))
    return pl.pallas_call(
        _compress_kernel,
        grid=(b,),
        in_specs=[tok, tok, _const_spec(pe.shape), _const_spec(w1.shape), _const_spec(w2.shape)],
        out_specs=[pl.BlockSpec((1, ns, LANES), lambda i: (i, 0, 0)),
                   pl.BlockSpec((1, NSA_KV_HEADS, VT_ROWS, ns), lambda i: (i, 0, 0, 0))],
        out_shape=[jax.ShapeDtypeStruct((b, ns, LANES), BF16),
                   jax.ShapeDtypeStruct((b, NSA_KV_HEADS, VT_ROWS, ns), BF16)],
        compiler_params=_params("parallel"), name="nsa_compress",
    )(kcmp, vcmp, pe, w1, w2)


TQ = 512
KT = 512
WIN_KEYS = WINDOW + TQ


def _softmax_rows(s, mask):
    s = jnp.where(mask, s, NEG_BIG)
    e = jnp.where(mask, jnp.exp(s - jnp.max(s, axis=-1, keepdims=True)), 0.0)
    return e * (1.0 / jnp.maximum(jnp.sum(e, axis=-1, keepdims=True), 1e-30))


def _topk_bias_t(imp_t, n_cand, n_sel):
    nr = imp_t.shape[0] // 8
    xs = [imp_t[8 * r:8 * r + 8, :] for r in range(nr)]
    cnt = [jnp.zeros_like(xs[0]) for _ in range(nr)]
    sub = lax.broadcasted_iota(jnp.int32, xs[0].shape, 0)
    for j in range(n_cand):
        rj = j // 8
        row = xs[rj][j % 8:j % 8 + 1, :]
        for r in range(nr):
            if r < rj:
                inc = jnp.where(row > xs[r], 1.0, 0.0)
            elif r > rj:
                inc = jnp.where(row >= xs[r], 1.0, 0.0)
            else:
                inc = jnp.where(sub > j % 8, jnp.where(row >= xs[r], 1.0, 0.0),
                                jnp.where(row > xs[r], 1.0, 0.0))
            cnt[r] = cnt[r] + inc
    return jnp.concatenate([jnp.where(c < n_sel, 0.0, NEG_BIG) for c in cnt], axis=0)


def _nsa_prompt_kernel(qraw_ref, qrot_ref, small_ref, kc_ref, vct_ref, ksa_ref, vst_ref, win_ref, vwt_ref,
                       ovt_ref, cbias_ref, dbias_ref, wbias_ref, o_ref):
    seq = ksa_ref.shape[1]
    ns = seq // SEL_BLOCK
    s0 = pl.multiple_of(pl.program_id(1) * TQ, TQ)
    cols = NSA_GROUP * TQ
    tile = lambda a: jnp.concatenate([a] * NSA_GROUP, axis=1)
    qpos_row = s0 + lax.broadcasted_iota(jnp.int32, (1, cols), 1) % TQ
    gates_t = _sigmoid(small_ref[...]).T

    blk_t = lax.broadcasted_iota(jnp.int32, (ns, TQ), 0)
    qpos_t = s0 + lax.broadcasted_iota(jnp.int32, (ns, TQ), 1)
    cur_t = qpos_t // SEL_BLOCK
    valid_t = blk_t * SEL_BLOCK <= qpos_t
    forced_t = (blk_t == 0) | (blk_t == cur_t) | (blk_t == cur_t - 1)
    cbias = tile(cbias_ref[0])
    dbias = tile(dbias_ref[0])
    wbias = tile(wbias_ref[0])

    def split_sum(aug):
        return aug[0:HEAD_DIM], aug[HEAD_DIM:HEAD_DIM + 1]

    kv_heads = range(NSA_KV_HEADS)
    w0 = pl.multiple_of(jnp.maximum(s0 - WINDOW, 0), TQ)
    o_c, o_w, q_sel = [], [], []
    for k in kv_heads:
        def q_t(ref):
            zeros = jnp.zeros((HEAD_DIM, TQ), F32)
            blocks = []
            for pair in range(NSA_GROUP // 2):
                j = k * (NSA_GROUP // 2) + pair
                both = ref[:, j * LANES:(j + 1) * LANES].astype(F32).T
                for half in range(2):
                    qh = both[half * HEAD_DIM:(half + 1) * HEAD_DIM]
                    blocks.append(jnp.concatenate([qh, zeros] if k == 0 else [zeros, qh], axis=0))
            return jnp.concatenate(blocks, axis=1).astype(BF16)
        qt_raw = q_t(qraw_ref)
        qt_rot = q_t(qrot_ref)

        s = _dot(kc_ref[0], qt_raw) + cbias
        e = jnp.exp2(s - jnp.max(s, axis=0, keepdims=True))
        acc, l = split_sum(_dot(vct_ref[0, k], e.astype(BF16)))
        inv = jnp.where(qpos_row >= CMP_BLOCK - 1, 1.0 / l, 0.0)
        o_c.append(acc * inv)
        p = e * inv
        p_sum = p[:, 0:TQ] + p[:, TQ:2 * TQ] + p[:, 2 * TQ:3 * TQ] + p[:, 3 * TQ:4 * TQ]
        imp_t = _dot_exact_lhs(ovt_ref[...], p_sum)[0:ns]
        imp_t = jnp.where(valid_t, imp_t + jnp.where(forced_t, FORCE_BONUS, 0.0), -jnp.inf)
        bias_t = _topk_bias_t(imp_t, ns, min(N_SEL, ns))
        if ns < LANES:
            bias_t = jnp.concatenate([bias_t, jnp.zeros((LANES - ns, TQ), F32)], axis=0)
        q_sel.append(jnp.concatenate([qt_rot, tile(bias_t).astype(BF16)], axis=0))

        s = _dot(win_ref[0, pl.ds(w0, WIN_KEYS), :], qt_rot) + wbias
        e = jnp.exp2(s - jnp.max(s, axis=0, keepdims=True)).astype(BF16)
        acc, l = split_sum(_dot(vwt_ref[k, :, pl.ds(w0, WIN_KEYS)], e))
        o_w.append(acc * (1.0 / l))

    def sel_step(c, carry, diagonal):
        k0 = pl.multiple_of(c * KT, KT)
        keys = ksa_ref[0, pl.ds(k0, KT), :]
        out = []
        for k in kv_heads:
            m, acc = carry[k]
            s = _dot(keys, q_sel[k])
            if diagonal:
                s = s + dbias
            m_new = jnp.maximum(m, jnp.max(s, axis=0, keepdims=True))
            p = jnp.exp2(s - m_new).astype(BF16)
            out.append((m_new, jnp.exp2(m - m_new) * acc + _dot(vst_ref[k, :, pl.ds(k0, KT)], p)))
        return tuple(out)
    init = tuple((jnp.full((1, cols), NEG_BIG, F32), jnp.zeros((VT_ROWS, cols), F32)) for _ in kv_heads)
    n_full = s0 // KT
    carry = lax.fori_loop(0, n_full, functools.partial(sel_step, diagonal=False), init)
    carry = sel_step(n_full, carry, True)
    res = []
    for k in kv_heads:
        acc, l = split_sum(carry[k][1])
        res.append((o_c[k], acc * (1.0 / l), o_w[k]))

    for g in range(NSA_GROUP):
        halves = []
        for k in range(NSA_KV_HEADS):
            r = SM_GATE + (k * NSA_GROUP + g) * 3
            halves.append(sum(gates_t[r + br:r + br + 1, :] * res[k][br][:, g * TQ:(g + 1) * TQ]
                              for br in range(3)))
        o_ref[:, g * LANES:(g + 1) * LANES] = jnp.concatenate(halves, axis=0).T.astype(BF16)


def _attn_bias_tables(seq):
    r = np.arange(TQ)[None, None, :]
    blocks = np.arange(seq // TQ)[:, None, None]
    c = np.arange(seq // CMP_STRIDE)[None, :, None]
    cb = c * CMP_STRIDE + CMP_BLOCK - 1 <= blocks * TQ + r
    kk = np.arange(KT)[None, :, None]
    off = np.arange(KT // TQ)[:, None, None] * TQ
    db = kk <= off + r
    kw = np.arange(WIN_KEYS)[None, :, None]
    nw = WINDOW // TQ
    j = np.arange(nw + 1)[:, None, None]
    qpos = j * TQ + r
    kpos = np.where(j < nw, kw, qpos - r - WINDOW + kw)
    wb = (kpos <= qpos) & (kpos > qpos - WINDOW)
    f = lambda m: jnp.asarray(np.where(m, 0.0, NEG_BIG).astype(np.float32))
    return f(cb), f(db), f(wb)


def _nsa_prompt(qraw, qrot, small, kc, vct, ksa, vst, kw16, vwt, ovt, tables, b, seq):
    nc = kc.shape[1]
    nq = seq // TQ
    cb, db, wb = tables
    tok = lambda width: pl.BlockSpec((TQ, width), lambda i, j: (i * nq + j, 0))
    per_b = lambda width: pl.BlockSpec((1, seq, width), lambda i, j: (i, 0, 0))
    vt_spec = pl.BlockSpec((NSA_KV_HEADS, VT_ROWS, seq), lambda i, j: (0, 0, i))
    return pl.pallas_call(
        _nsa_prompt_kernel,
        grid=(b, nq),
        in_specs=[tok(NSA_Q_W), tok(NSA_Q_W), tok(LANES),
                  pl.BlockSpec((1, nc, LANES), lambda i, j: (i, 0, 0)),
                  pl.BlockSpec((1, NSA_KV_HEADS, VT_ROWS, nc), lambda i, j: (i, 0, 0, 0)),
                  per_b(2 * LANES), vt_spec, per_b(LANES), vt_spec,
                  _const_spec(ovt.shape),
                  pl.BlockSpec((1, nc, TQ), lambda i, j: (j, 0, 0)),
                  pl.BlockSpec((1, KT, TQ), lambda i, j: (j % (KT // TQ), 0, 0)),
                  pl.BlockSpec((1, WIN_KEYS, TQ), lambda i, j: (jnp.minimum(j, WINDOW // TQ), 0, 0))],
        out_specs=tok(NSA_Q_W),
        out_shape=jax.ShapeDtypeStruct((b * seq, NSA_Q_W), BF16),
        compiler_params=_params("parallel", "parallel"), name="nsa_prompt",
    )(qraw, qrot, small, kc, vct, ksa.reshape(b, seq, 2 * LANES), vst, kw16.reshape(b, seq, LANES), vwt,
      ovt, cb, db, wb)


PAGES_PER_STEP = 16
SEG_PER_PAGE = PAGE_SIZE // CMP_STRIDE
SEG_W = CMP_STRIDE * 2 * NSA_KV_W


def _nsa_sample_kernel(ts, past, pt_ref, *refs):
    pages = refs[:PAGES_PER_STEP]
    (qraw_ref, qrot_ref, small_ref, rows_ref, winnew_ref, winst_ref, pe_ref, w1_ref, w2_ref,
     ovt_ref, rexp_ref, o_ref, seg_ref, ksat_ref, vst_ref, sm_ref, stage_ref, bias_ref) = refs[PAGES_PER_STEP:]
    del pt_ref
    pg = pl.program_id(1)
    nseg = seg_ref.shape[0]
    nrows = NSA_HEADS * ts
    kt = NSA_KV_HEADS * ts
    half_w = NSA_KV_HEADS * CMP_STRIDE * HEAD_DIM

    @pl.when((pl.program_id(0) == 0) & (pg == 0))
    def _():
        for part in range(2):
            hi, mid, lo = _split3(pe_ref[part])
            w = w1_ref[part]
            bias_ref[part] = _dot(hi, w) + _dot(mid, w) + _dot(lo, w)

    for e in range(PAGES_PER_STEP):
        for part in range(2):
            xt = pages[e][0, 0, part * LANES:(part + 1) * LANES, :].T
            for j in range(PAGE_SIZE // 8):
                start = (j % 2) * (PAGE_SIZE // 2) + j // 2
                stage_ref[e, part, pl.ds(start, 8, stride=SEG_PER_PAGE), :] = xt[8 * j:8 * j + 8, :]
        k0 = pl.multiple_of((pg * PAGES_PER_STEP + e) * PAGE_SIZE, PAGE_SIZE)
        ksat_ref[0:LANES, pl.ds(k0, PAGE_SIZE)] = pages[e][0, 0, 2 * LANES:3 * LANES, :].astype(BF16)
        vst_ref[:, pl.ds(k0, PAGE_SIZE)] = pages[e][0, 0, 3 * LANES:4 * LANES, :].astype(BF16)

        @pl.when(pl.program_id(0) == 0)
        def _():
            kpos = k0 + lax.broadcasted_iota(jnp.int32, (LANES, PAGE_SIZE), 1)
            blk = lax.broadcasted_iota(jnp.int32, (LANES, PAGE_SIZE), 0)
            ksat_ref[LANES:2 * LANES, pl.ds(k0, PAGE_SIZE)] = jnp.where(
                kpos // SEL_BLOCK == blk, 1.0, 0.0).astype(BF16)
    for pair in range(PAGES_PER_STEP // 2):
        r0 = pl.multiple_of(pg * (PAGES_PER_STEP * SEG_PER_PAGE) + pair * 2 * SEG_PER_PAGE, 2 * SEG_PER_PAGE)
        for part in range(2):
            for p in range(CMP_STRIDE):
                x = jnp.concatenate([stage_ref[2 * pair + e, part, p * SEG_PER_PAGE:(p + 1) * SEG_PER_PAGE, :]
                                     for e in range(2)], axis=0)
                c0 = part * half_w + p * LANES
                seg_ref[pl.ds(r0, 2 * SEG_PER_PAGE), c0:c0 + LANES] = x.astype(BF16)

    @pl.when(pg == pl.num_programs(1) - 1)
    def _():
        ns_past = past // SEL_BLOCK
        nc = nseg - 1
        q_raw = qraw_ref[0]
        q_rot = qrot_ref[0]
        q_rot_f = q_rot.astype(F32)
        row = lax.broadcasted_iota(jnp.int32, (nrows, 1), 0)
        t_row = row % ts
        kcvc = []
        for part in range(2):
            res = _dot(seg_ref[:, part * half_w:(part + 1) * half_w], w1_ref[part])
            b8 = bias_ref[part]
            pre = (res[:, 0:LANES] + b8[0:1, 0:LANES]
                   + pltpu.roll(res[:, LANES:2 * LANES] + b8[1:2, LANES:2 * LANES], nseg - 1, 0))
            kcvc.append(_dot(jax.nn.gelu(pre).astype(BF16), w2_ref[part]).astype(BF16))
        s_c = _dot_nt(q_raw, kcvc[0])
        p_c = _softmax_rows(s_c, lax.broadcasted_iota(jnp.int32, (nrows, nseg), 1) < nc)
        o_c = _dot(p_c.astype(BF16), kcvc[1])
        p_sum = p_c[0:kt] + p_c[kt:2 * kt] + p_c[2 * kt:3 * kt] + p_c[3 * kt:4 * kt]
        p_sum = jnp.concatenate([p_sum, jnp.zeros((LANES - kt, nseg), F32)], axis=0)
        nb = ovt_ref.shape[0]
        imp_t = _dot_exact_lhs_nt(ovt_ref[...], p_sum)
        blk_t = lax.broadcasted_iota(jnp.int32, (nb, LANES), 0)
        qpos_t = past + lax.broadcasted_iota(jnp.int32, (nb, LANES), 1) % ts
        cur_t = qpos_t // SEL_BLOCK
        forced_t = (blk_t == 0) | (blk_t == cur_t) | (blk_t == cur_t - 1)
        imp_t = jnp.where(blk_t * SEL_BLOCK <= qpos_t, imp_t + jnp.where(forced_t, FORCE_BONUS, 0.0), -jnp.inf)
        bias_t = _topk_bias_t(imp_t, -(-(past + ts) // SEL_BLOCK), N_SEL)[0:ns_past]
        bias = bias_t.T[0:kt]
        q_sel = jnp.concatenate([q_rot, jnp.concatenate([bias] * NSA_GROUP, axis=0).astype(BF16)], axis=1)

        def attend(s_past, v_past16_t, k_new, v_new, mask_past):
            if mask_past is not None:
                s_past = jnp.where(mask_past, s_past, NEG_BIG)
            s_new = [jnp.where(t_row >= j, jnp.sum(q_rot_f * k_new[j:j + 1, :], axis=-1, keepdims=True), NEG_BIG)
                     for j in range(ts)]
            m = jnp.max(s_past, axis=-1, keepdims=True)
            for s in s_new:
                m = jnp.maximum(m, s)
            p_past = jnp.exp(s_past - m)
            if mask_past is not None:
                p_past = jnp.where(mask_past, p_past, 0.0)
            acc = _dot_nt(p_past.astype(BF16), v_past16_t)
            l = jnp.sum(p_past, axis=-1, keepdims=True)
            for j, s in enumerate(s_new):
                p = jnp.where(t_row >= j, jnp.exp(s - m), 0.0)
                acc = acc + p * v_new[j:j + 1, :]
                l = l + p
            return acc * (1.0 / l)

        new = rows_ref[0]
        o_s = attend(_dot(q_sel, ksat_ref[...]), vst_ref[...], new[:, 2 * NSA_KV_W:3 * NSA_KV_W],
                     new[:, 3 * NSA_KV_W:4 * NSA_KV_W], None)
        wbuf = winst_ref.shape[3]
        wnew = winnew_ref[0]
        i_w = lax.broadcasted_iota(jnp.int32, (nrows, wbuf), 1)
        dist = t_row + wbuf - i_w
        o_w = attend(_dot(q_rot, winst_ref[0, 0, 0:LANES, :].astype(BF16)),
                     winst_ref[0, 0, LANES:2 * LANES, :].astype(BF16),
                     wnew[:, 0:LANES], wnew[:, LANES:2 * LANES],
                     (dist < WINDOW) & (past - wbuf + i_w >= 0))

        sm_ref[...] = jnp.zeros_like(sm_ref)
        sm_ref[0:ts, :] = small_ref[0]
        gexp = _dot_exact_rhs(_sigmoid(sm_ref[...]), rexp_ref[...])[0:kt]
        low = lax.broadcasted_iota(jnp.int32, (kt, LANES), 1) < HEAD_DIM
        for g in range(NSA_GROUP):
            out = jnp.zeros((kt, LANES), F32)
            for br, o_br in enumerate((o_c, o_s, o_w)):
                og = o_br[g * kt:(g + 1) * kt]
                both = jnp.where(low, og, pltpu.roll(og, kt - ts, 0))
                out = out + gexp[:, br * NSA_Q_W + g * LANES:br * NSA_Q_W + (g + 1) * LANES] * both
            o_ref[0, :, g * LANES:(g + 1) * LANES] = out


def _nsa_sample(layer, cache, page_table, qraw_s, qrot_s, small_s, rows_s, winnew_s, win_state,
                pe, w1, w2, ovt, rexp, past):
    b, ts = small_s.shape[:2]
    n_pages = page_table.shape[1]
    assert n_pages % PAGES_PER_STEP == 0 and ts * NSA_KV_HEADS == 8 and past == n_pages * PAGE_SIZE
    nseg = past // CMP_STRIDE
    wbuf = win_state.shape[3]
    kw = 4 * NSA_KV_W

    def page_spec(e):
        return pl.BlockSpec((1, 1, 4 * PAGE_SIZE, LANES),
                            lambda i, g, pt: (layer, pt[i, g * PAGES_PER_STEP + e], 0, 0))
    per_b = lambda shape: pl.BlockSpec((1,) + shape, lambda i, g, pt: (i,) + (0,) * len(shape))
    const = lambda shape: pl.BlockSpec(shape, lambda i, g, pt: (0,) * len(shape))
    grid_spec = pltpu.PrefetchScalarGridSpec(
        num_scalar_prefetch=1,
        grid=(b, n_pages // PAGES_PER_STEP),
        in_specs=[page_spec(e) for e in range(PAGES_PER_STEP)] + [
            per_b((NSA_HEADS * ts, LANES)), per_b((NSA_HEADS * ts, LANES)), per_b((ts, LANES)),
            per_b((ts, kw)), per_b((ts, 2 * NSA_KV_W)),
            pl.BlockSpec((1, 1, 2 * NSA_KV_W, wbuf), lambda i, g, pt: (layer, i, 0, 0)),
            const(pe.shape), const(w1.shape), const(w2.shape), const(ovt.shape), const(rexp.shape)],
        out_specs=per_b((NSA_KV_HEADS * ts, NSA_Q_W)),
        scratch_shapes=[pltpu.VMEM((nseg, SEG_W), BF16),
                        pltpu.VMEM((2 * LANES, past), BF16), pltpu.VMEM((LANES, past), BF16),
                        pltpu.VMEM((2 * NSA_KV_HEADS * ts, LANES), F32),
                        pltpu.VMEM((PAGES_PER_STEP, 2, PAGE_SIZE, LANES), F32),
                        pltpu.VMEM((2, 8, 2 * LANES), F32)])
    return pl.pallas_call(
        functools.partial(_nsa_sample_kernel, ts, past),
        grid_spec=grid_spec,
        out_shape=jax.ShapeDtypeStruct((b, NSA_KV_HEADS * ts, NSA_Q_W), F32),
        compiler_params=_params("arbitrary", "arbitrary"), name="nsa_sample",
    )(page_table, *([cache] * PAGES_PER_STEP), qraw_s, qrot_s, small_s, rows_s, winnew_s, win_state,
      pe, w1, w2, ovt, rexp)


DN_C = 128
DN_CHUNKS_PER_STEP = 2
HALO = 8


def _softplus(x):
    return jnp.maximum(x, 0.0) + jnp.log(1.0 + jnp.exp(-jnp.abs(x)))


def _l2n(x):
    return x * lax.rsqrt(jnp.sum(x * x, axis=-1, keepdims=True) + EPS)


def _dn_prep_kernel(n_valid, qkv_ref, halo_ref, prefix_ref, small_ref, cw_ref, alog_ref, dtb_ref,
                    tri_ref, u_ref, w_ref, qd_ref, qk_ref, kdt_ref, gl_ref, xcat_ref, sm_ref):
    n = pl.program_id(1)
    cps = kdt_ref.shape[0]
    rows = cps * DN_C
    if n_valid == DN_C:
        xcat_ref[HALO:HALO + rows, :] = qkv_ref[...]
        small = small_ref[...]

        @pl.when(n == 0)
        def _():
            xcat_ref[0:HALO, :] = prefix_ref[0]

        @pl.when(n > 0)
        def _():
            xcat_ref[0:HALO, :] = halo_ref[...]
    else:
        xcat_ref[...] = jnp.zeros_like(xcat_ref)
        xcat_ref[0:HALO, :] = prefix_ref[0]
        xcat_ref[HALO:HALO + n_valid, :] = qkv_ref[0]
        sm_ref[...] = jnp.zeros_like(sm_ref)
        sm_ref[0:n_valid, :] = small_ref[0]
        small = sm_ref[...]

    conv = jnp.zeros((rows, DN_QKV_W), F32)
    for i in range(CONV_W):
        conv = conv + xcat_ref[HALO - (CONV_W - 1) + i:HALO - (CONV_W - 1) + i + rows, :] * cw_ref[i:i + 1, :]
    conv = _silu(conv)

    ii = lax.broadcasted_iota(jnp.int32, (DN_C, DN_C), 0)
    jj = lax.broadcasted_iota(jnp.int32, (DN_C, DN_C), 1)
    g_rows = -jnp.exp(alog_ref[...]) * _softplus(small + dtb_ref[...])
    beta_rows = _sigmoid(small)
    chains = [(c, h) for c in range(cps) for h in range(DN_HEADS)]
    gcol, beta, decay = {}, {}, {}
    for c in range(cps):
        g_all = g_rows[c * DN_C:(c + 1) * DN_C]
        beta_all = beta_rows[c * DN_C:(c + 1) * DN_C]
        if n_valid != DN_C:
            g_all = jnp.where(ii < n_valid, g_all, 0.0)
            beta_all = jnp.where(ii < n_valid, beta_all, 0.0)
        gc_all = _dot_exact_lhs(tri_ref[...], g_all)
        gr_all = gc_all.T
        for h in range(DN_HEADS):
            gcol[c, h] = gc_all[:, SM_A + h:SM_A + h + 1]
            beta[c, h] = beta_all[:, SM_B + h:SM_B + h + 1]
            decay[c, h] = jnp.where(ii >= jj, jnp.exp(gcol[c, h] - gr_all[SM_A + h:SM_A + h + 1, :]), 0.0)
    tok = lambda c: slice(c * DN_C, (c + 1) * DN_C)
    q = {(c, h): _l2n(conv[tok(c), h * DN_DK:(h + 1) * DN_DK]) * DN_DK ** -0.5 for c, h in chains}
    k = {(c, h): _l2n(conv[tok(c), DN_QK_W + h * DN_DK:DN_QK_W + (h + 1) * DN_DK]) for c, h in chains}
    v = {(c, h): conv[tok(c), 2 * DN_QK_W + h * DN_DV:2 * DN_QK_W + (h + 1) * DN_DV] for c, h in chains}
    kb = {i: k[i] * beta[i] for i in chains}
    k16 = {i: k[i].astype(BF16) for i in chains}
    eg = {i: jnp.exp(gcol[i]) for i in chains}
    a = {i: jnp.where(ii > jj, _dot_nt(kb[i].astype(BF16), k16[i]) * decay[i], 0.0) for i in chains}
    t = {i: jnp.where(ii == jj, 1.0, 0.0) - a[i] for i in chains}
    for _ in range(max(1, math.ceil(math.log2(n_valid))) - 1):
        a = {i: _dot3(a[i], a[i]) for i in chains}
        t = {i: t[i] + _dot3(t[i], a[i]) for i in chains}
    x = {i: _dot3(t[i], jnp.concatenate([v[i] * beta[i], kb[i] * eg[i]], axis=1)) for i in chains}
    for c, h in chains:
        i = (c, h)
        u_ref[tok(c), h * DN_DV:(h + 1) * DN_DV] = x[i][:, 0:DN_DV]
        w_ref[tok(c), h * DN_DK:(h + 1) * DN_DK] = x[i][:, DN_DV:DN_DV + DN_DK].astype(BF16)
        qk = jnp.where(ii >= jj, _dot_nt(q[i].astype(BF16), k16[i]) * decay[i], 0.0)
        qk_ref[tok(c), h * DN_C:(h + 1) * DN_C] = qk.astype(BF16)
        qd_ref[tok(c), h * DN_DK:(h + 1) * DN_DK] = (q[i] * eg[i]).astype(BF16)
        glast = gcol[i][DN_C - 1:DN_C, :]
        kdt_ref[c, h * DN_DK:(h + 1) * DN_DK, :] = (k[i] * jnp.exp(glast - gcol[i])).T.astype(BF16)
        gl_ref[c, h:h + 1, :] = jnp.broadcast_to(jnp.exp(glast), (1, LANES))
    for c in range(cps):
        gl_ref[c, DN_HEADS:, :] = jnp.zeros((HALO - DN_HEADS, LANES), F32)


def _dn_prep(qkv, prefix8, small, cw, alog_row, dtb_row, tri, b, nch, n_valid):
    rows = b * nch * DN_C
    cps = DN_CHUNKS_PER_STEP if (n_valid == DN_C and nch % DN_CHUNKS_PER_STEP == 0) else 1
    steps = nch // cps
    blk = cps * DN_C
    if n_valid == DN_C:
        qkv_spec = pl.BlockSpec((blk, DN_QKV_W), lambda i, n: (i * steps + n, 0))
        halo_spec = pl.BlockSpec((HALO, DN_QKV_W),
                                 lambda i, n: (jnp.maximum((i * steps + n) * (blk // HALO) - 1, 0), 0))
        small_spec = pl.BlockSpec((blk, LANES), lambda i, n: (i * steps + n, 0))
        qkv_in, halo_in, small_in = qkv, qkv, small
    else:
        qkv_in = qkv.reshape(b, n_valid, DN_QKV_W)
        small_in = small.reshape(b, n_valid, LANES)
        halo_in = prefix8
        qkv_spec = pl.BlockSpec((1, n_valid, DN_QKV_W), lambda i, n: (i, 0, 0))
        halo_spec = pl.BlockSpec((1, HALO, DN_QKV_W), lambda i, n: (i, 0, 0))
        small_spec = pl.BlockSpec((1, n_valid, LANES), lambda i, n: (i, 0, 0))
    tok = lambda width: pl.BlockSpec((blk, width), lambda i, n: (i * steps + n, 0))
    return pl.pallas_call(
        functools.partial(_dn_prep_kernel, n_valid),
        grid=(b, steps),
        in_specs=[qkv_spec, halo_spec, pl.BlockSpec((1, HALO, DN_QKV_W), lambda i, n: (i, 0, 0)), small_spec,
                  _const_spec((CONV_W, DN_QKV_W)), _const_spec((1, LANES)), _const_spec((1, LANES)),
                  _const_spec((DN_C, DN_C))],
        out_specs=[tok(DN_W), tok(DN_QK_W), tok(DN_QK_W), tok(DN_HEADS * DN_C),
                   pl.BlockSpec((cps, DN_QK_W, DN_C), lambda i, n: (i * steps + n, 0, 0)),
                   pl.BlockSpec((cps, HALO, LANES), lambda i, n: (i * steps + n, 0, 0))],
        out_shape=[jax.ShapeDtypeStruct((rows, DN_W), F32), jax.ShapeDtypeStruct((rows, DN_QK_W), BF16),
                   jax.ShapeDtypeStruct((rows, DN_QK_W), BF16), jax.ShapeDtypeStruct((rows, DN_HEADS * DN_C), BF16),
                   jax.ShapeDtypeStruct((b * nch, DN_QK_W, DN_C), BF16),
                   jax.ShapeDtypeStruct((b * nch, HALO, LANES), F32)],
        scratch_shapes=[pltpu.VMEM((HALO + blk, DN_QKV_W), F32), pltpu.VMEM((DN_C, LANES), F32)],
        compiler_params=_params("parallel", "parallel"), name="dn_prep",
    )(qkv_in, halo_in, prefix8, small_in, cw, alog_row, dtb_row, tri)


def _dn_scan_kernel(n_valid, u_ref, w_ref, qd_ref, qk_ref, kdt_ref, gl_ref, zg_ref, s0_ref, nw_ref,
                    o_ref, s_ref):
    @pl.when(pl.program_id(1) == 0)
    def _():
        s_ref[...] = s0_ref[...]

    for b in range(u_ref.shape[0]):
        for h in range(DN_HEADS):
            cs = slice(h * DN_DK, (h + 1) * DN_DK)
            s = s_ref[b, h]
            s16 = s.astype(BF16)
            v_new = u_ref[b, :, cs] - _dot(w_ref[b, :, cs], s16)
            v16 = v_new.astype(BF16)
            o = _dot(qd_ref[b, :, cs], s16) + _dot(qk_ref[b, :, cs], v16)
            s_ref[b, h] = s * gl_ref[b, h:h + 1, :] + _dot(kdt_ref[b, cs, :], v16)
            on = o * lax.rsqrt(jnp.mean(o * o, axis=-1, keepdims=True) + EPS) * nw_ref[...]
            o_ref[b, :, cs] = (on[0:n_valid] * _silu(zg_ref[b, :, cs])).astype(BF16)


def _dn_scan(u, w, qd, qk, kdt, gl, zg, s0, nw, b, nch, bb, n_valid):
    tok = lambda width: pl.BlockSpec((bb, DN_C, width), lambda i, n: (i, n, 0))
    r3 = lambda a: a.reshape(b, nch * DN_C, a.shape[-1])
    st = pl.BlockSpec((bb, DN_HEADS, DN_DK, DN_DV), lambda i, n: (i, 0, 0, 0))
    return pl.pallas_call(
        functools.partial(_dn_scan_kernel, n_valid),
        grid=(b // bb, nch),
        in_specs=[tok(DN_W), tok(DN_QK_W), tok(DN_QK_W), tok(DN_HEADS * DN_C),
                  pl.BlockSpec((bb, DN_QK_W, DN_C), lambda i, n: (i, n, 0)),
                  pl.BlockSpec((bb, HALO, LANES), lambda i, n: (i, n, 0)),
                  pl.BlockSpec((bb, n_valid, DN_W), lambda i, n: (i, n, 0)), st, _const_spec((1, DN_DV))],
        out_specs=[pl.BlockSpec((bb, n_valid, DN_W), lambda i, n: (i, n, 0)), st],
        out_shape=[jax.ShapeDtypeStruct((b, nch * n_valid, DN_W), BF16),
                   jax.ShapeDtypeStruct((b, DN_HEADS, DN_DK, DN_DV), F32)],
        compiler_params=_params("parallel", "arbitrary"), name="dn_scan",
    )(r3(u), r3(w), r3(qd), r3(qk), kdt.reshape(b, nch * DN_QK_W, DN_C), gl.reshape(b, nch * HALO, LANES),
      zg.reshape(b, nch * n_valid, DN_W), s0, nw)


IN_WIDTHS = (NSA_Q_W, 6 * NSA_KV_W, 3 * NSA_HEADS, DN_QKV_W, DN_HEADS, DN_HEADS, DN_W, 2 * D_MODEL)


def _pack_mix_weight(w_in):
    offs = np.cumsum(IN_WIDTHS)[:-1].tolist()
    q, kv, gate, qkv, a, b, zg, merge = jnp.split(w_in, offs, axis=-1)
    lead = w_in.shape[:-1]
    small = jnp.concatenate([gate, a, b, jnp.zeros(lead + (LANES - SM_B - DN_HEADS,), w_in.dtype)], axis=-1)
    return jnp.concatenate([q, kv, small, qkv, zg, merge], axis=-1).astype(BF16)


def _rope_tables(pos):
    half = HEAD_DIM // 2
    freq = ROPE_THETA ** (-jnp.arange(half, dtype=F32) / half)
    ang = pos.astype(F32)[:, None] * freq[None, :]
    cos, sin = jnp.cos(ang), jnp.sin(ang)
    cos = jnp.concatenate([cos, cos] * (LANES // HEAD_DIM), axis=1)
    sin = jnp.concatenate([-sin, sin] * (LANES // HEAD_DIM), axis=1)
    return cos, sin


def _gate_expand_matrix():
    r = np.zeros((LANES, 3 * NSA_Q_W), np.float32)
    for h in range(NSA_HEADS):
        k, g = divmod(h, NSA_GROUP)
        for br in range(3):
            c0 = br * NSA_Q_W + g * LANES + k * HEAD_DIM
            r[SM_GATE + h * 3 + br, c0:c0 + HEAD_DIM] = 1.0
    return jnp.asarray(r, BF16)


def _overlap_t(nc_pad, nc, ns, rows):
    i = np.arange(nc_pad)[None, :]
    j = np.arange(rows)[:, None]
    ov = (i * CMP_STRIDE < (j + 1) * SEL_BLOCK) & (i * CMP_STRIDE + CMP_BLOCK > j * SEL_BLOCK)
    ov = ov & (i < nc) & (j < ns)
    return jnp.asarray(ov.astype(np.float32), BF16)


def _pack_compress_sample(pos_emb, w1, w2):
    eye = jnp.eye(NSA_KV_HEADS, dtype=w1.dtype)
    w = w1.reshape(2, 2, CMP_STRIDE, HEAD_DIM, HEAD_DIM)
    w = jnp.einsum('kapde,hg->kphdage', w, eye)
    w = w.reshape(2, NSA_KV_HEADS * CMP_STRIDE * HEAD_DIM, 2 * LANES).astype(BF16)
    pe = pos_emb.reshape(2, 2, CMP_STRIDE, 1, HEAD_DIM)
    pe = jnp.concatenate([pe] * NSA_KV_HEADS, axis=3).reshape(2, 2, -1)
    pe = jnp.concatenate([pe, jnp.zeros((2, 6, pe.shape[2]), pe.dtype)], axis=1)
    w2d = jnp.einsum('kde,hg->khdge', w2, eye).reshape(2, LANES, LANES).astype(BF16)
    return pe, w, w2d


def _sample_rows(q_pad, b, ts):
    q = q_pad.reshape(b, ts, NSA_KV_HEADS, NSA_GROUP, 1, HEAD_DIM)
    eye = jnp.eye(NSA_KV_HEADS, dtype=q.dtype).reshape(NSA_KV_HEADS, 1, NSA_KV_HEADS, 1)
    q = (q * eye).transpose(0, 3, 2, 1, 4, 5)
    return q.reshape(b, NSA_HEADS * ts, LANES)


def _dn_lane_row(vals):
    return jnp.zeros((1, LANES), F32).at[0, SM_A:SM_A + DN_HEADS].set(vals.astype(F32))


def _deltanet(qkv, small, zg, prefix, s0, dn_params, tri, b, seq_tokens):
    conv_w, a_log, dt_bias, norm_w = dn_params
    if seq_tokens % DN_C == 0:
        nch, n_valid, bb = seq_tokens // DN_C, DN_C, min(b, 4)
    else:
        nch, n_valid, bb = 1, seq_tokens, 4
    prefix8 = jnp.concatenate([jnp.zeros((b, HALO - (CONV_W - 1), DN_QKV_W), F32), prefix], axis=1)
    u, w, qd, qk, kdt, gl = _dn_prep(qkv, prefix8, small, conv_w, _dn_lane_row(a_log), _dn_lane_row(dt_bias),
                                     tri, b, nch, n_valid)
    ob, s_out = _dn_scan(u, w, qd, qk, kdt, gl, zg, s0, norm_w[None], b, nch, bb, n_valid)
    return ob.reshape(b * seq_tokens, DN_W), s_out


def _head_perm_rows(wa):
    d = wa.shape[1]
    return wa.reshape(NSA_KV_HEADS, NSA_GROUP, HEAD_DIM, d).transpose(1, 0, 2, 3).reshape(NSA_Q_W, d)


def _pack_compress(pos_emb, w1, w2):
    eye = jnp.eye(NSA_KV_HEADS, dtype=w1.dtype)
    pe = jnp.concatenate([pos_emb] * NSA_KV_HEADS, axis=2)
    w1p = w1.reshape(2, CMP_BLOCK, HEAD_DIM, HEAD_DIM)
    w1d = jnp.einsum('kpde,hg->kphdge', w1p, eye).reshape(2, CMP_BLOCK, LANES, LANES)
    w2d = jnp.einsum('kde,hg->khdge', w2, eye).reshape(2, LANES, LANES)
    return pe, w1d.astype(BF16), w2d.astype(BF16)


def kernel(x_prompt, x_sample, cache_nsa_kv, state_nsa_win, state_dn_S, state_dn_conv, page_table,
           ffn1_norm, ffn1_w_in, ffn1_w_out, mix_norm, w_in, nsa_cmp_pos, nsa_cmp_w1, nsa_cmp_w2,
           dn_conv_w, dn_A_log, dn_dt_bias, dn_out_norm, w_branch_a, w_branch_b, w_out,
           ffn2_norm, ffn2_w_in, ffn2_w_out, final_norm):
    b, seq = x_prompt.shape[:2]
    bs, ts = x_sample.shape[:2]
    depth = w_in.shape[0]
    n_phys = cache_nsa_kv.shape[1]
    past = page_table.shape[1] * PAGE_SIZE
    wbuf = state_nsa_win.shape[2]
    kvw = 4 * NSA_KV_W

    xp = x_prompt.reshape(b * seq, D_MODEL)
    xs = x_sample.reshape(bs * ts, D_MODEL)
    cos_p, sin_p = _rope_tables(jnp.arange(seq, dtype=jnp.int32))
    cos_s, sin_s = _rope_tables(jnp.tile(past + jnp.arange(ts, dtype=jnp.int32), bs))
    nc = (seq - CMP_BLOCK) // CMP_STRIDE + 1
    ovt_p = _overlap_t(seq // CMP_STRIDE, nc, seq // SEL_BLOCK, LANES)
    tables_p = _attn_bias_tables(seq)
    ns_s = -(-(past + ts) // SEL_BLOCK)
    ovt_s = _overlap_t(past // CMP_STRIDE, past // CMP_STRIDE - 1, ns_s, -(-ns_s // 16) * 16)
    rexp = _gate_expand_matrix()
    tri = jnp.asarray(np.tril(np.ones((DN_C, DN_C), np.float32)), BF16)
    cache = cache_nsa_kv.transpose(0, 1, 3, 4, 5, 2).reshape(depth, n_phys, kvw, PAGE_SIZE)
    win_state = state_nsa_win.transpose(0, 1, 3, 4, 5, 2).reshape(depth, bs, 2 * NSA_KV_W, wbuf)
    tm_p, tm_s = 512, bs * ts

    outs = [[] for _ in range(8)]
    rows_all = None
    w_mix_all = _pack_mix_weight(w_in)
    for l in range(depth):
        f1 = (ffn1_norm[l][None], ffn1_w_in[l].astype(BF16), ffn1_w_out[l].astype(BF16))
        f2 = (ffn2_norm[l][None], ffn2_w_in[l].astype(BF16), ffn2_w_out[l].astype(BF16))
        w_mix = w_mix_all[l]
        mrg = (_head_perm_rows(w_branch_a[l]).astype(BF16), w_branch_b[l].astype(BF16), w_out[l].astype(BF16))
        dn_params = (dn_conv_w[l], dn_A_log[l], dn_dt_bias[l], dn_out_norm[l])
        cmp_l = (nsa_cmp_pos[l], nsa_cmp_w1[l], nsa_cmp_w2[l])

        xp = _ffn(xp, *f1, tm_p)
        (qraw, qrot, small, qkv, zg, mg, rows_all, win_t, kcmp, vcmp, ksa, kw16, vst, vwt) = _mix(
            xp, mix_norm[l][None], w_mix, cos_p, sin_p, 256, seq, True, l, depth, rows_all)
        kc, vct = _compress_prompt(kcmp, vcmp, *_pack_compress(*cmp_l), b, seq)
        oa = _nsa_prompt(qraw, qrot, small, kc, vct, ksa, vst, kw16, vwt, ovt_p, tables_p, b, seq)
        ob, s_fin = _deltanet(qkv, small, zg, jnp.zeros((b, CONV_W - 1, DN_QKV_W), F32),
                              jnp.zeros((b, DN_HEADS, DN_DK, DN_DV), F32), dn_params, tri, b, seq)
        xp = _merge(xp, oa, ob, mg, *mrg, tm_p)
        xp = _ffn(xp, *f2, tm_p)
        outs[1].append(win_t[:, :, seq - min(WINDOW, seq):])
        outs[2].append(s_fin)
        outs[3].append(qkv.reshape(b, seq, DN_QKV_W)[:, seq - (CONV_W - 1):])

        xs = _ffn(xs, *f1, tm_s)
        (qraw, qrot, small, qkv, zg, mg, rows, win) = _mix(
            xs, mix_norm[l][None], w_mix, cos_s, sin_s, tm_s, tm_s, False)
        oa = _nsa_sample(l, cache, page_table, _sample_rows(qraw, bs, ts), _sample_rows(qrot, bs, ts),
                         small.reshape(bs, ts, LANES), rows.reshape(bs, ts, kvw), win.reshape(bs, ts, 2 * NSA_KV_W),
                         win_state, *_pack_compress_sample(*cmp_l), ovt_s, rexp, past)
        oa = oa[:, :ts].reshape(bs * ts, NSA_Q_W).astype(BF16)
        ob, s_fin = _deltanet(qkv, small, zg, state_dn_conv[l], state_dn_S[l], dn_params, tri, bs, ts)
        xs = _merge(xs, oa, ob, mg, *mrg, tm_s)
        xs = _ffn(xs, *f2, tm_s)
        outs[4].append(rows.reshape(bs, ts, 4, NSA_KV_HEADS, HEAD_DIM))
        outs[5].append(win.reshape(bs, ts, 2, NSA_KV_HEADS, HEAD_DIM))
        outs[6].append(s_fin)
        outs[7].append(qkv.reshape(bs, ts, DN_QKV_W))

    y_prompt = _final_norm(xp, final_norm[None], tm_p).reshape(b, seq, D_MODEL)
    y_sample = _final_norm(xs, final_norm[None], tm_s).reshape(bs, ts, D_MODEL)
    res = [rows_all] + [jnp.stack(o) for o in outs[1:]]
    res[5] = jnp.concatenate([state_nsa_win, res[5]], axis=2)[:, :, ts:]
    res[7] = jnp.concatenate([state_dn_conv, res[7]], axis=2)[:, :, ts:]
    for i, parts in ((0, 4), (1, 2)):
        r = res[i]
        res[i] = r.reshape(depth, b, parts, NSA_KV_HEADS, HEAD_DIM, r.shape[-1]).transpose(0, 1, 5, 2, 3, 4)
    return (y_prompt, y_sample) + tuple(res)
```

```python
import functools
import math

import jax
import jax.numpy as jnp
import numpy as np
from jax import lax
from jax.experimental import pallas as pl
from jax.experimental.pallas import tpu as pltpu

F32 = jnp.float32
BF16 = jnp.bfloat16

D_MODEL = 1024
DEPTH = 4
PAGE_SIZE = 128
NSA_HEADS = 8
NSA_KV_HEADS = 2
HEAD_DIM = 64
NSA_GROUP = NSA_HEADS // NSA_KV_HEADS
NSA_Q_W = NSA_HEADS * HEAD_DIM
NSA_KV_W = NSA_KV_HEADS * HEAD_DIM
CMP_BLOCK = 32
CMP_STRIDE = 16
SEL_BLOCK = 64
N_SEL = 16
WINDOW = 512
FORCE_BONUS = 1e6
ROPE_THETA = 10000.0
DN_HEADS = 4
DN_DK = 128
DN_DV = 128
DN_QK_W = DN_HEADS * DN_DK
DN_W = DN_HEADS * DN_DV
DN_QKV_W = 2 * DN_QK_W + DN_W
CONV_W = 4
D_FF = 2816
EPS = 1e-6

LANES = 128
VT_ROWS = HEAD_DIM + 16
VMEM_LIMIT = 56 * 1024 * 1024
NEG_BIG = -1e30

C_Q = 0
C_KV = C_Q + NSA_Q_W
C_SMALL = C_KV + 6 * NSA_KV_W
C_QKV = C_SMALL + LANES
C_ZG = C_QKV + DN_QKV_W
C_MERGE = C_ZG + DN_W
MIX_W = C_MERGE + 2 * D_MODEL
SM_GATE = 0
SM_A = 3 * NSA_HEADS
SM_B = SM_A + DN_HEADS


def _dot(a, b):
    return jnp.dot(a, b, preferred_element_type=F32)


def _dot_nt(a, b):
    return lax.dot_general(a, b, (((1,), (1,)), ((), ())), preferred_element_type=F32)


def _split3(a):
    hi = a.astype(BF16)
    r1 = a - hi.astype(F32)
    mid = r1.astype(BF16)
    lo = (r1 - mid.astype(F32)).astype(BF16)
    return hi, mid, lo


def _dot_exact_rhs(a, b01):
    hi, mid, lo = _split3(a)
    return _dot(hi, b01) + _dot(mid, b01) + _dot(lo, b01)


def _dot_exact_lhs(a01, b):
    hi, mid, lo = _split3(b)
    return _dot(a01, hi) + _dot(a01, mid) + _dot(a01, lo)


def _dot_exact_lhs_nt(a01, b):
    hi, mid, lo = _split3(b)
    return _dot_nt(a01, hi) + _dot_nt(a01, mid) + _dot_nt(a01, lo)


def _dot3(a, b):
    ah = a.astype(BF16)
    al = (a - ah.astype(F32)).astype(BF16)
    bh = b.astype(BF16)
    bl = (b - bh.astype(F32)).astype(BF16)
    return _dot(ah, bh) + (_dot(ah, bl) + _dot(al, bh))


def _rms(x, w):
    return x * lax.rsqrt(jnp.mean(x * x, axis=-1, keepdims=True) + EPS) * w


def _sigmoid(x):
    return 1.0 / (1.0 + jnp.exp(-x))


def _silu(x):
    return x * _sigmoid(x)


def _params(*sem):
    return pltpu.CompilerParams(dimension_semantics=sem, vmem_limit_bytes=VMEM_LIMIT)


def _const_spec(shape):
    nd = len(shape)
    return pl.BlockSpec(shape, lambda *_: (0,) * nd)


FF_TILE = 256


def _ffn_kernel(x_ref, nw_ref, wi_ref, wo_ref, o_ref, act_ref):
    x = x_ref[...]
    hb = _rms(x, nw_ref[...]).astype(BF16)
    for j in range(D_FF // FF_TILE):
        g = _dot(hb, wi_ref[:, j * FF_TILE:(j + 1) * FF_TILE])
        u = _dot(hb, wi_ref[:, D_FF + j * FF_TILE:D_FF + (j + 1) * FF_TILE])
        act_ref[:, j * FF_TILE:(j + 1) * FF_TILE] = (_silu(g) * u).astype(BF16)
    o_ref[...] = x + 0.5 * _dot(act_ref[...], wo_ref[...])


def _ffn(x, nw, wi, wo, tm):
    m = x.shape[0]
    return pl.pallas_call(
        _ffn_kernel,
        grid=(m // tm,),
        in_specs=[pl.BlockSpec((tm, D_MODEL), lambda i: (i, 0)),
                  _const_spec((1, D_MODEL)),
                  _const_spec((D_MODEL, 2 * D_FF)),
                  _const_spec((D_FF, D_MODEL))],
        out_specs=pl.BlockSpec((tm, D_MODEL), lambda i: (i, 0)),
        out_shape=jax.ShapeDtypeStruct((m, D_MODEL), F32),
        scratch_shapes=[pltpu.VMEM((tm, D_FF), BF16)],
        compiler_params=_params("parallel"), name="half_ffn",
    )(x, nw, wi, wo)


def _rope128(x, cos, sin):
    lane = lax.broadcasted_iota(jnp.int32, x.shape, 1)
    first = (lane % HEAD_DIM) < (HEAD_DIM // 2)
    swapped = jnp.where(first, pltpu.roll(x, LANES - HEAD_DIM // 2, 1), pltpu.roll(x, HEAD_DIM // 2, 1))
    return x * cos + swapped * sin


def _mix_kernel(seq_tiles, prompt, n_passthrough, x_ref, nw_ref, w_ref, cos_ref, sin_ref, *refs):
    qraw_ref, qrot_ref, small_ref, qkv_ref, zg_ref, merge_ref, *group_refs = refs[n_passthrough:]
    tm = x_ref.shape[0]
    hb = _rms(x_ref[...], nw_ref[...]).astype(BF16)
    cos = cos_ref[...]
    sin = sin_ref[...]
    scale = HEAD_DIM ** -0.5 * (math.log2(math.e) if prompt else 1.0)
    for j in range(NSA_Q_W // LANES):
        q = _dot(hb, w_ref[:, C_Q + j * LANES:C_Q + (j + 1) * LANES])
        qraw_ref[:, j * LANES:(j + 1) * LANES] = (q * scale).astype(BF16)
        qrot_ref[:, j * LANES:(j + 1) * LANES] = (_rope128(q, cos, sin) * scale).astype(BF16)
    kv = _dot(hb, w_ref[:, C_KV:C_KV + 6 * NSA_KV_W])
    k_cmp = kv[:, 0:LANES]
    v_cmp = kv[:, LANES:2 * LANES]
    k_sel = _rope128(kv[:, 2 * LANES:3 * LANES], cos, sin)
    v_sel = kv[:, 3 * LANES:4 * LANES]
    k_win = _rope128(kv[:, 4 * LANES:5 * LANES], cos, sin)
    v_win = kv[:, 5 * LANES:6 * LANES]
    if prompt:
        rows_t_ref, win_t_ref, kcmp_ref, vcmp_ref, ksa_ref, kw16_ref, vst_ref, vwt_ref = group_refs
        ones = jnp.ones((VT_ROWS - HEAD_DIM, tm), F32)
        for part, a in enumerate((k_cmp, v_cmp, k_sel, v_sel)):
            rows_t_ref[0, 0, part * LANES:(part + 1) * LANES, :] = a.T
        for part, a in enumerate((k_win, v_win)):
            win_t_ref[0, part * LANES:(part + 1) * LANES, :] = a.T
        for v, vt_ref in ((v_sel, vst_ref), (v_win, vwt_ref)):
            vt = v.T
            for k in range(NSA_KV_HEADS):
                vt_ref[k] = jnp.concatenate([vt[k * HEAD_DIM:(k + 1) * HEAD_DIM], ones], axis=0).astype(BF16)
        kcmp_ref[...] = k_cmp
        vcmp_ref[...] = v_cmp
        pos = (pl.program_id(0) % seq_tiles) * tm + lax.broadcasted_iota(jnp.int32, (tm, LANES), 0)
        lane = lax.broadcasted_iota(jnp.int32, (tm, LANES), 1)
        ksa_ref[:, 0:LANES] = k_sel.astype(BF16)
        ksa_ref[:, LANES:2 * LANES] = jnp.where(pos // SEL_BLOCK == lane, 1.0, 0.0).astype(BF16)
        kw16_ref[...] = k_win.astype(BF16)
    else:
        rows_ref, win_ref = group_refs
        for part, a in enumerate((k_cmp, v_cmp, k_sel, v_sel)):
            rows_ref[:, part * LANES:(part + 1) * LANES] = a
        win_ref[:, 0:LANES] = k_win
        win_ref[:, LANES:2 * LANES] = v_win
    small_ref[...] = _dot(hb, w_ref[:, C_SMALL:C_SMALL + LANES])
    qkv_ref[...] = _dot(hb, w_ref[:, C_QKV:C_QKV + DN_QKV_W])
    zg_ref[...] = _dot(hb, w_ref[:, C_ZG:C_ZG + DN_W])
    merge_ref[...] = _dot(hb, w_ref[:, C_MERGE:C_MERGE + 2 * D_MODEL])


def _mix(x, nw, w, cos, sin, tm, seq_len, prompt, layer=0, depth=1, rows_all=None):
    m = x.shape[0]
    seq_tiles = max(seq_len // tm, 1)
    row = lambda width: pl.BlockSpec((tm, width), lambda i: (i, 0))
    tab = pl.BlockSpec((tm, LANES), lambda i: (i % seq_tiles, 0))
    widths = [(NSA_Q_W, BF16), (NSA_Q_W, BF16), (LANES, F32), (DN_QKV_W, F32), (DN_W, F32), (2 * D_MODEL, F32)]
    inputs = [x, nw, w, cos, sin]
    in_specs = [row(D_MODEL), _const_spec((1, D_MODEL)), _const_spec((D_MODEL, MIX_W)), tab, tab]
    aliases = {}
    if prompt:
        nb = m // seq_len
        vt_spec = pl.BlockSpec((NSA_KV_HEADS, VT_ROWS, tm), lambda i: (0, 0, i))
        vt_shape = jax.ShapeDtypeStruct((NSA_KV_HEADS, VT_ROWS, m), BF16)
        g_widths = [(LANES, F32), (LANES, F32), (2 * LANES, BF16), (LANES, BF16)]
        g_specs = ([pl.BlockSpec((1, 1, 4 * NSA_KV_W, tm), lambda i: (layer, i // seq_tiles, 0, i % seq_tiles)),
                    pl.BlockSpec((1, 2 * NSA_KV_W, tm), lambda i: (i // seq_tiles, 0, i % seq_tiles))]
                   + [row(wd) for wd, _ in g_widths] + [vt_spec, vt_spec])
        g_shapes = ([jax.ShapeDtypeStruct((depth, nb, 4 * NSA_KV_W, seq_len), F32),
                     jax.ShapeDtypeStruct((nb, 2 * NSA_KV_W, seq_len), F32)]
                    + [jax.ShapeDtypeStruct((m, wd), dt) for wd, dt in g_widths] + [vt_shape, vt_shape])
        if rows_all is not None:
            aliases = {len(inputs): len(widths)}
            inputs.append(rows_all)
            in_specs.append(pl.BlockSpec(memory_space=pl.ANY))
    else:
        g_widths = [(4 * NSA_KV_W, F32), (2 * NSA_KV_W, F32)]
        g_specs = [row(wd) for wd, _ in g_widths]
        g_shapes = [jax.ShapeDtypeStruct((m, wd), dt) for wd, dt in g_widths]
    return pl.pallas_call(
        functools.partial(_mix_kernel, seq_tiles, prompt, len(aliases)),
        grid=(m // tm,),
        in_specs=in_specs,
        out_specs=[row(wd) for wd, _ in widths] + g_specs,
        out_shape=[jax.ShapeDtypeStruct((m, wd), dt) for wd, dt in widths] + g_shapes,
        input_output_aliases=aliases,
        compiler_params=_params("parallel"), name="mix_in",
    )(*inputs)


def _merge_kernel(x_ref, oa_ref, ob_ref, mg_ref, wa_ref, wb_ref, wo_ref, o_ref):
    mg = mg_ref[...]
    ya = _dot(oa_ref[...], wa_ref[...])
    yb = _dot(ob_ref[...], wb_ref[...])
    y = _sigmoid(mg[:, :D_MODEL]) * ya + _sigmoid(mg[:, D_MODEL:]) * yb
    o_ref[...] = x_ref[...] + _dot(y.astype(BF16), wo_ref[...])


def _merge(x, oa, ob, mg, wa, wb, wo, tm):
    m = x.shape[0]
    row = lambda width: pl.BlockSpec((tm, width), lambda i: (i, 0))
    return pl.pallas_call(
        _merge_kernel,
        grid=(m // tm,),
        in_specs=[row(D_MODEL), row(NSA_Q_W), row(DN_W), row(2 * D_MODEL),
                  _const_spec((NSA_Q_W, D_MODEL)), _const_spec((DN_W, D_MODEL)),
                  _const_spec((D_MODEL, D_MODEL))],
        out_specs=row(D_MODEL),
        out_shape=jax.ShapeDtypeStruct((m, D_MODEL), F32),
        compiler_params=_params("parallel"), name="branch_merge",
    )(x, oa, ob, mg, wa, wb, wo)


def _norm_kernel(x_ref, w_ref, o_ref):
    o_ref[...] = _rms(x_ref[...], w_ref[...])


def _final_norm(x, w, tm):
    m = x.shape[0]
    row = pl.BlockSpec((tm, D_MODEL), lambda i: (i, 0))
    return pl.pallas_call(
        _norm_kernel, grid=(m // tm,), in_specs=[row, _const_spec((1, D_MODEL))], out_specs=row,
        out_shape=jax.ShapeDtypeStruct((m, D_MODEL), F32), compiler_params=_params("parallel"),
    )(x, w)


def _compress_kernel(kcmp_ref, vcmp_ref, pe_ref, w1_ref, w2_ref, kc_ref, vct_ref):
    ns = kc_ref.shape[1]
    res = []
    for part, rows_ref in enumerate((kcmp_ref, vcmp_ref)):
        sa = jnp.zeros((ns, LANES), F32)
        sb = jnp.zeros((ns, LANES), F32)
        for p in range(CMP_STRIDE):
            x = rows_ref[pl.ds(p, ns, stride=CMP_STRIDE), :]
            sa = sa + _dot((x + pe_ref[part, p:p + 1, :]).astype(BF16), w1_ref[part, p])
            sb = sb + _dot((x + pe_ref[part, CMP_STRIDE + p:CMP_STRIDE + p + 1, :]).astype(BF16),
                           w1_ref[part, CMP_STRIDE + p])
        pre = sa + pltpu.roll(sb, ns - 1, 0)
        res.append(_dot(jax.nn.gelu(pre).astype(BF16), w2_ref[part]))
    kc_ref[0] = res[0].astype(BF16)
    vt = res[1].T
    ones = jnp.ones((VT_ROWS - HEAD_DIM, ns), F32)
    for k in range(NSA_KV_HEADS):
        vct_ref[0, k] = jnp.concatenate([vt[k * HEAD_DIM:(k + 1) * HEAD_DIM], ones], axis=0).astype(BF16)


def _compress_prompt(kcmp, vcmp, pe, w1, w2, b, seq):
    ns = seq // CMP_STRIDE
    tok = pl.BlockSpec((seq, LANES), lambda i: (i, 0))
    return pl.pallas_call(
        _compress_kernel,
        grid=(b,),
        in_specs=[tok, tok, _const_spec(pe.shape), _const_spec(w1.shape), _const_spec(w2.shape)],
        out_specs=[pl.BlockSpec((1, ns, LANES), lambda i: (i, 0, 0)),
                   pl.BlockSpec((1, NSA_KV_HEADS, VT_ROWS, ns), lambda i: (i, 0, 0, 0))],
        out_shape=[jax.ShapeDtypeStruct((b, ns, LANES), BF16),
                   jax.ShapeDtypeStruct((b, NSA_KV_HEADS, VT_ROWS, ns), BF16)],
        compiler_params=_params("parallel"), name="nsa_compress",
    )(kcmp, vcmp, pe, w1, w2)


TQ = 512
KT = TQ
SUB = 128
WIN_KEYS = WINDOW + SUB


def _softmax_rows(s, mask):
    s = jnp.where(mask, s, NEG_BIG)
    e = jnp.where(mask, jnp.exp(s - jnp.max(s, axis=-1, keepdims=True)), 0.0)
    return e * (1.0 / jnp.maximum(jnp.sum(e, axis=-1, keepdims=True), 1e-30))


def _topk_bias_t(imp_t, n_cand, n_sel):
    nr = imp_t.shape[0] // 8
    xs = [imp_t[8 * r:8 * r + 8, :] for r in range(nr)]
    cnt = [jnp.zeros_like(xs[0]) for _ in range(nr)]
    sub = lax.broadcasted_iota(jnp.int32, xs[0].shape, 0)
    for j in range(n_cand):
        rj = j // 8
        row = xs[rj][j % 8:j % 8 + 1, :]
        for r in range(nr):
            if r < rj:
                inc = jnp.where(row > xs[r], 1.0, 0.0)
            elif r > rj:
                inc = jnp.where(row >= xs[r], 1.0, 0.0)
            else:
                inc = jnp.where(sub > j % 8, jnp.where(row >= xs[r], 1.0, 0.0),
                                jnp.where(row > xs[r], 1.0, 0.0))
            cnt[r] = cnt[r] + inc
    return jnp.concatenate([jnp.where(c < n_sel, 0.0, NEG_BIG) for c in cnt], axis=0)


def _nsa_prompt_kernel(qraw_ref, qrot_ref, small_ref, kc_ref, vct_ref, ksa_ref, vst_ref, win_ref, vwt_ref,
                       ovt_ref, cbias_ref, dbias_ref, wbias_ref, o_ref):
    seq = ksa_ref.shape[1]
    ns = seq // SEL_BLOCK
    s0 = pl.multiple_of(pl.program_id(1) * TQ, TQ)
    cols = NSA_GROUP * TQ
    tile = lambda a: jnp.concatenate([a] * NSA_GROUP, axis=1)
    qpos_row = s0 + lax.broadcasted_iota(jnp.int32, (1, cols), 1) % TQ
    gates_t = _sigmoid(small_ref[...]).T

    blk_t = lax.broadcasted_iota(jnp.int32, (ns, TQ), 0)
    qpos_t = s0 + lax.broadcasted_iota(jnp.int32, (ns, TQ), 1)
    cur_t = qpos_t // SEL_BLOCK
    valid_t = blk_t * SEL_BLOCK <= qpos_t
    forced_t = (blk_t == 0) | (blk_t == cur_t) | (blk_t == cur_t - 1)
    cbias = tile(cbias_ref[0])
    dbias = tile(dbias_ref[...])
    nsub = TQ // SUB

    def split_sum(aug):
        return aug[0:HEAD_DIM], aug[HEAD_DIM:HEAD_DIM + 1]

    def sub_cols(a, r):
        return jnp.concatenate([a[:, g * TQ + r * SUB:g * TQ + (r + 1) * SUB] for g in range(NSA_GROUP)], axis=1)

    def join_subs(parts):
        return jnp.concatenate([parts[r][:, g * SUB:(g + 1) * SUB]
                                for g in range(NSA_GROUP) for r in range(nsub)], axis=1)

    kv_heads = range(NSA_KV_HEADS)
    o_c, o_w, q_sel = [], [], []
    for k in kv_heads:
        def q_t(ref):
            zeros = jnp.zeros((HEAD_DIM, TQ), F32)
            blocks = []
            for pair in range(NSA_GROUP // 2):
                j = k * (NSA_GROUP // 2) + pair
                both = ref[:, j * LANES:(j + 1) * LANES].astype(F32).T
                for half in range(2):
                    qh = both[half * HEAD_DIM:(half + 1) * HEAD_DIM]
                    blocks.append(jnp.concatenate([qh, zeros] if k == 0 else [zeros, qh], axis=0))
            return jnp.concatenate(blocks, axis=1).astype(BF16)
        qt_raw = q_t(qraw_ref)
        qt_rot = q_t(qrot_ref)

        s = _dot(kc_ref[0], qt_raw) + cbias
        e = jnp.exp2(s - jnp.max(s, axis=0, keepdims=True))
        acc, l = split_sum(_dot(vct_ref[0, k], e.astype(BF16)))
        inv = jnp.where(qpos_row >= CMP_BLOCK - 1, 1.0 / l, 0.0)
        o_c.append(acc * inv)
        p = e * inv
        p_sum = p[:, 0:TQ] + p[:, TQ:2 * TQ] + p[:, 2 * TQ:3 * TQ] + p[:, 3 * TQ:4 * TQ]
        imp_t = _dot_exact_lhs(ovt_ref[...], p_sum)[0:ns]
        imp_t = jnp.where(valid_t, imp_t + jnp.where(forced_t, FORCE_BONUS, 0.0), -jnp.inf)
        bias_t = _topk_bias_t(imp_t, ns, min(N_SEL, ns))
        if ns < LANES:
            bias_t = jnp.concatenate([bias_t, jnp.zeros((LANES - ns, TQ), F32)], axis=0)
        q_sel.append(jnp.concatenate([qt_rot, tile(bias_t).astype(BF16)], axis=0))

        parts = []
        for r in range(nsub):
            q0 = s0 + r * SUB
            w0 = pl.multiple_of(jnp.maximum(q0 - WINDOW, 0), SUB)
            bias = tile(wbias_ref[jnp.minimum(q0 // SUB, WINDOW // SUB)])
            s = _dot(win_ref[0, pl.ds(w0, WIN_KEYS), :], sub_cols(qt_rot, r)) + bias
            e = jnp.exp2(s - jnp.max(s, axis=0, keepdims=True)).astype(BF16)
            acc, l = split_sum(_dot(vwt_ref[k, :, pl.ds(w0, WIN_KEYS)], e))
            parts.append(acc * (1.0 / l))
        o_w.append(join_subs(parts))

    def sel_step(c, carry):
        k0 = pl.multiple_of(c * KT, KT)
        keys = ksa_ref[0, pl.ds(k0, KT), :]
        out = []
        for k in kv_heads:
            m, acc = carry[k]
            s = _dot(keys, q_sel[k])
            m_new = jnp.maximum(m, jnp.max(s, axis=0, keepdims=True))
            p = jnp.exp2(s - m_new).astype(BF16)
            out.append((m_new, jnp.exp2(m - m_new) * acc + _dot(vst_ref[k, :, pl.ds(k0, KT)], p)))
        return tuple(out)
    init = tuple((jnp.full((1, cols), NEG_BIG, F32), jnp.zeros((VT_ROWS, cols), F32)) for _ in kv_heads)
    carry = lax.fori_loop(0, s0 // KT, sel_step, init)
    res = []
    for k in kv_heads:
        parts = []
        for r in range(nsub):
            nk = (r + 1) * SUB
            m, acc = sub_cols(carry[k][0], r), sub_cols(carry[k][1], r)
            s = _dot(ksa_ref[0, pl.ds(s0, nk), :], sub_cols(q_sel[k], r))
            s = jnp.concatenate([s[0:r * SUB], s[r * SUB:nk] + dbias], axis=0) if r else s + dbias
            m_new = jnp.maximum(m, jnp.max(s, axis=0, keepdims=True))
            p = jnp.exp2(s - m_new).astype(BF16)
            acc, l = split_sum(jnp.exp2(m - m_new) * acc + _dot(vst_ref[k, :, pl.ds(s0, nk)], p))
            parts.append(acc * (1.0 / l))
        res.append((o_c[k], join_subs(parts), o_w[k]))

    for g in range(NSA_GROUP):
        halves = []
        for k in range(NSA_KV_HEADS):
            r = SM_GATE + (k * NSA_GROUP + g) * 3
            halves.append(sum(gates_t[r + br:r + br + 1, :] * res[k][br][:, g * TQ:(g + 1) * TQ]
                              for br in range(3)))
        o_ref[:, g * LANES:(g + 1) * LANES] = jnp.concatenate(halves, axis=0).T.astype(BF16)


def _attn_bias_tables(seq):
    r = np.arange(TQ)[None, None, :]
    blocks = np.arange(seq // TQ)[:, None, None]
    c = np.arange(seq // CMP_STRIDE)[None, :, None]
    cb = c * CMP_STRIDE + CMP_BLOCK - 1 <= blocks * TQ + r
    rs = np.arange(SUB)[None, :]
    db = np.arange(SUB)[:, None] <= rs
    kw = np.arange(WIN_KEYS)[None, :, None]
    nw = WINDOW // SUB
    qpos = np.arange(nw + 1)[:, None, None] * SUB + rs[None]
    wb = (kw <= qpos) & (kw > qpos - WINDOW)
    f = lambda m: jnp.asarray(np.where(m, 0.0, NEG_BIG).astype(np.float32))
    return f(cb), f(db), f(wb)


def _nsa_prompt(qraw, qrot, small, kc, vct, ksa, vst, kw16, vwt, ovt, tables, b, seq):
    nc = kc.shape[1]
    nq = seq // TQ
    cb, db, wb = tables
    tok = lambda width: pl.BlockSpec((TQ, width), lambda i, j: (i * nq + j, 0))
    per_b = lambda width: pl.BlockSpec((1, seq, width), lambda i, j: (i, 0, 0))
    vt_spec = pl.BlockSpec((NSA_KV_HEADS, VT_ROWS, seq), lambda i, j: (0, 0, i))
    return pl.pallas_call(
        _nsa_prompt_kernel,
        grid=(b, nq),
        in_specs=[tok(NSA_Q_W), tok(NSA_Q_W), tok(LANES),
                  pl.BlockSpec((1, nc, LANES), lambda i, j: (i, 0, 0)),
                  pl.BlockSpec((1, NSA_KV_HEADS, VT_ROWS, nc), lambda i, j: (i, 0, 0, 0)),
                  per_b(2 * LANES), vt_spec, per_b(LANES), vt_spec,
                  _const_spec(ovt.shape),
                  pl.BlockSpec((1, nc, TQ), lambda i, j: (j, 0, 0)),
                  _const_spec(db.shape), _const_spec(wb.shape)],
        out_specs=tok(NSA_Q_W),
        out_shape=jax.ShapeDtypeStruct((b * seq, NSA_Q_W), BF16),
        compiler_params=_params("parallel", "parallel"), name="nsa_prompt",
    )(qraw, qrot, small, kc, vct, ksa.reshape(b, seq, 2 * LANES), vst, kw16.reshape(b, seq, LANES), vwt,
      ovt, cb, db, wb)


PAGES_PER_STEP = 32
SEG_PER_PAGE = PAGE_SIZE // CMP_STRIDE
SEG_W = CMP_STRIDE * 2 * NSA_KV_W


def _nsa_sample_kernel(ts, past, pt_ref, *refs):
    pages = refs[:PAGES_PER_STEP]
    (qraw_ref, qrot_ref, small_ref, rows_ref, winnew_ref, winst_ref, pe_ref, w1_ref, w2_ref,
     ovt_ref, rexp_ref, o_ref, seg_ref, ksat_ref, vst_ref, sm_ref, stage_ref, bias_ref) = refs[PAGES_PER_STEP:]
    del pt_ref
    pg = pl.program_id(1)
    nseg = seg_ref.shape[0]
    nrows = NSA_HEADS * ts
    kt = NSA_KV_HEADS * ts
    half_w = NSA_KV_HEADS * CMP_STRIDE * HEAD_DIM

    @pl.when((pl.program_id(0) == 0) & (pg == 0))
    def _():
        for part in range(2):
            hi, mid, lo = _split3(pe_ref[part])
            w = w1_ref[part]
            bias_ref[part] = _dot(hi, w) + _dot(mid, w) + _dot(lo, w)

    for e in range(PAGES_PER_STEP):
        for part in range(2):
            xt = pages[e][0, 0, part * LANES:(part + 1) * LANES, :].T
            for j in range(PAGE_SIZE // 8):
                start = (j % 2) * (PAGE_SIZE // 2) + j // 2
                stage_ref[e, part, pl.ds(start, 8, stride=SEG_PER_PAGE), :] = xt[8 * j:8 * j + 8, :]
        k0 = pl.multiple_of((pg * PAGES_PER_STEP + e) * PAGE_SIZE, PAGE_SIZE)
        ksat_ref[0:LANES, pl.ds(k0, PAGE_SIZE)] = pages[e][0, 0, 2 * LANES:3 * LANES, :].astype(BF16)
        vst_ref[:, pl.ds(k0, PAGE_SIZE)] = pages[e][0, 0, 3 * LANES:4 * LANES, :].astype(BF16)

        @pl.when(pl.program_id(0) == 0)
        def _():
            kpos = k0 + lax.broadcasted_iota(jnp.int32, (LANES, PAGE_SIZE), 1)
            blk = lax.broadcasted_iota(jnp.int32, (LANES, PAGE_SIZE), 0)
            ksat_ref[LANES:2 * LANES, pl.ds(k0, PAGE_SIZE)] = jnp.where(
                kpos // SEL_BLOCK == blk, 1.0, 0.0).astype(BF16)
    for pair in range(PAGES_PER_STEP // 2):
        r0 = pl.multiple_of(pg * (PAGES_PER_STEP * SEG_PER_PAGE) + pair * 2 * SEG_PER_PAGE, 2 * SEG_PER_PAGE)
        for part in range(2):
            for p in range(CMP_STRIDE):
                x = jnp.concatenate([stage_ref[2 * pair + e, part, p * SEG_PER_PAGE:(p + 1) * SEG_PER_PAGE, :]
                                     for e in range(2)], axis=0)
                c0 = part * half_w + p * LANES
                seg_ref[pl.ds(r0, 2 * SEG_PER_PAGE), c0:c0 + LANES] = x.astype(BF16)

    @pl.when(pg == pl.num_programs(1) - 1)
    def _():
        ns_past = past // SEL_BLOCK
        nc = nseg - 1
        q_raw = qraw_ref[0]
        q_rot = qrot_ref[0]
        q_rot_f = q_rot.astype(F32)
        row = lax.broadcasted_iota(jnp.int32, (nrows, 1), 0)
        t_row = row % ts
        kcvc = []
        for part in range(2):
            res = _dot(seg_ref[:, part * half_w:(part + 1) * half_w], w1_ref[part])
            b8 = bias_ref[part]
            pre = (res[:, 0:LANES] + b8[0:1, 0:LANES]
                   + pltpu.roll(res[:, LANES:2 * LANES] + b8[1:2, LANES:2 * LANES], nseg - 1, 0))
            kcvc.append(_dot(jax.nn.gelu(pre).astype(BF16), w2_ref[part]).astype(BF16))
        s_c = _dot_nt(q_raw, kcvc[0])
        p_c = _softmax_rows(s_c, lax.broadcasted_iota(jnp.int32, (nrows, nseg), 1) < nc)
        o_c = _dot(p_c.astype(BF16), kcvc[1])
        p_sum = p_c[0:kt] + p_c[kt:2 * kt] + p_c[2 * kt:3 * kt] + p_c[3 * kt:4 * kt]
        p_sum = jnp.concatenate([p_sum, jnp.zeros((LANES - kt, nseg), F32)], axis=0)
        nb = ovt_ref.shape[0]
        imp_t = _dot_exact_lhs_nt(ovt_ref[...], p_sum)
        blk_t = lax.broadcasted_iota(jnp.int32, (nb, LANES), 0)
        qpos_t = past + lax.broadcasted_iota(jnp.int32, (nb, LANES), 1) % ts
        cur_t = qpos_t // SEL_BLOCK
        forced_t = (blk_t == 0) | (blk_t == cur_t) | (blk_t == cur_t - 1)
        imp_t = jnp.where(blk_t * SEL_BLOCK <= qpos_t, imp_t + jnp.where(forced_t, FORCE_BONUS, 0.0), -jnp.inf)
        bias_t = _topk_bias_t(imp_t, -(-(past + ts) // SEL_BLOCK), N_SEL)[0:ns_past]
        bias = bias_t.T[0:kt]
        q_sel = jnp.concatenate([q_rot, jnp.concatenate([bias] * NSA_GROUP, axis=0).astype(BF16)], axis=1)

        def attend(s_past, v_past16_t, k_new, v_new, mask_past):
            if mask_past is not None:
                s_past = jnp.where(mask_past, s_past, NEG_BIG)
            s_new = [jnp.where(t_row >= j, jnp.sum(q_rot_f * k_new[j:j + 1, :], axis=-1, keepdims=True), NEG_BIG)
                     for j in range(ts)]
            m = jnp.max(s_past, axis=-1, keepdims=True)
            for s in s_new:
                m = jnp.maximum(m, s)
            p_past = jnp.exp(s_past - m)
            if mask_past is not None:
                p_past = jnp.where(mask_past, p_past, 0.0)
            acc = _dot_nt(p_past.astype(BF16), v_past16_t)
            l = jnp.sum(p_past, axis=-1, keepdims=True)
            for j, s in enumerate(s_new):
                p = jnp.where(t_row >= j, jnp.exp(s - m), 0.0)
                acc = acc + p * v_new[j:j + 1, :]
                l = l + p
            return acc * (1.0 / l)

        new = rows_ref[0]
        o_s = attend(_dot(q_sel, ksat_ref[...]), vst_ref[...], new[:, 2 * NSA_KV_W:3 * NSA_KV_W],
                     new[:, 3 * NSA_KV_W:4 * NSA_KV_W], None)
        wbuf = winst_ref.shape[3]
        wnew = winnew_ref[0]
        i_w = lax.broadcasted_iota(jnp.int32, (nrows, wbuf), 1)
        dist = t_row + wbuf - i_w
        o_w = attend(_dot(q_rot, winst_ref[0, 0, 0:LANES, :].astype(BF16)),
                     winst_ref[0, 0, LANES:2 * LANES, :].astype(BF16),
                     wnew[:, 0:LANES], wnew[:, LANES:2 * LANES],
                     (dist < WINDOW) & (past - wbuf + i_w >= 0))

        sm_ref[...] = jnp.zeros_like(sm_ref)
        sm_ref[0:ts, :] = small_ref[0]
        gexp = _dot_exact_rhs(_sigmoid(sm_ref[...]), rexp_ref[...])[0:kt]
        low = lax.broadcasted_iota(jnp.int32, (kt, LANES), 1) < HEAD_DIM
        for g in range(NSA_GROUP):
            out = jnp.zeros((kt, LANES), F32)
            for br, o_br in enumerate((o_c, o_s, o_w)):
                og = o_br[g * kt:(g + 1) * kt]
                both = jnp.where(low, og, pltpu.roll(og, kt - ts, 0))
                out = out + gexp[:, br * NSA_Q_W + g * LANES:br * NSA_Q_W + (g + 1) * LANES] * both
            o_ref[0, :, g * LANES:(g + 1) * LANES] = out


def _nsa_sample(layer, cache, page_table, qraw_s, qrot_s, small_s, rows_s, winnew_s, win_state,
                pe, w1, w2, ovt, rexp, past):
    b, ts = small_s.shape[:2]
    n_pages = page_table.shape[1]
    assert n_pages % PAGES_PER_STEP == 0 and ts * NSA_KV_HEADS == 8 and past == n_pages * PAGE_SIZE
    nseg = past // CMP_STRIDE
    wbuf = win_state.shape[3]
    kw = 4 * NSA_KV_W

    def page_spec(e):
        return pl.BlockSpec((1, 1, 4 * PAGE_SIZE, LANES),
                            lambda i, g, pt: (layer, pt[i, g * PAGES_PER_STEP + e], 0, 0))
    per_b = lambda shape: pl.BlockSpec((1,) + shape, lambda i, g, pt: (i,) + (0,) * len(shape))
    const = lambda shape: pl.BlockSpec(shape, lambda i, g, pt: (0,) * len(shape))
    grid_spec = pltpu.PrefetchScalarGridSpec(
        num_scalar_prefetch=1,
        grid=(b, n_pages // PAGES_PER_STEP),
        in_specs=[page_spec(e) for e in range(PAGES_PER_STEP)] + [
            per_b((NSA_HEADS * ts, LANES)), per_b((NSA_HEADS * ts, LANES)), per_b((ts, LANES)),
            per_b((ts, kw)), per_b((ts, 2 * NSA_KV_W)),
            pl.BlockSpec((1, 1, 2 * NSA_KV_W, wbuf), lambda i, g, pt: (layer, i, 0, 0)),
            const(pe.shape), const(w1.shape), const(w2.shape), const(ovt.shape), const(rexp.shape)],
        out_specs=per_b((NSA_KV_HEADS * ts, NSA_Q_W)),
        scratch_shapes=[pltpu.VMEM((nseg, SEG_W), BF16),
                        pltpu.VMEM((2 * LANES, past), BF16), pltpu.VMEM((LANES, past), BF16),
                        pltpu.VMEM((2 * NSA_KV_HEADS * ts, LANES), F32),
                        pltpu.VMEM((PAGES_PER_STEP, 2, PAGE_SIZE, LANES), F32),
                        pltpu.VMEM((2, 8, 2 * LANES), F32)])
    return pl.pallas_call(
        functools.partial(_nsa_sample_kernel, ts, past),
        grid_spec=grid_spec,
        out_shape=jax.ShapeDtypeStruct((b, NSA_KV_HEADS * ts, NSA_Q_W), F32),
        compiler_params=_params("arbitrary", "arbitrary"), name="nsa_sample",
    )(page_table, *([cache] * PAGES_PER_STEP), qraw_s, qrot_s, small_s, rows_s, winnew_s, win_state,
      pe, w1, w2, ovt, rexp)


DN_C = 128
DN_CHUNKS_PER_STEP = 2
HALO = 8


def _softplus(x):
    return jnp.maximum(x, 0.0) + jnp.log(1.0 + jnp.exp(-jnp.abs(x)))


def _l2n(x):
    return x * lax.rsqrt(jnp.sum(x * x, axis=-1, keepdims=True) + EPS)


def _dn_prep_kernel(n_valid, qkv_ref, halo_ref, prefix_ref, small_ref, cw_ref, alog_ref, dtb_ref,
                    tri_ref, u_ref, w_ref, qd_ref, qk_ref, kdt_ref, gl_ref, xcat_ref, sm_ref):
    n = pl.program_id(1)
    cps = kdt_ref.shape[0]
    rows = cps * DN_C
    if n_valid == DN_C:
        xcat_ref[HALO:HALO + rows, :] = qkv_ref[...]
        small = small_ref[...]

        @pl.when(n == 0)
        def _():
            xcat_ref[0:HALO, :] = prefix_ref[0]

        @pl.when(n > 0)
        def _():
            xcat_ref[0:HALO, :] = halo_ref[...]
    else:
        xcat_ref[...] = jnp.zeros_like(xcat_ref)
        xcat_ref[0:HALO, :] = prefix_ref[0]
        xcat_ref[HALO:HALO + n_valid, :] = qkv_ref[0]
        sm_ref[...] = jnp.zeros_like(sm_ref)
        sm_ref[0:n_valid, :] = small_ref[0]
        small = sm_ref[...]

    conv = jnp.zeros((rows, DN_QKV_W), F32)
    for i in range(CONV_W):
        conv = conv + xcat_ref[HALO - (CONV_W - 1) + i:HALO - (CONV_W - 1) + i + rows, :] * cw_ref[i:i + 1, :]
    conv = _silu(conv)

    ii = lax.broadcasted_iota(jnp.int32, (DN_C, DN_C), 0)
    jj = lax.broadcasted_iota(jnp.int32, (DN_C, DN_C), 1)
    g_rows = -jnp.exp(alog_ref[...]) * _softplus(small + dtb_ref[...])
    beta_rows = _sigmoid(small)
    chains = [(c, h) for c in range(cps) for h in range(DN_HEADS)]
    gcol, beta, decay = {}, {}, {}
    for c in range(cps):
        g_all = g_rows[c * DN_C:(c + 1) * DN_C]
        beta_all = beta_rows[c * DN_C:(c + 1) * DN_C]
        if n_valid != DN_C:
            g_all = jnp.where(ii < n_valid, g_all, 0.0)
            beta_all = jnp.where(ii < n_valid, beta_all, 0.0)
        gc_all = _dot_exact_lhs(tri_ref[...], g_all)
        gr_all = gc_all.T
        for h in range(DN_HEADS):
            gcol[c, h] = gc_all[:, SM_A + h:SM_A + h + 1]
            beta[c, h] = beta_all[:, SM_B + h:SM_B + h + 1]
            decay[c, h] = jnp.where(ii >= jj, jnp.exp(gcol[c, h] - gr_all[SM_A + h:SM_A + h + 1, :]), 0.0)
    tok = lambda c: slice(c * DN_C, (c + 1) * DN_C)
    q = {(c, h): _l2n(conv[tok(c), h * DN_DK:(h + 1) * DN_DK]) * DN_DK ** -0.5 for c, h in chains}
    k = {(c, h): _l2n(conv[tok(c), DN_QK_W + h * DN_DK:DN_QK_W + (h + 1) * DN_DK]) for c, h in chains}
    v = {(c, h): conv[tok(c), 2 * DN_QK_W + h * DN_DV:2 * DN_QK_W + (h + 1) * DN_DV] for c, h in chains}
    kb = {i: k[i] * beta[i] for i in chains}
    k16 = {i: k[i].astype(BF16) for i in chains}
    eg = {i: jnp.exp(gcol[i]) for i in chains}
    a = {i: jnp.where(ii > jj, _dot_nt(kb[i].astype(BF16), k16[i]) * decay[i], 0.0) for i in chains}
    t = {i: jnp.where(ii == jj, 1.0, 0.0) - a[i] for i in chains}
    for _ in range(max(1, math.ceil(math.log2(n_valid))) - 1):
        a = {i: _dot3(a[i], a[i]) for i in chains}
        t = {i: t[i] + _dot3(t[i], a[i]) for i in chains}
    x = {i: _dot3(t[i], jnp.concatenate([v[i] * beta[i], kb[i] * eg[i]], axis=1)) for i in chains}
    for c, h in chains:
        i = (c, h)
        u_ref[tok(c), h * DN_DV:(h + 1) * DN_DV] = x[i][:, 0:DN_DV]
        w_ref[tok(c), h * DN_DK:(h + 1) * DN_DK] = x[i][:, DN_DV:DN_DV + DN_DK].astype(BF16)
        qk = jnp.where(ii >= jj, _dot_nt(q[i].astype(BF16), k16[i]) * decay[i], 0.0)
        qk_ref[tok(c), h * DN_C:(h + 1) * DN_C] = qk.astype(BF16)
        qd_ref[tok(c), h * DN_DK:(h + 1) * DN_DK] = (q[i] * eg[i]).astype(BF16)
        glast = gcol[i][DN_C - 1:DN_C, :]
        kdt_ref[c, h * DN_DK:(h + 1) * DN_DK, :] = (k[i] * jnp.exp(glast - gcol[i])).T.astype(BF16)
        gl_ref[c, h:h + 1, :] = jnp.broadcast_to(jnp.exp(glast), (1, LANES))
    for c in range(cps):
        gl_ref[c, DN_HEADS:, :] = jnp.zeros((HALO - DN_HEADS, LANES), F32)


def _dn_prep(qkv, prefix8, small, cw, alog_row, dtb_row, tri, b, nch, n_valid):
    rows = b * nch * DN_C
    cps = DN_CHUNKS_PER_STEP if (n_valid == DN_C and nch % DN_CHUNKS_PER_STEP == 0) else 1
    steps = nch // cps
    blk = cps * DN_C
    if n_valid == DN_C:
        qkv_spec = pl.BlockSpec((blk, DN_QKV_W), lambda i, n: (i * steps + n, 0))
        halo_spec = pl.BlockSpec((HALO, DN_QKV_W),
                                 lambda i, n: (jnp.maximum((i * steps + n) * (blk // HALO) - 1, 0), 0))
        small_spec = pl.BlockSpec((blk, LANES), lambda i, n: (i * steps + n, 0))
        qkv_in, halo_in, small_in = qkv, qkv, small
    else:
        qkv_in = qkv.reshape(b, n_valid, DN_QKV_W)
        small_in = small.reshape(b, n_valid, LANES)
        halo_in = prefix8
        qkv_spec = pl.BlockSpec((1, n_valid, DN_QKV_W), lambda i, n: (i, 0, 0))
        halo_spec = pl.BlockSpec((1, HALO, DN_QKV_W), lambda i, n: (i, 0, 0))
        small_spec = pl.BlockSpec((1, n_valid, LANES), lambda i, n: (i, 0, 0))
    tok = lambda width: pl.BlockSpec((blk, width), lambda i, n: (i * steps + n, 0))
    return pl.pallas_call(
        functools.partial(_dn_prep_kernel, n_valid),
        grid=(b, steps),
        in_specs=[qkv_spec, halo_spec, pl.BlockSpec((1, HALO, DN_QKV_W), lambda i, n: (i, 0, 0)), small_spec,
                  _const_spec((CONV_W, DN_QKV_W)), _const_spec((1, LANES)), _const_spec((1, LANES)),
                  _const_spec((DN_C, DN_C))],
        out_specs=[tok(DN_W), tok(DN_QK_W), tok(DN_QK_W), tok(DN_HEADS * DN_C),
                   pl.BlockSpec((cps, DN_QK_W, DN_C), lambda i, n: (i * steps + n, 0, 0)),
                   pl.BlockSpec((cps, HALO, LANES), lambda i, n: (i * steps + n, 0, 0))],
        out_shape=[jax.ShapeDtypeStruct((rows, DN_W), F32), jax.ShapeDtypeStruct((rows, DN_QK_W), BF16),
                   jax.ShapeDtypeStruct((rows, DN_QK_W), BF16), jax.ShapeDtypeStruct((rows, DN_HEADS * DN_C), BF16),
                   jax.ShapeDtypeStruct((b * nch, DN_QK_W, DN_C), BF16),
                   jax.ShapeDtypeStruct((b * nch, HALO, LANES), F32)],
        scratch_shapes=[pltpu.VMEM((HALO + blk, DN_QKV_W), F32), pltpu.VMEM((DN_C, LANES), F32)],
        compiler_params=_params("parallel", "parallel"), name="dn_prep",
    )(qkv_in, halo_in, prefix8, small_in, cw, alog_row, dtb_row, tri)


def _dn_scan_kernel(n_valid, u_ref, w_ref, qd_ref, qk_ref, kdt_ref, gl_ref, zg_ref, s0_ref, nw_ref,
                    o_ref, s_ref):
    @pl.when(pl.program_id(1) == 0)
    def _():
        s_ref[...] = s0_ref[...]

    for b in range(u_ref.shape[0]):
        for h in range(DN_HEADS):
            cs = slice(h * DN_DK, (h + 1) * DN_DK)
            s = s_ref[b, h]
            s16 = s.astype(BF16)
            v_new = u_ref[b, :, cs] - _dot(w_ref[b, :, cs], s16)
            v16 = v_new.astype(BF16)
            o = _dot(qd_ref[b, :, cs], s16) + _dot(qk_ref[b, :, cs], v16)
            s_ref[b, h] = s * gl_ref[b, h:h + 1, :] + _dot(kdt_ref[b, cs, :], v16)
            on = o * lax.rsqrt(jnp.mean(o * o, axis=-1, keepdims=True) + EPS) * nw_ref[...]
            o_ref[b, :, cs] = (on[0:n_valid] * _silu(zg_ref[b, :, cs])).astype(BF16)


def _dn_scan(u, w, qd, qk, kdt, gl, zg, s0, nw, b, nch, bb, n_valid):
    tok = lambda width: pl.BlockSpec((bb, DN_C, width), lambda i, n: (i, n, 0))
    r3 = lambda a: a.reshape(b, nch * DN_C, a.shape[-1])
    st = pl.BlockSpec((bb, DN_HEADS, DN_DK, DN_DV), lambda i, n: (i, 0, 0, 0))
    return pl.pallas_call(
        functools.partial(_dn_scan_kernel, n_valid),
        grid=(b // bb, nch),
        in_specs=[tok(DN_W), tok(DN_QK_W), tok(DN_QK_W), tok(DN_HEADS * DN_C),
                  pl.BlockSpec((bb, DN_QK_W, DN_C), lambda i, n: (i, n, 0)),
                  pl.BlockSpec((bb, HALO, LANES), lambda i, n: (i, n, 0)),
                  pl.BlockSpec((bb, n_valid, DN_W), lambda i, n: (i, n, 0)), st, _const_spec((1, DN_DV))],
        out_specs=[pl.BlockSpec((bb, n_valid, DN_W), lambda i, n: (i, n, 0)), st],
        out_shape=[jax.ShapeDtypeStruct((b, nch * n_valid, DN_W), BF16),
                   jax.ShapeDtypeStruct((b, DN_HEADS, DN_DK, DN_DV), F32)],
        compiler_params=_params("parallel", "arbitrary"), name="dn_scan",
    )(r3(u), r3(w), r3(qd), r3(qk), kdt.reshape(b, nch * DN_QK_W, DN_C), gl.reshape(b, nch * HALO, LANES),
      zg.reshape(b, nch * n_valid, DN_W), s0, nw)


IN_WIDTHS = (NSA_Q_W, 6 * NSA_KV_W, 3 * NSA_HEADS, DN_QKV_W, DN_HEADS, DN_HEADS, DN_W, 2 * D_MODEL)


def _pack_mix_weight(w_in):
    offs = np.cumsum(IN_WIDTHS)[:-1].tolist()
    q, kv, gate, qkv, a, b, zg, merge = jnp.split(w_in, offs, axis=-1)
    lead = w_in.shape[:-1]
    small = jnp.concatenate([gate, a, b, jnp.zeros(lead + (LANES - SM_B - DN_HEADS,), w_in.dtype)], axis=-1)
    return jnp.concatenate([q, kv, small, qkv, zg, merge], axis=-1).astype(BF16)


def _rope_tables(pos):
    half = HEAD_DIM // 2
    freq = ROPE_THETA ** (-jnp.arange(half, dtype=F32) / half)
    ang = pos.astype(F32)[:, None] * freq[None, :]
    cos, sin = jnp.cos(ang), jnp.sin(ang)
    cos = jnp.concatenate([cos, cos] * (LANES // HEAD_DIM), axis=1)
    sin = jnp.concatenate([-sin, sin] * (LANES // HEAD_DIM), axis=1)
    return cos, sin


def _gate_expand_matrix():
    r = np.zeros((LANES, 3 * NSA_Q_W), np.float32)
    for h in range(NSA_HEADS):
        k, g = divmod(h, NSA_GROUP)
        for br in range(3):
            c0 = br * NSA_Q_W + g * LANES + k * HEAD_DIM
            r[SM_GATE + h * 3 + br, c0:c0 + HEAD_DIM] = 1.0
    return jnp.asarray(r, BF16)


def _overlap_t(nc_pad, nc, ns, rows):
    i = np.arange(nc_pad)[None, :]
    j = np.arange(rows)[:, None]
    ov = (i * CMP_STRIDE < (j + 1) * SEL_BLOCK) & (i * CMP_STRIDE + CMP_BLOCK > j * SEL_BLOCK)
    ov = ov & (i < nc) & (j < ns)
    return jnp.asarray(ov.astype(np.float32), BF16)


def _pack_compress_sample(pos_emb, w1, w2):
    eye = jnp.eye(NSA_KV_HEADS, dtype=w1.dtype)
    w = w1.reshape(2, 2, CMP_STRIDE, HEAD_DIM, HEAD_DIM)
    w = jnp.einsum('kapde,hg->kphdage', w, eye)
    w = w.reshape(2, NSA_KV_HEADS * CMP_STRIDE * HEAD_DIM, 2 * LANES).astype(BF16)
    pe = pos_emb.reshape(2, 2, CMP_STRIDE, 1, HEAD_DIM)
    pe = jnp.concatenate([pe] * NSA_KV_HEADS, axis=3).reshape(2, 2, -1)
    pe = jnp.concatenate([pe, jnp.zeros((2, 6, pe.shape[2]), pe.dtype)], axis=1)
    w2d = jnp.einsum('kde,hg->khdge', w2, eye).reshape(2, LANES, LANES).astype(BF16)
    return pe, w, w2d


def _sample_rows(q_pad, b, ts):
    q = q_pad.reshape(b, ts, NSA_KV_HEADS, NSA_GROUP, 1, HEAD_DIM)
    eye = jnp.eye(NSA_KV_HEADS, dtype=q.dtype).reshape(NSA_KV_HEADS, 1, NSA_KV_HEADS, 1)
    q = (q * eye).transpose(0, 3, 2, 1, 4, 5)
    return q.reshape(b, NSA_HEADS * ts, LANES)


def _dn_lane_row(vals):
    return jnp.zeros((1, LANES), F32).at[0, SM_A:SM_A + DN_HEADS].set(vals.astype(F32))


def _deltanet(qkv, small, zg, prefix, s0, dn_params, tri, b, seq_tokens):
    conv_w, a_log, dt_bias, norm_w = dn_params
    if seq_tokens % DN_C == 0:
        nch, n_valid, bb = seq_tokens // DN_C, DN_C, min(b, 4)
    else:
        nch, n_valid, bb = 1, seq_tokens, 4
    prefix8 = jnp.concatenate([jnp.zeros((b, HALO - (CONV_W - 1), DN_QKV_W), F32), prefix], axis=1)
    u, w, qd, qk, kdt, gl = _dn_prep(qkv, prefix8, small, conv_w, _dn_lane_row(a_log), _dn_lane_row(dt_bias),
                                     tri, b, nch, n_valid)
    ob, s_out = _dn_scan(u, w, qd, qk, kdt, gl, zg, s0, norm_w[None], b, nch, bb, n_valid)
    return ob.reshape(b * seq_tokens, DN_W), s_out


def _head_perm_rows(wa):
    d = wa.shape[1]
    return wa.reshape(NSA_KV_HEADS, NSA_GROUP, HEAD_DIM, d).transpose(1, 0, 2, 3).reshape(NSA_Q_W, d)


def _pack_compress(pos_emb, w1, w2):
    eye = jnp.eye(NSA_KV_HEADS, dtype=w1.dtype)
    pe = jnp.concatenate([pos_emb] * NSA_KV_HEADS, axis=2)
    w1p = w1.reshape(2, CMP_BLOCK, HEAD_DIM, HEAD_DIM)
    w1d = jnp.einsum('kpde,hg->kphdge', w1p, eye).reshape(2, CMP_BLOCK, LANES, LANES)
    w2d = jnp.einsum('kde,hg->khdge', w2, eye).reshape(2, LANES, LANES)
    return pe, w1d.astype(BF16), w2d.astype(BF16)


def kernel(x_prompt, x_sample, cache_nsa_kv, state_nsa_win, state_dn_S, state_dn_conv, page_table,
           ffn1_norm, ffn1_w_in, ffn1_w_out, mix_norm, w_in, nsa_cmp_pos, nsa_cmp_w1, nsa_cmp_w2,
           dn_conv_w, dn_A_log, dn_dt_bias, dn_out_norm, w_branch_a, w_branch_b, w_out,
           ffn2_norm, ffn2_w_in, ffn2_w_out, final_norm):
    b, seq = x_prompt.shape[:2]
    bs, ts = x_sample.shape[:2]
    depth = w_in.shape[0]
    n_phys = cache_nsa_kv.shape[1]
    past = page_table.shape[1] * PAGE_SIZE
    wbuf = state_nsa_win.shape[2]
    kvw = 4 * NSA_KV_W

    xp = x_prompt.reshape(b * seq, D_MODEL)
    xs = x_sample.reshape(bs * ts, D_MODEL)
    cos_p, sin_p = _rope_tables(jnp.arange(seq, dtype=jnp.int32))
    cos_s, sin_s = _rope_tables(jnp.tile(past + jnp.arange(ts, dtype=jnp.int32), bs))
    nc = (seq - CMP_BLOCK) // CMP_STRIDE + 1
    ovt_p = _overlap_t(seq // CMP_STRIDE, nc, seq // SEL_BLOCK, LANES)
    tables_p = _attn_bias_tables(seq)
    ns_s = -(-(past + ts) // SEL_BLOCK)
    ovt_s = _overlap_t(past // CMP_STRIDE, past // CMP_STRIDE - 1, ns_s, -(-ns_s // 16) * 16)
    rexp = _gate_expand_matrix()
    tri = jnp.asarray(np.tril(np.ones((DN_C, DN_C), np.float32)), BF16)
    cache = cache_nsa_kv.transpose(0, 1, 3, 4, 5, 2).reshape(depth, n_phys, kvw, PAGE_SIZE)
    win_state = state_nsa_win.transpose(0, 1, 3, 4, 5, 2).reshape(depth, bs, 2 * NSA_KV_W, wbuf)
    tm_p, tm_s = 512, bs * ts

    outs = [[] for _ in range(8)]
    rows_all = None
    w_mix_all = _pack_mix_weight(w_in)
    for l in range(depth):
        f1 = (ffn1_norm[l][None], ffn1_w_in[l].astype(BF16), ffn1_w_out[l].astype(BF16))
        f2 = (ffn2_norm[l][None], ffn2_w_in[l].astype(BF16), ffn2_w_out[l].astype(BF16))
        w_mix = w_mix_all[l]
        mrg = (_head_perm_rows(w_branch_a[l]).astype(BF16), w_branch_b[l].astype(BF16), w_out[l].astype(BF16))
        dn_params = (dn_conv_w[l], dn_A_log[l], dn_dt_bias[l], dn_out_norm[l])
        cmp_l = (nsa_cmp_pos[l], nsa_cmp_w1[l], nsa_cmp_w2[l])

        xp = _ffn(xp, *f1, tm_p)
        (qraw, qrot, small, qkv, zg, mg, rows_all, win_t, kcmp, vcmp, ksa, kw16, vst, vwt) = _mix(
            xp, mix_norm[l][None], w_mix, cos_p, sin_p, 256, seq, True, l, depth, rows_all)
        kc, vct = _compress_prompt(kcmp, vcmp, *_pack_compress(*cmp_l), b, seq)
        oa = _nsa_prompt(qraw, qrot, small, kc, vct, ksa, vst, kw16, vwt, ovt_p, tables_p, b, seq)
        ob, s_fin = _deltanet(qkv, small, zg, jnp.zeros((b, CONV_W - 1, DN_QKV_W), F32),
                              jnp.zeros((b, DN_HEADS, DN_DK, DN_DV), F32), dn_params, tri, b, seq)
        xp = _merge(xp, oa, ob, mg, *mrg, tm_p)
        xp = _ffn(xp, *f2, tm_p)
        outs[1].append(win_t[:, :, seq - min(WINDOW, seq):])
        outs[2].append(s_fin)
        outs[3].append(qkv.reshape(b, seq, DN_QKV_W)[:, seq - (CONV_W - 1):])

        xs = _ffn(xs, *f1, tm_s)
        (qraw, qrot, small, qkv, zg, mg, rows, win) = _mix(
            xs, mix_norm[l][None], w_mix, cos_s, sin_s, tm_s, tm_s, False)
        oa = _nsa_sample(l, cache, page_table, _sample_rows(qraw, bs, ts), _sample_rows(qrot, bs, ts),
                         small.reshape(bs, ts, LANES), rows.reshape(bs, ts, kvw), win.reshape(bs, ts, 2 * NSA_KV_W),
                         win_state, *_pack_compress_sample(*cmp_l), ovt_s, rexp, past)
        oa = oa[:, :ts].reshape(bs * ts, NSA_Q_W).astype(BF16)
        ob, s_fin = _deltanet(qkv, small, zg, state_dn_conv[l], state_dn_S[l], dn_params, tri, bs, ts)
        xs = _merge(xs, oa, ob, mg, *mrg, tm_s)
        xs = _ffn(xs, *f2, tm_s)
        outs[4].append(rows.reshape(bs, ts, 4, NSA_KV_HEADS, HEAD_DIM))
        outs[5].append(win.reshape(bs, ts, 2, NSA_KV_HEADS, HEAD_DIM))
        outs[6].append(s_fin)
        outs[7].append(qkv.reshape(bs, ts, DN_QKV_W))

    y_prompt = _final_norm(xp, final_norm[None], tm_p).reshape(b, seq, D_MODEL)
    y_sample = _final_norm(xs, final_norm[None], tm_s).reshape(bs, ts, D_MODEL)
    res = [rows_all] + [jnp.stack(o) for o in outs[1:]]
    res[5] = jnp.concatenate([state_nsa_win, res[5]], axis=2)[:, :, ts:]
    res[7] = jnp.concatenate([state_dn_conv, res[7]], axis=2)[:, :, ts:]
    for i, parts in ((0, 4), (1, 2)):
        r = res[i]
        res[i] = r.reshape(depth, b, parts, NSA_KV_HEADS, HEAD_DIM, r.shape[-1]).transpose(0, 1, 5, 2, 3, 4)
    return (y_prompt, y_sample) + tuple(res)
```

```python
import functools
import math

import jax
import jax.numpy as jnp
import numpy as np
from jax import lax
from jax.experimental import pallas as pl
from jax.experimental.pallas import tpu as pltpu

F32 = jnp.float32
BF16 = jnp.bfloat16

D_MODEL = 1024
DEPTH = 4
PAGE_SIZE = 128
NSA_HEADS = 8
NSA_KV_HEADS = 2
HEAD_DIM = 64
NSA_GROUP = NSA_HEADS // NSA_KV_HEADS
NSA_Q_W = NSA_HEADS * HEAD_DIM
NSA_KV_W = NSA_KV_HEADS * HEAD_DIM
CMP_BLOCK = 32
CMP_STRIDE = 16
SEL_BLOCK = 64
N_SEL = 16
WINDOW = 512
FORCE_BONUS = 1e6
ROPE_THETA = 10000.0
DN_HEADS = 4
DN_DK = 128
DN_DV = 128
DN_QK_W = DN_HEADS * DN_DK
DN_W = DN_HEADS * DN_DV
DN_QKV_W = 2 * DN_QK_W + DN_W
CONV_W = 4
D_FF = 2816
EPS = 1e-6

LANES = 128
VT_ROWS = HEAD_DIM + 16
VMEM_LIMIT = 56 * 1024 * 1024
NEG_BIG = -1e30

C_Q = 0
C_KV = C_Q + NSA_Q_W
C_SMALL = C_KV + 6 * NSA_KV_W
C_QKV = C_SMALL + LANES
C_ZG = C_QKV + DN_QKV_W
C_MERGE = C_ZG + DN_W
MIX_W = C_MERGE + 2 * D_MODEL
SM_GATE = 0
SM_A = 3 * NSA_HEADS
SM_B = SM_A + DN_HEADS


def _dot(a, b):
    return jnp.dot(a, b, preferred_element_type=F32)


def _dot_nt(a, b):
    return lax.dot_general(a, b, (((1,), (1,)), ((), ())), preferred_element_type=F32)


def _split3(a):
    hi = a.astype(BF16)
    r1 = a - hi.astype(F32)
    mid = r1.astype(BF16)
    lo = (r1 - mid.astype(F32)).astype(BF16)
    return hi, mid, lo


def _dot_exact_rhs(a, b01):
    hi, mid, lo = _split3(a)
    return _dot(hi, b01) + _dot(mid, b01) + _dot(lo, b01)


def _dot_exact_lhs(a01, b):
    hi, mid, lo = _split3(b)
    return _dot(a01, hi) + _dot(a01, mid) + _dot(a01, lo)


def _dot_exact_lhs_nt(a01, b):
    hi, mid, lo = _split3(b)
    return _dot_nt(a01, hi) + _dot_nt(a01, mid) + _dot_nt(a01, lo)


def _dot3(a, b):
    ah = a.astype(BF16)
    al = (a - ah.astype(F32)).astype(BF16)
    bh = b.astype(BF16)
    bl = (b - bh.astype(F32)).astype(BF16)
    return _dot(ah, bh) + (_dot(ah, bl) + _dot(al, bh))


def _rms(x, w):
    return x * lax.rsqrt(jnp.mean(x * x, axis=-1, keepdims=True) + EPS) * w


def _sigmoid(x):
    return 1.0 / (1.0 + jnp.exp(-x))


def _silu(x):
    return x * _sigmoid(x)


def _params(*sem):
    return pltpu.CompilerParams(dimension_semantics=sem, vmem_limit_bytes=VMEM_LIMIT)


def _const_spec(shape):
    nd = len(shape)
    return pl.BlockSpec(shape, lambda *_: (0,) * nd)


FF_TILE = 256


def _ffn_kernel(x_ref, nw_ref, wi_ref, wo_ref, o_ref, act_ref):
    x = x_ref[...]
    hb = _rms(x, nw_ref[...]).astype(BF16)
    for j in range(D_FF // FF_TILE):
        g = _dot(hb, wi_ref[:, j * FF_TILE:(j + 1) * FF_TILE])
        u = _dot(hb, wi_ref[:, D_FF + j * FF_TILE:D_FF + (j + 1) * FF_TILE])
        act_ref[:, j * FF_TILE:(j + 1) * FF_TILE] = (_silu(g) * u).astype(BF16)
    o_ref[...] = x + 0.5 * _dot(act_ref[...], wo_ref[...])


def _ffn(x, nw, wi, wo, tm):
    m = x.shape[0]
    return pl.pallas_call(
        _ffn_kernel,
        grid=(m // tm,),
        in_specs=[pl.BlockSpec((tm, D_MODEL), lambda i: (i, 0)),
                  _const_spec((1, D_MODEL)),
                  _const_spec((D_MODEL, 2 * D_FF)),
                  _const_spec((D_FF, D_MODEL))],
        out_specs=pl.BlockSpec((tm, D_MODEL), lambda i: (i, 0)),
        out_shape=jax.ShapeDtypeStruct((m, D_MODEL), F32),
        scratch_shapes=[pltpu.VMEM((tm, D_FF), BF16)],
        compiler_params=_params("parallel"), name="half_ffn",
    )(x, nw, wi, wo)


def _rope128(x, cos, sin):
    lane = lax.broadcasted_iota(jnp.int32, x.shape, 1)
    first = (lane % HEAD_DIM) < (HEAD_DIM // 2)
    swapped = jnp.where(first, pltpu.roll(x, LANES - HEAD_DIM // 2, 1), pltpu.roll(x, HEAD_DIM // 2, 1))
    return x * cos + swapped * sin


def _mix_kernel(seq_tiles, prompt, n_passthrough, x_ref, nw_ref, w_ref, cos_ref, sin_ref, *refs):
    qraw_ref, qrot_ref, small_ref, qkv_ref, zg_ref, merge_ref, *group_refs = refs[n_passthrough:]
    tm = x_ref.shape[0]
    hb = _rms(x_ref[...], nw_ref[...]).astype(BF16)
    cos = cos_ref[...]
    sin = sin_ref[...]
    scale = HEAD_DIM ** -0.5 * (math.log2(math.e) if prompt else 1.0)
    for j in range(NSA_Q_W // LANES):
        q = _dot(hb, w_ref[:, C_Q + j * LANES:C_Q + (j + 1) * LANES])
        qraw_ref[:, j * LANES:(j + 1) * LANES] = (q * scale).astype(BF16)
        qrot_ref[:, j * LANES:(j + 1) * LANES] = (_rope128(q, cos, sin) * scale).astype(BF16)
    kv = _dot(hb, w_ref[:, C_KV:C_KV + 6 * NSA_KV_W])
    k_cmp = kv[:, 0:LANES]
    v_cmp = kv[:, LANES:2 * LANES]
    k_sel = _rope128(kv[:, 2 * LANES:3 * LANES], cos, sin)
    v_sel = kv[:, 3 * LANES:4 * LANES]
    k_win = _rope128(kv[:, 4 * LANES:5 * LANES], cos, sin)
    v_win = kv[:, 5 * LANES:6 * LANES]
    if prompt:
        rows_t_ref, win_t_ref, kcmp_ref, vcmp_ref, ksa_ref, kw16_ref, vst_ref, vwt_ref = group_refs
        ones = jnp.ones((VT_ROWS - HEAD_DIM, tm), F32)
        for part, a in enumerate((k_cmp, v_cmp, k_sel, v_sel)):
            rows_t_ref[0, 0, part * LANES:(part + 1) * LANES, :] = a.T
        for part, a in enumerate((k_win, v_win)):
            win_t_ref[0, part * LANES:(part + 1) * LANES, :] = a.T
        for v, vt_ref in ((v_sel, vst_ref), (v_win, vwt_ref)):
            vt = v.T
            for k in range(NSA_KV_HEADS):
                vt_ref[k] = jnp.concatenate([vt[k * HEAD_DIM:(k + 1) * HEAD_DIM], ones], axis=0).astype(BF16)
        kcmp_ref[...] = k_cmp
        vcmp_ref[...] = v_cmp
        pos = (pl.program_id(0) % seq_tiles) * tm + lax.broadcasted_iota(jnp.int32, (tm, LANES), 0)
        lane = lax.broadcasted_iota(jnp.int32, (tm, LANES), 1)
        ksa_ref[:, 0:LANES] = k_sel.astype(BF16)
        ksa_ref[:, LANES:2 * LANES] = jnp.where(pos // SEL_BLOCK == lane, 1.0, 0.0).astype(BF16)
        kw16_ref[...] = k_win.astype(BF16)
    else:
        rows_ref, win_ref = group_refs
        for part, a in enumerate((k_cmp, v_cmp, k_sel, v_sel)):
            rows_ref[:, part * LANES:(part + 1) * LANES] = a
        win_ref[:, 0:LANES] = k_win
        win_ref[:, LANES:2 * LANES] = v_win
    small_ref[...] = _dot(hb, w_ref[:, C_SMALL:C_SMALL + LANES])
    qkv_ref[...] = _dot(hb, w_ref[:, C_QKV:C_QKV + DN_QKV_W])
    zg_ref[...] = _dot(hb, w_ref[:, C_ZG:C_ZG + DN_W])
    merge_ref[...] = _dot(hb, w_ref[:, C_MERGE:C_MERGE + 2 * D_MODEL])


def _mix(x, nw, w, cos, sin, tm, seq_len, prompt, layer=0, depth=1, rows_all=None):
    m = x.shape[0]
    seq_tiles = max(seq_len // tm, 1)
    row = lambda width: pl.BlockSpec((tm, width), lambda i: (i, 0))
    tab = pl.BlockSpec((tm, LANES), lambda i: (i % seq_tiles, 0))
    widths = [(NSA_Q_W, BF16), (NSA_Q_W, BF16), (LANES, F32), (DN_QKV_W, F32), (DN_W, F32), (2 * D_MODEL, F32)]
    inputs = [x, nw, w, cos, sin]
    in_specs = [row(D_MODEL), _const_spec((1, D_MODEL)), _const_spec((D_MODEL, MIX_W)), tab, tab]
    aliases = {}
    if prompt:
        nb = m // seq_len
        vt_spec = pl.BlockSpec((NSA_KV_HEADS, VT_ROWS, tm), lambda i: (0, 0, i))
        vt_shape = jax.ShapeDtypeStruct((NSA_KV_HEADS, VT_ROWS, m), BF16)
        g_widths = [(LANES, F32), (LANES, F32), (2 * LANES, BF16), (LANES, BF16)]
        g_specs = ([pl.BlockSpec((1, 1, 4 * NSA_KV_W, tm), lambda i: (layer, i // seq_tiles, 0, i % seq_tiles)),
                    pl.BlockSpec((1, 2 * NSA_KV_W, tm), lambda i: (i // seq_tiles, 0, i % seq_tiles))]
                   + [row(wd) for wd, _ in g_widths] + [vt_spec, vt_spec])
        g_shapes = ([jax.ShapeDtypeStruct((depth, nb, 4 * NSA_KV_W, seq_len), F32),
                     jax.ShapeDtypeStruct((nb, 2 * NSA_KV_W, seq_len), F32)]
                    + [jax.ShapeDtypeStruct((m, wd), dt) for wd, dt in g_widths] + [vt_shape, vt_shape])
        if rows_all is not None:
            aliases = {len(inputs): len(widths)}
            inputs.append(rows_all)
            in_specs.append(pl.BlockSpec(memory_space=pl.ANY))
    else:
        g_widths = [(4 * NSA_KV_W, F32), (2 * NSA_KV_W, F32)]
        g_specs = [row(wd) for wd, _ in g_widths]
        g_shapes = [jax.ShapeDtypeStruct((m, wd), dt) for wd, dt in g_widths]
    return pl.pallas_call(
        functools.partial(_mix_kernel, seq_tiles, prompt, len(aliases)),
        grid=(m // tm,),
        in_specs=in_specs,
        out_specs=[row(wd) for wd, _ in widths] + g_specs,
        out_shape=[jax.ShapeDtypeStruct((m, wd), dt) for wd, dt in widths] + g_shapes,
        input_output_aliases=aliases,
        compiler_params=_params("parallel"), name="mix_in",
    )(*inputs)


def _merge_kernel(x_ref, oa_ref, ob_ref, mg_ref, wa_ref, wb_ref, wo_ref, o_ref):
    mg = mg_ref[...]
    ya = _dot(oa_ref[...], wa_ref[...])
    yb = _dot(ob_ref[...], wb_ref[...])
    y = _sigmoid(mg[:, :D_MODEL]) * ya + _sigmoid(mg[:, D_MODEL:]) * yb
    o_ref[...] = x_ref[...] + _dot(y.astype(BF16), wo_ref[...])


def _merge(x, oa, ob, mg, wa, wb, wo, tm):
    m = x.shape[0]
    row = lambda width: pl.BlockSpec((tm, width), lambda i: (i, 0))
    return pl.pallas_call(
        _merge_kernel,
        grid=(m // tm,),
        in_specs=[row(D_MODEL), row(NSA_Q_W), row(DN_W), row(2 * D_MODEL),
                  _const_spec((NSA_Q_W, D_MODEL)), _const_spec((DN_W, D_MODEL)),
                  _const_spec((D_MODEL, D_MODEL))],
        out_specs=row(D_MODEL),
        out_shape=jax.ShapeDtypeStruct((m, D_MODEL), F32),
        compiler_params=_params("parallel"), name="branch_merge",
    )(x, oa, ob, mg, wa, wb, wo)


def _norm_kernel(x_ref, w_ref, o_ref):
    o_ref[...] = _rms(x_ref[...], w_ref[...])


def _final_norm(x, w, tm):
    m = x.shape[0]
    row = pl.BlockSpec((tm, D_MODEL), lambda i: (i, 0))
    return pl.pallas_call(
        _norm_kernel, grid=(m // tm,), in_specs=[row, _const_spec((1, D_MODEL))], out_specs=row,
        out_shape=jax.ShapeDtypeStruct((m, D_MODEL), F32), compiler_params=_params("parallel"),
    )(x, w)


def _compress_kernel(kcmp_ref, vcmp_ref, pe_ref, w1_ref, w2_ref, kc_ref, vct_ref):
    ns = kc_ref.shape[1]
    res = []
    for part, rows_ref in enumerate((kcmp_ref, vcmp_ref)):
        sa = jnp.zeros((ns, LANES), F32)
        sb = jnp.zeros((ns, LANES), F32)
        for p in range(CMP_STRIDE):
            x = rows_ref[pl.ds(p, ns, stride=CMP_STRIDE), :]
            sa = sa + _dot((x + pe_ref[part, p:p + 1, :]).astype(BF16), w1_ref[part, p])
            sb = sb + _dot((x + pe_ref[part, CMP_STRIDE + p:CMP_STRIDE + p + 1, :]).astype(BF16),
                           w1_ref[part, CMP_STRIDE + p])
        pre = sa + pltpu.roll(sb, ns - 1, 0)
        res.append(_dot(jax.nn.gelu(pre).astype(BF16), w2_ref[part]))
    kc_ref[0] = res[0].astype(BF16)
    vt = res[1].T
    ones = jnp.ones((VT_ROWS - HEAD_DIM, ns), F32)
    for k in range(NSA_KV_HEADS):
        vct_ref[0, k] = jnp.concatenate([vt[k * HEAD_DIM:(k + 1) * HEAD_DIM], ones], axis=0).astype(BF16)


def _compress_prompt(kcmp, vcmp, pe, w1, w2, b, seq):
    ns = seq // CMP_STRIDE
    tok = pl.BlockSpec((seq, LANES), lambda i: (i, 0))
    return pl.pallas_call(
        _compress_kernel,
        grid=(b,),
        in_specs=[tok, tok, _const_spec(pe.shape), _const_spec(w1.shape), _const_spec(w2.shape)],
        out_specs=[pl.BlockSpec((1, ns, LANES), lambda i: (i, 0, 0)),
                   pl.BlockSpec((1, NSA_KV_HEADS, VT_ROWS, ns), lambda i: (i, 0, 0, 0))],
        out_shape=[jax.ShapeDtypeStruct((b, ns, LANES), BF16),
                   jax.ShapeDtypeStruct((b, NSA_KV_HEADS, VT_ROWS, ns), BF16)],
        compiler_params=_params("parallel"), name="nsa_compress",
    )(kcmp, vcmp, pe, w1, w2)


TQ = 512
KT = TQ
SUB = 128
WIN_KEYS = WINDOW + SUB


def _softmax_rows(s, mask):
    s = jnp.where(mask, s, NEG_BIG)
    e = jnp.where(mask, jnp.exp(s - jnp.max(s, axis=-1, keepdims=True)), 0.0)
    return e * (1.0 / jnp.maximum(jnp.sum(e, axis=-1, keepdims=True), 1e-30))


def _topk_bias_t(imp_t, n_cand, n_sel):
    nr = imp_t.shape[0] // 8
    xs = [imp_t[8 * r:8 * r + 8, :] for r in range(nr)]
    cnt = [jnp.zeros_like(xs[0]) for _ in range(nr)]
    sub = lax.broadcasted_iota(jnp.int32, xs[0].shape, 0)
    for j in range(n_cand):
        rj = j // 8
        row = xs[rj][j % 8:j % 8 + 1, :]
        for r in range(nr):
            if r < rj:
                inc = jnp.where(row > xs[r], 1.0, 0.0)
            elif r > rj:
                inc = jnp.where(row >= xs[r], 1.0, 0.0)
            else:
                inc = jnp.where(sub > j % 8, jnp.where(row >= xs[r], 1.0, 0.0),
                                jnp.where(row > xs[r], 1.0, 0.0))
            cnt[r] = cnt[r] + inc
    return jnp.concatenate([jnp.where(c < n_sel, 0.0, NEG_BIG) for c in cnt], axis=0)


def _nsa_prompt_kernel(qraw_ref, qrot_ref, small_ref, kc_ref, vct_ref, ksa_ref, vst_ref, win_ref, vwt_ref,
                       ovt_ref, cbias_ref, dbias_ref, wbias_ref, o_ref):
    seq = ksa_ref.shape[1]
    ns = seq // SEL_BLOCK
    s0 = pl.multiple_of(pl.program_id(1) * TQ, TQ)
    cols = NSA_GROUP * TQ
    tile = lambda a: jnp.concatenate([a] * NSA_GROUP, axis=1)
    qpos_row = s0 + lax.broadcasted_iota(jnp.int32, (1, cols), 1) % TQ
    gates_t = _sigmoid(small_ref[...]).T

    blk_t = lax.broadcasted_iota(jnp.int32, (ns, TQ), 0)
    qpos_t = s0 + lax.broadcasted_iota(jnp.int32, (ns, TQ), 1)
    cur_t = qpos_t // SEL_BLOCK
    valid_t = blk_t * SEL_BLOCK <= qpos_t
    forced_t = (blk_t == 0) | (blk_t == cur_t) | (blk_t == cur_t - 1)
    cbias = tile(cbias_ref[0])
    dbias = tile(dbias_ref[...])
    nsub = TQ // SUB

    def split_sum(aug):
        return aug[0:HEAD_DIM], aug[HEAD_DIM:HEAD_DIM + 1]

    def sub_cols(a, r):
        return jnp.concatenate([a[:, g * TQ + r * SUB:g * TQ + (r + 1) * SUB] for g in range(NSA_GROUP)], axis=1)

    def join_subs(parts):
        return jnp.concatenate([parts[r][:, g * SUB:(g + 1) * SUB]
                                for g in range(NSA_GROUP) for r in range(nsub)], axis=1)

    kv_heads = range(NSA_KV_HEADS)
    o_c, o_w, q_sel = [], [], []
    for k in kv_heads:
        def q_t(ref):
            zeros = jnp.zeros((HEAD_DIM, TQ), F32)
            blocks = []
            for pair in range(NSA_GROUP // 2):
                j = k * (NSA_GROUP // 2) + pair
                both = ref[:, j * LANES:(j + 1) * LANES].astype(F32).T
                for half in range(2):
                    qh = both[half * HEAD_DIM:(half + 1) * HEAD_DIM]
                    blocks.append(jnp.concatenate([qh, zeros] if k == 0 else [zeros, qh], axis=0))
            return jnp.concatenate(blocks, axis=1).astype(BF16)
        qt_raw = q_t(qraw_ref)
        qt_rot = q_t(qrot_ref)

        s = _dot(kc_ref[0], qt_raw) + cbias
        e = jnp.exp2(s - jnp.max(s, axis=0, keepdims=True))
        acc, l = split_sum(_dot(vct_ref[0, k], e.astype(BF16)))
        inv = jnp.where(qpos_row >= CMP_BLOCK - 1, 1.0 / l, 0.0)
        o_c.append(acc * inv)
        p = e * inv
        p_sum = p[:, 0:TQ] + p[:, TQ:2 * TQ] + p[:, 2 * TQ:3 * TQ] + p[:, 3 * TQ:4 * TQ]
        imp_t = _dot_exact_lhs(ovt_ref[...], p_sum)[0:ns]
        imp_t = jnp.where(valid_t, imp_t + jnp.where(forced_t, FORCE_BONUS, 0.0), -jnp.inf)
        bias_t = _topk_bias_t(imp_t, ns, min(N_SEL, ns))
        if ns < LANES:
            bias_t = jnp.concatenate([bias_t, jnp.zeros((LANES - ns, TQ), F32)], axis=0)
        q_sel.append(jnp.concatenate([qt_rot, tile(bias_t).astype(BF16)], axis=0))

        parts = []
        for r in range(nsub):
            q0 = s0 + r * SUB
            w0 = pl.multiple_of(jnp.maximum(q0 - WINDOW, 0), SUB)
            bias = tile(wbias_ref[jnp.minimum(q0 // SUB, WINDOW // SUB)])
            s = _dot(win_ref[0, pl.ds(w0, WIN_KEYS), :], sub_cols(qt_rot, r)) + bias
            e = jnp.exp2(s - jnp.max(s, axis=0, keepdims=True)).astype(BF16)
            acc, l = split_sum(_dot(vwt_ref[k, :, pl.ds(w0, WIN_KEYS)], e))
            parts.append(acc * (1.0 / l))
        o_w.append(join_subs(parts))

    def sel_step(c, carry):
        k0 = pl.multiple_of(c * KT, KT)
        keys = ksa_ref[0, pl.ds(k0, KT), :]
        out = []
        for k in kv_heads:
            m, acc = carry[k]
            s = _dot(keys, q_sel[k])
            m_new = jnp.maximum(m, jnp.max(s, axis=0, keepdims=True))
            p = jnp.exp2(s - m_new).astype(BF16)
            out.append((m_new, jnp.exp2(m - m_new) * acc + _dot(vst_ref[k, :, pl.ds(k0, KT)], p)))
        return tuple(out)
    init = tuple((jnp.full((1, cols), NEG_BIG, F32), jnp.zeros((VT_ROWS, cols), F32)) for _ in kv_heads)
    carry = lax.fori_loop(0, s0 // KT, sel_step, init)
    res = []
    for k in kv_heads:
        parts = []
        for r in range(nsub):
            nk = (r + 1) * SUB
            m, acc = sub_cols(carry[k][0], r), sub_cols(carry[k][1], r)
            s = _dot(ksa_ref[0, pl.ds(s0, nk), :], sub_cols(q_sel[k], r))
            s = jnp.concatenate([s[0:r * SUB], s[r * SUB:nk] + dbias], axis=0) if r else s + dbias
            m_new = jnp.maximum(m, jnp.max(s, axis=0, keepdims=True))
            p = jnp.exp2(s - m_new).astype(BF16)
            acc, l = split_sum(jnp.exp2(m - m_new) * acc + _dot(vst_ref[k, :, pl.ds(s0, nk)], p))
            parts.append(acc * (1.0 / l))
        res.append((o_c[k], join_subs(parts), o_w[k]))

    for g in range(NSA_GROUP):
        halves = []
        for k in range(NSA_KV_HEADS):
            r = SM_GATE + (k * NSA_GROUP + g) * 3
            halves.append(sum(gates_t[r + br:r + br + 1, :] * res[k][br][:, g * TQ:(g + 1) * TQ]
                              for br in range(3)))
        o_ref[:, g * LANES:(g + 1) * LANES] = jnp.concatenate(halves, axis=0).T.astype(BF16)


def _attn_bias_tables(seq):
    r = np.arange(TQ)[None, None, :]
    blocks = np.arange(seq // TQ)[:, None, None]
    c = np.arange(seq // CMP_STRIDE)[None, :, None]
    cb = c * CMP_STRIDE + CMP_BLOCK - 1 <= blocks * TQ + r
    rs = np.arange(SUB)[None, :]
    db = np.arange(SUB)[:, None] <= rs
    kw = np.arange(WIN_KEYS)[None, :, None]
    nw = WINDOW // SUB
    qpos = np.arange(nw + 1)[:, None, None] * SUB + rs[None]
    wb = (kw <= qpos) & (kw > qpos - WINDOW)
    f = lambda m: jnp.asarray(np.where(m, 0.0, NEG_BIG).astype(np.float32))
    return f(cb), f(db), f(wb)


def _nsa_prompt(qraw, qrot, small, kc, vct, ksa, vst, kw16, vwt, ovt, tables, b, seq):
    nc = kc.shape[1]
    nq = seq // TQ
    cb, db, wb = tables
    tok = lambda width: pl.BlockSpec((TQ, width), lambda i, j: (i * nq + j, 0))
    per_b = lambda width: pl.BlockSpec((1, seq, width), lambda i, j: (i, 0, 0))
    vt_spec = pl.BlockSpec((NSA_KV_HEADS, VT_ROWS, seq), lambda i, j: (0, 0, i))
    return pl.pallas_call(
        _nsa_prompt_kernel,
        grid=(b, nq),
        in_specs=[tok(NSA_Q_W), tok(NSA_Q_W), tok(LANES),
                  pl.BlockSpec((1, nc, LANES), lambda i, j: (i, 0, 0)),
                  pl.BlockSpec((1, NSA_KV_HEADS, VT_ROWS, nc), lambda i, j: (i, 0, 0, 0)),
                  per_b(2 * LANES), vt_spec, per_b(LANES), vt_spec,
                  _const_spec(ovt.shape),
                  pl.BlockSpec((1, nc, TQ), lambda i, j: (j, 0, 0)),
                  _const_spec(db.shape), _const_spec(wb.shape)],
        out_specs=tok(NSA_Q_W),
        out_shape=jax.ShapeDtypeStruct((b * seq, NSA_Q_W), BF16),
        compiler_params=_params("parallel", "parallel"), name="nsa_prompt",
    )(qraw, qrot, small, kc, vct, ksa.reshape(b, seq, 2 * LANES), vst, kw16.reshape(b, seq, LANES), vwt,
      ovt, cb, db, wb)


PAGES_PER_STEP = 32
SEG_PER_PAGE = PAGE_SIZE // CMP_STRIDE
SEG_W = CMP_STRIDE * 2 * NSA_KV_W


def _nsa_sample_kernel(ts, past, pt_ref, *refs):
    pages = refs[:PAGES_PER_STEP]
    (qraw_ref, qrot_ref, small_ref, rows_ref, winnew_ref, winst_ref, pe_ref, w1_ref, w2_ref,
     ovt_ref, rexp_ref, o_ref, seg_ref, ksat_ref, vst_ref, sm_ref, stage_ref, bias_ref, res_ref) = refs[PAGES_PER_STEP:]
    del pt_ref
    pg = pl.program_id(1)
    nseg = res_ref.shape[1]
    nrows = NSA_HEADS * ts
    kt = NSA_KV_HEADS * ts
    half_w = NSA_KV_HEADS * CMP_STRIDE * HEAD_DIM

    @pl.when((pl.program_id(0) == 0) & (pg == 0))
    def _():
        for part in range(2):
            hi, mid, lo = _split3(pe_ref[part])
            w = w1_ref[part]
            bias_ref[part] = _dot(hi, w) + _dot(mid, w) + _dot(lo, w)

    for e in range(PAGES_PER_STEP):
        for part in range(2):
            xt = pages[e][0, 0, part * LANES:(part + 1) * LANES, :].T
            for j in range(PAGE_SIZE // 8):
                start = (j % 2) * (PAGE_SIZE // 2) + j // 2
                stage_ref[e, part, pl.ds(start, 8, stride=SEG_PER_PAGE), :] = xt[8 * j:8 * j + 8, :]
        k0 = pl.multiple_of((pg * PAGES_PER_STEP + e) * PAGE_SIZE, PAGE_SIZE)
        ksat_ref[0:LANES, pl.ds(k0, PAGE_SIZE)] = pages[e][0, 0, 2 * LANES:3 * LANES, :].astype(BF16)
        vst_ref[:, pl.ds(k0, PAGE_SIZE)] = pages[e][0, 0, 3 * LANES:4 * LANES, :].astype(BF16)

        @pl.when(pl.program_id(0) == 0)
        def _():
            kpos = k0 + lax.broadcasted_iota(jnp.int32, (LANES, PAGE_SIZE), 1)
            blk = lax.broadcasted_iota(jnp.int32, (LANES, PAGE_SIZE), 0)
            ksat_ref[LANES:2 * LANES, pl.ds(k0, PAGE_SIZE)] = jnp.where(
                kpos // SEL_BLOCK == blk, 1.0, 0.0).astype(BF16)
    for pair in range(PAGES_PER_STEP // 2):
        r0 = pair * 2 * SEG_PER_PAGE
        for part in range(2):
            for p in range(CMP_STRIDE):
                x = jnp.concatenate([stage_ref[2 * pair + e, part, p * SEG_PER_PAGE:(p + 1) * SEG_PER_PAGE, :]
                                     for e in range(2)], axis=0)
                c0 = part * half_w + p * LANES
                seg_ref[r0:r0 + 2 * SEG_PER_PAGE, c0:c0 + LANES] = x.astype(BF16)
    g_rows = PAGES_PER_STEP * SEG_PER_PAGE
    for part in range(2):
        res_ref[part, pl.ds(pl.multiple_of(pg * g_rows, g_rows), g_rows), :] = _dot(
            seg_ref[:, part * half_w:(part + 1) * half_w], w1_ref[part])

    @pl.when(pg == pl.num_programs(1) - 1)
    def _():
        ns_past = past // SEL_BLOCK
        nc = nseg - 1
        q_raw = qraw_ref[0]
        q_rot = qrot_ref[0]
        q_rot_f = q_rot.astype(F32)
        row = lax.broadcasted_iota(jnp.int32, (nrows, 1), 0)
        t_row = row % ts
        kcvc = []
        for part in range(2):
            res = res_ref[part]
            b8 = bias_ref[part]
            pre = (res[:, 0:LANES] + b8[0:1, 0:LANES]
                   + pltpu.roll(res[:, LANES:2 * LANES] + b8[1:2, LANES:2 * LANES], nseg - 1, 0))
            kcvc.append(_dot(jax.nn.gelu(pre).astype(BF16), w2_ref[part]).astype(BF16))
        s_c = _dot_nt(q_raw, kcvc[0])
        p_c = _softmax_rows(s_c, lax.broadcasted_iota(jnp.int32, (nrows, nseg), 1) < nc)
        o_c = _dot(p_c.astype(BF16), kcvc[1])
        p_sum = p_c[0:kt] + p_c[kt:2 * kt] + p_c[2 * kt:3 * kt] + p_c[3 * kt:4 * kt]
        p_sum = jnp.concatenate([p_sum, jnp.zeros((LANES - kt, nseg), F32)], axis=0)
        nb = ovt_ref.shape[0]
        imp_t = _dot_exact_lhs_nt(ovt_ref[...], p_sum)
        blk_t = lax.broadcasted_iota(jnp.int32, (nb, LANES), 0)
        qpos_t = past + lax.broadcasted_iota(jnp.int32, (nb, LANES), 1) % ts
        cur_t = qpos_t // SEL_BLOCK
        forced_t = (blk_t == 0) | (blk_t == cur_t) | (blk_t == cur_t - 1)
        imp_t = jnp.where(blk_t * SEL_BLOCK <= qpos_t, imp_t + jnp.where(forced_t, FORCE_BONUS, 0.0), -jnp.inf)
        bias_t = _topk_bias_t(imp_t, -(-(past + ts) // SEL_BLOCK), N_SEL)[0:ns_past]
        bias = bias_t.T[0:kt]
        q_sel = jnp.concatenate([q_rot, jnp.concatenate([bias] * NSA_GROUP, axis=0).astype(BF16)], axis=1)

        def attend(s_past, v_past16_t, k_new, v_new, mask_past):
            if mask_past is not None:
                s_past = jnp.where(mask_past, s_past, NEG_BIG)
            s_new = [jnp.where(t_row >= j, jnp.sum(q_rot_f * k_new[j:j + 1, :], axis=-1, keepdims=True), NEG_BIG)
                     for j in range(ts)]
            m = jnp.max(s_past, axis=-1, keepdims=True)
            for s in s_new:
                m = jnp.maximum(m, s)
            p_past = jnp.exp(s_past - m)
            if mask_past is not None:
                p_past = jnp.where(mask_past, p_past, 0.0)
            acc = _dot_nt(p_past.astype(BF16), v_past16_t)
            l = jnp.sum(p_past, axis=-1, keepdims=True)
            for j, s in enumerate(s_new):
                p = jnp.where(t_row >= j, jnp.exp(s - m), 0.0)
                acc = acc + p * v_new[j:j + 1, :]
                l = l + p
            return acc * (1.0 / l)

        new = rows_ref[0]
        o_s = attend(_dot(q_sel, ksat_ref[...]), vst_ref[...], new[:, 2 * NSA_KV_W:3 * NSA_KV_W],
                     new[:, 3 * NSA_KV_W:4 * NSA_KV_W], None)
        wbuf = winst_ref.shape[3]
        wnew = winnew_ref[0]
        i_w = lax.broadcasted_iota(jnp.int32, (nrows, wbuf), 1)
        dist = t_row + wbuf - i_w
        o_w = attend(_dot(q_rot, winst_ref[0, 0, 0:LANES, :].astype(BF16)),
                     winst_ref[0, 0, LANES:2 * LANES, :].astype(BF16),
                     wnew[:, 0:LANES], wnew[:, LANES:2 * LANES],
                     (dist < WINDOW) & (past - wbuf + i_w >= 0))

        sm_ref[...] = jnp.zeros_like(sm_ref)
        sm_ref[0:ts, :] = small_ref[0]
        gexp = _dot_exact_rhs(_sigmoid(sm_ref[...]), rexp_ref[...])[0:kt]
        low = lax.broadcasted_iota(jnp.int32, (kt, LANES), 1) < HEAD_DIM
        for g in range(NSA_GROUP):
            out = jnp.zeros((kt, LANES), F32)
            for br, o_br in enumerate((o_c, o_s, o_w)):
                og = o_br[g * kt:(g + 1) * kt]
                both = jnp.where(low, og, pltpu.roll(og, kt - ts, 0))
                out = out + gexp[:, br * NSA_Q_W + g * LANES:br * NSA_Q_W + (g + 1) * LANES] * both
            o_ref[0, :, g * LANES:(g + 1) * LANES] = out


def _nsa_sample(layer, cache, page_table, qraw_s, qrot_s, small_s, rows_s, winnew_s, win_state,
                pe, w1, w2, ovt, rexp, past):
    b, ts = small_s.shape[:2]
    n_pages = page_table.shape[1]
    assert n_pages % PAGES_PER_STEP == 0 and ts * NSA_KV_HEADS == 8 and past == n_pages * PAGE_SIZE
    nseg = past // CMP_STRIDE
    wbuf = win_state.shape[3]
    kw = 4 * NSA_KV_W

    def page_spec(e):
        return pl.BlockSpec((1, 1, 4 * PAGE_SIZE, LANES),
                            lambda i, g, pt: (layer, pt[i, g * PAGES_PER_STEP + e], 0, 0))
    per_b = lambda shape: pl.BlockSpec((1,) + shape, lambda i, g, pt: (i,) + (0,) * len(shape))
    const = lambda shape: pl.BlockSpec(shape, lambda i, g, pt: (0,) * len(shape))
    grid_spec = pltpu.PrefetchScalarGridSpec(
        num_scalar_prefetch=1,
        grid=(b, n_pages // PAGES_PER_STEP),
        in_specs=[page_spec(e) for e in range(PAGES_PER_STEP)] + [
            per_b((NSA_HEADS * ts, LANES)), per_b((NSA_HEADS * ts, LANES)), per_b((ts, LANES)),
            per_b((ts, kw)), per_b((ts, 2 * NSA_KV_W)),
            pl.BlockSpec((1, 1, 2 * NSA_KV_W, wbuf), lambda i, g, pt: (layer, i, 0, 0)),
            const(pe.shape), const(w1.shape), const(w2.shape), const(ovt.shape), const(rexp.shape)],
        out_specs=per_b((NSA_KV_HEADS * ts, NSA_Q_W)),
        scratch_shapes=[pltpu.VMEM((PAGES_PER_STEP * SEG_PER_PAGE, SEG_W), BF16),
                        pltpu.VMEM((2 * LANES, past), BF16), pltpu.VMEM((LANES, past), BF16),
                        pltpu.VMEM((2 * NSA_KV_HEADS * ts, LANES), F32),
                        pltpu.VMEM((PAGES_PER_STEP, 2, PAGE_SIZE, LANES), F32),
                        pltpu.VMEM((2, 8, 2 * LANES), F32),
                        pltpu.VMEM((2, nseg, 2 * LANES), F32)])
    return pl.pallas_call(
        functools.partial(_nsa_sample_kernel, ts, past),
        grid_spec=grid_spec,
        out_shape=jax.ShapeDtypeStruct((b, NSA_KV_HEADS * ts, NSA_Q_W), F32),
        compiler_params=_params("arbitrary", "arbitrary"), name="nsa_sample",
    )(page_table, *([cache] * PAGES_PER_STEP), qraw_s, qrot_s, small_s, rows_s, winnew_s, win_state,
      pe, w1, w2, ovt, rexp)


DN_C = 128
DN_CHUNKS_PER_STEP = 2
HALO = 8


def _softplus(x):
    return jnp.maximum(x, 0.0) + jnp.log(1.0 + jnp.exp(-jnp.abs(x)))


def _l2n(x):
    return x * lax.rsqrt(jnp.sum(x * x, axis=-1, keepdims=True) + EPS)


def _dn_prep_kernel(n_valid, qkv_ref, halo_ref, prefix_ref, small_ref, cw_ref, alog_ref, dtb_ref,
                    tri_ref, u_ref, w_ref, qd_ref, qk_ref, kdt_ref, gl_ref, xcat_ref, sm_ref):
    n = pl.program_id(1)
    cps = kdt_ref.shape[0]
    rows = cps * DN_C
    if n_valid == DN_C:
        xcat_ref[HALO:HALO + rows, :] = qkv_ref[...]
        small = small_ref[...]

        @pl.when(n == 0)
        def _():
            xcat_ref[0:HALO, :] = prefix_ref[0]

        @pl.when(n > 0)
        def _():
            xcat_ref[0:HALO, :] = halo_ref[...]
    else:
        xcat_ref[...] = jnp.zeros_like(xcat_ref)
        xcat_ref[0:HALO, :] = prefix_ref[0]
        xcat_ref[HALO:HALO + n_valid, :] = qkv_ref[0]
        sm_ref[...] = jnp.zeros_like(sm_ref)
        sm_ref[0:n_valid, :] = small_ref[0]
        small = sm_ref[...]

    conv = jnp.zeros((rows, DN_QKV_W), F32)
    for i in range(CONV_W):
        conv = conv + xcat_ref[HALO - (CONV_W - 1) + i:HALO - (CONV_W - 1) + i + rows, :] * cw_ref[i:i + 1, :]
    conv = _silu(conv)

    ii = lax.broadcasted_iota(jnp.int32, (DN_C, DN_C), 0)
    jj = lax.broadcasted_iota(jnp.int32, (DN_C, DN_C), 1)
    g_rows = -jnp.exp(alog_ref[...]) * _softplus(small + dtb_ref[...])
    beta_rows = _sigmoid(small)
    chains = [(c, h) for c in range(cps) for h in range(DN_HEADS)]
    gcol, beta, decay = {}, {}, {}
    for c in range(cps):
        g_all = g_rows[c * DN_C:(c + 1) * DN_C]
        beta_all = beta_rows[c * DN_C:(c + 1) * DN_C]
        if n_valid != DN_C:
            g_all = jnp.where(ii < n_valid, g_all, 0.0)
            beta_all = jnp.where(ii < n_valid, beta_all, 0.0)
        gc_all = _dot_exact_lhs(tri_ref[...], g_all)
        gr_all = gc_all.T
        for h in range(DN_HEADS):
            gcol[c, h] = gc_all[:, SM_A + h:SM_A + h + 1]
            beta[c, h] = beta_all[:, SM_B + h:SM_B + h + 1]
            decay[c, h] = jnp.where(ii >= jj, jnp.exp(gcol[c, h] - gr_all[SM_A + h:SM_A + h + 1, :]), 0.0)
    tok = lambda c: slice(c * DN_C, (c + 1) * DN_C)
    q = {(c, h): _l2n(conv[tok(c), h * DN_DK:(h + 1) * DN_DK]) * DN_DK ** -0.5 for c, h in chains}
    k = {(c, h): _l2n(conv[tok(c), DN_QK_W + h * DN_DK:DN_QK_W + (h + 1) * DN_DK]) for c, h in chains}
    v = {(c, h): conv[tok(c), 2 * DN_QK_W + h * DN_DV:2 * DN_QK_W + (h + 1) * DN_DV] for c, h in chains}
    kb = {i: k[i] * beta[i] for i in chains}
    k16 = {i: k[i].astype(BF16) for i in chains}
    eg = {i: jnp.exp(gcol[i]) for i in chains}
    a = {i: jnp.where(ii > jj, _dot_nt(kb[i].astype(BF16), k16[i]) * decay[i], 0.0) for i in chains}
    t = {i: jnp.where(ii == jj, 1.0, 0.0) - a[i] for i in chains}
    for _ in range(max(1, math.ceil(math.log2(n_valid))) - 1):
        a = {i: _dot3(a[i], a[i]) for i in chains}
        t = {i: t[i] + _dot3(t[i], a[i]) for i in chains}
    x = {i: _dot3(t[i], jnp.concatenate([v[i] * beta[i], kb[i] * eg[i]], axis=1)) for i in chains}
    for c, h in chains:
        i = (c, h)
        u_ref[tok(c), h * DN_DV:(h + 1) * DN_DV] = x[i][:, 0:DN_DV]
        w_ref[tok(c), h * DN_DK:(h + 1) * DN_DK] = x[i][:, DN_DV:DN_DV + DN_DK].astype(BF16)
        qk = jnp.where(ii >= jj, _dot_nt(q[i].astype(BF16), k16[i]) * decay[i], 0.0)
        qk_ref[tok(c), h * DN_C:(h + 1) * DN_C] = qk.astype(BF16)
        qd_ref[tok(c), h * DN_DK:(h + 1) * DN_DK] = (q[i] * eg[i]).astype(BF16)
        glast = gcol[i][DN_C - 1:DN_C, :]
        kdt_ref[c, h * DN_DK:(h + 1) * DN_DK, :] = (k[i] * jnp.exp(glast - gcol[i])).T.astype(BF16)
        gl_ref[c, h:h + 1, :] = jnp.broadcast_to(jnp.exp(glast), (1, LANES))
    for c in range(cps):
        gl_ref[c, DN_HEADS:, :] = jnp.zeros((HALO - DN_HEADS, LANES), F32)


def _dn_prep(qkv, prefix8, small, cw, alog_row, dtb_row, tri, b, nch, n_valid):
    rows = b * nch * DN_C
    cps = DN_CHUNKS_PER_STEP if (n_valid == DN_C and nch % DN_CHUNKS_PER_STEP == 0) else 1
    steps = nch // cps
    blk = cps * DN_C
    if n_valid == DN_C:
        qkv_spec = pl.BlockSpec((blk, DN_QKV_W), lambda i, n: (i * steps + n, 0))
        halo_spec = pl.BlockSpec((HALO, DN_QKV_W),
                                 lambda i, n: (jnp.maximum((i * steps + n) * (blk // HALO) - 1, 0), 0))
        small_spec = pl.BlockSpec((blk, LANES), lambda i, n: (i * steps + n, 0))
        qkv_in, halo_in, small_in = qkv, qkv, small
    else:
        qkv_in = qkv.reshape(b, n_valid, DN_QKV_W)
        small_in = small.reshape(b, n_valid, LANES)
        halo_in = prefix8
        qkv_spec = pl.BlockSpec((1, n_valid, DN_QKV_W), lambda i, n: (i, 0, 0))
        halo_spec = pl.BlockSpec((1, HALO, DN_QKV_W), lambda i, n: (i, 0, 0))
        small_spec = pl.BlockSpec((1, n_valid, LANES), lambda i, n: (i, 0, 0))
    tok = lambda width: pl.BlockSpec((blk, width), lambda i, n: (i * steps + n, 0))
    return pl.pallas_call(
        functools.partial(_dn_prep_kernel, n_valid),
        grid=(b, steps),
        in_specs=[qkv_spec, halo_spec, pl.BlockSpec((1, HALO, DN_QKV_W), lambda i, n: (i, 0, 0)), small_spec,
                  _const_spec((CONV_W, DN_QKV_W)), _const_spec((1, LANES)), _const_spec((1, LANES)),
                  _const_spec((DN_C, DN_C))],
        out_specs=[tok(DN_W), tok(DN_QK_W), tok(DN_QK_W), tok(DN_HEADS * DN_C),
                   pl.BlockSpec((cps, DN_QK_W, DN_C), lambda i, n: (i * steps + n, 0, 0)),
                   pl.BlockSpec((cps, HALO, LANES), lambda i, n: (i * steps + n, 0, 0))],
        out_shape=[jax.ShapeDtypeStruct((rows, DN_W), F32), jax.ShapeDtypeStruct((rows, DN_QK_W), BF16),
                   jax.ShapeDtypeStruct((rows, DN_QK_W), BF16), jax.ShapeDtypeStruct((rows, DN_HEADS * DN_C), BF16),
                   jax.ShapeDtypeStruct((b * nch, DN_QK_W, DN_C), BF16),
                   jax.ShapeDtypeStruct((b * nch, HALO, LANES), F32)],
        scratch_shapes=[pltpu.VMEM((HALO + blk, DN_QKV_W), F32), pltpu.VMEM((DN_C, LANES), F32)],
        compiler_params=_params("parallel", "parallel"), name="dn_prep",
    )(qkv_in, halo_in, prefix8, small_in, cw, alog_row, dtb_row, tri)


def _dn_scan_kernel(n_valid, u_ref, w_ref, qd_ref, qk_ref, kdt_ref, gl_ref, zg_ref, s0_ref, nw_ref,
                    o_ref, s_ref):
    @pl.when(pl.program_id(1) == 0)
    def _():
        s_ref[...] = s0_ref[...]

    for b in range(u_ref.shape[0]):
        for h in range(DN_HEADS):
            cs = slice(h * DN_DK, (h + 1) * DN_DK)
            s = s_ref[b, h]
            s16 = s.astype(BF16)
            v_new = u_ref[b, :, cs] - _dot(w_ref[b, :, cs], s16)
            v16 = v_new.astype(BF16)
            o = _dot(qd_ref[b, :, cs], s16) + _dot(qk_ref[b, :, cs], v16)
            s_ref[b, h] = s * gl_ref[b, h:h + 1, :] + _dot(kdt_ref[b, cs, :], v16)
            on = o * lax.rsqrt(jnp.mean(o * o, axis=-1, keepdims=True) + EPS) * nw_ref[...]
            o_ref[b, :, cs] = (on[0:n_valid] * _silu(zg_ref[b, :, cs])).astype(BF16)


def _dn_scan(u, w, qd, qk, kdt, gl, zg, s0, nw, b, nch, bb, n_valid):
    tok = lambda width: pl.BlockSpec((bb, DN_C, width), lambda i, n: (i, n, 0))
    r3 = lambda a: a.reshape(b, nch * DN_C, a.shape[-1])
    st = pl.BlockSpec((bb, DN_HEADS, DN_DK, DN_DV), lambda i, n: (i, 0, 0, 0))
    return pl.pallas_call(
        functools.partial(_dn_scan_kernel, n_valid),
        grid=(b // bb, nch),
        in_specs=[tok(DN_W), tok(DN_QK_W), tok(DN_QK_W), tok(DN_HEADS * DN_C),
                  pl.BlockSpec((bb, DN_QK_W, DN_C), lambda i, n: (i, n, 0)),
                  pl.BlockSpec((bb, HALO, LANES), lambda i, n: (i, n, 0)),
                  pl.BlockSpec((bb, n_valid, DN_W), lambda i, n: (i, n, 0)), st, _const_spec((1, DN_DV))],
        out_specs=[pl.BlockSpec((bb, n_valid, DN_W), lambda i, n: (i, n, 0)), st],
        out_shape=[jax.ShapeDtypeStruct((b, nch * n_valid, DN_W), BF16),
                   jax.ShapeDtypeStruct((b, DN_HEADS, DN_DK, DN_DV), F32)],
        compiler_params=_params("parallel", "arbitrary"), name="dn_scan",
    )(r3(u), r3(w), r3(qd), r3(qk), kdt.reshape(b, nch * DN_QK_W, DN_C), gl.reshape(b, nch * HALO, LANES),
      zg.reshape(b, nch * n_valid, DN_W), s0, nw)


IN_WIDTHS = (NSA_Q_W, 6 * NSA_KV_W, 3 * NSA_HEADS, DN_QKV_W, DN_HEADS, DN_HEADS, DN_W, 2 * D_MODEL)


def _pack_mix_weight(w_in):
    offs = np.cumsum(IN_WIDTHS)[:-1].tolist()
    q, kv, gate, qkv, a, b, zg, merge = jnp.split(w_in, offs, axis=-1)
    lead = w_in.shape[:-1]
    small = jnp.concatenate([gate, a, b, jnp.zeros(lead + (LANES - SM_B - DN_HEADS,), w_in.dtype)], axis=-1)
    return jnp.concatenate([q, kv, small, qkv, zg, merge], axis=-1).astype(BF16)


def _rope_tables(pos):
    half = HEAD_DIM // 2
    freq = ROPE_THETA ** (-jnp.arange(half, dtype=F32) / half)
    ang = pos.astype(F32)[:, None] * freq[None, :]
    cos, sin = jnp.cos(ang), jnp.sin(ang)
    cos = jnp.concatenate([cos, cos] * (LANES // HEAD_DIM), axis=1)
    sin = jnp.concatenate([-sin, sin] * (LANES // HEAD_DIM), axis=1)
    return cos, sin


def _gate_expand_matrix():
    r = np.zeros((LANES, 3 * NSA_Q_W), np.float32)
    for h in range(NSA_HEADS):
        k, g = divmod(h, NSA_GROUP)
        for br in range(3):
            c0 = br * NSA_Q_W + g * LANES + k * HEAD_DIM
            r[SM_GATE + h * 3 + br, c0:c0 + HEAD_DIM] = 1.0
    return jnp.asarray(r, BF16)


def _overlap_t(nc_pad, nc, ns, rows):
    i = np.arange(nc_pad)[None, :]
    j = np.arange(rows)[:, None]
    ov = (i * CMP_STRIDE < (j + 1) * SEL_BLOCK) & (i * CMP_STRIDE + CMP_BLOCK > j * SEL_BLOCK)
    ov = ov & (i < nc) & (j < ns)
    return jnp.asarray(ov.astype(np.float32), BF16)


def _pack_compress_sample(pos_emb, w1, w2):
    eye = jnp.eye(NSA_KV_HEADS, dtype=w1.dtype)
    w = w1.reshape(2, 2, CMP_STRIDE, HEAD_DIM, HEAD_DIM)
    w = jnp.einsum('kapde,hg->kphdage', w, eye)
    w = w.reshape(2, NSA_KV_HEADS * CMP_STRIDE * HEAD_DIM, 2 * LANES).astype(BF16)
    pe = pos_emb.reshape(2, 2, CMP_STRIDE, 1, HEAD_DIM)
    pe = jnp.concatenate([pe] * NSA_KV_HEADS, axis=3).reshape(2, 2, -1)
    pe = jnp.concatenate([pe, jnp.zeros((2, 6, pe.shape[2]), pe.dtype)], axis=1)
    w2d = jnp.einsum('kde,hg->khdge', w2, eye).reshape(2, LANES, LANES).astype(BF16)
    return pe, w, w2d


def _sample_rows(q_pad, b, ts):
    q = q_pad.reshape(b, ts, NSA_KV_HEADS, NSA_GROUP, 1, HEAD_DIM)
    eye = jnp.eye(NSA_KV_HEADS, dtype=q.dtype).reshape(NSA_KV_HEADS, 1, NSA_KV_HEADS, 1)
    q = (q * eye).transpose(0, 3, 2, 1, 4, 5)
    return q.reshape(b, NSA_HEADS * ts, LANES)


def _dn_lane_row(vals):
    return jnp.zeros((1, LANES), F32).at[0, SM_A:SM_A + DN_HEADS].set(vals.astype(F32))


def _deltanet(qkv, small, zg, prefix, s0, dn_params, tri, b, seq_tokens):
    conv_w, a_log, dt_bias, norm_w = dn_params
    if seq_tokens % DN_C == 0:
        nch, n_valid, bb = seq_tokens // DN_C, DN_C, min(b, 4)
    else:
        nch, n_valid, bb = 1, seq_tokens, 4
    prefix8 = jnp.concatenate([jnp.zeros((b, HALO - (CONV_W - 1), DN_QKV_W), F32), prefix], axis=1)
    u, w, qd, qk, kdt, gl = _dn_prep(qkv, prefix8, small, conv_w, _dn_lane_row(a_log), _dn_lane_row(dt_bias),
                                     tri, b, nch, n_valid)
    ob, s_out = _dn_scan(u, w, qd, qk, kdt, gl, zg, s0, norm_w[None], b, nch, bb, n_valid)
    return ob.reshape(b * seq_tokens, DN_W), s_out


def _head_perm_rows(wa):
    d = wa.shape[1]
    return wa.reshape(NSA_KV_HEADS, NSA_GROUP, HEAD_DIM, d).transpose(1, 0, 2, 3).reshape(NSA_Q_W, d)


def _pack_compress(pos_emb, w1, w2):
    eye = jnp.eye(NSA_KV_HEADS, dtype=w1.dtype)
    pe = jnp.concatenate([pos_emb] * NSA_KV_HEADS, axis=2)
    w1p = w1.reshape(2, CMP_BLOCK, HEAD_DIM, HEAD_DIM)
    w1d = jnp.einsum('kpde,hg->kphdge', w1p, eye).reshape(2, CMP_BLOCK, LANES, LANES)
    w2d = jnp.einsum('kde,hg->khdge', w2, eye).reshape(2, LANES, LANES)
    return pe, w1d.astype(BF16), w2d.astype(BF16)


def kernel(x_prompt, x_sample, cache_nsa_kv, state_nsa_win, state_dn_S, state_dn_conv, page_table,
           ffn1_norm, ffn1_w_in, ffn1_w_out, mix_norm, w_in, nsa_cmp_pos, nsa_cmp_w1, nsa_cmp_w2,
           dn_conv_w, dn_A_log, dn_dt_bias, dn_out_norm, w_branch_a, w_branch_b, w_out,
           ffn2_norm, ffn2_w_in, ffn2_w_out, final_norm):
    b, seq = x_prompt.shape[:2]
    bs, ts = x_sample.shape[:2]
    depth = w_in.shape[0]
    n_phys = cache_nsa_kv.shape[1]
    past = page_table.shape[1] * PAGE_SIZE
    wbuf = state_nsa_win.shape[2]
    kvw = 4 * NSA_KV_W

    xp = x_prompt.reshape(b * seq, D_MODEL)
    xs = x_sample.reshape(bs * ts, D_MODEL)
    cos_p, sin_p = _rope_tables(jnp.arange(seq, dtype=jnp.int32))
    cos_s, sin_s = _rope_tables(jnp.tile(past + jnp.arange(ts, dtype=jnp.int32), bs))
    nc = (seq - CMP_BLOCK) // CMP_STRIDE + 1
    ovt_p = _overlap_t(seq // CMP_STRIDE, nc, seq // SEL_BLOCK, LANES)
    tables_p = _attn_bias_tables(seq)
    ns_s = -(-(past + ts) // SEL_BLOCK)
    ovt_s = _overlap_t(past // CMP_STRIDE, past // CMP_STRIDE - 1, ns_s, -(-ns_s // 16) * 16)
    rexp = _gate_expand_matrix()
    tri = jnp.asarray(np.tril(np.ones((DN_C, DN_C), np.float32)), BF16)
    cache = cache_nsa_kv.transpose(0, 1, 3, 4, 5, 2).reshape(depth, n_phys, kvw, PAGE_SIZE)
    win_state = state_nsa_win.transpose(0, 1, 3, 4, 5, 2).reshape(depth, bs, 2 * NSA_KV_W, wbuf)
    tm_p, tm_s = 512, bs * ts

    outs = [[] for _ in range(8)]
    rows_all = None
    w_mix_all = _pack_mix_weight(w_in)
    for l in range(depth):
        f1 = (ffn1_norm[l][None], ffn1_w_in[l].astype(BF16), ffn1_w_out[l].astype(BF16))
        f2 = (ffn2_norm[l][None], ffn2_w_in[l].astype(BF16), ffn2_w_out[l].astype(BF16))
        w_mix = w_mix_all[l]
        mrg = (_head_perm_rows(w_branch_a[l]).astype(BF16), w_branch_b[l].astype(BF16), w_out[l].astype(BF16))
        dn_params = (dn_conv_w[l], dn_A_log[l], dn_dt_bias[l], dn_out_norm[l])
        cmp_l = (nsa_cmp_pos[l], nsa_cmp_w1[l], nsa_cmp_w2[l])

        xp = _ffn(xp, *f1, tm_p)
        (qraw, qrot, small, qkv, zg, mg, rows_all, win_t, kcmp, vcmp, ksa, kw16, vst, vwt) = _mix(
            xp, mix_norm[l][None], w_mix, cos_p, sin_p, 256, seq, True, l, depth, rows_all)
        kc, vct = _compress_prompt(kcmp, vcmp, *_pack_compress(*cmp_l), b, seq)
        oa = _nsa_prompt(qraw, qrot, small, kc, vct, ksa, vst, kw16, vwt, ovt_p, tables_p, b, seq)
        ob, s_fin = _deltanet(qkv, small, zg, jnp.zeros((b, CONV_W - 1, DN_QKV_W), F32),
                              jnp.zeros((b, DN_HEADS, DN_DK, DN_DV), F32), dn_params, tri, b, seq)
        xp = _merge(xp, oa, ob, mg, *mrg, tm_p)
        xp = _ffn(xp, *f2, tm_p)
        outs[1].append(win_t[:, :, seq - min(WINDOW, seq):])
        outs[2].append(s_fin)
        outs[3].append(qkv.reshape(b, seq, DN_QKV_W)[:, seq - (CONV_W - 1):])

        xs = _ffn(xs, *f1, tm_s)
        (qraw, qrot, small, qkv, zg, mg, rows, win) = _mix(
            xs, mix_norm[l][None], w_mix, cos_s, sin_s, tm_s, tm_s, False)
        oa = _nsa_sample(l, cache, page_table, _sample_rows(qraw, bs, ts), _sample_rows(qrot, bs, ts),
                         small.reshape(bs, ts, LANES), rows.reshape(bs, ts, kvw), win.reshape(bs, ts, 2 * NSA_KV_W),
                         win_state, *_pack_compress_sample(*cmp_l), ovt_s, rexp, past)
        oa = oa[:, :ts].reshape(bs * ts, NSA_Q_W).astype(BF16)
        ob, s_fin = _deltanet(qkv, small, zg, state_dn_conv[l], state_dn_S[l], dn_params, tri, bs, ts)
        xs = _merge(xs, oa, ob, mg, *mrg, tm_s)
        xs = _ffn(xs, *f2, tm_s)
        outs[4].append(rows.reshape(bs, ts, 4, NSA_KV_HEADS, HEAD_DIM))
        outs[5].append(win.reshape(bs, ts, 2, NSA_KV_HEADS, HEAD_DIM))
        outs[6].append(s_fin)
        outs[7].append(qkv.reshape(bs, ts, DN_QKV_W))

    y_prompt = _final_norm(xp, final_norm[None], tm_p).reshape(b, seq, D_MODEL)
    y_sample = _final_norm(xs, final_norm[None], tm_s).reshape(bs, ts, D_MODEL)
    res = [rows_all] + [jnp.stack(o) for o in outs[1:]]
    res[5] = jnp.concatenate([state_nsa_win, res[5]], axis=2)[:, :, ts:]
    res[7] = jnp.concatenate([state_dn_conv, res[7]], axis=2)[:, :, ts:]
    for i, parts in ((0, 4), (1, 2)):
        r = res[i]
        res[i] = r.reshape(depth, b, parts, NSA_KV_HEADS, HEAD_DIM, r.shape[-1]).transpose(0, 1, 5, 2, 3, 4)
    return (y_prompt, y_sample) + tuple(res)
```

```python
import functools
import math

import jax
import jax.numpy as jnp
import numpy as np
from jax import lax
from jax.experimental import pallas as pl
from jax.experimental.pallas import tpu as pltpu

F32 = jnp.float32
BF16 = jnp.bfloat16

D_MODEL = 1024
DEPTH = 4
PAGE_SIZE = 128
NSA_HEADS = 8
NSA_KV_HEADS = 2
HEAD_DIM = 64
NSA_GROUP = NSA_HEADS // NSA_KV_HEADS
NSA_Q_W = NSA_HEADS * HEAD_DIM
NSA_KV_W = NSA_KV_HEADS * HEAD_DIM
CMP_BLOCK = 32
CMP_STRIDE = 16
SEL_BLOCK = 64
N_SEL = 16
WINDOW = 512
FORCE_BONUS = 1e6
ROPE_THETA = 10000.0
DN_HEADS = 4
DN_DK = 128
DN_DV = 128
DN_QK_W = DN_HEADS * DN_DK
DN_W = DN_HEADS * DN_DV
DN_QKV_W = 2 * DN_QK_W + DN_W
CONV_W = 4
D_FF = 2816
EPS = 1e-6

LANES = 128
VT_ROWS = HEAD_DIM + 16
VMEM_LIMIT = 56 * 1024 * 1024
NEG_BIG = -1e30

C_Q = 0
C_KV = C_Q + NSA_Q_W
C_SMALL = C_KV + 6 * NSA_KV_W
C_QKV = C_SMALL + LANES
C_ZG = C_QKV + DN_QKV_W
C_MERGE = C_ZG + DN_W
MIX_W = C_MERGE + 2 * D_MODEL
SM_GATE = 0
SM_A = 3 * NSA_HEADS
SM_B = SM_A + DN_HEADS


def _dot(a, b):
    return jnp.dot(a, b, preferred_element_type=F32)


def _dot_nt(a, b):
    return lax.dot_general(a, b, (((1,), (1,)), ((), ())), preferred_element_type=F32)


def _split3(a):
    hi = a.astype(BF16)
    r1 = a - hi.astype(F32)
    mid = r1.astype(BF16)
    lo = (r1 - mid.astype(F32)).astype(BF16)
    return hi, mid, lo


def _dot_exact_rhs(a, b01):
    hi, mid, lo = _split3(a)
    return _dot(hi, b01) + _dot(mid, b01) + _dot(lo, b01)


def _dot_exact_lhs(a01, b):
    hi, mid, lo = _split3(b)
    return _dot(a01, hi) + _dot(a01, mid) + _dot(a01, lo)


def _dot_exact_lhs_nt(a01, b):
    hi, mid, lo = _split3(b)
    return _dot_nt(a01, hi) + _dot_nt(a01, mid) + _dot_nt(a01, lo)


def _dot3(a, b):
    ah = a.astype(BF16)
    al = (a - ah.astype(F32)).astype(BF16)
    bh = b.astype(BF16)
    bl = (b - bh.astype(F32)).astype(BF16)
    return _dot(ah, bh) + (_dot(ah, bl) + _dot(al, bh))


def _rms(x, w):
    return x * lax.rsqrt(jnp.mean(x * x, axis=-1, keepdims=True) + EPS) * w


def _sigmoid(x):
    return 1.0 / (1.0 + jnp.exp(-x))


def _silu(x):
    return x * _sigmoid(x)


def _params(*sem):
    return pltpu.CompilerParams(dimension_semantics=sem, vmem_limit_bytes=VMEM_LIMIT)


def _const_spec(shape):
    nd = len(shape)
    return pl.BlockSpec(shape, lambda *_: (0,) * nd)


FF_TILE = 256


def _ffn_kernel(x_ref, nw_ref, wi_ref, wo_ref, o_ref, act_ref):
    x = x_ref[...]
    hb = _rms(x, nw_ref[...]).astype(BF16)
    for j in range(D_FF // FF_TILE):
        g = _dot(hb, wi_ref[:, j * FF_TILE:(j + 1) * FF_TILE])
        u = _dot(hb, wi_ref[:, D_FF + j * FF_TILE:D_FF + (j + 1) * FF_TILE])
        act_ref[:, j * FF_TILE:(j + 1) * FF_TILE] = (_silu(g) * u).astype(BF16)
    o_ref[...] = x + 0.5 * _dot(act_ref[...], wo_ref[...])


def _ffn(x, nw, wi, wo, tm):
    m = x.shape[0]
    return pl.pallas_call(
        _ffn_kernel,
        grid=(m // tm,),
        in_specs=[pl.BlockSpec((tm, D_MODEL), lambda i: (i, 0)),
                  _const_spec((1, D_MODEL)),
                  _const_spec((D_MODEL, 2 * D_FF)),
                  _const_spec((D_FF, D_MODEL))],
        out_specs=pl.BlockSpec((tm, D_MODEL), lambda i: (i, 0)),
        out_shape=jax.ShapeDtypeStruct((m, D_MODEL), F32),
        scratch_shapes=[pltpu.VMEM((tm, D_FF), BF16)],
        compiler_params=_params("parallel"), name="half_ffn",
    )(x, nw, wi, wo)


def _rope128(x, cos, sin):
    lane = lax.broadcasted_iota(jnp.int32, x.shape, 1)
    first = (lane % HEAD_DIM) < (HEAD_DIM // 2)
    swapped = jnp.where(first, pltpu.roll(x, LANES - HEAD_DIM // 2, 1), pltpu.roll(x, HEAD_DIM // 2, 1))
    return x * cos + swapped * sin


def _mix_kernel(seq_tiles, prompt, n_passthrough, x_ref, nw_ref, w_ref, cos_ref, sin_ref, *refs):
    qraw_ref, qrot_ref, small_ref, qkv_ref, zg_ref, merge_ref, *group_refs = refs[n_passthrough:]
    tm = x_ref.shape[0]
    hb = _rms(x_ref[...], nw_ref[...]).astype(BF16)
    cos = cos_ref[...]
    sin = sin_ref[...]
    scale = HEAD_DIM ** -0.5 * (math.log2(math.e) if prompt else 1.0)
    for j in range(NSA_Q_W // LANES):
        q = _dot(hb, w_ref[:, C_Q + j * LANES:C_Q + (j + 1) * LANES])
        qraw_ref[:, j * LANES:(j + 1) * LANES] = (q * scale).astype(BF16)
        qrot_ref[:, j * LANES:(j + 1) * LANES] = (_rope128(q, cos, sin) * scale).astype(BF16)
    kv = _dot(hb, w_ref[:, C_KV:C_KV + 6 * NSA_KV_W])
    k_cmp = kv[:, 0:LANES]
    v_cmp = kv[:, LANES:2 * LANES]
    k_sel = _rope128(kv[:, 2 * LANES:3 * LANES], cos, sin)
    v_sel = kv[:, 3 * LANES:4 * LANES]
    k_win = _rope128(kv[:, 4 * LANES:5 * LANES], cos, sin)
    v_win = kv[:, 5 * LANES:6 * LANES]
    if prompt:
        rows_t_ref, win_t_ref, kcmp_ref, vcmp_ref, ksa_ref, kw16_ref, vst_ref, vwt_ref = group_refs
        ones = jnp.ones((VT_ROWS - HEAD_DIM, tm), F32)
        for part, a in enumerate((k_cmp, v_cmp, k_sel, v_sel)):
            rows_t_ref[0, 0, part * LANES:(part + 1) * LANES, :] = a.T
        for part, a in enumerate((k_win, v_win)):
            win_t_ref[0, part * LANES:(part + 1) * LANES, :] = a.T
        for v, vt_ref in ((v_sel, vst_ref), (v_win, vwt_ref)):
            vt = v.T
            for k in range(NSA_KV_HEADS):
                vt_ref[k] = jnp.concatenate([vt[k * HEAD_DIM:(k + 1) * HEAD_DIM], ones], axis=0).astype(BF16)
        kcmp_ref[...] = k_cmp
        vcmp_ref[...] = v_cmp
        pos = (pl.program_id(0) % seq_tiles) * tm + lax.broadcasted_iota(jnp.int32, (tm, LANES), 0)
        lane = lax.broadcasted_iota(jnp.int32, (tm, LANES), 1)
        ksa_ref[:, 0:LANES] = k_sel.astype(BF16)
        ksa_ref[:, LANES:2 * LANES] = jnp.where(pos // SEL_BLOCK == lane, 1.0, 0.0).astype(BF16)
        kw16_ref[...] = k_win.astype(BF16)
    else:
        rows_ref, win_ref = group_refs
        for part, a in enumerate((k_cmp, v_cmp, k_sel, v_sel)):
            rows_ref[:, part * LANES:(part + 1) * LANES] = a
        win_ref[:, 0:LANES] = k_win
        win_ref[:, LANES:2 * LANES] = v_win
    small_ref[...] = _dot(hb, w_ref[:, C_SMALL:C_SMALL + LANES])
    qkv_ref[...] = _dot(hb, w_ref[:, C_QKV:C_QKV + DN_QKV_W])
    zg_ref[...] = _dot(hb, w_ref[:, C_ZG:C_ZG + DN_W])
    merge_ref[...] = _dot(hb, w_ref[:, C_MERGE:C_MERGE + 2 * D_MODEL])


def _mix(x, nw, w, cos, sin, tm, seq_len, prompt, layer=0, depth=1, rows_all=None):
    m = x.shape[0]
    seq_tiles = max(seq_len // tm, 1)
    row = lambda width: pl.BlockSpec((tm, width), lambda i: (i, 0))
    tab = pl.BlockSpec((tm, LANES), lambda i: (i % seq_tiles, 0))
    widths = [(NSA_Q_W, BF16), (NSA_Q_W, BF16), (LANES, F32), (DN_QKV_W, F32), (DN_W, F32), (2 * D_MODEL, F32)]
    inputs = [x, nw, w, cos, sin]
    in_specs = [row(D_MODEL), _const_spec((1, D_MODEL)), _const_spec((D_MODEL, MIX_W)), tab, tab]
    aliases = {}
    if prompt:
        nb = m // seq_len
        vt_spec = pl.BlockSpec((NSA_KV_HEADS, VT_ROWS, tm), lambda i: (0, 0, i))
        vt_shape = jax.ShapeDtypeStruct((NSA_KV_HEADS, VT_ROWS, m), BF16)
        g_widths = [(LANES, F32), (LANES, F32), (2 * LANES, BF16), (LANES, BF16)]
        g_specs = ([pl.BlockSpec((1, 1, 4 * NSA_KV_W, tm), lambda i: (layer, i // seq_tiles, 0, i % seq_tiles)),
                    pl.BlockSpec((1, 2 * NSA_KV_W, tm), lambda i: (i // seq_tiles, 0, i % seq_tiles))]
                   + [row(wd) for wd, _ in g_widths] + [vt_spec, vt_spec])
        g_shapes = ([jax.ShapeDtypeStruct((depth, nb, 4 * NSA_KV_W, seq_len), F32),
                     jax.ShapeDtypeStruct((nb, 2 * NSA_KV_W, seq_len), F32)]
                    + [jax.ShapeDtypeStruct((m, wd), dt) for wd, dt in g_widths] + [vt_shape, vt_shape])
        if rows_all is not None:
            aliases = {len(inputs): len(widths)}
            inputs.append(rows_all)
            in_specs.append(pl.BlockSpec(memory_space=pl.ANY))
    else:
        g_widths = [(4 * NSA_KV_W, F32), (2 * NSA_KV_W, F32)]
        g_specs = [row(wd) for wd, _ in g_widths]
        g_shapes = [jax.ShapeDtypeStruct((m, wd), dt) for wd, dt in g_widths]
    return pl.pallas_call(
        functools.partial(_mix_kernel, seq_tiles, prompt, len(aliases)),
        grid=(m // tm,),
        in_specs=in_specs,
        out_specs=[row(wd) for wd, _ in widths] + g_specs,
        out_shape=[jax.ShapeDtypeStruct((m, wd), dt) for wd, dt in widths] + g_shapes,
        input_output_aliases=aliases,
        compiler_params=_params("parallel"), name="mix_in",
    )(*inputs)


def _merge_kernel(x_ref, oa_ref, ob_ref, mg_ref, wa_ref, wb_ref, wo_ref, o_ref):
    mg = mg_ref[...]
    ya = _dot(oa_ref[...], wa_ref[...])
    yb = _dot(ob_ref[...], wb_ref[...])
    y = _sigmoid(mg[:, :D_MODEL]) * ya + _sigmoid(mg[:, D_MODEL:]) * yb
    o_ref[...] = x_ref[...] + _dot(y.astype(BF16), wo_ref[...])


def _merge(x, oa, ob, mg, wa, wb, wo, tm):
    m = x.shape[0]
    row = lambda width: pl.BlockSpec((tm, width), lambda i: (i, 0))
    return pl.pallas_call(
        _merge_kernel,
        grid=(m // tm,),
        in_specs=[row(D_MODEL), row(NSA_Q_W), row(DN_W), row(2 * D_MODEL),
                  _const_spec((NSA_Q_W, D_MODEL)), _const_spec((DN_W, D_MODEL)),
                  _const_spec((D_MODEL, D_MODEL))],
        out_specs=row(D_MODEL),
        out_shape=jax.ShapeDtypeStruct((m, D_MODEL), F32),
        compiler_params=_params("parallel"), name="branch_merge",
    )(x, oa, ob, mg, wa, wb, wo)


def _norm_kernel(x_ref, w_ref, o_ref):
    o_ref[...] = _rms(x_ref[...], w_ref[...])


def _final_norm(x, w, tm):
    m = x.shape[0]
    row = pl.BlockSpec((tm, D_MODEL), lambda i: (i, 0))
    return pl.pallas_call(
        _norm_kernel, grid=(m // tm,), in_specs=[row, _const_spec((1, D_MODEL))], out_specs=row,
        out_shape=jax.ShapeDtypeStruct((m, D_MODEL), F32), compiler_params=_params("parallel"),
    )(x, w)


def _compress_kernel(kcmp_ref, vcmp_ref, pe_ref, w1_ref, w2_ref, kc_ref, vct_ref):
    ns = kc_ref.shape[1]
    res = []
    for part, rows_ref in enumerate((kcmp_ref, vcmp_ref)):
        sa = jnp.zeros((ns, LANES), F32)
        sb = jnp.zeros((ns, LANES), F32)
        for p in range(CMP_STRIDE):
            x = rows_ref[pl.ds(p, ns, stride=CMP_STRIDE), :]
            sa = sa + _dot((x + pe_ref[part, p:p + 1, :]).astype(BF16), w1_ref[part, p])
            sb = sb + _dot((x + pe_ref[part, CMP_STRIDE + p:CMP_STRIDE + p + 1, :]).astype(BF16),
                           w1_ref[part, CMP_STRIDE + p])
        pre = sa + pltpu.roll(sb, ns - 1, 0)
        res.append(_dot(jax.nn.gelu(pre).astype(BF16), w2_ref[part]))
    kc_ref[0] = res[0].astype(BF16)
    vt = res[1].T
    ones = jnp.ones((VT_ROWS - HEAD_DIM, ns), F32)
    for k in range(NSA_KV_HEADS):
        vct_ref[0, k] = jnp.concatenate([vt[k * HEAD_DIM:(k + 1) * HEAD_DIM], ones], axis=0).astype(BF16)


def _compress_prompt(kcmp, vcmp, pe, w1, w2, b, seq):
    ns = seq // CMP_STRIDE
    tok = pl.BlockSpec((seq, LANES), lambda i: (i, 0))
    return pl.pallas_call(
        _compress_kernel,
        grid=(b,),
        in_specs=[tok, tok, _const_spec(pe.shape), _const_spec(w1.shape), _const_spec(w2.shape)],
        out_specs=[pl.BlockSpec((1, ns, LANES), lambda i: (i, 0, 0)),
                   pl.BlockSpec((1, NSA_KV_HEADS, VT_ROWS, ns), lambda i: (i, 0, 0, 0))],
        out_shape=[jax.ShapeDtypeStruct((b, ns, LANES), BF16),
                   jax.ShapeDtypeStruct((b, NSA_KV_HEADS, VT_ROWS, ns), BF16)],
        compiler_params=_params("parallel"), name="nsa_compress",
    )(kcmp, vcmp, pe, w1, w2)


TQ = 512
KT = TQ
SUB = 128
WIN_KEYS = WINDOW + SUB


def _softmax_rows(s, mask):
    s = jnp.where(mask, s, NEG_BIG)
    e = jnp.where(mask, jnp.exp(s - jnp.max(s, axis=-1, keepdims=True)), 0.0)
    return e * (1.0 / jnp.maximum(jnp.sum(e, axis=-1, keepdims=True), 1e-30))


def _topk_bias_t(imp_t, n_cand, n_sel):
    nr = imp_t.shape[0] // 8
    xs = [imp_t[8 * r:8 * r + 8, :] for r in range(nr)]
    cnt = [jnp.zeros_like(xs[0]) for _ in range(nr)]
    sub = lax.broadcasted_iota(jnp.int32, xs[0].shape, 0)
    for j in range(n_cand):
        rj = j // 8
        row = xs[rj][j % 8:j % 8 + 1, :]
        for r in range(nr):
            if r < rj:
                inc = jnp.where(row > xs[r], 1.0, 0.0)
            elif r > rj:
                inc = jnp.where(row >= xs[r], 1.0, 0.0)
            else:
                inc = jnp.where(sub > j % 8, jnp.where(row >= xs[r], 1.0, 0.0),
                                jnp.where(row > xs[r], 1.0, 0.0))
            cnt[r] = cnt[r] + inc
    return jnp.concatenate([jnp.where(c < n_sel, 0.0, NEG_BIG) for c in cnt], axis=0)


def _nsa_prompt_kernel(qraw_ref, qrot_ref, small_ref, kc_ref, vct_ref, ksa_ref, vst_ref, win_ref, vwt_ref,
                       ovt_ref, cbias_ref, dbias_ref, wbias_ref, o_ref):
    seq = ksa_ref.shape[1]
    ns = seq // SEL_BLOCK
    s0 = pl.multiple_of(pl.program_id(1) * TQ, TQ)
    cols = NSA_GROUP * TQ
    tile = lambda a: jnp.concatenate([a] * NSA_GROUP, axis=1)
    qpos_row = s0 + lax.broadcasted_iota(jnp.int32, (1, cols), 1) % TQ
    gates_t = _sigmoid(small_ref[...]).T

    blk_t = lax.broadcasted_iota(jnp.int32, (ns, TQ), 0)
    qpos_t = s0 + lax.broadcasted_iota(jnp.int32, (ns, TQ), 1)
    cur_t = qpos_t // SEL_BLOCK
    valid_t = blk_t * SEL_BLOCK <= qpos_t
    forced_t = (blk_t == 0) | (blk_t == cur_t) | (blk_t == cur_t - 1)
    cbias = tile(cbias_ref[0])
    dbias = tile(dbias_ref[...])
    nsub = TQ // SUB

    def split_sum(aug):
        return aug[0:HEAD_DIM], aug[HEAD_DIM:HEAD_DIM + 1]

    def sub_cols(a, r):
        return jnp.concatenate([a[:, g * TQ + r * SUB:g * TQ + (r + 1) * SUB] for g in range(NSA_GROUP)], axis=1)

    def join_subs(parts):
        return jnp.concatenate([parts[r][:, g * SUB:(g + 1) * SUB]
                                for g in range(NSA_GROUP) for r in range(nsub)], axis=1)

    kv_heads = range(NSA_KV_HEADS)
    o_c, o_w, q_sel = [], [], []
    for k in kv_heads:
        def q_t(ref):
            zeros = jnp.zeros((HEAD_DIM, TQ), F32)
            blocks = []
            for pair in range(NSA_GROUP // 2):
                j = k * (NSA_GROUP // 2) + pair
                both = ref[:, j * LANES:(j + 1) * LANES].astype(F32).T
                for half in range(2):
                    qh = both[half * HEAD_DIM:(half + 1) * HEAD_DIM]
                    blocks.append(jnp.concatenate([qh, zeros] if k == 0 else [zeros, qh], axis=0))
            return jnp.concatenate(blocks, axis=1).astype(BF16)
        qt_raw = q_t(qraw_ref)
        qt_rot = q_t(qrot_ref)

        s = _dot(kc_ref[0], qt_raw) + cbias
        e = jnp.exp2(s - jnp.max(s, axis=0, keepdims=True))
        acc, l = split_sum(_dot(vct_ref[0, k], e.astype(BF16)))
        inv = jnp.where(qpos_row >= CMP_BLOCK - 1, 1.0 / l, 0.0)
        o_c.append(acc * inv)
        p = e * inv
        p_sum = p[:, 0:TQ] + p[:, TQ:2 * TQ] + p[:, 2 * TQ:3 * TQ] + p[:, 3 * TQ:4 * TQ]
        imp_t = _dot_exact_lhs(ovt_ref[...], p_sum)[0:ns]
        imp_t = jnp.where(valid_t, imp_t + jnp.where(forced_t, FORCE_BONUS, 0.0), -jnp.inf)
        bias_t = _topk_bias_t(imp_t, ns, min(N_SEL, ns))
        if ns < LANES:
            bias_t = jnp.concatenate([bias_t, jnp.zeros((LANES - ns, TQ), F32)], axis=0)
        q_sel.append(jnp.concatenate([qt_rot, tile(bias_t).astype(BF16)], axis=0))

        parts = []
        for r in range(nsub):
            q0 = s0 + r * SUB
            w0 = pl.multiple_of(jnp.maximum(q0 - WINDOW, 0), SUB)
            bias = tile(wbias_ref[jnp.minimum(q0 // SUB, WINDOW // SUB)])
            s = _dot(win_ref[0, pl.ds(w0, WIN_KEYS), :], sub_cols(qt_rot, r)) + bias
            e = jnp.exp2(s - jnp.max(s, axis=0, keepdims=True)).astype(BF16)
            acc, l = split_sum(_dot(vwt_ref[k, :, pl.ds(w0, WIN_KEYS)], e))
            parts.append(acc * (1.0 / l))
        o_w.append(join_subs(parts))

    def sel_step(c, carry):
        k0 = pl.multiple_of(c * KT, KT)
        keys = ksa_ref[0, pl.ds(k0, KT), :]
        out = []
        for k in kv_heads:
            m, acc = carry[k]
            s = _dot(keys, q_sel[k])
            m_new = jnp.maximum(m, jnp.max(s, axis=0, keepdims=True))
            p = jnp.exp2(s - m_new).astype(BF16)
            out.append((m_new, jnp.exp2(m - m_new) * acc + _dot(vst_ref[k, :, pl.ds(k0, KT)], p)))
        return tuple(out)
    init = tuple((jnp.full((1, cols), NEG_BIG, F32), jnp.zeros((VT_ROWS, cols), F32)) for _ in kv_heads)
    carry = lax.fori_loop(0, s0 // KT, sel_step, init)
    res = []
    for k in kv_heads:
        parts = []
        for r in range(nsub):
            nk = (r + 1) * SUB
            m, acc = sub_cols(carry[k][0], r), sub_cols(carry[k][1], r)
            s = _dot(ksa_ref[0, pl.ds(s0, nk), :], sub_cols(q_sel[k], r))
            s = jnp.concatenate([s[0:r * SUB], s[r * SUB:nk] + dbias], axis=0) if r else s + dbias
            m_new = jnp.maximum(m, jnp.max(s, axis=0, keepdims=True))
            p = jnp.exp2(s - m_new).astype(BF16)
            acc, l = split_sum(jnp.exp2(m - m_new) * acc + _dot(vst_ref[k, :, pl.ds(s0, nk)], p))
            parts.append(acc * (1.0 / l))
        res.append((o_c[k], join_subs(parts), o_w[k]))

    for g in range(NSA_GROUP):
        halves = []
        for k in range(NSA_KV_HEADS):
            r = SM_GATE + (k * NSA_GROUP + g) * 3
            halves.append(sum(gates_t[r + br:r + br + 1, :] * res[k][br][:, g * TQ:(g + 1) * TQ]
                              for br in range(3)))
        o_ref[:, g * LANES:(g + 1) * LANES] = jnp.concatenate(halves, axis=0).T.astype(BF16)


def _attn_bias_tables(seq):
    r = np.arange(TQ)[None, None, :]
    blocks = np.arange(seq // TQ)[:, None, None]
    c = np.arange(seq // CMP_STRIDE)[None, :, None]
    cb = c * CMP_STRIDE + CMP_BLOCK - 1 <= blocks * TQ + r
    rs = np.arange(SUB)[None, :]
    db = np.arange(SUB)[:, None] <= rs
    kw = np.arange(WIN_KEYS)[None, :, None]
    nw = WINDOW // SUB
    qpos = np.arange(nw + 1)[:, None, None] * SUB + rs[None]
    wb = (kw <= qpos) & (kw > qpos - WINDOW)
    f = lambda m: jnp.asarray(np.where(m, 0.0, NEG_BIG).astype(np.float32))
    return f(cb), f(db), f(wb)


def _nsa_prompt(qraw, qrot, small, kc, vct, ksa, vst, kw16, vwt, ovt, tables, b, seq):
    nc = kc.shape[1]
    nq = seq // TQ
    cb, db, wb = tables
    tok = lambda width: pl.BlockSpec((TQ, width), lambda i, j: (i * nq + j, 0))
    per_b = lambda width: pl.BlockSpec((1, seq, width), lambda i, j: (i, 0, 0))
    vt_spec = pl.BlockSpec((NSA_KV_HEADS, VT_ROWS, seq), lambda i, j: (0, 0, i))
    return pl.pallas_call(
        _nsa_prompt_kernel,
        grid=(b, nq),
        in_specs=[tok(NSA_Q_W), tok(NSA_Q_W), tok(LANES),
                  pl.BlockSpec((1, nc, LANES), lambda i, j: (i, 0, 0)),
                  pl.BlockSpec((1, NSA_KV_HEADS, VT_ROWS, nc), lambda i, j: (i, 0, 0, 0)),
                  per_b(2 * LANES), vt_spec, per_b(LANES), vt_spec,
                  _const_spec(ovt.shape),
                  pl.BlockSpec((1, nc, TQ), lambda i, j: (j, 0, 0)),
                  _const_spec(db.shape), _const_spec(wb.shape)],
        out_specs=tok(NSA_Q_W),
        out_shape=jax.ShapeDtypeStruct((b * seq, NSA_Q_W), BF16),
        compiler_params=_params("parallel", "parallel"), name="nsa_prompt",
    )(qraw, qrot, small, kc, vct, ksa.reshape(b, seq, 2 * LANES), vst, kw16.reshape(b, seq, LANES), vwt,
      ovt, cb, db, wb)


PAGES_PER_STEP = 32
SEG_PER_PAGE = PAGE_SIZE // CMP_STRIDE
SEG_W = CMP_STRIDE * 2 * NSA_KV_W


def _nsa_sample_kernel(ts, past, pt_ref, *refs):
    pages = refs[:PAGES_PER_STEP]
    (qraw_ref, qrot_ref, small_ref, rows_ref, winnew_ref, winst_ref, pe_ref, w1_ref, w2_ref,
     ovt_ref, rexp_ref, o_ref, seg_ref, ksat_ref, vst_ref, sm_ref, stage_ref, bias_ref, res_ref) = refs[PAGES_PER_STEP:]
    del pt_ref
    pg = pl.program_id(1)
    nseg = res_ref.shape[1]
    nrows = NSA_HEADS * ts
    kt = NSA_KV_HEADS * ts
    half_w = NSA_KV_HEADS * CMP_STRIDE * HEAD_DIM

    @pl.when((pl.program_id(0) == 0) & (pg == 0))
    def _():
        for part in range(2):
            hi, mid, lo = _split3(pe_ref[part])
            w = w1_ref[part]
            bias_ref[part] = _dot(hi, w) + _dot(mid, w) + _dot(lo, w)

    for e in range(PAGES_PER_STEP):
        for part in range(2):
            xt = pages[e][0, 0, part * LANES:(part + 1) * LANES, :].T
            for j in range(PAGE_SIZE // 8):
                start = (j % 2) * (PAGE_SIZE // 2) + j // 2
                stage_ref[e, part, pl.ds(start, 8, stride=SEG_PER_PAGE), :] = xt[8 * j:8 * j + 8, :]
        k0 = pl.multiple_of((pg * PAGES_PER_STEP + e) * PAGE_SIZE, PAGE_SIZE)
        ksat_ref[0:LANES, pl.ds(k0, PAGE_SIZE)] = pages[e][0, 0, 2 * LANES:3 * LANES, :].astype(BF16)
        vst_ref[:, pl.ds(k0, PAGE_SIZE)] = pages[e][0, 0, 3 * LANES:4 * LANES, :].astype(BF16)

        @pl.when(pl.program_id(0) == 0)
        def _():
            kpos = k0 + lax.broadcasted_iota(jnp.int32, (LANES, PAGE_SIZE), 1)
            blk = lax.broadcasted_iota(jnp.int32, (LANES, PAGE_SIZE), 0)
            ksat_ref[LANES:2 * LANES, pl.ds(k0, PAGE_SIZE)] = jnp.where(
                kpos // SEL_BLOCK == blk, 1.0, 0.0).astype(BF16)
    for pair in range(PAGES_PER_STEP // 2):
        r0 = pair * 2 * SEG_PER_PAGE
        for part in range(2):
            for p in range(CMP_STRIDE):
                x = jnp.concatenate([stage_ref[2 * pair + e, part, p * SEG_PER_PAGE:(p + 1) * SEG_PER_PAGE, :]
                                     for e in range(2)], axis=0)
                c0 = part * half_w + p * LANES
                seg_ref[r0:r0 + 2 * SEG_PER_PAGE, c0:c0 + LANES] = x.astype(BF16)
    g_rows = PAGES_PER_STEP * SEG_PER_PAGE
    for part in range(2):
        res_ref[part, pl.ds(pl.multiple_of(pg * g_rows, g_rows), g_rows), :] = _dot(
            seg_ref[:, part * half_w:(part + 1) * half_w], w1_ref[part])

    @pl.when(pg == pl.num_programs(1) - 1)
    def _():
        ns_past = past // SEL_BLOCK
        nc = nseg - 1
        q_raw = qraw_ref[0]
        q_rot = qrot_ref[0]
        q_rot_f = q_rot.astype(F32)
        row = lax.broadcasted_iota(jnp.int32, (nrows, 1), 0)
        t_row = row % ts
        kcvc = []
        for part in range(2):
            res = res_ref[part]
            b8 = bias_ref[part]
            pre = (res[:, 0:LANES] + b8[0:1, 0:LANES]
                   + pltpu.roll(res[:, LANES:2 * LANES] + b8[1:2, LANES:2 * LANES], nseg - 1, 0))
            kcvc.append(_dot(jax.nn.gelu(pre).astype(BF16), w2_ref[part]).astype(BF16))
        s_c = _dot_nt(q_raw, kcvc[0])
        p_c = _softmax_rows(s_c, lax.broadcasted_iota(jnp.int32, (nrows, nseg), 1) < nc)
        o_c = _dot(p_c.astype(BF16), kcvc[1])
        p_sum = p_c[0:kt] + p_c[kt:2 * kt] + p_c[2 * kt:3 * kt] + p_c[3 * kt:4 * kt]
        p_sum = jnp.concatenate([p_sum, jnp.zeros((LANES - kt, nseg), F32)], axis=0)
        nb = ovt_ref.shape[0]
        imp_t = _dot_exact_lhs_nt(ovt_ref[...], p_sum)
        blk_t = lax.broadcasted_iota(jnp.int32, (nb, LANES), 0)
        qpos_t = past + lax.broadcasted_iota(jnp.int32, (nb, LANES), 1) % ts
        cur_t = qpos_t // SEL_BLOCK
        forced_t = (blk_t == 0) | (blk_t == cur_t) | (blk_t == cur_t - 1)
        imp_t = jnp.where(blk_t * SEL_BLOCK <= qpos_t, imp_t + jnp.where(forced_t, FORCE_BONUS, 0.0), -jnp.inf)
        bias_t = _topk_bias_t(imp_t, -(-(past + ts) // SEL_BLOCK), N_SEL)[0:ns_past]
        bias = bias_t.T[0:kt]
        q_sel = jnp.concatenate([q_rot, jnp.concatenate([bias] * NSA_GROUP, axis=0).astype(BF16)], axis=1)

        def attend(s_past, v_past16_t, k_new, v_new, mask_past):
            if mask_past is not None:
                s_past = jnp.where(mask_past, s_past, NEG_BIG)
            s_new = [jnp.where(t_row >= j, jnp.sum(q_rot_f * k_new[j:j + 1, :], axis=-1, keepdims=True), NEG_BIG)
                     for j in range(ts)]
            m = jnp.max(s_past, axis=-1, keepdims=True)
            for s in s_new:
                m = jnp.maximum(m, s)
            p_past = jnp.exp(s_past - m)
            if mask_past is not None:
                p_past = jnp.where(mask_past, p_past, 0.0)
            acc = _dot_nt(p_past.astype(BF16), v_past16_t)
            l = jnp.sum(p_past, axis=-1, keepdims=True)
            for j, s in enumerate(s_new):
                p = jnp.where(t_row >= j, jnp.exp(s - m), 0.0)
                acc = acc + p * v_new[j:j + 1, :]
                l = l + p
            return acc * (1.0 / l)

        new = rows_ref[0]
        o_s = attend(_dot(q_sel, ksat_ref[...]), vst_ref[...], new[:, 2 * NSA_KV_W:3 * NSA_KV_W],
                     new[:, 3 * NSA_KV_W:4 * NSA_KV_W], None)
        wbuf = winst_ref.shape[3]
        wnew = winnew_ref[0]
        i_w = lax.broadcasted_iota(jnp.int32, (nrows, wbuf), 1)
        dist = t_row + wbuf - i_w
        o_w = attend(_dot(q_rot, winst_ref[0, 0, 0:LANES, :].astype(BF16)),
                     winst_ref[0, 0, LANES:2 * LANES, :].astype(BF16),
                     wnew[:, 0:LANES], wnew[:, LANES:2 * LANES],
                     (dist < WINDOW) & (past - wbuf + i_w >= 0))

        sm_ref[...] = jnp.zeros_like(sm_ref)
        sm_ref[0:ts, :] = small_ref[0]
        gexp = _dot_exact_rhs(_sigmoid(sm_ref[...]), rexp_ref[...])[0:kt]
        low = lax.broadcasted_iota(jnp.int32, (kt, LANES), 1) < HEAD_DIM
        for g in range(NSA_GROUP):
            out = jnp.zeros((kt, LANES), F32)
            for br, o_br in enumerate((o_c, o_s, o_w)):
                og = o_br[g * kt:(g + 1) * kt]
                both = jnp.where(low, og, pltpu.roll(og, kt - ts, 0))
                out = out + gexp[:, br * NSA_Q_W + g * LANES:br * NSA_Q_W + (g + 1) * LANES] * both
            o_ref[0, :, g * LANES:(g + 1) * LANES] = out


def _nsa_sample(layer, cache, page_table, qraw_s, qrot_s, small_s, rows_s, winnew_s, win_state,
                pe, w1, w2, ovt, rexp, past):
    b, ts = small_s.shape[:2]
    n_pages = page_table.shape[1]
    assert n_pages % PAGES_PER_STEP == 0 and ts * NSA_KV_HEADS == 8 and past == n_pages * PAGE_SIZE
    nseg = past // CMP_STRIDE
    wbuf = win_state.shape[3]
    kw = 4 * NSA_KV_W

    def page_spec(e):
        return pl.BlockSpec((1, 1, 4 * PAGE_SIZE, LANES),
                            lambda i, g, pt: (layer, pt[i, g * PAGES_PER_STEP + e], 0, 0))
    per_b = lambda shape: pl.BlockSpec((1,) + shape, lambda i, g, pt: (i,) + (0,) * len(shape))
    const = lambda shape: pl.BlockSpec(shape, lambda i, g, pt: (0,) * len(shape))
    grid_spec = pltpu.PrefetchScalarGridSpec(
        num_scalar_prefetch=1,
        grid=(b, n_pages // PAGES_PER_STEP),
        in_specs=[page_spec(e) for e in range(PAGES_PER_STEP)] + [
            per_b((NSA_HEADS * ts, LANES)), per_b((NSA_HEADS * ts, LANES)), per_b((ts, LANES)),
            per_b((ts, kw)), per_b((ts, 2 * NSA_KV_W)),
            pl.BlockSpec((1, 1, 2 * NSA_KV_W, wbuf), lambda i, g, pt: (layer, i, 0, 0)),
            const(pe.shape), const(w1.shape), const(w2.shape), const(ovt.shape), const(rexp.shape)],
        out_specs=per_b((NSA_KV_HEADS * ts, NSA_Q_W)),
        scratch_shapes=[pltpu.VMEM((PAGES_PER_STEP * SEG_PER_PAGE, SEG_W), BF16),
                        pltpu.VMEM((2 * LANES, past), BF16), pltpu.VMEM((LANES, past), BF16),
                        pltpu.VMEM((2 * NSA_KV_HEADS * ts, LANES), F32),
                        pltpu.VMEM((PAGES_PER_STEP, 2, PAGE_SIZE, LANES), F32),
                        pltpu.VMEM((2, 8, 2 * LANES), F32),
                        pltpu.VMEM((2, nseg, 2 * LANES), F32)])
    return pl.pallas_call(
        functools.partial(_nsa_sample_kernel, ts, past),
        grid_spec=grid_spec,
        out_shape=jax.ShapeDtypeStruct((b, NSA_KV_HEADS * ts, NSA_Q_W), F32),
        compiler_params=_params("arbitrary", "arbitrary"), name="nsa_sample",
    )(page_table, *([cache] * PAGES_PER_STEP), qraw_s, qrot_s, small_s, rows_s, winnew_s, win_state,
      pe, w1, w2, ovt, rexp)


DN_C = 128
DN_CHUNKS_PER_STEP = 2
HALO = 8


def _softplus(x):
    return jnp.maximum(x, 0.0) + jnp.log(1.0 + jnp.exp(-jnp.abs(x)))


def _l2n(x):
    return x * lax.rsqrt(jnp.sum(x * x, axis=-1, keepdims=True) + EPS)


def _dn_prep_kernel(n_valid, qkv_ref, halo_ref, prefix_ref, small_ref, cw_ref, alog_ref, dtb_ref,
                    tri_ref, u_ref, w_ref, qd_ref, qk_ref, kdt_ref, gl_ref, xcat_ref, sm_ref):
    n = pl.program_id(1)
    cps = kdt_ref.shape[0]
    rows = cps * DN_C
    if n_valid == DN_C:
        xcat_ref[HALO:HALO + rows, :] = qkv_ref[...]
        small = small_ref[...]

        @pl.when(n == 0)
        def _():
            xcat_ref[0:HALO, :] = prefix_ref[0]

        @pl.when(n > 0)
        def _():
            xcat_ref[0:HALO, :] = halo_ref[...]
    else:
        xcat_ref[...] = jnp.zeros_like(xcat_ref)
        xcat_ref[0:HALO, :] = prefix_ref[0]
        xcat_ref[HALO:HALO + n_valid, :] = qkv_ref[0]
        sm_ref[...] = jnp.zeros_like(sm_ref)
        sm_ref[0:n_valid, :] = small_ref[0]
        small = sm_ref[...]

    conv = jnp.zeros((rows, DN_QKV_W), F32)
    for i in range(CONV_W):
        conv = conv + xcat_ref[HALO - (CONV_W - 1) + i:HALO - (CONV_W - 1) + i + rows, :] * cw_ref[i:i + 1, :]
    conv = _silu(conv)

    ii = lax.broadcasted_iota(jnp.int32, (DN_C, DN_C), 0)
    jj = lax.broadcasted_iota(jnp.int32, (DN_C, DN_C), 1)
    g_rows = -jnp.exp(alog_ref[...]) * _softplus(small + dtb_ref[...])
    beta_rows = _sigmoid(small)
    chains = [(c, h) for c in range(cps) for h in range(DN_HEADS)]
    gcol, beta, decay = {}, {}, {}
    for c in range(cps):
        g_all = g_rows[c * DN_C:(c + 1) * DN_C]
        beta_all = beta_rows[c * DN_C:(c + 1) * DN_C]
        if n_valid != DN_C:
            g_all = jnp.where(ii < n_valid, g_all, 0.0)
            beta_all = jnp.where(ii < n_valid, beta_all, 0.0)
        gc_all = _dot_exact_lhs(tri_ref[...], g_all)
        gr_all = gc_all.T
        for h in range(DN_HEADS):
            gcol[c, h] = gc_all[:, SM_A + h:SM_A + h + 1]
            beta[c, h] = beta_all[:, SM_B + h:SM_B + h + 1]
            decay[c, h] = jnp.where(ii >= jj, jnp.exp(gcol[c, h] - gr_all[SM_A + h:SM_A + h + 1, :]), 0.0)
    tok = lambda c: slice(c * DN_C, (c + 1) * DN_C)
    q = {(c, h): _l2n(conv[tok(c), h * DN_DK:(h + 1) * DN_DK]) * DN_DK ** -0.5 for c, h in chains}
    k = {(c, h): _l2n(conv[tok(c), DN_QK_W + h * DN_DK:DN_QK_W + (h + 1) * DN_DK]) for c, h in chains}
    v = {(c, h): conv[tok(c), 2 * DN_QK_W + h * DN_DV:2 * DN_QK_W + (h + 1) * DN_DV] for c, h in chains}
    kb = {i: k[i] * beta[i] for i in chains}
    k16 = {i: k[i].astype(BF16) for i in chains}
    eg = {i: jnp.exp(gcol[i]) for i in chains}
    a = {i: jnp.where(ii > jj, _dot_nt(kb[i].astype(BF16), k16[i]) * decay[i], 0.0) for i in chains}
    t = {i: jnp.where(ii == jj, 1.0, 0.0) - a[i] for i in chains}
    for _ in range(max(1, math.ceil(math.log2(n_valid))) - 1):
        a = {i: _dot3(a[i], a[i]) for i in chains}
        t = {i: t[i] + _dot3(t[i], a[i]) for i in chains}
    x = {i: _dot3(t[i], jnp.concatenate([v[i] * beta[i], kb[i] * eg[i]], axis=1)) for i in chains}
    for c, h in chains:
        i = (c, h)
        u_ref[tok(c), h * DN_DV:(h + 1) * DN_DV] = x[i][:, 0:DN_DV]
        w_ref[tok(c), h * DN_DK:(h + 1) * DN_DK] = x[i][:, DN_DV:DN_DV + DN_DK].astype(BF16)
        qk = jnp.where(ii >= jj, _dot_nt(q[i].astype(BF16), k16[i]) * decay[i], 0.0)
        qk_ref[tok(c), h * DN_C:(h + 1) * DN_C] = qk.astype(BF16)
        qd_ref[tok(c), h * DN_DK:(h + 1) * DN_DK] = (q[i] * eg[i]).astype(BF16)
        glast = gcol[i][DN_C - 1:DN_C, :]
        kdt_ref[c, h * DN_DK:(h + 1) * DN_DK, :] = (k[i] * jnp.exp(glast - gcol[i])).T.astype(BF16)
        gl_ref[c, h:h + 1, :] = jnp.broadcast_to(jnp.exp(glast), (1, LANES))
    for c in range(cps):
        gl_ref[c, DN_HEADS:, :] = jnp.zeros((HALO - DN_HEADS, LANES), F32)


def _dn_prep(qkv, prefix8, small, cw, alog_row, dtb_row, tri, b, nch, n_valid):
    rows = b * nch * DN_C
    cps = DN_CHUNKS_PER_STEP if (n_valid == DN_C and nch % DN_CHUNKS_PER_STEP == 0) else 1
    steps = nch // cps
    blk = cps * DN_C
    if n_valid == DN_C:
        qkv_spec = pl.BlockSpec((blk, DN_QKV_W), lambda i, n: (i * steps + n, 0))
        halo_spec = pl.BlockSpec((HALO, DN_QKV_W),
                                 lambda i, n: (jnp.maximum((i * steps + n) * (blk // HALO) - 1, 0), 0))
        small_spec = pl.BlockSpec((blk, LANES), lambda i, n: (i * steps + n, 0))
        qkv_in, halo_in, small_in = qkv, qkv, small
    else:
        qkv_in = qkv.reshape(b, n_valid, DN_QKV_W)
        small_in = small.reshape(b, n_valid, LANES)
        halo_in = prefix8
        qkv_spec = pl.BlockSpec((1, n_valid, DN_QKV_W), lambda i, n: (i, 0, 0))
        halo_spec = pl.BlockSpec((1, HALO, DN_QKV_W), lambda i, n: (i, 0, 0))
        small_spec = pl.BlockSpec((1, n_valid, LANES), lambda i, n: (i, 0, 0))
    tok = lambda width: pl.BlockSpec((blk, width), lambda i, n: (i * steps + n, 0))
    return pl.pallas_call(
        functools.partial(_dn_prep_kernel, n_valid),
        grid=(b, steps),
        in_specs=[qkv_spec, halo_spec, pl.BlockSpec((1, HALO, DN_QKV_W), lambda i, n: (i, 0, 0)), small_spec,
                  _const_spec((CONV_W, DN_QKV_W)), _const_spec((1, LANES)), _const_spec((1, LANES)),
                  _const_spec((DN_C, DN_C))],
        out_specs=[tok(DN_W), tok(DN_QK_W), tok(DN_QK_W), tok(DN_HEADS * DN_C),
                   pl.BlockSpec((cps, DN_QK_W, DN_C), lambda i, n: (i * steps + n, 0, 0)),
                   pl.BlockSpec((cps, HALO, LANES), lambda i, n: (i * steps + n, 0, 0))],
        out_shape=[jax.ShapeDtypeStruct((rows, DN_W), F32), jax.ShapeDtypeStruct((rows, DN_QK_W), BF16),
                   jax.ShapeDtypeStruct((rows, DN_QK_W), BF16), jax.ShapeDtypeStruct((rows, DN_HEADS * DN_C), BF16),
                   jax.ShapeDtypeStruct((b * nch, DN_QK_W, DN_C), BF16),
                   jax.ShapeDtypeStruct((b * nch, HALO, LANES), F32)],
        scratch_shapes=[pltpu.VMEM((HALO + blk, DN_QKV_W), F32), pltpu.VMEM((DN_C, LANES), F32)],
        compiler_params=_params("parallel", "parallel"), name="dn_prep",
    )(qkv_in, halo_in, prefix8, small_in, cw, alog_row, dtb_row, tri)


def _dn_scan_kernel(n_valid, u_ref, w_ref, qd_ref, qk_ref, kdt_ref, gl_ref, zg_ref, s0_ref, nw_ref,
                    o_ref, s_ref):
    @pl.when(pl.program_id(1) == 0)
    def _():
        s_ref[...] = s0_ref[...]

    def block_diag(a0, a1):
        z = jnp.zeros_like(a0)
        return jnp.concatenate([jnp.concatenate([a0, z], axis=1), jnp.concatenate([z, a1], axis=1)], axis=0)

    for b in range(u_ref.shape[0]):
        for pair in range(DN_HEADS // 2):
            heads = (2 * pair, 2 * pair + 1)
            cs2 = slice(2 * pair * DN_DK, (2 * pair + 2) * DN_DK)
            s = [s_ref[b, h] for h in heads]
            s_bd = block_diag(*[x.astype(BF16) for x in s])
            v_new = u_ref[b, :, cs2] - _dot(w_ref[b, :, cs2], s_bd)
            v16 = v_new.astype(BF16)
            v_bd = block_diag(v16[:, 0:DN_DV], v16[:, DN_DV:2 * DN_DV])
            o2 = _dot(qd_ref[b, :, cs2], s_bd) + _dot(qk_ref[b, :, cs2], v_bd)
            for e, h in enumerate(heads):
                cs = slice(h * DN_DK, (h + 1) * DN_DK)
                s_ref[b, h] = s[e] * gl_ref[b, h:h + 1, :] + _dot(kdt_ref[b, cs, :], v16[:, e * DN_DV:(e + 1) * DN_DV])
                o = o2[:, e * DN_DV:(e + 1) * DN_DV]
                on = o * lax.rsqrt(jnp.mean(o * o, axis=-1, keepdims=True) + EPS) * nw_ref[...]
                o_ref[b, :, cs] = (on[0:n_valid] * _silu(zg_ref[b, :, cs])).astype(BF16)


def _dn_scan(u, w, qd, qk, kdt, gl, zg, s0, nw, b, nch, bb, n_valid):
    tok = lambda width: pl.BlockSpec((bb, DN_C, width), lambda i, n: (i, n, 0))
    r3 = lambda a: a.reshape(b, nch * DN_C, a.shape[-1])
    st = pl.BlockSpec((bb, DN_HEADS, DN_DK, DN_DV), lambda i, n: (i, 0, 0, 0))
    return pl.pallas_call(
        functools.partial(_dn_scan_kernel, n_valid),
        grid=(b // bb, nch),
        in_specs=[tok(DN_W), tok(DN_QK_W), tok(DN_QK_W), tok(DN_HEADS * DN_C),
                  pl.BlockSpec((bb, DN_QK_W, DN_C), lambda i, n: (i, n, 0)),
                  pl.BlockSpec((bb, HALO, LANES), lambda i, n: (i, n, 0)),
                  pl.BlockSpec((bb, n_valid, DN_W), lambda i, n: (i, n, 0)), st, _const_spec((1, DN_DV))],
        out_specs=[pl.BlockSpec((bb, n_valid, DN_W), lambda i, n: (i, n, 0)), st],
        out_shape=[jax.ShapeDtypeStruct((b, nch * n_valid, DN_W), BF16),
                   jax.ShapeDtypeStruct((b, DN_HEADS, DN_DK, DN_DV), F32)],
        compiler_params=_params("parallel", "arbitrary"), name="dn_scan",
    )(r3(u), r3(w), r3(qd), r3(qk), kdt.reshape(b, nch * DN_QK_W, DN_C), gl.reshape(b, nch * HALO, LANES),
      zg.reshape(b, nch * n_valid, DN_W), s0, nw)


IN_WIDTHS = (NSA_Q_W, 6 * NSA_KV_W, 3 * NSA_HEADS, DN_QKV_W, DN_HEADS, DN_HEADS, DN_W, 2 * D_MODEL)


def _pack_mix_weight(w_in):
    offs = np.cumsum(IN_WIDTHS)[:-1].tolist()
    q, kv, gate, qkv, a, b, zg, merge = jnp.split(w_in, offs, axis=-1)
    lead = w_in.shape[:-1]
    small = jnp.concatenate([gate, a, b, jnp.zeros(lead + (LANES - SM_B - DN_HEADS,), w_in.dtype)], axis=-1)
    return jnp.concatenate([q, kv, small, qkv, zg, merge], axis=-1).astype(BF16)


def _rope_tables(pos):
    half = HEAD_DIM // 2
    freq = ROPE_THETA ** (-jnp.arange(half, dtype=F32) / half)
    ang = pos.astype(F32)[:, None] * freq[None, :]
    cos, sin = jnp.cos(ang), jnp.sin(ang)
    cos = jnp.concatenate([cos, cos] * (LANES // HEAD_DIM), axis=1)
    sin = jnp.concatenate([-sin, sin] * (LANES // HEAD_DIM), axis=1)
    return cos, sin


def _gate_expand_matrix():
    r = np.zeros((LANES, 3 * NSA_Q_W), np.float32)
    for h in range(NSA_HEADS):
        k, g = divmod(h, NSA_GROUP)
        for br in range(3):
            c0 = br * NSA_Q_W + g * LANES + k * HEAD_DIM
            r[SM_GATE + h * 3 + br, c0:c0 + HEAD_DIM] = 1.0
    return jnp.asarray(r, BF16)


def _overlap_t(nc_pad, nc, ns, rows):
    i = np.arange(nc_pad)[None, :]
    j = np.arange(rows)[:, None]
    ov = (i * CMP_STRIDE < (j + 1) * SEL_BLOCK) & (i * CMP_STRIDE + CMP_BLOCK > j * SEL_BLOCK)
    ov = ov & (i < nc) & (j < ns)
    return jnp.asarray(ov.astype(np.float32), BF16)


def _pack_compress_sample(pos_emb, w1, w2):
    eye = jnp.eye(NSA_KV_HEADS, dtype=w1.dtype)
    w = w1.reshape(2, 2, CMP_STRIDE, HEAD_DIM, HEAD_DIM)
    w = jnp.einsum('kapde,hg->kphdage', w, eye)
    w = w.reshape(2, NSA_KV_HEADS * CMP_STRIDE * HEAD_DIM, 2 * LANES).astype(BF16)
    pe = pos_emb.reshape(2, 2, CMP_STRIDE, 1, HEAD_DIM)
    pe = jnp.concatenate([pe] * NSA_KV_HEADS, axis=3).reshape(2, 2, -1)
    pe = jnp.concatenate([pe, jnp.zeros((2, 6, pe.shape[2]), pe.dtype)], axis=1)
    w2d = jnp.einsum('kde,hg->khdge', w2, eye).reshape(2, LANES, LANES).astype(BF16)
    return pe, w, w2d


def _sample_rows(q_pad, b, ts):
    q = q_pad.reshape(b, ts, NSA_KV_HEADS, NSA_GROUP, 1, HEAD_DIM)
    eye = jnp.eye(NSA_KV_HEADS, dtype=q.dtype).reshape(NSA_KV_HEADS, 1, NSA_KV_HEADS, 1)
    q = (q * eye).transpose(0, 3, 2, 1, 4, 5)
    return q.reshape(b, NSA_HEADS * ts, LANES)


def _dn_lane_row(vals):
    return jnp.zeros((1, LANES), F32).at[0, SM_A:SM_A + DN_HEADS].set(vals.astype(F32))


def _deltanet(qkv, small, zg, prefix, s0, dn_params, tri, b, seq_tokens):
    conv_w, a_log, dt_bias, norm_w = dn_params
    if seq_tokens % DN_C == 0:
        nch, n_valid, bb = seq_tokens // DN_C, DN_C, min(b, 4)
    else:
        nch, n_valid, bb = 1, seq_tokens, 4
    prefix8 = jnp.concatenate([jnp.zeros((b, HALO - (CONV_W - 1), DN_QKV_W), F32), prefix], axis=1)
    u, w, qd, qk, kdt, gl = _dn_prep(qkv, prefix8, small, conv_w, _dn_lane_row(a_log), _dn_lane_row(dt_bias),
                                     tri, b, nch, n_valid)
    ob, s_out = _dn_scan(u, w, qd, qk, kdt, gl, zg, s0, norm_w[None], b, nch, bb, n_valid)
    return ob.reshape(b * seq_tokens, DN_W), s_out


def _head_perm_rows(wa):
    d = wa.shape[1]
    return wa.reshape(NSA_KV_HEADS, NSA_GROUP, HEAD_DIM, d).transpose(1, 0, 2, 3).reshape(NSA_Q_W, d)


def _pack_compress(pos_emb, w1, w2):
    eye = jnp.eye(NSA_KV_HEADS, dtype=w1.dtype)
    pe = jnp.concatenate([pos_emb] * NSA_KV_HEADS, axis=2)
    w1p = w1.reshape(2, CMP_BLOCK, HEAD_DIM, HEAD_DIM)
    w1d = jnp.einsum('kpde,hg->kphdge', w1p, eye).reshape(2, CMP_BLOCK, LANES, LANES)
    w2d = jnp.einsum('kde,hg->khdge', w2, eye).reshape(2, LANES, LANES)
    return pe, w1d.astype(BF16), w2d.astype(BF16)


def kernel(x_prompt, x_sample, cache_nsa_kv, state_nsa_win, state_dn_S, state_dn_conv, page_table,
           ffn1_norm, ffn1_w_in, ffn1_w_out, mix_norm, w_in, nsa_cmp_pos, nsa_cmp_w1, nsa_cmp_w2,
           dn_conv_w, dn_A_log, dn_dt_bias, dn_out_norm, w_branch_a, w_branch_b, w_out,
           ffn2_norm, ffn2_w_in, ffn2_w_out, final_norm):
    b, seq = x_prompt.shape[:2]
    bs, ts = x_sample.shape[:2]
    depth = w_in.shape[0]
    n_phys = cache_nsa_kv.shape[1]
    past = page_table.shape[1] * PAGE_SIZE
    wbuf = state_nsa_win.shape[2]
    kvw = 4 * NSA_KV_W

    xp = x_prompt.reshape(b * seq, D_MODEL)
    xs = x_sample.reshape(bs * ts, D_MODEL)
    cos_p, sin_p = _rope_tables(jnp.arange(seq, dtype=jnp.int32))
    cos_s, sin_s = _rope_tables(jnp.tile(past + jnp.arange(ts, dtype=jnp.int32), bs))
    nc = (seq - CMP_BLOCK) // CMP_STRIDE + 1
    ovt_p = _overlap_t(seq // CMP_STRIDE, nc, seq // SEL_BLOCK, LANES)
    tables_p = _attn_bias_tables(seq)
    ns_s = -(-(past + ts) // SEL_BLOCK)
    ovt_s = _overlap_t(past // CMP_STRIDE, past // CMP_STRIDE - 1, ns_s, -(-ns_s // 16) * 16)
    rexp = _gate_expand_matrix()
    tri = jnp.asarray(np.tril(np.ones((DN_C, DN_C), np.float32)), BF16)
    cache = cache_nsa_kv.transpose(0, 1, 3, 4, 5, 2).reshape(depth, n_phys, kvw, PAGE_SIZE)
    win_state = state_nsa_win.transpose(0, 1, 3, 4, 5, 2).reshape(depth, bs, 2 * NSA_KV_W, wbuf)
    tm_p, tm_s = 512, bs * ts

    outs = [[] for _ in range(8)]
    rows_all = None
    w_mix_all = _pack_mix_weight(w_in)
    for l in range(depth):
        f1 = (ffn1_norm[l][None], ffn1_w_in[l].astype(BF16), ffn1_w_out[l].astype(BF16))
        f2 = (ffn2_norm[l][None], ffn2_w_in[l].astype(BF16), ffn2_w_out[l].astype(BF16))
        w_mix = w_mix_all[l]
        mrg = (_head_perm_rows(w_branch_a[l]).astype(BF16), w_branch_b[l].astype(BF16), w_out[l].astype(BF16))
        dn_params = (dn_conv_w[l], dn_A_log[l], dn_dt_bias[l], dn_out_norm[l])
        cmp_l = (nsa_cmp_pos[l], nsa_cmp_w1[l], nsa_cmp_w2[l])

        xp = _ffn(xp, *f1, tm_p)
        (qraw, qrot, small, qkv, zg, mg, rows_all, win_t, kcmp, vcmp, ksa, kw16, vst, vwt) = _mix(
            xp, mix_norm[l][None], w_mix, cos_p, sin_p, 256, seq, True, l, depth, rows_all)
        kc, vct = _compress_prompt(kcmp, vcmp, *_pack_compress(*cmp_l), b, seq)
        oa = _nsa_prompt(qraw, qrot, small, kc, vct, ksa, vst, kw16, vwt, ovt_p, tables_p, b, seq)
        ob, s_fin = _deltanet(qkv, small, zg, jnp.zeros((b, CONV_W - 1, DN_QKV_W), F32),
                              jnp.zeros((b, DN_HEADS, DN_DK, DN_DV), F32), dn_params, tri, b, seq)
        xp = _merge(xp, oa, ob, mg, *mrg, tm_p)
        xp = _ffn(xp, *f2, tm_p)
        outs[1].append(win_t[:, :, seq - min(WINDOW, seq):])
        outs[2].append(s_fin)
        outs[3].append(qkv.reshape(b, seq, DN_QKV_W)[:, seq - (CONV_W - 1):])

        xs = _ffn(xs, *f1, tm_s)
        (qraw, qrot, small, qkv, zg, mg, rows, win) = _mix(
            xs, mix_norm[l][None], w_mix, cos_s, sin_s, tm_s, tm_s, False)
        oa = _nsa_sample(l, cache, page_table, _sample_rows(qraw, bs, ts), _sample_rows(qrot, bs, ts),
                         small.reshape(bs, ts, LANES), rows.reshape(bs, ts, kvw), win.reshape(bs, ts, 2 * NSA_KV_W),
                         win_state, *_pack_compress_sample(*cmp_l), ovt_s, rexp, past)
        oa = oa[:, :ts].reshape(bs * ts, NSA_Q_W).astype(BF16)
        ob, s_fin = _deltanet(qkv, small, zg, state_dn_conv[l], state_dn_S[l], dn_params, tri, bs, ts)
        xs = _merge(xs, oa, ob, mg, *mrg, tm_s)
        xs = _ffn(xs, *f2, tm_s)
        outs[4].append(rows.reshape(bs, ts, 4, NSA_KV_HEADS, HEAD_DIM))
        outs[5].append(win.reshape(bs, ts, 2, NSA_KV_HEADS, HEAD_DIM))
        outs[6].append(s_fin)
        outs[7].append(qkv.reshape(bs, ts, DN_QKV_W))

    y_prompt = _final_norm(xp, final_norm[None], tm_p).reshape(b, seq, D_MODEL)
    y_sample = _final_norm(xs, final_norm[None], tm_s).reshape(bs, ts, D_MODEL)
    res = [rows_all] + [jnp.stack(o) for o in outs[1:]]
    res[5] = jnp.concatenate([state_nsa_win, res[5]], axis=2)[:, :, ts:]
    res[7] = jnp.concatenate([state_dn_conv, res[7]], axis=2)[:, :, ts:]
    for i, parts in ((0, 4), (1, 2)):
        r = res[i]
        res[i] = r.reshape(depth, b, parts, NSA_KV_HEADS, HEAD_DIM, r.shape[-1]).transpose(0, 1, 5, 2, 3, 4)
    return (y_prompt, y_sample) + tuple(res)
```

```python
import functools
import math

import jax
import jax.numpy as jnp
import numpy as np
from jax import lax
from jax.experimental import pallas as pl
from jax.experimental.pallas import tpu as pltpu

F32 = jnp.float32
BF16 = jnp.bfloat16

D_MODEL = 1024
DEPTH = 4
PAGE_SIZE = 128
NSA_HEADS = 8
NSA_KV_HEADS = 2
HEAD_DIM = 64
NSA_GROUP = NSA_HEADS // NSA_KV_HEADS
NSA_Q_W = NSA_HEADS * HEAD_DIM
NSA_KV_W = NSA_KV_HEADS * HEAD_DIM
CMP_BLOCK = 32
CMP_STRIDE = 16
SEL_BLOCK = 64
N_SEL = 16
WINDOW = 512
FORCE_BONUS = 1e6
ROPE_THETA = 10000.0
DN_HEADS = 4
DN_DK = 128
DN_DV = 128
DN_QK_W = DN_HEADS * DN_DK
DN_W = DN_HEADS * DN_DV
DN_QKV_W = 2 * DN_QK_W + DN_W
CONV_W = 4
D_FF = 2816
EPS = 1e-6

LANES = 128
VT_ROWS = HEAD_DIM + 16
VMEM_LIMIT = 56 * 1024 * 1024
NEG_BIG = -1e30

C_Q = 0
C_KV = C_Q + NSA_Q_W
C_SMALL = C_KV + 6 * NSA_KV_W
C_QKV = C_SMALL + LANES
C_ZG = C_QKV + DN_QKV_W
C_MERGE = C_ZG + DN_W
MIX_W = C_MERGE + 2 * D_MODEL
SM_GATE = 0
SM_A = 3 * NSA_HEADS
SM_B = SM_A + DN_HEADS


def _dot(a, b):
    return jnp.dot(a, b, preferred_element_type=F32)


def _dot_nt(a, b):
    return lax.dot_general(a, b, (((1,), (1,)), ((), ())), preferred_element_type=F32)


def _split3(a):
    hi = a.astype(BF16)
    r1 = a - hi.astype(F32)
    mid = r1.astype(BF16)
    lo = (r1 - mid.astype(F32)).astype(BF16)
    return hi, mid, lo


def _dot_exact_rhs(a, b01):
    hi, mid, lo = _split3(a)
    return _dot(hi, b01) + _dot(mid, b01) + _dot(lo, b01)


def _dot_exact_lhs(a01, b):
    hi, mid, lo = _split3(b)
    return _dot(a01, hi) + _dot(a01, mid) + _dot(a01, lo)


def _dot_exact_lhs_nt(a01, b):
    hi, mid, lo = _split3(b)
    return _dot_nt(a01, hi) + _dot_nt(a01, mid) + _dot_nt(a01, lo)


def _dot3(a, b):
    ah = a.astype(BF16)
    al = (a - ah.astype(F32)).astype(BF16)
    bh = b.astype(BF16)
    bl = (b - bh.astype(F32)).astype(BF16)
    return _dot(ah, bh) + (_dot(ah, bl) + _dot(al, bh))


def _rms(x, w):
    return x * lax.rsqrt(jnp.mean(x * x, axis=-1, keepdims=True) + EPS) * w


def _sigmoid(x):
    return 1.0 / (1.0 + jnp.exp(-x))


def _silu(x):
    return x * _sigmoid(x)


def _params(*sem):
    return pltpu.CompilerParams(dimension_semantics=sem, vmem_limit_bytes=VMEM_LIMIT)


def _const_spec(shape):
    nd = len(shape)
    return pl.BlockSpec(shape, lambda *_: (0,) * nd)


FF_TILE = 256


def _ffn_kernel(x_ref, nw_ref, wi_ref, wo_ref, o_ref, act_ref):
    x = x_ref[...]
    hb = _rms(x, nw_ref[...]).astype(BF16)
    for j in range(D_FF // FF_TILE):
        g = _dot(hb, wi_ref[:, j * FF_TILE:(j + 1) * FF_TILE])
        u = _dot(hb, wi_ref[:, D_FF + j * FF_TILE:D_FF + (j + 1) * FF_TILE])
        act_ref[:, j * FF_TILE:(j + 1) * FF_TILE] = (_silu(g) * u).astype(BF16)
    o_ref[...] = x + 0.5 * _dot(act_ref[...], wo_ref[...])


def _ffn(x, nw, wi, wo, tm):
    m = x.shape[0]
    return pl.pallas_call(
        _ffn_kernel,
        grid=(m // tm,),
        in_specs=[pl.BlockSpec((tm, D_MODEL), lambda i: (i, 0)),
                  _const_spec((1, D_MODEL)),
                  _const_spec((D_MODEL, 2 * D_FF)),
                  _const_spec((D_FF, D_MODEL))],
        out_specs=pl.BlockSpec((tm, D_MODEL), lambda i: (i, 0)),
        out_shape=jax.ShapeDtypeStruct((m, D_MODEL), F32),
        scratch_shapes=[pltpu.VMEM((tm, D_FF), BF16)],
        compiler_params=_params("parallel"), name="half_ffn",
    )(x, nw, wi, wo)


def _rope128(x, cos, sin):
    lane = lax.broadcasted_iota(jnp.int32, x.shape, 1)
    first = (lane % HEAD_DIM) < (HEAD_DIM // 2)
    swapped = jnp.where(first, pltpu.roll(x, LANES - HEAD_DIM // 2, 1), pltpu.roll(x, HEAD_DIM // 2, 1))
    return x * cos + swapped * sin


def _mix_kernel(seq_tiles, prompt, n_passthrough, x_ref, nw_ref, w_ref, cos_ref, sin_ref, *refs):
    qraw_ref, qrot_ref, small_ref, qkv_ref, zg_ref, merge_ref, *group_refs = refs[n_passthrough:]
    tm = x_ref.shape[0]
    hb = _rms(x_ref[...], nw_ref[...]).astype(BF16)
    cos = cos_ref[...]
    sin = sin_ref[...]
    scale = HEAD_DIM ** -0.5 * (math.log2(math.e) if prompt else 1.0)
    z = _dot(hb, w_ref[:, C_Q:C_QKV])
    for j in range(NSA_Q_W // LANES):
        q = z[:, C_Q + j * LANES:C_Q + (j + 1) * LANES]
        qraw_ref[:, j * LANES:(j + 1) * LANES] = (q * scale).astype(BF16)
        qrot_ref[:, j * LANES:(j + 1) * LANES] = (_rope128(q, cos, sin) * scale).astype(BF16)
    kv = z[:, C_KV:C_SMALL]
    small_ref[...] = z[:, C_SMALL:C_QKV]
    k_cmp = kv[:, 0:LANES]
    v_cmp = kv[:, LANES:2 * LANES]
    k_sel = _rope128(kv[:, 2 * LANES:3 * LANES], cos, sin)
    v_sel = kv[:, 3 * LANES:4 * LANES]
    k_win = _rope128(kv[:, 4 * LANES:5 * LANES], cos, sin)
    v_win = kv[:, 5 * LANES:6 * LANES]
    if prompt:
        rows_t_ref, win_t_ref, kcmp_ref, vcmp_ref, ksa_ref, kw16_ref, vst_ref, vwt_ref = group_refs
        ones = jnp.ones((VT_ROWS - HEAD_DIM, tm), F32)
        for part, a in enumerate((k_cmp, v_cmp, k_sel, v_sel)):
            rows_t_ref[0, 0, part * LANES:(part + 1) * LANES, :] = a.T
        for part, a in enumerate((k_win, v_win)):
            win_t_ref[0, part * LANES:(part + 1) * LANES, :] = a.T
        for v, vt_ref in ((v_sel, vst_ref), (v_win, vwt_ref)):
            vt = v.T
            for k in range(NSA_KV_HEADS):
                vt_ref[k] = jnp.concatenate([vt[k * HEAD_DIM:(k + 1) * HEAD_DIM], ones], axis=0).astype(BF16)
        kcmp_ref[...] = k_cmp
        vcmp_ref[...] = v_cmp
        pos = (pl.program_id(0) % seq_tiles) * tm + lax.broadcasted_iota(jnp.int32, (tm, LANES), 0)
        lane = lax.broadcasted_iota(jnp.int32, (tm, LANES), 1)
        ksa_ref[:, 0:LANES] = k_sel.astype(BF16)
        ksa_ref[:, LANES:2 * LANES] = jnp.where(pos // SEL_BLOCK == lane, 1.0, 0.0).astype(BF16)
        kw16_ref[...] = k_win.astype(BF16)
    else:
        rows_ref, win_ref = group_refs
        for part, a in enumerate((k_cmp, v_cmp, k_sel, v_sel)):
            rows_ref[:, part * LANES:(part + 1) * LANES] = a
        win_ref[:, 0:LANES] = k_win
        win_ref[:, LANES:2 * LANES] = v_win
    qkv_ref[...] = _dot(hb, w_ref[:, C_QKV:C_QKV + DN_QKV_W])
    zg_ref[...] = _dot(hb, w_ref[:, C_ZG:C_ZG + DN_W])
    merge_ref[...] = _dot(hb, w_ref[:, C_MERGE:C_MERGE + 2 * D_MODEL])


def _mix(x, nw, w, cos, sin, tm, seq_len, prompt, layer=0, depth=1, rows_all=None):
    m = x.shape[0]
    seq_tiles = max(seq_len // tm, 1)
    row = lambda width: pl.BlockSpec((tm, width), lambda i: (i, 0))
    tab = pl.BlockSpec((tm, LANES), lambda i: (i % seq_tiles, 0))
    widths = [(NSA_Q_W, BF16), (NSA_Q_W, BF16), (LANES, F32), (DN_QKV_W, F32), (DN_W, F32), (2 * D_MODEL, F32)]
    inputs = [x, nw, w, cos, sin]
    in_specs = [row(D_MODEL), _const_spec((1, D_MODEL)), _const_spec((D_MODEL, MIX_W)), tab, tab]
    aliases = {}
    if prompt:
        nb = m // seq_len
        vt_spec = pl.BlockSpec((NSA_KV_HEADS, VT_ROWS, tm), lambda i: (0, 0, i))
        vt_shape = jax.ShapeDtypeStruct((NSA_KV_HEADS, VT_ROWS, m), BF16)
        g_widths = [(LANES, F32), (LANES, F32), (2 * LANES, BF16), (LANES, BF16)]
        g_specs = ([pl.BlockSpec((1, 1, 4 * NSA_KV_W, tm), lambda i: (layer, i // seq_tiles, 0, i % seq_tiles)),
                    pl.BlockSpec((1, 2 * NSA_KV_W, tm), lambda i: (i // seq_tiles, 0, i % seq_tiles))]
                   + [row(wd) for wd, _ in g_widths] + [vt_spec, vt_spec])
        g_shapes = ([jax.ShapeDtypeStruct((depth, nb, 4 * NSA_KV_W, seq_len), F32),
                     jax.ShapeDtypeStruct((nb, 2 * NSA_KV_W, seq_len), F32)]
                    + [jax.ShapeDtypeStruct((m, wd), dt) for wd, dt in g_widths] + [vt_shape, vt_shape])
        if rows_all is not None:
            aliases = {len(inputs): len(widths)}
            inputs.append(rows_all)
            in_specs.append(pl.BlockSpec(memory_space=pl.ANY))
    else:
        g_widths = [(4 * NSA_KV_W, F32), (2 * NSA_KV_W, F32)]
        g_specs = [row(wd) for wd, _ in g_widths]
        g_shapes = [jax.ShapeDtypeStruct((m, wd), dt) for wd, dt in g_widths]
    return pl.pallas_call(
        functools.partial(_mix_kernel, seq_tiles, prompt, len(aliases)),
        grid=(m // tm,),
        in_specs=in_specs,
        out_specs=[row(wd) for wd, _ in widths] + g_specs,
        out_shape=[jax.ShapeDtypeStruct((m, wd), dt) for wd, dt in widths] + g_shapes,
        input_output_aliases=aliases,
        compiler_params=_params("parallel"), name="mix_in",
    )(*inputs)


def _merge_kernel(x_ref, oa_ref, ob_ref, mg_ref, wa_ref, wb_ref, wo_ref, o_ref):
    mg = mg_ref[...]
    ya = _dot(oa_ref[...], wa_ref[...])
    yb = _dot(ob_ref[...], wb_ref[...])
    y = _sigmoid(mg[:, :D_MODEL]) * ya + _sigmoid(mg[:, D_MODEL:]) * yb
    o_ref[...] = x_ref[...] + _dot(y.astype(BF16), wo_ref[...])


def _merge(x, oa, ob, mg, wa, wb, wo, tm):
    m = x.shape[0]
    row = lambda width: pl.BlockSpec((tm, width), lambda i: (i, 0))
    return pl.pallas_call(
        _merge_kernel,
        grid=(m // tm,),
        in_specs=[row(D_MODEL), row(NSA_Q_W), row(DN_W), row(2 * D_MODEL),
                  _const_spec((NSA_Q_W, D_MODEL)), _const_spec((DN_W, D_MODEL)),
                  _const_spec((D_MODEL, D_MODEL))],
        out_specs=row(D_MODEL),
        out_shape=jax.ShapeDtypeStruct((m, D_MODEL), F32),
        compiler_params=_params("parallel"), name="branch_merge",
    )(x, oa, ob, mg, wa, wb, wo)


def _norm_kernel(x_ref, w_ref, o_ref):
    o_ref[...] = _rms(x_ref[...], w_ref[...])


def _final_norm(x, w, tm):
    m = x.shape[0]
    row = pl.BlockSpec((tm, D_MODEL), lambda i: (i, 0))
    return pl.pallas_call(
        _norm_kernel, grid=(m // tm,), in_specs=[row, _const_spec((1, D_MODEL))], out_specs=row,
        out_shape=jax.ShapeDtypeStruct((m, D_MODEL), F32), compiler_params=_params("parallel"),
    )(x, w)


def _compress_kernel(kcmp_ref, vcmp_ref, pe_ref, w1_ref, w2_ref, kc_ref, vct_ref):
    ns = kc_ref.shape[1]
    res = []
    for part, rows_ref in enumerate((kcmp_ref, vcmp_ref)):
        sa = jnp.zeros((ns, LANES), F32)
        sb = jnp.zeros((ns, LANES), F32)
        for p in range(CMP_STRIDE):
            x = rows_ref[pl.ds(p, ns, stride=CMP_STRIDE), :]
            sa = sa + _dot((x + pe_ref[part, p:p + 1, :]).astype(BF16), w1_ref[part, p])
            sb = sb + _dot((x + pe_ref[part, CMP_STRIDE + p:CMP_STRIDE + p + 1, :]).astype(BF16),
                           w1_ref[part, CMP_STRIDE + p])
        pre = sa + pltpu.roll(sb, ns - 1, 0)
        res.append(_dot(jax.nn.gelu(pre).astype(BF16), w2_ref[part]))
    kc_ref[0] = res[0].astype(BF16)
    vt = res[1].T
    ones = jnp.ones((VT_ROWS - HEAD_DIM, ns), F32)
    for k in range(NSA_KV_HEADS):
        vct_ref[0, k] = jnp.concatenate([vt[k * HEAD_DIM:(k + 1) * HEAD_DIM], ones], axis=0).astype(BF16)


def _compress_prompt(kcmp, vcmp, pe, w1, w2, b, seq):
    ns = seq // CMP_STRIDE
    tok = pl.BlockSpec((seq, LANES), lambda i: (i, 0))
    return pl.pallas_call(
        _compress_kernel,
        grid=(b,),
        in_specs=[tok, tok, _const_spec(pe.shape), _const_spec(w1.shape), _const_spec(w2.shape)],
        out_specs=[pl.BlockSpec((1, ns, LANES), lambda i: (i, 0, 0)),
                   pl.BlockSpec((1, NSA_KV_HEADS, VT_ROWS, ns), lambda i: (i, 0, 0, 0))],
        out_shape=[jax.ShapeDtypeStruct((b, ns, LANES), BF16),
                   jax.ShapeDtypeStruct((b, NSA_KV_HEADS, VT_ROWS, ns), BF16)],
        compiler_params=_params("parallel"), name="nsa_compress",
    )(kcmp, vcmp, pe, w1, w2)


TQ = 512
KT = TQ
SUB = 128
WIN_KEYS = WINDOW + SUB


def _softmax_rows(s, mask):
    s = jnp.where(mask, s, NEG_BIG)
    e = jnp.where(mask, jnp.exp(s - jnp.max(s, axis=-1, keepdims=True)), 0.0)
    return e * (1.0 / jnp.maximum(jnp.sum(e, axis=-1, keepdims=True), 1e-30))


def _topk_bias_t(imp_t, n_cand, n_sel):
    nr = imp_t.shape[0] // 8
    xs = [imp_t[8 * r:8 * r + 8, :] for r in range(nr)]
    cnt = [jnp.zeros_like(xs[0]) for _ in range(nr)]
    sub = lax.broadcasted_iota(jnp.int32, xs[0].shape, 0)
    for j in range(n_cand):
        rj = j // 8
        row = xs[rj][j % 8:j % 8 + 1, :]
        for r in range(nr):
            if r < rj:
                inc = jnp.where(row > xs[r], 1.0, 0.0)
            elif r > rj:
                inc = jnp.where(row >= xs[r], 1.0, 0.0)
            else:
                inc = jnp.where(sub > j % 8, jnp.where(row >= xs[r], 1.0, 0.0),
                                jnp.where(row > xs[r], 1.0, 0.0))
            cnt[r] = cnt[r] + inc
    return jnp.concatenate([jnp.where(c < n_sel, 0.0, NEG_BIG) for c in cnt], axis=0)


def _nsa_prompt_kernel(qraw_ref, qrot_ref, small_ref, kc_ref, vct_ref, ksa_ref, vst_ref, win_ref, vwt_ref,
                       ovt_ref, cbias_ref, dbias_ref, wbias_ref, o_ref):
    seq = ksa_ref.shape[1]
    ns = seq // SEL_BLOCK
    s0 = pl.multiple_of(pl.program_id(1) * TQ, TQ)
    cols = NSA_GROUP * TQ
    tile = lambda a: jnp.concatenate([a] * NSA_GROUP, axis=1)
    qpos_row = s0 + lax.broadcasted_iota(jnp.int32, (1, cols), 1) % TQ
    gates_t = _sigmoid(small_ref[...]).T

    blk_t = lax.broadcasted_iota(jnp.int32, (ns, TQ), 0)
    qpos_t = s0 + lax.broadcasted_iota(jnp.int32, (ns, TQ), 1)
    cur_t = qpos_t // SEL_BLOCK
    valid_t = blk_t * SEL_BLOCK <= qpos_t
    forced_t = (blk_t == 0) | (blk_t == cur_t) | (blk_t == cur_t - 1)
    cbias = tile(cbias_ref[0])
    dbias = tile(dbias_ref[...])
    nsub = TQ // SUB

    def split_sum(aug):
        return aug[0:HEAD_DIM], aug[HEAD_DIM:HEAD_DIM + 1]

    def sub_cols(a, r):
        return jnp.concatenate([a[:, g * TQ + r * SUB:g * TQ + (r + 1) * SUB] for g in range(NSA_GROUP)], axis=1)

    def join_subs(parts):
        return jnp.concatenate([parts[r][:, g * SUB:(g + 1) * SUB]
                                for g in range(NSA_GROUP) for r in range(nsub)], axis=1)

    kv_heads = range(NSA_KV_HEADS)
    o_c, o_w, q_sel = [], [], []
    for k in kv_heads:
        def q_t(ref):
            zeros = jnp.zeros((HEAD_DIM, TQ), F32)
            blocks = []
            for pair in range(NSA_GROUP // 2):
                j = k * (NSA_GROUP // 2) + pair
                both = ref[:, j * LANES:(j + 1) * LANES].astype(F32).T
                for half in range(2):
                    qh = both[half * HEAD_DIM:(half + 1) * HEAD_DIM]
                    blocks.append(jnp.concatenate([qh, zeros] if k == 0 else [zeros, qh], axis=0))
            return jnp.concatenate(blocks, axis=1).astype(BF16)
        qt_raw = q_t(qraw_ref)
        qt_rot = q_t(qrot_ref)

        s = _dot(kc_ref[0], qt_raw) + cbias
        e = jnp.exp2(s - jnp.max(s, axis=0, keepdims=True))
        acc, l = split_sum(_dot(vct_ref[0, k], e.astype(BF16)))
        inv = jnp.where(qpos_row >= CMP_BLOCK - 1, 1.0 / l, 0.0)
        o_c.append(acc * inv)
        p = e * inv
        p_sum = p[:, 0:TQ] + p[:, TQ:2 * TQ] + p[:, 2 * TQ:3 * TQ] + p[:, 3 * TQ:4 * TQ]
        imp_t = _dot_exact_lhs(ovt_ref[...], p_sum)[0:ns]
        imp_t = jnp.where(valid_t, imp_t + jnp.where(forced_t, FORCE_BONUS, 0.0), -jnp.inf)
        bias_t = _topk_bias_t(imp_t, ns, min(N_SEL, ns))
        if ns < LANES:
            bias_t = jnp.concatenate([bias_t, jnp.zeros((LANES - ns, TQ), F32)], axis=0)
        q_sel.append(jnp.concatenate([qt_rot, tile(bias_t).astype(BF16)], axis=0))

        parts = []
        for r in range(nsub):
            q0 = s0 + r * SUB
            w0 = pl.multiple_of(jnp.maximum(q0 - WINDOW, 0), SUB)
            bias = tile(wbias_ref[jnp.minimum(q0 // SUB, WINDOW // SUB)])
            s = _dot(win_ref[0, pl.ds(w0, WIN_KEYS), :], sub_cols(qt_rot, r)) + bias
            e = jnp.exp2(s - jnp.max(s, axis=0, keepdims=True)).astype(BF16)
            acc, l = split_sum(_dot(vwt_ref[k, :, pl.ds(w0, WIN_KEYS)], e))
            parts.append(acc * (1.0 / l))
        o_w.append(join_subs(parts))

    def sel_step(c, carry):
        k0 = pl.multiple_of(c * KT, KT)
        keys = ksa_ref[0, pl.ds(k0, KT), :]
        out = []
        for k in kv_heads:
            m, acc = carry[k]
            s = _dot(keys, q_sel[k])
            m_new = jnp.maximum(m, jnp.max(s, axis=0, keepdims=True))
            p = jnp.exp2(s - m_new).astype(BF16)
            out.append((m_new, jnp.exp2(m - m_new) * acc + _dot(vst_ref[k, :, pl.ds(k0, KT)], p)))
        return tuple(out)
    init = tuple((jnp.full((1, cols), NEG_BIG, F32), jnp.zeros((VT_ROWS, cols), F32)) for _ in kv_heads)
    carry = lax.fori_loop(0, s0 // KT, sel_step, init)
    res = []
    for k in kv_heads:
        parts = []
        for r in range(nsub):
            nk = (r + 1) * SUB
            m, acc = sub_cols(carry[k][0], r), sub_cols(carry[k][1], r)
            s = _dot(ksa_ref[0, pl.ds(s0, nk), :], sub_cols(q_sel[k], r))
            s = jnp.concatenate([s[0:r * SUB], s[r * SUB:nk] + dbias], axis=0) if r else s + dbias
            m_new = jnp.maximum(m, jnp.max(s, axis=0, keepdims=True))
            p = jnp.exp2(s - m_new).astype(BF16)
            acc, l = split_sum(jnp.exp2(m - m_new) * acc + _dot(vst_ref[k, :, pl.ds(s0, nk)], p))
            parts.append(acc * (1.0 / l))
        res.append((o_c[k], join_subs(parts), o_w[k]))

    for g in range(NSA_GROUP):
        halves = []
        for k in range(NSA_KV_HEADS):
            r = SM_GATE + (k * NSA_GROUP + g) * 3
            halves.append(sum(gates_t[r + br:r + br + 1, :] * res[k][br][:, g * TQ:(g + 1) * TQ]
                              for br in range(3)))
        o_ref[:, g * LANES:(g + 1) * LANES] = jnp.concatenate(halves, axis=0).T.astype(BF16)


def _attn_bias_tables(seq):
    r = np.arange(TQ)[None, None, :]
    blocks = np.arange(seq // TQ)[:, None, None]
    c = np.arange(seq // CMP_STRIDE)[None, :, None]
    cb = c * CMP_STRIDE + CMP_BLOCK - 1 <= blocks * TQ + r
    rs = np.arange(SUB)[None, :]
    db = np.arange(SUB)[:, None] <= rs
    kw = np.arange(WIN_KEYS)[None, :, None]
    nw = WINDOW // SUB
    qpos = np.arange(nw + 1)[:, None, None] * SUB + rs[None]
    wb = (kw <= qpos) & (kw > qpos - WINDOW)
    f = lambda m: jnp.asarray(np.where(m, 0.0, NEG_BIG).astype(np.float32))
    return f(cb), f(db), f(wb)


def _nsa_prompt(qraw, qrot, small, kc, vct, ksa, vst, kw16, vwt, ovt, tables, b, seq):
    nc = kc.shape[1]
    nq = seq // TQ
    cb, db, wb = tables
    tok = lambda width: pl.BlockSpec((TQ, width), lambda i, j: (i * nq + j, 0))
    per_b = lambda width: pl.BlockSpec((1, seq, width), lambda i, j: (i, 0, 0))
    vt_spec = pl.BlockSpec((NSA_KV_HEADS, VT_ROWS, seq), lambda i, j: (0, 0, i))
    return pl.pallas_call(
        _nsa_prompt_kernel,
        grid=(b, nq),
        in_specs=[tok(NSA_Q_W), tok(NSA_Q_W), tok(LANES),
                  pl.BlockSpec((1, nc, LANES), lambda i, j: (i, 0, 0)),
                  pl.BlockSpec((1, NSA_KV_HEADS, VT_ROWS, nc), lambda i, j: (i, 0, 0, 0)),
                  per_b(2 * LANES), vt_spec, per_b(LANES), vt_spec,
                  _const_spec(ovt.shape),
                  pl.BlockSpec((1, nc, TQ), lambda i, j: (j, 0, 0)),
                  _const_spec(db.shape), _const_spec(wb.shape)],
        out_specs=tok(NSA_Q_W),
        out_shape=jax.ShapeDtypeStruct((b * seq, NSA_Q_W), BF16),
        compiler_params=_params("parallel", "parallel"), name="nsa_prompt",
    )(qraw, qrot, small, kc, vct, ksa.reshape(b, seq, 2 * LANES), vst, kw16.reshape(b, seq, LANES), vwt,
      ovt, cb, db, wb)


PAGES_PER_STEP = 32
SEG_PER_PAGE = PAGE_SIZE // CMP_STRIDE
SEG_W = CMP_STRIDE * 2 * NSA_KV_W


def _nsa_sample_kernel(ts, past, pt_ref, *refs):
    pages = refs[:PAGES_PER_STEP]
    (qraw_ref, qrot_ref, small_ref, rows_ref, winnew_ref, winst_ref, pe_ref, w1_ref, w2_ref,
     ovt_ref, rexp_ref, o_ref, seg_ref, ksat_ref, vst_ref, sm_ref, stage_ref, bias_ref, res_ref) = refs[PAGES_PER_STEP:]
    del pt_ref
    pg = pl.program_id(1)
    nseg = res_ref.shape[1]
    nrows = NSA_HEADS * ts
    kt = NSA_KV_HEADS * ts
    half_w = NSA_KV_HEADS * CMP_STRIDE * HEAD_DIM

    @pl.when((pl.program_id(0) == 0) & (pg == 0))
    def _():
        for part in range(2):
            hi, mid, lo = _split3(pe_ref[part])
            w = w1_ref[part]
            bias_ref[part] = _dot(hi, w) + _dot(mid, w) + _dot(lo, w)

    for e in range(PAGES_PER_STEP):
        for part in range(2):
            xt = pages[e][0, 0, part * LANES:(part + 1) * LANES, :].T
            for j in range(PAGE_SIZE // 8):
                start = (j % 2) * (PAGE_SIZE // 2) + j // 2
                stage_ref[e, part, pl.ds(start, 8, stride=SEG_PER_PAGE), :] = xt[8 * j:8 * j + 8, :]
        k0 = pl.multiple_of((pg * PAGES_PER_STEP + e) * PAGE_SIZE, PAGE_SIZE)
        ksat_ref[0:LANES, pl.ds(k0, PAGE_SIZE)] = pages[e][0, 0, 2 * LANES:3 * LANES, :].astype(BF16)
        vst_ref[:, pl.ds(k0, PAGE_SIZE)] = pages[e][0, 0, 3 * LANES:4 * LANES, :].astype(BF16)

        @pl.when(pl.program_id(0) == 0)
        def _():
            kpos = k0 + lax.broadcasted_iota(jnp.int32, (LANES, PAGE_SIZE), 1)
            blk = lax.broadcasted_iota(jnp.int32, (LANES, PAGE_SIZE), 0)
            ksat_ref[LANES:2 * LANES, pl.ds(k0, PAGE_SIZE)] = jnp.where(
                kpos // SEL_BLOCK == blk, 1.0, 0.0).astype(BF16)
    for pair in range(PAGES_PER_STEP // 2):
        r0 = pair * 2 * SEG_PER_PAGE
        for part in range(2):
            for p in range(CMP_STRIDE):
                x = jnp.concatenate([stage_ref[2 * pair + e, part, p * SEG_PER_PAGE:(p + 1) * SEG_PER_PAGE, :]
                                     for e in range(2)], axis=0)
                c0 = part * half_w + p * LANES
                seg_ref[r0:r0 + 2 * SEG_PER_PAGE, c0:c0 + LANES] = x.astype(BF16)
    g_rows = PAGES_PER_STEP * SEG_PER_PAGE
    for part in range(2):
        res_ref[part, pl.ds(pl.multiple_of(pg * g_rows, g_rows), g_rows), :] = _dot(
            seg_ref[:, part * half_w:(part + 1) * half_w], w1_ref[part])

    @pl.when(pg == pl.num_programs(1) - 1)
    def _():
        ns_past = past // SEL_BLOCK
        nc = nseg - 1
        q_raw = qraw_ref[0]
        q_rot = qrot_ref[0]
        q_rot_f = q_rot.astype(F32)
        row = lax.broadcasted_iota(jnp.int32, (nrows, 1), 0)
        t_row = row % ts
        kcvc = []
        for part in range(2):
            res = res_ref[part]
            b8 = bias_ref[part]
            pre = (res[:, 0:LANES] + b8[0:1, 0:LANES]
                   + pltpu.roll(res[:, LANES:2 * LANES] + b8[1:2, LANES:2 * LANES], nseg - 1, 0))
            kcvc.append(_dot(jax.nn.gelu(pre).astype(BF16), w2_ref[part]).astype(BF16))
        s_c = _dot_nt(q_raw, kcvc[0])
        p_c = _softmax_rows(s_c, lax.broadcasted_iota(jnp.int32, (nrows, nseg), 1) < nc)
        o_c = _dot(p_c.astype(BF16), kcvc[1])
        p_sum = p_c[0:kt] + p_c[kt:2 * kt] + p_c[2 * kt:3 * kt] + p_c[3 * kt:4 * kt]
        p_sum = jnp.concatenate([p_sum, jnp.zeros((LANES - kt, nseg), F32)], axis=0)
        nb = ovt_ref.shape[0]
        imp_t = _dot_exact_lhs_nt(ovt_ref[...], p_sum)
        blk_t = lax.broadcasted_iota(jnp.int32, (nb, LANES), 0)
        qpos_t = past + lax.broadcasted_iota(jnp.int32, (nb, LANES), 1) % ts
        cur_t = qpos_t // SEL_BLOCK
        forced_t = (blk_t == 0) | (blk_t == cur_t) | (blk_t == cur_t - 1)
        imp_t = jnp.where(blk_t * SEL_BLOCK <= qpos_t, imp_t + jnp.where(forced_t, FORCE_BONUS, 0.0), -jnp.inf)
        bias_t = _topk_bias_t(imp_t, -(-(past + ts) // SEL_BLOCK), N_SEL)[0:ns_past]
        bias = bias_t.T[0:kt]
        q_sel = jnp.concatenate([q_rot, jnp.concatenate([bias] * NSA_GROUP, axis=0).astype(BF16)], axis=1)

        def attend(s_past, v_past16_t, k_new, v_new, mask_past):
            if mask_past is not None:
                s_past = jnp.where(mask_past, s_past, NEG_BIG)
            s_new = [jnp.where(t_row >= j, jnp.sum(q_rot_f * k_new[j:j + 1, :], axis=-1, keepdims=True), NEG_BIG)
                     for j in range(ts)]
            m = jnp.max(s_past, axis=-1, keepdims=True)
            for s in s_new:
                m = jnp.maximum(m, s)
            p_past = jnp.exp(s_past - m)
            if mask_past is not None:
                p_past = jnp.where(mask_past, p_past, 0.0)
            acc = _dot_nt(p_past.astype(BF16), v_past16_t)
            l = jnp.sum(p_past, axis=-1, keepdims=True)
            for j, s in enumerate(s_new):
                p = jnp.where(t_row >= j, jnp.exp(s - m), 0.0)
                acc = acc + p * v_new[j:j + 1, :]
                l = l + p
            return acc * (1.0 / l)

        new = rows_ref[0]
        o_s = attend(_dot(q_sel, ksat_ref[...]), vst_ref[...], new[:, 2 * NSA_KV_W:3 * NSA_KV_W],
                     new[:, 3 * NSA_KV_W:4 * NSA_KV_W], None)
        wbuf = winst_ref.shape[3]
        wnew = winnew_ref[0]
        i_w = lax.broadcasted_iota(jnp.int32, (nrows, wbuf), 1)
        dist = t_row + wbuf - i_w
        o_w = attend(_dot(q_rot, winst_ref[0, 0, 0:LANES, :].astype(BF16)),
                     winst_ref[0, 0, LANES:2 * LANES, :].astype(BF16),
                     wnew[:, 0:LANES], wnew[:, LANES:2 * LANES],
                     (dist < WINDOW) & (past - wbuf + i_w >= 0))

        sm_ref[...] = jnp.zeros_like(sm_ref)
        sm_ref[0:ts, :] = small_ref[0]
        gexp = _dot_exact_rhs(_sigmoid(sm_ref[...]), rexp_ref[...])[0:kt]
        low = lax.broadcasted_iota(jnp.int32, (kt, LANES), 1) < HEAD_DIM
        for g in range(NSA_GROUP):
            out = jnp.zeros((kt, LANES), F32)
            for br, o_br in enumerate((o_c, o_s, o_w)):
                og = o_br[g * kt:(g + 1) * kt]
                both = jnp.where(low, og, pltpu.roll(og, kt - ts, 0))
                out = out + gexp[:, br * NSA_Q_W + g * LANES:br * NSA_Q_W + (g + 1) * LANES] * both
            o_ref[0, :, g * LANES:(g + 1) * LANES] = out


def _nsa_sample(layer, cache, page_table, qraw_s, qrot_s, small_s, rows_s, winnew_s, win_state,
                pe, w1, w2, ovt, rexp, past):
    b, ts = small_s.shape[:2]
    n_pages = page_table.shape[1]
    assert n_pages % PAGES_PER_STEP == 0 and ts * NSA_KV_HEADS == 8 and past == n_pages * PAGE_SIZE
    nseg = past // CMP_STRIDE
    wbuf = win_state.shape[3]
    kw = 4 * NSA_KV_W

    def page_spec(e):
        return pl.BlockSpec((1, 1, 4 * PAGE_SIZE, LANES),
                            lambda i, g, pt: (layer, pt[i, g * PAGES_PER_STEP + e], 0, 0))
    per_b = lambda shape: pl.BlockSpec((1,) + shape, lambda i, g, pt: (i,) + (0,) * len(shape))
    const = lambda shape: pl.BlockSpec(shape, lambda i, g, pt: (0,) * len(shape))
    grid_spec = pltpu.PrefetchScalarGridSpec(
        num_scalar_prefetch=1,
        grid=(b, n_pages // PAGES_PER_STEP),
        in_specs=[page_spec(e) for e in range(PAGES_PER_STEP)] + [
            per_b((NSA_HEADS * ts, LANES)), per_b((NSA_HEADS * ts, LANES)), per_b((ts, LANES)),
            per_b((ts, kw)), per_b((ts, 2 * NSA_KV_W)),
            pl.BlockSpec((1, 1, 2 * NSA_KV_W, wbuf), lambda i, g, pt: (layer, i, 0, 0)),
            const(pe.shape), const(w1.shape), const(w2.shape), const(ovt.shape), const(rexp.shape)],
        out_specs=per_b((NSA_KV_HEADS * ts, NSA_Q_W)),
        scratch_shapes=[pltpu.VMEM((PAGES_PER_STEP * SEG_PER_PAGE, SEG_W), BF16),
                        pltpu.VMEM((2 * LANES, past), BF16), pltpu.VMEM((LANES, past), BF16),
                        pltpu.VMEM((2 * NSA_KV_HEADS * ts, LANES), F32),
                        pltpu.VMEM((PAGES_PER_STEP, 2, PAGE_SIZE, LANES), F32),
                        pltpu.VMEM((2, 8, 2 * LANES), F32),
                        pltpu.VMEM((2, nseg, 2 * LANES), F32)])
    return pl.pallas_call(
        functools.partial(_nsa_sample_kernel, ts, past),
        grid_spec=grid_spec,
        out_shape=jax.ShapeDtypeStruct((b, NSA_KV_HEADS * ts, NSA_Q_W), F32),
        compiler_params=_params("arbitrary", "arbitrary"), name="nsa_sample",
    )(page_table, *([cache] * PAGES_PER_STEP), qraw_s, qrot_s, small_s, rows_s, winnew_s, win_state,
      pe, w1, w2, ovt, rexp)


DN_C = 128
DN_CHUNKS_PER_STEP = 2
HALO = 8


def _softplus(x):
    return jnp.maximum(x, 0.0) + jnp.log(1.0 + jnp.exp(-jnp.abs(x)))


def _l2n(x):
    return x * lax.rsqrt(jnp.sum(x * x, axis=-1, keepdims=True) + EPS)


def _dn_prep_kernel(n_valid, qkv_ref, halo_ref, prefix_ref, small_ref, cw_ref, alog_ref, dtb_ref,
                    tri_ref, u_ref, w_ref, qd_ref, qk_ref, kdt_ref, gl_ref, xcat_ref, sm_ref):
    n = pl.program_id(1)
    cps = kdt_ref.shape[0]
    rows = cps * DN_C
    if n_valid == DN_C:
        xcat_ref[HALO:HALO + rows, :] = qkv_ref[...]
        small = small_ref[...]

        @pl.when(n == 0)
        def _():
            xcat_ref[0:HALO, :] = prefix_ref[0]

        @pl.when(n > 0)
        def _():
            xcat_ref[0:HALO, :] = halo_ref[...]
    else:
        xcat_ref[...] = jnp.zeros_like(xcat_ref)
        xcat_ref[0:HALO, :] = prefix_ref[0]
        xcat_ref[HALO:HALO + n_valid, :] = qkv_ref[0]
        sm_ref[...] = jnp.zeros_like(sm_ref)
        sm_ref[0:n_valid, :] = small_ref[0]
        small = sm_ref[...]

    conv = jnp.zeros((rows, DN_QKV_W), F32)
    for i in range(CONV_W):
        conv = conv + xcat_ref[HALO - (CONV_W - 1) + i:HALO - (CONV_W - 1) + i + rows, :] * cw_ref[i:i + 1, :]
    conv = _silu(conv)

    ii = lax.broadcasted_iota(jnp.int32, (DN_C, DN_C), 0)
    jj = lax.broadcasted_iota(jnp.int32, (DN_C, DN_C), 1)
    g_rows = -jnp.exp(alog_ref[...]) * _softplus(small + dtb_ref[...])
    beta_rows = _sigmoid(small)
    chains = [(c, h) for c in range(cps) for h in range(DN_HEADS)]
    gcol, beta, decay = {}, {}, {}
    for c in range(cps):
        g_all = g_rows[c * DN_C:(c + 1) * DN_C]
        beta_all = beta_rows[c * DN_C:(c + 1) * DN_C]
        if n_valid != DN_C:
            g_all = jnp.where(ii < n_valid, g_all, 0.0)
            beta_all = jnp.where(ii < n_valid, beta_all, 0.0)
        gc_all = _dot_exact_lhs(tri_ref[...], g_all)
        gr_all = gc_all.T
        for h in range(DN_HEADS):
            gcol[c, h] = gc_all[:, SM_A + h:SM_A + h + 1]
            beta[c, h] = beta_all[:, SM_B + h:SM_B + h + 1]
            decay[c, h] = jnp.where(ii >= jj, jnp.exp(gcol[c, h] - gr_all[SM_A + h:SM_A + h + 1, :]), 0.0)
    tok = lambda c: slice(c * DN_C, (c + 1) * DN_C)
    q = {(c, h): _l2n(conv[tok(c), h * DN_DK:(h + 1) * DN_DK]) * DN_DK ** -0.5 for c, h in chains}
    k = {(c, h): _l2n(conv[tok(c), DN_QK_W + h * DN_DK:DN_QK_W + (h + 1) * DN_DK]) for c, h in chains}
    v = {(c, h): conv[tok(c), 2 * DN_QK_W + h * DN_DV:2 * DN_QK_W + (h + 1) * DN_DV] for c, h in chains}
    kb = {i: k[i] * beta[i] for i in chains}
    k16 = {i: k[i].astype(BF16) for i in chains}
    eg = {i: jnp.exp(gcol[i]) for i in chains}
    a = {i: jnp.where(ii > jj, _dot_nt(kb[i].astype(BF16), k16[i]) * decay[i], 0.0) for i in chains}
    t = {i: jnp.where(ii == jj, 1.0, 0.0) - a[i] for i in chains}
    for _ in range(max(1, math.ceil(math.log2(n_valid))) - 1):
        a = {i: _dot3(a[i], a[i]) for i in chains}
        t = {i: t[i] + _dot3(t[i], a[i]) for i in chains}
    x = {i: _dot3(t[i], jnp.concatenate([v[i] * beta[i], kb[i] * eg[i]], axis=1)) for i in chains}
    for c, h in chains:
        i = (c, h)
        u_ref[tok(c), h * DN_DV:(h + 1) * DN_DV] = x[i][:, 0:DN_DV]
        w_ref[tok(c), h * DN_DK:(h + 1) * DN_DK] = x[i][:, DN_DV:DN_DV + DN_DK].astype(BF16)
        qk = jnp.where(ii >= jj, _dot_nt(q[i].astype(BF16), k16[i]) * decay[i], 0.0)
        qk_ref[tok(c), h * DN_C:(h + 1) * DN_C] = qk.astype(BF16)
        qd_ref[tok(c), h * DN_DK:(h + 1) * DN_DK] = (q[i] * eg[i]).astype(BF16)
        glast = gcol[i][DN_C - 1:DN_C, :]
        kdt_ref[c, h * DN_DK:(h + 1) * DN_DK, :] = (k[i] * jnp.exp(glast - gcol[i])).T.astype(BF16)
        gl_ref[c, h:h + 1, :] = jnp.broadcast_to(jnp.exp(glast), (1, LANES))
    for c in range(cps):
        gl_ref[c, DN_HEADS:, :] = jnp.zeros((HALO - DN_HEADS, LANES), F32)


def _dn_prep(qkv, prefix8, small, cw, alog_row, dtb_row, tri, b, nch, n_valid):
    rows = b * nch * DN_C
    cps = DN_CHUNKS_PER_STEP if (n_valid == DN_C and nch % DN_CHUNKS_PER_STEP == 0) else 1
    steps = nch // cps
    blk = cps * DN_C
    if n_valid == DN_C:
        qkv_spec = pl.BlockSpec((blk, DN_QKV_W), lambda i, n: (i * steps + n, 0))
        halo_spec = pl.BlockSpec((HALO, DN_QKV_W),
                                 lambda i, n: (jnp.maximum((i * steps + n) * (blk // HALO) - 1, 0), 0))
        small_spec = pl.BlockSpec((blk, LANES), lambda i, n: (i * steps + n, 0))
        qkv_in, halo_in, small_in = qkv, qkv, small
    else:
        qkv_in = qkv.reshape(b, n_valid, DN_QKV_W)
        small_in = small.reshape(b, n_valid, LANES)
        halo_in = prefix8
        qkv_spec = pl.BlockSpec((1, n_valid, DN_QKV_W), lambda i, n: (i, 0, 0))
        halo_spec = pl.BlockSpec((1, HALO, DN_QKV_W), lambda i, n: (i, 0, 0))
        small_spec = pl.BlockSpec((1, n_valid, LANES), lambda i, n: (i, 0, 0))
    tok = lambda width: pl.BlockSpec((blk, width), lambda i, n: (i * steps + n, 0))
    return pl.pallas_call(
        functools.partial(_dn_prep_kernel, n_valid),
        grid=(b, steps),
        in_specs=[qkv_spec, halo_spec, pl.BlockSpec((1, HALO, DN_QKV_W), lambda i, n: (i, 0, 0)), small_spec,
                  _const_spec((CONV_W, DN_QKV_W)), _const_spec((1, LANES)), _const_spec((1, LANES)),
                  _const_spec((DN_C, DN_C))],
        out_specs=[tok(DN_W), tok(DN_QK_W), tok(DN_QK_W), tok(DN_HEADS * DN_C),
                   pl.BlockSpec((cps, DN_QK_W, DN_C), lambda i, n: (i * steps + n, 0, 0)),
                   pl.BlockSpec((cps, HALO, LANES), lambda i, n: (i * steps + n, 0, 0))],
        out_shape=[jax.ShapeDtypeStruct((rows, DN_W), F32), jax.ShapeDtypeStruct((rows, DN_QK_W), BF16),
                   jax.ShapeDtypeStruct((rows, DN_QK_W), BF16), jax.ShapeDtypeStruct((rows, DN_HEADS * DN_C), BF16),
                   jax.ShapeDtypeStruct((b * nch, DN_QK_W, DN_C), BF16),
                   jax.ShapeDtypeStruct((b * nch, HALO, LANES), F32)],
        scratch_shapes=[pltpu.VMEM((HALO + blk, DN_QKV_W), F32), pltpu.VMEM((DN_C, LANES), F32)],
        compiler_params=_params("parallel", "parallel"), name="dn_prep",
    )(qkv_in, halo_in, prefix8, small_in, cw, alog_row, dtb_row, tri)


def _dn_scan_kernel(n_valid, u_ref, w_ref, qd_ref, qk_ref, kdt_ref, gl_ref, zg_ref, s0_ref, nw_ref,
                    o_ref, s_ref):
    @pl.when(pl.program_id(1) == 0)
    def _():
        s_ref[...] = s0_ref[...]

    def block_diag(a0, a1):
        z = jnp.zeros_like(a0)
        return jnp.concatenate([jnp.concatenate([a0, z], axis=1), jnp.concatenate([z, a1], axis=1)], axis=0)

    for b in range(u_ref.shape[0]):
        for pair in range(DN_HEADS // 2):
            heads = (2 * pair, 2 * pair + 1)
            cs2 = slice(2 * pair * DN_DK, (2 * pair + 2) * DN_DK)
            s = [s_ref[b, h] for h in heads]
            s_bd = block_diag(*[x.astype(BF16) for x in s])
            v_new = u_ref[b, :, cs2] - _dot(w_ref[b, :, cs2], s_bd)
            v16 = v_new.astype(BF16)
            v_bd = block_diag(v16[:, 0:DN_DV], v16[:, DN_DV:2 * DN_DV])
            o2 = _dot(qd_ref[b, :, cs2], s_bd) + _dot(qk_ref[b, :, cs2], v_bd)
            for e, h in enumerate(heads):
                cs = slice(h * DN_DK, (h + 1) * DN_DK)
                s_ref[b, h] = s[e] * gl_ref[b, h:h + 1, :] + _dot(kdt_ref[b, cs, :], v16[:, e * DN_DV:(e + 1) * DN_DV])
                o = o2[:, e * DN_DV:(e + 1) * DN_DV]
                on = o * lax.rsqrt(jnp.mean(o * o, axis=-1, keepdims=True) + EPS) * nw_ref[...]
                o_ref[b, :, cs] = (on[0:n_valid] * _silu(zg_ref[b, :, cs])).astype(BF16)


def _dn_scan(u, w, qd, qk, kdt, gl, zg, s0, nw, b, nch, bb, n_valid):
    tok = lambda width: pl.BlockSpec((bb, DN_C, width), lambda i, n: (i, n, 0))
    r3 = lambda a: a.reshape(b, nch * DN_C, a.shape[-1])
    st = pl.BlockSpec((bb, DN_HEADS, DN_DK, DN_DV), lambda i, n: (i, 0, 0, 0))
    return pl.pallas_call(
        functools.partial(_dn_scan_kernel, n_valid),
        grid=(b // bb, nch),
        in_specs=[tok(DN_W), tok(DN_QK_W), tok(DN_QK_W), tok(DN_HEADS * DN_C),
                  pl.BlockSpec((bb, DN_QK_W, DN_C), lambda i, n: (i, n, 0)),
                  pl.BlockSpec((bb, HALO, LANES), lambda i, n: (i, n, 0)),
                  pl.BlockSpec((bb, n_valid, DN_W), lambda i, n: (i, n, 0)), st, _const_spec((1, DN_DV))],
        out_specs=[pl.BlockSpec((bb, n_valid, DN_W), lambda i, n: (i, n, 0)), st],
        out_shape=[jax.ShapeDtypeStruct((b, nch * n_valid, DN_W), BF16),
                   jax.ShapeDtypeStruct((b, DN_HEADS, DN_DK, DN_DV), F32)],
        compiler_params=_params("parallel", "arbitrary"), name="dn_scan",
    )(r3(u), r3(w), r3(qd), r3(qk), kdt.reshape(b, nch * DN_QK_W, DN_C), gl.reshape(b, nch * HALO, LANES),
      zg.reshape(b, nch * n_valid, DN_W), s0, nw)


IN_WIDTHS = (NSA_Q_W, 6 * NSA_KV_W, 3 * NSA_HEADS, DN_QKV_W, DN_HEADS, DN_HEADS, DN_W, 2 * D_MODEL)


def _pack_mix_weight(w_in):
    offs = np.cumsum(IN_WIDTHS)[:-1].tolist()
    q, kv, gate, qkv, a, b, zg, merge = jnp.split(w_in, offs, axis=-1)
    lead = w_in.shape[:-1]
    small = jnp.concatenate([gate, a, b, jnp.zeros(lead + (LANES - SM_B - DN_HEADS,), w_in.dtype)], axis=-1)
    return jnp.concatenate([q, kv, small, qkv, zg, merge], axis=-1).astype(BF16)


def _rope_tables(pos):
    half = HEAD_DIM // 2
    freq = ROPE_THETA ** (-jnp.arange(half, dtype=F32) / half)
    ang = pos.astype(F32)[:, None] * freq[None, :]
    cos, sin = jnp.cos(ang), jnp.sin(ang)
    cos = jnp.concatenate([cos, cos] * (LANES // HEAD_DIM), axis=1)
    sin = jnp.concatenate([-sin, sin] * (LANES // HEAD_DIM), axis=1)
    return cos, sin


def _gate_expand_matrix():
    r = np.zeros((LANES, 3 * NSA_Q_W), np.float32)
    for h in range(NSA_HEADS):
        k, g = divmod(h, NSA_GROUP)
        for br in range(3):
            c0 = br * NSA_Q_W + g * LANES + k * HEAD_DIM
            r[SM_GATE + h * 3 + br, c0:c0 + HEAD_DIM] = 1.0
    return jnp.asarray(r, BF16)


def _overlap_t(nc_pad, nc, ns, rows):
    i = np.arange(nc_pad)[None, :]
    j = np.arange(rows)[:, None]
    ov = (i * CMP_STRIDE < (j + 1) * SEL_BLOCK) & (i * CMP_STRIDE + CMP_BLOCK > j * SEL_BLOCK)
    ov = ov & (i < nc) & (j < ns)
    return jnp.asarray(ov.astype(np.float32), BF16)


def _pack_compress_sample(pos_emb, w1, w2):
    eye = jnp.eye(NSA_KV_HEADS, dtype=w1.dtype)
    w = w1.reshape(2, 2, CMP_STRIDE, HEAD_DIM, HEAD_DIM)
    w = jnp.einsum('kapde,hg->kphdage', w, eye)
    w = w.reshape(2, NSA_KV_HEADS * CMP_STRIDE * HEAD_DIM, 2 * LANES).astype(BF16)
    pe = pos_emb.reshape(2, 2, CMP_STRIDE, 1, HEAD_DIM)
    pe = jnp.concatenate([pe] * NSA_KV_HEADS, axis=3).reshape(2, 2, -1)
    pe = jnp.concatenate([pe, jnp.zeros((2, 6, pe.shape[2]), pe.dtype)], axis=1)
    w2d = jnp.einsum('kde,hg->khdge', w2, eye).reshape(2, LANES, LANES).astype(BF16)
    return pe, w, w2d


def _sample_rows(q_pad, b, ts):
    q = q_pad.reshape(b, ts, NSA_KV_HEADS, NSA_GROUP, 1, HEAD_DIM)
    eye = jnp.eye(NSA_KV_HEADS, dtype=q.dtype).reshape(NSA_KV_HEADS, 1, NSA_KV_HEADS, 1)
    q = (q * eye).transpose(0, 3, 2, 1, 4, 5)
    return q.reshape(b, NSA_HEADS * ts, LANES)


def _dn_lane_row(vals):
    return jnp.zeros((1, LANES), F32).at[0, SM_A:SM_A + DN_HEADS].set(vals.astype(F32))


def _deltanet(qkv, small, zg, prefix, s0, dn_params, tri, b, seq_tokens):
    conv_w, a_log, dt_bias, norm_w = dn_params
    if seq_tokens % DN_C == 0:
        nch, n_valid, bb = seq_tokens // DN_C, DN_C, min(b, 4)
    else:
        nch, n_valid, bb = 1, seq_tokens, 4
    prefix8 = jnp.concatenate([jnp.zeros((b, HALO - (CONV_W - 1), DN_QKV_W), F32), prefix], axis=1)
    u, w, qd, qk, kdt, gl = _dn_prep(qkv, prefix8, small, conv_w, _dn_lane_row(a_log), _dn_lane_row(dt_bias),
                                     tri, b, nch, n_valid)
    ob, s_out = _dn_scan(u, w, qd, qk, kdt, gl, zg, s0, norm_w[None], b, nch, bb, n_valid)
    return ob.reshape(b * seq_tokens, DN_W), s_out


def _head_perm_rows(wa):
    d = wa.shape[1]
    return wa.reshape(NSA_KV_HEADS, NSA_GROUP, HEAD_DIM, d).transpose(1, 0, 2, 3).reshape(NSA_Q_W, d)


def _pack_compress(pos_emb, w1, w2):
    eye = jnp.eye(NSA_KV_HEADS, dtype=w1.dtype)
    pe = jnp.concatenate([pos_emb] * NSA_KV_HEADS, axis=2)
    w1p = w1.reshape(2, CMP_BLOCK, HEAD_DIM, HEAD_DIM)
    w1d = jnp.einsum('kpde,hg->kphdge', w1p, eye).reshape(2, CMP_BLOCK, LANES, LANES)
    w2d = jnp.einsum('kde,hg->khdge', w2, eye).reshape(2, LANES, LANES)
    return pe, w1d.astype(BF16), w2d.astype(BF16)


def kernel(x_prompt, x_sample, cache_nsa_kv, state_nsa_win, state_dn_S, state_dn_conv, page_table,
           ffn1_norm, ffn1_w_in, ffn1_w_out, mix_norm, w_in, nsa_cmp_pos, nsa_cmp_w1, nsa_cmp_w2,
           dn_conv_w, dn_A_log, dn_dt_bias, dn_out_norm, w_branch_a, w_branch_b, w_out,
           ffn2_norm, ffn2_w_in, ffn2_w_out, final_norm):
    b, seq = x_prompt.shape[:2]
    bs, ts = x_sample.shape[:2]
    depth = w_in.shape[0]
    n_phys = cache_nsa_kv.shape[1]
    past = page_table.shape[1] * PAGE_SIZE
    wbuf = state_nsa_win.shape[2]
    kvw = 4 * NSA_KV_W

    xp = x_prompt.reshape(b * seq, D_MODEL)
    xs = x_sample.reshape(bs * ts, D_MODEL)
    cos_p, sin_p = _rope_tables(jnp.arange(seq, dtype=jnp.int32))
    cos_s, sin_s = _rope_tables(jnp.tile(past + jnp.arange(ts, dtype=jnp.int32), bs))
    nc = (seq - CMP_BLOCK) // CMP_STRIDE + 1
    ovt_p = _overlap_t(seq // CMP_STRIDE, nc, seq // SEL_BLOCK, LANES)
    tables_p = _attn_bias_tables(seq)
    ns_s = -(-(past + ts) // SEL_BLOCK)
    ovt_s = _overlap_t(past // CMP_STRIDE, past // CMP_STRIDE - 1, ns_s, -(-ns_s // 16) * 16)
    rexp = _gate_expand_matrix()
    tri = jnp.asarray(np.tril(np.ones((DN_C, DN_C), np.float32)), BF16)
    cache = cache_nsa_kv.transpose(0, 1, 3, 4, 5, 2).reshape(depth, n_phys, kvw, PAGE_SIZE)
    win_state = state_nsa_win.transpose(0, 1, 3, 4, 5, 2).reshape(depth, bs, 2 * NSA_KV_W, wbuf)
    tm_p, tm_s = 512, bs * ts

    outs = [[] for _ in range(8)]
    rows_all = None
    w_mix_all = _pack_mix_weight(w_in)
    for l in range(depth):
        f1 = (ffn1_norm[l][None], ffn1_w_in[l].astype(BF16), ffn1_w_out[l].astype(BF16))
        f2 = (ffn2_norm[l][None], ffn2_w_in[l].astype(BF16), ffn2_w_out[l].astype(BF16))
        w_mix = w_mix_all[l]
        mrg = (_head_perm_rows(w_branch_a[l]).astype(BF16), w_branch_b[l].astype(BF16), w_out[l].astype(BF16))
        dn_params = (dn_conv_w[l], dn_A_log[l], dn_dt_bias[l], dn_out_norm[l])
        cmp_l = (nsa_cmp_pos[l], nsa_cmp_w1[l], nsa_cmp_w2[l])

        xp = _ffn(xp, *f1, tm_p)
        (qraw, qrot, small, qkv, zg, mg, rows_all, win_t, kcmp, vcmp, ksa, kw16, vst, vwt) = _mix(
            xp, mix_norm[l][None], w_mix, cos_p, sin_p, 256, seq, True, l, depth, rows_all)
        kc, vct = _compress_prompt(kcmp, vcmp, *_pack_compress(*cmp_l), b, seq)
        oa = _nsa_prompt(qraw, qrot, small, kc, vct, ksa, vst, kw16, vwt, ovt_p, tables_p, b, seq)
        ob, s_fin = _deltanet(qkv, small, zg, jnp.zeros((b, CONV_W - 1, DN_QKV_W), F32),
                              jnp.zeros((b, DN_HEADS, DN_DK, DN_DV), F32), dn_params, tri, b, seq)
        xp = _merge(xp, oa, ob, mg, *mrg, tm_p)
        xp = _ffn(xp, *f2, tm_p)
        outs[1].append(win_t[:, :, seq - min(WINDOW, seq):])
        outs[2].append(s_fin)
        outs[3].append(qkv.reshape(b, seq, DN_QKV_W)[:, seq - (CONV_W - 1):])

        xs = _ffn(xs, *f1, tm_s)
        (qraw, qrot, small, qkv, zg, mg, rows, win) = _mix(
            xs, mix_norm[l][None], w_mix, cos_s, sin_s, tm_s, tm_s, False)
        oa = _nsa_sample(l, cache, page_table, _sample_rows(qraw, bs, ts), _sample_rows(qrot, bs, ts),
                         small.reshape(bs, ts, LANES), rows.reshape(bs, ts, kvw), win.reshape(bs, ts, 2 * NSA_KV_W),
                         win_state, *_pack_compress_sample(*cmp_l), ovt_s, rexp, past)
        oa = oa[:, :ts].reshape(bs * ts, NSA_Q_W).astype(BF16)
        ob, s_fin = _deltanet(qkv, small, zg, state_dn_conv[l], state_dn_S[l], dn_params, tri, bs, ts)
        xs = _merge(xs, oa, ob, mg, *mrg, tm_s)
        xs = _ffn(xs, *f2, tm_s)
        outs[4].append(rows.reshape(bs, ts, 4, NSA_KV_HEADS, HEAD_DIM))
        outs[5].append(win.reshape(bs, ts, 2, NSA_KV_HEADS, HEAD_DIM))
        outs[6].append(s_fin)
        outs[7].append(qkv.reshape(bs, ts, DN_QKV_W))

    y_prompt = _final_norm(xp, final_norm[None], tm_p).reshape(b, seq, D_MODEL)
    y_sample = _final_norm(xs, final_norm[None], tm_s).reshape(bs, ts, D_MODEL)
    res = [rows_all] + [jnp.stack(o) for o in outs[1:]]
    res[5] = jnp.concatenate([state_nsa_win, res[5]], axis=2)[:, :, ts:]
    res[7] = jnp.concatenate([state_dn_conv, res[7]], axis=2)[:, :, ts:]
    for i, parts in ((0, 4), (1, 2)):
        r = res[i]
        res[i] = r.reshape(depth, b, parts, NSA_KV_HEADS, HEAD_DIM, r.shape[-1]).transpose(0, 1, 5, 2, 3, 4)
    return (y_prompt, y_sample) + tuple(res)
```

```python
import functools
import math

import jax
import jax.numpy as jnp
import numpy as np
from jax import lax
from jax.experimental import pallas as pl
from jax.experimental.pallas import tpu as pltpu

F32 = jnp.float32
BF16 = jnp.bfloat16

D_MODEL = 1024
DEPTH = 4
PAGE_SIZE = 128
NSA_HEADS = 8
NSA_KV_HEADS = 2
HEAD_DIM = 64
NSA_GROUP = NSA_HEADS // NSA_KV_HEADS
NSA_Q_W = NSA_HEADS * HEAD_DIM
NSA_KV_W = NSA_KV_HEADS * HEAD_DIM
CMP_BLOCK = 32
CMP_STRIDE = 16
SEL_BLOCK = 64
N_SEL = 16
WINDOW = 512
FORCE_BONUS = 1e6
ROPE_THETA = 10000.0
DN_HEADS = 4
DN_DK = 128
DN_DV = 128
DN_QK_W = DN_HEADS * DN_DK
DN_W = DN_HEADS * DN_DV
DN_QKV_W = 2 * DN_QK_W + DN_W
CONV_W = 4
D_FF = 2816
EPS = 1e-6

LANES = 128
VT_ROWS = HEAD_DIM + 16
VMEM_LIMIT = 56 * 1024 * 1024
NEG_BIG = -1e30

C_Q = 0
C_KV = C_Q + NSA_Q_W
C_SMALL = C_KV + 6 * NSA_KV_W
C_QKV = C_SMALL + LANES
C_ZG = C_QKV + DN_QKV_W
C_MERGE = C_ZG + DN_W
MIX_W = C_MERGE + 2 * D_MODEL
SM_GATE = 0
SM_A = 3 * NSA_HEADS
SM_B = SM_A + DN_HEADS


def _dot(a, b):
    return jnp.dot(a, b, preferred_element_type=F32)


def _dot_nt(a, b):
    return lax.dot_general(a, b, (((1,), (1,)), ((), ())), preferred_element_type=F32)


def _split3(a):
    hi = a.astype(BF16)
    r1 = a - hi.astype(F32)
    mid = r1.astype(BF16)
    lo = (r1 - mid.astype(F32)).astype(BF16)
    return hi, mid, lo


def _dot_exact_rhs(a, b01):
    hi, mid, lo = _split3(a)
    return _dot(hi, b01) + _dot(mid, b01) + _dot(lo, b01)


def _dot_exact_lhs(a01, b):
    hi, mid, lo = _split3(b)
    return _dot(a01, hi) + _dot(a01, mid) + _dot(a01, lo)


def _dot_exact_lhs_nt(a01, b):
    hi, mid, lo = _split3(b)
    return _dot_nt(a01, hi) + _dot_nt(a01, mid) + _dot_nt(a01, lo)


def _dot3(a, b):
    ah = a.astype(BF16)
    al = (a - ah.astype(F32)).astype(BF16)
    bh = b.astype(BF16)
    bl = (b - bh.astype(F32)).astype(BF16)
    return _dot(ah, bh) + (_dot(ah, bl) + _dot(al, bh))


def _rms(x, w):
    return x * lax.rsqrt(jnp.mean(x * x, axis=-1, keepdims=True) + EPS) * w


def _sigmoid(x):
    return 1.0 / (1.0 + jnp.exp(-x))


def _silu(x):
    return x * _sigmoid(x)


def _params(*sem):
    return pltpu.CompilerParams(dimension_semantics=sem, vmem_limit_bytes=VMEM_LIMIT)


def _const_spec(shape):
    nd = len(shape)
    return pl.BlockSpec(shape, lambda *_: (0,) * nd)


FF_TILE = 256


def _ffn_kernel(x_ref, nw_ref, wi_ref, wo_ref, o_ref, act_ref):
    x = x_ref[...]
    hb = _rms(x, nw_ref[...]).astype(BF16)
    for j in range(D_FF // FF_TILE):
        g = _dot(hb, wi_ref[0, :, j * FF_TILE:(j + 1) * FF_TILE])
        u = _dot(hb, wi_ref[0, :, D_FF + j * FF_TILE:D_FF + (j + 1) * FF_TILE])
        act_ref[:, j * FF_TILE:(j + 1) * FF_TILE] = (_silu(g) * u).astype(BF16)
    o_ref[...] = x + 0.5 * _dot(act_ref[...], wo_ref[0])


def _ffn(x, nw, wi, wo, tm, layer):
    m = x.shape[0]
    return pl.pallas_call(
        _ffn_kernel,
        grid=(m // tm,),
        in_specs=[pl.BlockSpec((tm, D_MODEL), lambda i: (i, 0)),
                  _const_spec((1, D_MODEL)),
                  pl.BlockSpec((1, D_MODEL, 2 * D_FF), lambda i: (layer, 0, 0)),
                  pl.BlockSpec((1, D_FF, D_MODEL), lambda i: (layer, 0, 0))],
        out_specs=pl.BlockSpec((tm, D_MODEL), lambda i: (i, 0)),
        out_shape=jax.ShapeDtypeStruct((m, D_MODEL), F32),
        scratch_shapes=[pltpu.VMEM((tm, D_FF), BF16)],
        compiler_params=_params("parallel"), name="half_ffn",
    )(x, nw, wi, wo)


def _rope128(x, cos, sin):
    lane = lax.broadcasted_iota(jnp.int32, x.shape, 1)
    first = (lane % HEAD_DIM) < (HEAD_DIM // 2)
    swapped = jnp.where(first, pltpu.roll(x, LANES - HEAD_DIM // 2, 1), pltpu.roll(x, HEAD_DIM // 2, 1))
    return x * cos + swapped * sin


def _mix_kernel(seq_tiles, prompt, n_passthrough, x_ref, nw_ref, w_ref, cos_ref, sin_ref, *refs):
    qraw_ref, qrot_ref, small_ref, qkv_ref, zg_ref, merge_ref, *group_refs = refs[n_passthrough:]
    tm = x_ref.shape[0]
    hb = _rms(x_ref[...], nw_ref[...]).astype(BF16)
    cos = cos_ref[...]
    sin = sin_ref[...]
    scale = HEAD_DIM ** -0.5 * (math.log2(math.e) if prompt else 1.0)
    z = _dot(hb, w_ref[0, :, C_Q:C_QKV])
    for j in range(NSA_Q_W // LANES):
        q = z[:, C_Q + j * LANES:C_Q + (j + 1) * LANES]
        qraw_ref[:, j * LANES:(j + 1) * LANES] = (q * scale).astype(BF16)
        qrot_ref[:, j * LANES:(j + 1) * LANES] = (_rope128(q, cos, sin) * scale).astype(BF16)
    kv = z[:, C_KV:C_SMALL]
    small_ref[...] = z[:, C_SMALL:C_QKV]
    k_cmp = kv[:, 0:LANES]
    v_cmp = kv[:, LANES:2 * LANES]
    k_sel = _rope128(kv[:, 2 * LANES:3 * LANES], cos, sin)
    v_sel = kv[:, 3 * LANES:4 * LANES]
    k_win = _rope128(kv[:, 4 * LANES:5 * LANES], cos, sin)
    v_win = kv[:, 5 * LANES:6 * LANES]
    if prompt:
        rows_t_ref, win_t_ref, kcmp_ref, vcmp_ref, ksa_ref, kw16_ref, vst_ref, vwt_ref = group_refs
        ones = jnp.ones((VT_ROWS - HEAD_DIM, tm), F32)
        for part, a in enumerate((k_cmp, v_cmp, k_sel, v_sel)):
            rows_t_ref[0, 0, part * LANES:(part + 1) * LANES, :] = a.T
        for part, a in enumerate((k_win, v_win)):
            win_t_ref[0, part * LANES:(part + 1) * LANES, :] = a.T
        for v, vt_ref in ((v_sel, vst_ref), (v_win, vwt_ref)):
            vt = v.T
            for k in range(NSA_KV_HEADS):
                vt_ref[k] = jnp.concatenate([vt[k * HEAD_DIM:(k + 1) * HEAD_DIM], ones], axis=0).astype(BF16)
        kcmp_ref[...] = k_cmp
        vcmp_ref[...] = v_cmp
        pos = (pl.program_id(0) % seq_tiles) * tm + lax.broadcasted_iota(jnp.int32, (tm, LANES), 0)
        lane = lax.broadcasted_iota(jnp.int32, (tm, LANES), 1)
        ksa_ref[:, 0:LANES] = k_sel.astype(BF16)
        ksa_ref[:, LANES:2 * LANES] = jnp.where(pos // SEL_BLOCK == lane, 1.0, 0.0).astype(BF16)
        kw16_ref[...] = k_win.astype(BF16)
    else:
        rows_ref, win_ref = group_refs
        for part, a in enumerate((k_cmp, v_cmp, k_sel, v_sel)):
            rows_ref[:, part * LANES:(part + 1) * LANES] = a
        win_ref[:, 0:LANES] = k_win
        win_ref[:, LANES:2 * LANES] = v_win
    qkv_ref[...] = _dot(hb, w_ref[0, :, C_QKV:C_QKV + DN_QKV_W])
    zg_ref[...] = _dot(hb, w_ref[0, :, C_ZG:C_ZG + DN_W])
    merge_ref[...] = _dot(hb, w_ref[0, :, C_MERGE:C_MERGE + 2 * D_MODEL])


def _mix(x, nw, w, cos, sin, tm, seq_len, prompt, layer=0, depth=1, rows_all=None):
    m = x.shape[0]
    seq_tiles = max(seq_len // tm, 1)
    row = lambda width: pl.BlockSpec((tm, width), lambda i: (i, 0))
    tab = pl.BlockSpec((tm, LANES), lambda i: (i % seq_tiles, 0))
    widths = [(NSA_Q_W, BF16), (NSA_Q_W, BF16), (LANES, F32), (DN_QKV_W, F32), (DN_W, F32), (2 * D_MODEL, F32)]
    inputs = [x, nw, w, cos, sin]
    in_specs = [row(D_MODEL), _const_spec((1, D_MODEL)),
                pl.BlockSpec((1, D_MODEL, MIX_W), lambda i: (layer, 0, 0)), tab, tab]
    aliases = {}
    if prompt:
        nb = m // seq_len
        vt_spec = pl.BlockSpec((NSA_KV_HEADS, VT_ROWS, tm), lambda i: (0, 0, i))
        vt_shape = jax.ShapeDtypeStruct((NSA_KV_HEADS, VT_ROWS, m), BF16)
        g_widths = [(LANES, F32), (LANES, F32), (2 * LANES, BF16), (LANES, BF16)]
        g_specs = ([pl.BlockSpec((1, 1, 4 * NSA_KV_W, tm), lambda i: (layer, i // seq_tiles, 0, i % seq_tiles)),
                    pl.BlockSpec((1, 2 * NSA_KV_W, tm), lambda i: (i // seq_tiles, 0, i % seq_tiles))]
                   + [row(wd) for wd, _ in g_widths] + [vt_spec, vt_spec])
        g_shapes = ([jax.ShapeDtypeStruct((depth, nb, 4 * NSA_KV_W, seq_len), F32),
                     jax.ShapeDtypeStruct((nb, 2 * NSA_KV_W, seq_len), F32)]
                    + [jax.ShapeDtypeStruct((m, wd), dt) for wd, dt in g_widths] + [vt_shape, vt_shape])
        if rows_all is not None:
            aliases = {len(inputs): len(widths)}
            inputs.append(rows_all)
            in_specs.append(pl.BlockSpec(memory_space=pl.ANY))
    else:
        g_widths = [(4 * NSA_KV_W, F32), (2 * NSA_KV_W, F32)]
        g_specs = [row(wd) for wd, _ in g_widths]
        g_shapes = [jax.ShapeDtypeStruct((m, wd), dt) for wd, dt in g_widths]
    return pl.pallas_call(
        functools.partial(_mix_kernel, seq_tiles, prompt, len(aliases)),
        grid=(m // tm,),
        in_specs=in_specs,
        out_specs=[row(wd) for wd, _ in widths] + g_specs,
        out_shape=[jax.ShapeDtypeStruct((m, wd), dt) for wd, dt in widths] + g_shapes,
        input_output_aliases=aliases,
        compiler_params=_params("parallel"), name="mix_in",
    )(*inputs)


def _merge_kernel(x_ref, oa_ref, ob_ref, mg_ref, wa_ref, wb_ref, wo_ref, o_ref):
    mg = mg_ref[...]
    ya = _dot(oa_ref[...], wa_ref[...])
    yb = _dot(ob_ref[...], wb_ref[...])
    y = _sigmoid(mg[:, :D_MODEL]) * ya + _sigmoid(mg[:, D_MODEL:]) * yb
    o_ref[...] = x_ref[...] + _dot(y.astype(BF16), wo_ref[...])


def _merge(x, oa, ob, mg, wa, wb, wo, tm):
    m = x.shape[0]
    row = lambda width: pl.BlockSpec((tm, width), lambda i: (i, 0))
    return pl.pallas_call(
        _merge_kernel,
        grid=(m // tm,),
        in_specs=[row(D_MODEL), row(NSA_Q_W), row(DN_W), row(2 * D_MODEL),
                  _const_spec((NSA_Q_W, D_MODEL)), _const_spec((DN_W, D_MODEL)),
                  _const_spec((D_MODEL, D_MODEL))],
        out_specs=row(D_MODEL),
        out_shape=jax.ShapeDtypeStruct((m, D_MODEL), F32),
        compiler_params=_params("parallel"), name="branch_merge",
    )(x, oa, ob, mg, wa, wb, wo)


def _norm_kernel(x_ref, w_ref, o_ref):
    o_ref[...] = _rms(x_ref[...], w_ref[...])


def _final_norm(x, w, tm):
    m = x.shape[0]
    row = pl.BlockSpec((tm, D_MODEL), lambda i: (i, 0))
    return pl.pallas_call(
        _norm_kernel, grid=(m // tm,), in_specs=[row, _const_spec((1, D_MODEL))], out_specs=row,
        out_shape=jax.ShapeDtypeStruct((m, D_MODEL), F32), compiler_params=_params("parallel"),
    )(x, w)


def _compress_kernel(kcmp_ref, vcmp_ref, pe_ref, w1_ref, w2_ref, kc_ref, vct_ref):
    ns = kc_ref.shape[1]
    res = []
    for part, rows_ref in enumerate((kcmp_ref, vcmp_ref)):
        sa = jnp.zeros((ns, LANES), F32)
        sb = jnp.zeros((ns, LANES), F32)
        for p in range(CMP_STRIDE):
            x = rows_ref[pl.ds(p, ns, stride=CMP_STRIDE), :]
            sa = sa + _dot((x + pe_ref[part, p:p + 1, :]).astype(BF16), w1_ref[part, p])
            sb = sb + _dot((x + pe_ref[part, CMP_STRIDE + p:CMP_STRIDE + p + 1, :]).astype(BF16),
                           w1_ref[part, CMP_STRIDE + p])
        pre = sa + pltpu.roll(sb, ns - 1, 0)
        res.append(_dot(jax.nn.gelu(pre).astype(BF16), w2_ref[part]))
    kc_ref[0] = res[0].astype(BF16)
    vt = res[1].T
    ones = jnp.ones((VT_ROWS - HEAD_DIM, ns), F32)
    for k in range(NSA_KV_HEADS):
        vct_ref[0, k] = jnp.concatenate([vt[k * HEAD_DIM:(k + 1) * HEAD_DIM], ones], axis=0).astype(BF16)


def _compress_prompt(kcmp, vcmp, pe, w1, w2, b, seq):
    ns = seq // CMP_STRIDE
    tok = pl.BlockSpec((seq, LANES), lambda i: (i, 0))
    return pl.pallas_call(
        _compress_kernel,
        grid=(b,),
        in_specs=[tok, tok, _const_spec(pe.shape), _const_spec(w1.shape), _const_spec(w2.shape)],
        out_specs=[pl.BlockSpec((1, ns, LANES), lambda i: (i, 0, 0)),
                   pl.BlockSpec((1, NSA_KV_HEADS, VT_ROWS, ns), lambda i: (i, 0, 0, 0))],
        out_shape=[jax.ShapeDtypeStruct((b, ns, LANES), BF16),
                   jax.ShapeDtypeStruct((b, NSA_KV_HEADS, VT_ROWS, ns), BF16)],
        compiler_params=_params("parallel"), name="nsa_compress",
    )(kcmp, vcmp, pe, w1, w2)


TQ = 512
KT = TQ
SUB = 128
WIN_KEYS = WINDOW + SUB


def _softmax_rows(s, mask):
    s = jnp.where(mask, s, NEG_BIG)
    e = jnp.where(mask, jnp.exp(s - jnp.max(s, axis=-1, keepdims=True)), 0.0)
    return e * (1.0 / jnp.maximum(jnp.sum(e, axis=-1, keepdims=True), 1e-30))


def _topk_bias_t(imp_t, n_cand, n_sel):
    nr = imp_t.shape[0] // 8
    xs = [imp_t[8 * r:8 * r + 8, :] for r in range(nr)]
    cnt = [jnp.zeros_like(xs[0]) for _ in range(nr)]
    sub = lax.broadcasted_iota(jnp.int32, xs[0].shape, 0)
    for j in range(n_cand):
        rj = j // 8
        row = xs[rj][j % 8:j % 8 + 1, :]
        for r in range(nr):
            if r < rj:
                inc = jnp.where(row > xs[r], 1.0, 0.0)
            elif r > rj:
                inc = jnp.where(row >= xs[r], 1.0, 0.0)
            else:
                inc = jnp.where(sub > j % 8, jnp.where(row >= xs[r], 1.0, 0.0),
                                jnp.where(row > xs[r], 1.0, 0.0))
            cnt[r] = cnt[r] + inc
    return jnp.concatenate([jnp.where(c < n_sel, 0.0, NEG_BIG) for c in cnt], axis=0)


def _nsa_prompt_kernel(qraw_ref, qrot_ref, small_ref, kc_ref, vct_ref, ksa_ref, vst_ref, win_ref, vwt_ref,
                       ovt_ref, cbias_ref, dbias_ref, wbias_ref, o_ref):
    seq = ksa_ref.shape[1]
    ns = seq // SEL_BLOCK
    s0 = pl.multiple_of(pl.program_id(1) * TQ, TQ)
    cols = NSA_GROUP * TQ
    tile = lambda a: jnp.concatenate([a] * NSA_GROUP, axis=1)
    qpos_row = s0 + lax.broadcasted_iota(jnp.int32, (1, cols), 1) % TQ
    gates_t = _sigmoid(small_ref[...]).T

    blk_t = lax.broadcasted_iota(jnp.int32, (ns, TQ), 0)
    qpos_t = s0 + lax.broadcasted_iota(jnp.int32, (ns, TQ), 1)
    cur_t = qpos_t // SEL_BLOCK
    valid_t = blk_t * SEL_BLOCK <= qpos_t
    forced_t = (blk_t == 0) | (blk_t == cur_t) | (blk_t == cur_t - 1)
    cbias = tile(cbias_ref[0])
    dbias = tile(dbias_ref[...])
    nsub = TQ // SUB

    def split_sum(aug):
        return aug[0:HEAD_DIM], aug[HEAD_DIM:HEAD_DIM + 1]

    def sub_cols(a, r):
        return jnp.concatenate([a[:, g * TQ + r * SUB:g * TQ + (r + 1) * SUB] for g in range(NSA_GROUP)], axis=1)

    def join_subs(parts):
        return jnp.concatenate([parts[r][:, g * SUB:(g + 1) * SUB]
                                for g in range(NSA_GROUP) for r in range(nsub)], axis=1)

    kv_heads = range(NSA_KV_HEADS)
    o_c, o_w, q_sel = [], [], []
    for k in kv_heads:
        def q_t(ref):
            zeros = jnp.zeros((HEAD_DIM, TQ), F32)
            blocks = []
            for pair in range(NSA_GROUP // 2):
                j = k * (NSA_GROUP // 2) + pair
                both = ref[:, j * LANES:(j + 1) * LANES].astype(F32).T
                for half in range(2):
                    qh = both[half * HEAD_DIM:(half + 1) * HEAD_DIM]
                    blocks.append(jnp.concatenate([qh, zeros] if k == 0 else [zeros, qh], axis=0))
            return jnp.concatenate(blocks, axis=1).astype(BF16)
        qt_raw = q_t(qraw_ref)
        qt_rot = q_t(qrot_ref)

        s = _dot(kc_ref[0], qt_raw) + cbias
        e = jnp.exp2(s - jnp.max(s, axis=0, keepdims=True))
        acc, l = split_sum(_dot(vct_ref[0, k], e.astype(BF16)))
        inv = jnp.where(qpos_row >= CMP_BLOCK - 1, 1.0 / l, 0.0)
        o_c.append(acc * inv)
        p = e * inv
        p_sum = p[:, 0:TQ] + p[:, TQ:2 * TQ] + p[:, 2 * TQ:3 * TQ] + p[:, 3 * TQ:4 * TQ]
        imp_t = _dot_exact_lhs(ovt_ref[...], p_sum)[0:ns]
        imp_t = jnp.where(valid_t, imp_t + jnp.where(forced_t, FORCE_BONUS, 0.0), -jnp.inf)
        bias_t = _topk_bias_t(imp_t, ns, min(N_SEL, ns))
        if ns < LANES:
            bias_t = jnp.concatenate([bias_t, jnp.zeros((LANES - ns, TQ), F32)], axis=0)
        q_sel.append(jnp.concatenate([qt_rot, tile(bias_t).astype(BF16)], axis=0))

        parts = []
        for r in range(nsub):
            q0 = s0 + r * SUB
            w0 = pl.multiple_of(jnp.maximum(q0 - WINDOW, 0), SUB)
            bias = tile(wbias_ref[jnp.minimum(q0 // SUB, WINDOW // SUB)])
            s = _dot(win_ref[0, pl.ds(w0, WIN_KEYS), :], sub_cols(qt_rot, r)) + bias
            e = jnp.exp2(s - jnp.max(s, axis=0, keepdims=True)).astype(BF16)
            acc, l = split_sum(_dot(vwt_ref[k, :, pl.ds(w0, WIN_KEYS)], e))
            parts.append(acc * (1.0 / l))
        o_w.append(join_subs(parts))

    def sel_step(c, carry):
        k0 = pl.multiple_of(c * KT, KT)
        keys = ksa_ref[0, pl.ds(k0, KT), :]
        out = []
        for k in kv_heads:
            m, acc = carry[k]
            s = _dot(keys, q_sel[k])
            m_new = jnp.maximum(m, jnp.max(s, axis=0, keepdims=True))
            p = jnp.exp2(s - m_new).astype(BF16)
            out.append((m_new, jnp.exp2(m - m_new) * acc + _dot(vst_ref[k, :, pl.ds(k0, KT)], p)))
        return tuple(out)
    init = tuple((jnp.full((1, cols), NEG_BIG, F32), jnp.zeros((VT_ROWS, cols), F32)) for _ in kv_heads)
    carry = lax.fori_loop(0, s0 // KT, sel_step, init)
    res = []
    for k in kv_heads:
        parts = []
        for r in range(nsub):
            nk = (r + 1) * SUB
            m, acc = sub_cols(carry[k][0], r), sub_cols(carry[k][1], r)
            s = _dot(ksa_ref[0, pl.ds(s0, nk), :], sub_cols(q_sel[k], r))
            s = jnp.concatenate([s[0:r * SUB], s[r * SUB:nk] + dbias], axis=0) if r else s + dbias
            m_new = jnp.maximum(m, jnp.max(s, axis=0, keepdims=True))
            p = jnp.exp2(s - m_new).astype(BF16)
            acc, l = split_sum(jnp.exp2(m - m_new) * acc + _dot(vst_ref[k, :, pl.ds(s0, nk)], p))
            parts.append(acc * (1.0 / l))
        res.append((o_c[k], join_subs(parts), o_w[k]))

    for g in range(NSA_GROUP):
        halves = []
        for k in range(NSA_KV_HEADS):
            r = SM_GATE + (k * NSA_GROUP + g) * 3
            halves.append(sum(gates_t[r + br:r + br + 1, :] * res[k][br][:, g * TQ:(g + 1) * TQ]
                              for br in range(3)))
        o_ref[:, g * LANES:(g + 1) * LANES] = jnp.concatenate(halves, axis=0).T.astype(BF16)


def _attn_bias_tables(seq):
    r = np.arange(TQ)[None, None, :]
    blocks = np.arange(seq // TQ)[:, None, None]
    c = np.arange(seq // CMP_STRIDE)[None, :, None]
    cb = c * CMP_STRIDE + CMP_BLOCK - 1 <= blocks * TQ + r
    rs = np.arange(SUB)[None, :]
    db = np.arange(SUB)[:, None] <= rs
    kw = np.arange(WIN_KEYS)[None, :, None]
    nw = WINDOW // SUB
    qpos = np.arange(nw + 1)[:, None, None] * SUB + rs[None]
    wb = (kw <= qpos) & (kw > qpos - WINDOW)
    f = lambda m: jnp.asarray(np.where(m, 0.0, NEG_BIG).astype(np.float32))
    return f(cb), f(db), f(wb)


def _nsa_prompt(qraw, qrot, small, kc, vct, ksa, vst, kw16, vwt, ovt, tables, b, seq):
    nc = kc.shape[1]
    nq = seq // TQ
    cb, db, wb = tables
    tok = lambda width: pl.BlockSpec((TQ, width), lambda i, j: (i * nq + j, 0))
    per_b = lambda width: pl.BlockSpec((1, seq, width), lambda i, j: (i, 0, 0))
    vt_spec = pl.BlockSpec((NSA_KV_HEADS, VT_ROWS, seq), lambda i, j: (0, 0, i))
    return pl.pallas_call(
        _nsa_prompt_kernel,
        grid=(b, nq),
        in_specs=[tok(NSA_Q_W), tok(NSA_Q_W), tok(LANES),
                  pl.BlockSpec((1, nc, LANES), lambda i, j: (i, 0, 0)),
                  pl.BlockSpec((1, NSA_KV_HEADS, VT_ROWS, nc), lambda i, j: (i, 0, 0, 0)),
                  per_b(2 * LANES), vt_spec, per_b(LANES), vt_spec,
                  _const_spec(ovt.shape),
                  pl.BlockSpec((1, nc, TQ), lambda i, j: (j, 0, 0)),
                  _const_spec(db.shape), _const_spec(wb.shape)],
        out_specs=tok(NSA_Q_W),
        out_shape=jax.ShapeDtypeStruct((b * seq, NSA_Q_W), BF16),
        compiler_params=_params("parallel", "parallel"), name="nsa_prompt",
    )(qraw, qrot, small, kc, vct, ksa.reshape(b, seq, 2 * LANES), vst, kw16.reshape(b, seq, LANES), vwt,
      ovt, cb, db, wb)


PAGES_PER_STEP = 32
SEG_PER_PAGE = PAGE_SIZE // CMP_STRIDE
SEG_W = CMP_STRIDE * 2 * NSA_KV_W


def _nsa_sample_kernel(ts, past, pt_ref, *refs):
    pages = refs[:PAGES_PER_STEP]
    (qraw_ref, qrot_ref, small_ref, rows_ref, winnew_ref, winst_ref, pe_ref, w1_ref, w2_ref,
     ovt_ref, rexp_ref, o_ref, seg_ref, ksat_ref, vst_ref, sm_ref, stage_ref, bias_ref, res_ref) = refs[PAGES_PER_STEP:]
    del pt_ref
    pg = pl.program_id(1)
    nseg = res_ref.shape[1]
    nrows = NSA_HEADS * ts
    kt = NSA_KV_HEADS * ts
    half_w = NSA_KV_HEADS * CMP_STRIDE * HEAD_DIM

    @pl.when((pl.program_id(0) == 0) & (pg == 0))
    def _():
        for part in range(2):
            hi, mid, lo = _split3(pe_ref[part])
            w = w1_ref[part]
            bias_ref[part] = _dot(hi, w) + _dot(mid, w) + _dot(lo, w)

    for e in range(PAGES_PER_STEP):
        for part in range(2):
            xt = pages[e][0, 0, part * LANES:(part + 1) * LANES, :].T
            for j in range(PAGE_SIZE // 8):
                start = (j % 2) * (PAGE_SIZE // 2) + j // 2
                stage_ref[e, part, pl.ds(start, 8, stride=SEG_PER_PAGE), :] = xt[8 * j:8 * j + 8, :]
        k0 = pl.multiple_of((pg * PAGES_PER_STEP + e) * PAGE_SIZE, PAGE_SIZE)
        ksat_ref[0:LANES, pl.ds(k0, PAGE_SIZE)] = pages[e][0, 0, 2 * LANES:3 * LANES, :].astype(BF16)
        vst_ref[:, pl.ds(k0, PAGE_SIZE)] = pages[e][0, 0, 3 * LANES:4 * LANES, :].astype(BF16)

        @pl.when(pl.program_id(0) == 0)
        def _():
            kpos = k0 + lax.broadcasted_iota(jnp.int32, (LANES, PAGE_SIZE), 1)
            blk = lax.broadcasted_iota(jnp.int32, (LANES, PAGE_SIZE), 0)
            ksat_ref[LANES:2 * LANES, pl.ds(k0, PAGE_SIZE)] = jnp.where(
                kpos // SEL_BLOCK == blk, 1.0, 0.0).astype(BF16)
    for pair in range(PAGES_PER_STEP // 2):
        r0 = pair * 2 * SEG_PER_PAGE
        for part in range(2):
            for p in range(CMP_STRIDE):
                x = jnp.concatenate([stage_ref[2 * pair + e, part, p * SEG_PER_PAGE:(p + 1) * SEG_PER_PAGE, :]
                                     for e in range(2)], axis=0)
                c0 = part * half_w + p * LANES
                seg_ref[r0:r0 + 2 * SEG_PER_PAGE, c0:c0 + LANES] = x.astype(BF16)
    g_rows = PAGES_PER_STEP * SEG_PER_PAGE
    for part in range(2):
        res_ref[part, pl.ds(pl.multiple_of(pg * g_rows, g_rows), g_rows), :] = _dot(
            seg_ref[:, part * half_w:(part + 1) * half_w], w1_ref[part])

    @pl.when(pg == pl.num_programs(1) - 1)
    def _():
        ns_past = past // SEL_BLOCK
        nc = nseg - 1
        q_raw = qraw_ref[0]
        q_rot = qrot_ref[0]
        q_rot_f = q_rot.astype(F32)
        row = lax.broadcasted_iota(jnp.int32, (nrows, 1), 0)
        t_row = row % ts
        kcvc = []
        for part in range(2):
            res = res_ref[part]
            b8 = bias_ref[part]
            pre = (res[:, 0:LANES] + b8[0:1, 0:LANES]
                   + pltpu.roll(res[:, LANES:2 * LANES] + b8[1:2, LANES:2 * LANES], nseg - 1, 0))
            kcvc.append(_dot(jax.nn.gelu(pre).astype(BF16), w2_ref[part]).astype(BF16))
        s_c = _dot_nt(q_raw, kcvc[0])
        p_c = _softmax_rows(s_c, lax.broadcasted_iota(jnp.int32, (nrows, nseg), 1) < nc)
        o_c = _dot(p_c.astype(BF16), kcvc[1])
        p_sum = p_c[0:kt] + p_c[kt:2 * kt] + p_c[2 * kt:3 * kt] + p_c[3 * kt:4 * kt]
        p_sum = jnp.concatenate([p_sum, jnp.zeros((LANES - kt, nseg), F32)], axis=0)
        nb = ovt_ref.shape[0]
        imp_t = _dot_exact_lhs_nt(ovt_ref[...], p_sum)
        blk_t = lax.broadcasted_iota(jnp.int32, (nb, LANES), 0)
        qpos_t = past + lax.broadcasted_iota(jnp.int32, (nb, LANES), 1) % ts
        cur_t = qpos_t // SEL_BLOCK
        forced_t = (blk_t == 0) | (blk_t == cur_t) | (blk_t == cur_t - 1)
        imp_t = jnp.where(blk_t * SEL_BLOCK <= qpos_t, imp_t + jnp.where(forced_t, FORCE_BONUS, 0.0), -jnp.inf)
        bias_t = _topk_bias_t(imp_t, -(-(past + ts) // SEL_BLOCK), N_SEL)[0:ns_past]
        bias = bias_t.T[0:kt]
        q_sel = jnp.concatenate([q_rot, jnp.concatenate([bias] * NSA_GROUP, axis=0).astype(BF16)], axis=1)

        def attend(s_past, v_past16_t, k_new, v_new, mask_past):
            if mask_past is not None:
                s_past = jnp.where(mask_past, s_past, NEG_BIG)
            s_new = [jnp.where(t_row >= j, jnp.sum(q_rot_f * k_new[j:j + 1, :], axis=-1, keepdims=True), NEG_BIG)
                     for j in range(ts)]
            m = jnp.max(s_past, axis=-1, keepdims=True)
            for s in s_new:
                m = jnp.maximum(m, s)
            p_past = jnp.exp(s_past - m)
            if mask_past is not None:
                p_past = jnp.where(mask_past, p_past, 0.0)
            acc = _dot_nt(p_past.astype(BF16), v_past16_t)
            l = jnp.sum(p_past, axis=-1, keepdims=True)
            for j, s in enumerate(s_new):
                p = jnp.where(t_row >= j, jnp.exp(s - m), 0.0)
                acc = acc + p * v_new[j:j + 1, :]
                l = l + p
            return acc * (1.0 / l)

        new = rows_ref[0]
        o_s = attend(_dot(q_sel, ksat_ref[...]), vst_ref[...], new[:, 2 * NSA_KV_W:3 * NSA_KV_W],
                     new[:, 3 * NSA_KV_W:4 * NSA_KV_W], None)
        wbuf = winst_ref.shape[3]
        wnew = winnew_ref[0]
        i_w = lax.broadcasted_iota(jnp.int32, (nrows, wbuf), 1)
        dist = t_row + wbuf - i_w
        o_w = attend(_dot(q_rot, winst_ref[0, 0, 0:LANES, :].astype(BF16)),
                     winst_ref[0, 0, LANES:2 * LANES, :].astype(BF16),
                     wnew[:, 0:LANES], wnew[:, LANES:2 * LANES],
                     (dist < WINDOW) & (past - wbuf + i_w >= 0))

        sm_ref[...] = jnp.zeros_like(sm_ref)
        sm_ref[0:ts, :] = small_ref[0]
        gexp = _dot_exact_rhs(_sigmoid(sm_ref[...]), rexp_ref[...])[0:kt]
        low = lax.broadcasted_iota(jnp.int32, (kt, LANES), 1) < HEAD_DIM
        for g in range(NSA_GROUP):
            out = jnp.zeros((kt, LANES), F32)
            for br, o_br in enumerate((o_c, o_s, o_w)):
                og = o_br[g * kt:(g + 1) * kt]
                both = jnp.where(low, og, pltpu.roll(og, kt - ts, 0))
                out = out + gexp[:, br * NSA_Q_W + g * LANES:br * NSA_Q_W + (g + 1) * LANES] * both
            o_ref[0, :, g * LANES:(g + 1) * LANES] = out


def _nsa_sample(layer, cache, page_table, qraw_s, qrot_s, small_s, rows_s, winnew_s, win_state,
                pe, w1, w2, ovt, rexp, past):
    b, ts = small_s.shape[:2]
    n_pages = page_table.shape[1]
    assert n_pages % PAGES_PER_STEP == 0 and ts * NSA_KV_HEADS == 8 and past == n_pages * PAGE_SIZE
    nseg = past // CMP_STRIDE
    wbuf = win_state.shape[3]
    kw = 4 * NSA_KV_W

    def page_spec(e):
        return pl.BlockSpec((1, 1, 4 * PAGE_SIZE, LANES),
                            lambda i, g, pt: (layer, pt[i, g * PAGES_PER_STEP + e], 0, 0))
    per_b = lambda shape: pl.BlockSpec((1,) + shape, lambda i, g, pt: (i,) + (0,) * len(shape))
    const = lambda shape: pl.BlockSpec(shape, lambda i, g, pt: (0,) * len(shape))
    grid_spec = pltpu.PrefetchScalarGridSpec(
        num_scalar_prefetch=1,
        grid=(b, n_pages // PAGES_PER_STEP),
        in_specs=[page_spec(e) for e in range(PAGES_PER_STEP)] + [
            per_b((NSA_HEADS * ts, LANES)), per_b((NSA_HEADS * ts, LANES)), per_b((ts, LANES)),
            per_b((ts, kw)), per_b((ts, 2 * NSA_KV_W)),
            pl.BlockSpec((1, 1, 2 * NSA_KV_W, wbuf), lambda i, g, pt: (layer, i, 0, 0)),
            const(pe.shape), const(w1.shape), const(w2.shape), const(ovt.shape), const(rexp.shape)],
        out_specs=per_b((NSA_KV_HEADS * ts, NSA_Q_W)),
        scratch_shapes=[pltpu.VMEM((PAGES_PER_STEP * SEG_PER_PAGE, SEG_W), BF16),
                        pltpu.VMEM((2 * LANES, past), BF16), pltpu.VMEM((LANES, past), BF16),
                        pltpu.VMEM((2 * NSA_KV_HEADS * ts, LANES), F32),
                        pltpu.VMEM((PAGES_PER_STEP, 2, PAGE_SIZE, LANES), F32),
                        pltpu.VMEM((2, 8, 2 * LANES), F32),
                        pltpu.VMEM((2, nseg, 2 * LANES), F32)])
    return pl.pallas_call(
        functools.partial(_nsa_sample_kernel, ts, past),
        grid_spec=grid_spec,
        out_shape=jax.ShapeDtypeStruct((b, NSA_KV_HEADS * ts, NSA_Q_W), F32),
        compiler_params=_params("arbitrary", "arbitrary"), name="nsa_sample",
    )(page_table, *([cache] * PAGES_PER_STEP), qraw_s, qrot_s, small_s, rows_s, winnew_s, win_state,
      pe, w1, w2, ovt, rexp)


DN_C = 128
DN_CHUNKS_PER_STEP = 2
HALO = 8


def _softplus(x):
    return jnp.maximum(x, 0.0) + jnp.log(1.0 + jnp.exp(-jnp.abs(x)))


def _l2n(x):
    return x * lax.rsqrt(jnp.sum(x * x, axis=-1, keepdims=True) + EPS)


def _dn_prep_kernel(n_valid, qkv_ref, halo_ref, prefix_ref, small_ref, cw_ref, alog_ref, dtb_ref,
                    tri_ref, u_ref, w_ref, qd_ref, qk_ref, kdt_ref, gl_ref, xcat_ref, sm_ref):
    n = pl.program_id(1)
    cps = kdt_ref.shape[0]
    rows = cps * DN_C
    if n_valid == DN_C:
        xcat_ref[HALO:HALO + rows, :] = qkv_ref[...]
        small = small_ref[...]

        @pl.when(n == 0)
        def _():
            xcat_ref[0:HALO, :] = prefix_ref[0]

        @pl.when(n > 0)
        def _():
            xcat_ref[0:HALO, :] = halo_ref[...]
    else:
        xcat_ref[...] = jnp.zeros_like(xcat_ref)
        xcat_ref[0:HALO, :] = prefix_ref[0]
        xcat_ref[HALO:HALO + n_valid, :] = qkv_ref[0]
        sm_ref[...] = jnp.zeros_like(sm_ref)
        sm_ref[0:n_valid, :] = small_ref[0]
        small = sm_ref[...]

    conv = jnp.zeros((rows, DN_QKV_W), F32)
    for i in range(CONV_W):
        conv = conv + xcat_ref[HALO - (CONV_W - 1) + i:HALO - (CONV_W - 1) + i + rows, :] * cw_ref[i:i + 1, :]
    conv = _silu(conv)

    ii = lax.broadcasted_iota(jnp.int32, (DN_C, DN_C), 0)
    jj = lax.broadcasted_iota(jnp.int32, (DN_C, DN_C), 1)
    g_rows = -jnp.exp(alog_ref[...]) * _softplus(small + dtb_ref[...])
    beta_rows = _sigmoid(small)
    chains = [(c, h) for c in range(cps) for h in range(DN_HEADS)]
    gcol, beta, decay = {}, {}, {}
    for c in range(cps):
        g_all = g_rows[c * DN_C:(c + 1) * DN_C]
        beta_all = beta_rows[c * DN_C:(c + 1) * DN_C]
        if n_valid != DN_C:
            g_all = jnp.where(ii < n_valid, g_all, 0.0)
            beta_all = jnp.where(ii < n_valid, beta_all, 0.0)
        gc_all = _dot_exact_lhs(tri_ref[...], g_all)
        gr_all = gc_all.T
        for h in range(DN_HEADS):
            gcol[c, h] = gc_all[:, SM_A + h:SM_A + h + 1]
            beta[c, h] = beta_all[:, SM_B + h:SM_B + h + 1]
            decay[c, h] = jnp.where(ii >= jj, jnp.exp(gcol[c, h] - gr_all[SM_A + h:SM_A + h + 1, :]), 0.0)
    tok = lambda c: slice(c * DN_C, (c + 1) * DN_C)
    q = {(c, h): _l2n(conv[tok(c), h * DN_DK:(h + 1) * DN_DK]) * DN_DK ** -0.5 for c, h in chains}
    k = {(c, h): _l2n(conv[tok(c), DN_QK_W + h * DN_DK:DN_QK_W + (h + 1) * DN_DK]) for c, h in chains}
    v = {(c, h): conv[tok(c), 2 * DN_QK_W + h * DN_DV:2 * DN_QK_W + (h + 1) * DN_DV] for c, h in chains}
    kb = {i: k[i] * beta[i] for i in chains}
    k16 = {i: k[i].astype(BF16) for i in chains}
    eg = {i: jnp.exp(gcol[i]) for i in chains}
    a = {i: jnp.where(ii > jj, _dot_nt(kb[i].astype(BF16), k16[i]) * decay[i], 0.0) for i in chains}
    t = {i: jnp.where(ii == jj, 1.0, 0.0) - a[i] for i in chains}
    for _ in range(max(1, math.ceil(math.log2(n_valid))) - 1):
        a = {i: _dot3(a[i], a[i]) for i in chains}
        t = {i: t[i] + _dot3(t[i], a[i]) for i in chains}
    x = {i: _dot3(t[i], jnp.concatenate([v[i] * beta[i], kb[i] * eg[i]], axis=1)) for i in chains}
    for c, h in chains:
        i = (c, h)
        u_ref[tok(c), h * DN_DV:(h + 1) * DN_DV] = x[i][:, 0:DN_DV]
        w_ref[tok(c), h * DN_DK:(h + 1) * DN_DK] = x[i][:, DN_DV:DN_DV + DN_DK].astype(BF16)
        qk = jnp.where(ii >= jj, _dot_nt(q[i].astype(BF16), k16[i]) * decay[i], 0.0)
        qk_ref[tok(c), h * DN_C:(h + 1) * DN_C] = qk.astype(BF16)
        qd_ref[tok(c), h * DN_DK:(h + 1) * DN_DK] = (q[i] * eg[i]).astype(BF16)
        glast = gcol[i][DN_C - 1:DN_C, :]
        kdt_ref[c, h * DN_DK:(h + 1) * DN_DK, :] = (k[i] * jnp.exp(glast - gcol[i])).T.astype(BF16)
        gl_ref[c, h:h + 1, :] = jnp.broadcast_to(jnp.exp(glast), (1, LANES))
    for c in range(cps):
        gl_ref[c, DN_HEADS:, :] = jnp.zeros((HALO - DN_HEADS, LANES), F32)


def _dn_prep(qkv, prefix8, small, cw, alog_row, dtb_row, tri, b, nch, n_valid):
    rows = b * nch * DN_C
    cps = DN_CHUNKS_PER_STEP if (n_valid == DN_C and nch % DN_CHUNKS_PER_STEP == 0) else 1
    steps = nch // cps
    blk = cps * DN_C
    if n_valid == DN_C:
        qkv_spec = pl.BlockSpec((blk, DN_QKV_W), lambda i, n: (i * steps + n, 0))
        halo_spec = pl.BlockSpec((HALO, DN_QKV_W),
                                 lambda i, n: (jnp.maximum((i * steps + n) * (blk // HALO) - 1, 0), 0))
        small_spec = pl.BlockSpec((blk, LANES), lambda i, n: (i * steps + n, 0))
        qkv_in, halo_in, small_in = qkv, qkv, small
    else:
        qkv_in = qkv.reshape(b, n_valid, DN_QKV_W)
        small_in = small.reshape(b, n_valid, LANES)
        halo_in = prefix8
        qkv_spec = pl.BlockSpec((1, n_valid, DN_QKV_W), lambda i, n: (i, 0, 0))
        halo_spec = pl.BlockSpec((1, HALO, DN_QKV_W), lambda i, n: (i, 0, 0))
        small_spec = pl.BlockSpec((1, n_valid, LANES), lambda i, n: (i, 0, 0))
    tok = lambda width: pl.BlockSpec((blk, width), lambda i, n: (i * steps + n, 0))
    return pl.pallas_call(
        functools.partial(_dn_prep_kernel, n_valid),
        grid=(b, steps),
        in_specs=[qkv_spec, halo_spec, pl.BlockSpec((1, HALO, DN_QKV_W), lambda i, n: (i, 0, 0)), small_spec,
                  _const_spec((CONV_W, DN_QKV_W)), _const_spec((1, LANES)), _const_spec((1, LANES)),
                  _const_spec((DN_C, DN_C))],
        out_specs=[tok(DN_W), tok(DN_QK_W), tok(DN_QK_W), tok(DN_HEADS * DN_C),
                   pl.BlockSpec((cps, DN_QK_W, DN_C), lambda i, n: (i * steps + n, 0, 0)),
                   pl.BlockSpec((cps, HALO, LANES), lambda i, n: (i * steps + n, 0, 0))],
        out_shape=[jax.ShapeDtypeStruct((rows, DN_W), F32), jax.ShapeDtypeStruct((rows, DN_QK_W), BF16),
                   jax.ShapeDtypeStruct((rows, DN_QK_W), BF16), jax.ShapeDtypeStruct((rows, DN_HEADS * DN_C), BF16),
                   jax.ShapeDtypeStruct((b * nch, DN_QK_W, DN_C), BF16),
                   jax.ShapeDtypeStruct((b * nch, HALO, LANES), F32)],
        scratch_shapes=[pltpu.VMEM((HALO + blk, DN_QKV_W), F32), pltpu.VMEM((DN_C, LANES), F32)],
        compiler_params=_params("parallel", "parallel"), name="dn_prep",
    )(qkv_in, halo_in, prefix8, small_in, cw, alog_row, dtb_row, tri)


def _dn_scan_kernel(n_valid, u_ref, w_ref, qd_ref, qk_ref, kdt_ref, gl_ref, zg_ref, s0_ref, nw_ref,
                    o_ref, s_ref):
    @pl.when(pl.program_id(1) == 0)
    def _():
        s_ref[...] = s0_ref[...]

    def block_diag(a0, a1):
        z = jnp.zeros_like(a0)
        return jnp.concatenate([jnp.concatenate([a0, z], axis=1), jnp.concatenate([z, a1], axis=1)], axis=0)

    for b in range(u_ref.shape[0]):
        for pair in range(DN_HEADS // 2):
            heads = (2 * pair, 2 * pair + 1)
            cs2 = slice(2 * pair * DN_DK, (2 * pair + 2) * DN_DK)
            s = [s_ref[b, h] for h in heads]
            s_bd = block_diag(*[x.astype(BF16) for x in s])
            v_new = u_ref[b, :, cs2] - _dot(w_ref[b, :, cs2], s_bd)
            v16 = v_new.astype(BF16)
            v_bd = block_diag(v16[:, 0:DN_DV], v16[:, DN_DV:2 * DN_DV])
            o2 = _dot(qd_ref[b, :, cs2], s_bd) + _dot(qk_ref[b, :, cs2], v_bd)
            for e, h in enumerate(heads):
                cs = slice(h * DN_DK, (h + 1) * DN_DK)
                s_ref[b, h] = s[e] * gl_ref[b, h:h + 1, :] + _dot(kdt_ref[b, cs, :], v16[:, e * DN_DV:(e + 1) * DN_DV])
                o = o2[:, e * DN_DV:(e + 1) * DN_DV]
                on = o * lax.rsqrt(jnp.mean(o * o, axis=-1, keepdims=True) + EPS) * nw_ref[...]
                o_ref[b, :, cs] = (on[0:n_valid] * _silu(zg_ref[b, :, cs])).astype(BF16)


def _dn_scan(u, w, qd, qk, kdt, gl, zg, s0, nw, b, nch, bb, n_valid):
    tok = lambda width: pl.BlockSpec((bb, DN_C, width), lambda i, n: (i, n, 0))
    r3 = lambda a: a.reshape(b, nch * DN_C, a.shape[-1])
    st = pl.BlockSpec((bb, DN_HEADS, DN_DK, DN_DV), lambda i, n: (i, 0, 0, 0))
    return pl.pallas_call(
        functools.partial(_dn_scan_kernel, n_valid),
        grid=(b // bb, nch),
        in_specs=[tok(DN_W), tok(DN_QK_W), tok(DN_QK_W), tok(DN_HEADS * DN_C),
                  pl.BlockSpec((bb, DN_QK_W, DN_C), lambda i, n: (i, n, 0)),
                  pl.BlockSpec((bb, HALO, LANES), lambda i, n: (i, n, 0)),
                  pl.BlockSpec((bb, n_valid, DN_W), lambda i, n: (i, n, 0)), st, _const_spec((1, DN_DV))],
        out_specs=[pl.BlockSpec((bb, n_valid, DN_W), lambda i, n: (i, n, 0)), st],
        out_shape=[jax.ShapeDtypeStruct((b, nch * n_valid, DN_W), BF16),
                   jax.ShapeDtypeStruct((b, DN_HEADS, DN_DK, DN_DV), F32)],
        compiler_params=_params("parallel", "arbitrary"), name="dn_scan",
    )(r3(u), r3(w), r3(qd), r3(qk), kdt.reshape(b, nch * DN_QK_W, DN_C), gl.reshape(b, nch * HALO, LANES),
      zg.reshape(b, nch * n_valid, DN_W), s0, nw)


IN_WIDTHS = (NSA_Q_W, 6 * NSA_KV_W, 3 * NSA_HEADS, DN_QKV_W, DN_HEADS, DN_HEADS, DN_W, 2 * D_MODEL)


def _pack_mix_weight(w_in):
    offs = np.cumsum(IN_WIDTHS)[:-1].tolist()
    q, kv, gate, qkv, a, b, zg, merge = jnp.split(w_in, offs, axis=-1)
    lead = w_in.shape[:-1]
    small = jnp.concatenate([gate, a, b, jnp.zeros(lead + (LANES - SM_B - DN_HEADS,), w_in.dtype)], axis=-1)
    return jnp.concatenate([q, kv, small, qkv, zg, merge], axis=-1).astype(BF16)


def _rope_tables(pos):
    half = HEAD_DIM // 2
    freq = ROPE_THETA ** (-jnp.arange(half, dtype=F32) / half)
    ang = pos.astype(F32)[:, None] * freq[None, :]
    cos, sin = jnp.cos(ang), jnp.sin(ang)
    cos = jnp.concatenate([cos, cos] * (LANES // HEAD_DIM), axis=1)
    sin = jnp.concatenate([-sin, sin] * (LANES // HEAD_DIM), axis=1)
    return cos, sin


def _gate_expand_matrix():
    r = np.zeros((LANES, 3 * NSA_Q_W), np.float32)
    for h in range(NSA_HEADS):
        k, g = divmod(h, NSA_GROUP)
        for br in range(3):
            c0 = br * NSA_Q_W + g * LANES + k * HEAD_DIM
            r[SM_GATE + h * 3 + br, c0:c0 + HEAD_DIM] = 1.0
    return jnp.asarray(r, BF16)


def _overlap_t(nc_pad, nc, ns, rows):
    i = np.arange(nc_pad)[None, :]
    j = np.arange(rows)[:, None]
    ov = (i * CMP_STRIDE < (j + 1) * SEL_BLOCK) & (i * CMP_STRIDE + CMP_BLOCK > j * SEL_BLOCK)
    ov = ov & (i < nc) & (j < ns)
    return jnp.asarray(ov.astype(np.float32), BF16)


def _pack_compress_sample(pos_emb, w1, w2):
    eye = jnp.eye(NSA_KV_HEADS, dtype=w1.dtype)
    w = w1.reshape(2, 2, CMP_STRIDE, HEAD_DIM, HEAD_DIM)
    w = jnp.einsum('kapde,hg->kphdage', w, eye)
    w = w.reshape(2, NSA_KV_HEADS * CMP_STRIDE * HEAD_DIM, 2 * LANES).astype(BF16)
    pe = pos_emb.reshape(2, 2, CMP_STRIDE, 1, HEAD_DIM)
    pe = jnp.concatenate([pe] * NSA_KV_HEADS, axis=3).reshape(2, 2, -1)
    pe = jnp.concatenate([pe, jnp.zeros((2, 6, pe.shape[2]), pe.dtype)], axis=1)
    w2d = jnp.einsum('kde,hg->khdge', w2, eye).reshape(2, LANES, LANES).astype(BF16)
    return pe, w, w2d


def _sample_rows(q_pad, b, ts):
    q = q_pad.reshape(b, ts, NSA_KV_HEADS, NSA_GROUP, 1, HEAD_DIM)
    eye = jnp.eye(NSA_KV_HEADS, dtype=q.dtype).reshape(NSA_KV_HEADS, 1, NSA_KV_HEADS, 1)
    q = (q * eye).transpose(0, 3, 2, 1, 4, 5)
    return q.reshape(b, NSA_HEADS * ts, LANES)


def _dn_lane_row(vals):
    return jnp.zeros((1, LANES), F32).at[0, SM_A:SM_A + DN_HEADS].set(vals.astype(F32))


def _deltanet(qkv, small, zg, prefix, s0, dn_params, tri, b, seq_tokens):
    conv_w, a_log, dt_bias, norm_w = dn_params
    if seq_tokens % DN_C == 0:
        nch, n_valid, bb = seq_tokens // DN_C, DN_C, min(b, 4)
    else:
        nch, n_valid, bb = 1, seq_tokens, 4
    prefix8 = jnp.concatenate([jnp.zeros((b, HALO - (CONV_W - 1), DN_QKV_W), F32), prefix], axis=1)
    u, w, qd, qk, kdt, gl = _dn_prep(qkv, prefix8, small, conv_w, _dn_lane_row(a_log), _dn_lane_row(dt_bias),
                                     tri, b, nch, n_valid)
    ob, s_out = _dn_scan(u, w, qd, qk, kdt, gl, zg, s0, norm_w[None], b, nch, bb, n_valid)
    return ob.reshape(b * seq_tokens, DN_W), s_out


def _head_perm_rows(wa):
    d = wa.shape[1]
    return wa.reshape(NSA_KV_HEADS, NSA_GROUP, HEAD_DIM, d).transpose(1, 0, 2, 3).reshape(NSA_Q_W, d)


def _pack_compress(pos_emb, w1, w2):
    eye = jnp.eye(NSA_KV_HEADS, dtype=w1.dtype)
    pe = jnp.concatenate([pos_emb] * NSA_KV_HEADS, axis=2)
    w1p = w1.reshape(2, CMP_BLOCK, HEAD_DIM, HEAD_DIM)
    w1d = jnp.einsum('kpde,hg->kphdge', w1p, eye).reshape(2, CMP_BLOCK, LANES, LANES)
    w2d = jnp.einsum('kde,hg->khdge', w2, eye).reshape(2, LANES, LANES)
    return pe, w1d.astype(BF16), w2d.astype(BF16)


def kernel(x_prompt, x_sample, cache_nsa_kv, state_nsa_win, state_dn_S, state_dn_conv, page_table,
           ffn1_norm, ffn1_w_in, ffn1_w_out, mix_norm, w_in, nsa_cmp_pos, nsa_cmp_w1, nsa_cmp_w2,
           dn_conv_w, dn_A_log, dn_dt_bias, dn_out_norm, w_branch_a, w_branch_b, w_out,
           ffn2_norm, ffn2_w_in, ffn2_w_out, final_norm):
    b, seq = x_prompt.shape[:2]
    bs, ts = x_sample.shape[:2]
    depth = w_in.shape[0]
    n_phys = cache_nsa_kv.shape[1]
    past = page_table.shape[1] * PAGE_SIZE
    wbuf = state_nsa_win.shape[2]
    kvw = 4 * NSA_KV_W

    xp = x_prompt.reshape(b * seq, D_MODEL)
    xs = x_sample.reshape(bs * ts, D_MODEL)
    cos_p, sin_p = _rope_tables(jnp.arange(seq, dtype=jnp.int32))
    cos_s, sin_s = _rope_tables(jnp.tile(past + jnp.arange(ts, dtype=jnp.int32), bs))
    nc = (seq - CMP_BLOCK) // CMP_STRIDE + 1
    ovt_p = _overlap_t(seq // CMP_STRIDE, nc, seq // SEL_BLOCK, LANES)
    tables_p = _attn_bias_tables(seq)
    ns_s = -(-(past + ts) // SEL_BLOCK)
    ovt_s = _overlap_t(past // CMP_STRIDE, past // CMP_STRIDE - 1, ns_s, -(-ns_s // 16) * 16)
    rexp = _gate_expand_matrix()
    tri = jnp.asarray(np.tril(np.ones((DN_C, DN_C), np.float32)), BF16)
    cache = cache_nsa_kv.transpose(0, 1, 3, 4, 5, 2).reshape(depth, n_phys, kvw, PAGE_SIZE)
    win_state = state_nsa_win.transpose(0, 1, 3, 4, 5, 2).reshape(depth, bs, 2 * NSA_KV_W, wbuf)
    tm_p, tm_s = 512, bs * ts

    outs = [[] for _ in range(8)]
    rows_all = None
    w_mix = _pack_mix_weight(w_in)
    w1i, w1o = ffn1_w_in.astype(BF16), ffn1_w_out.astype(BF16)
    w2i, w2o = ffn2_w_in.astype(BF16), ffn2_w_out.astype(BF16)
    for l in range(depth):
        f1 = (ffn1_norm[l][None], w1i, w1o)
        f2 = (ffn2_norm[l][None], w2i, w2o)
        mrg = (_head_perm_rows(w_branch_a[l]).astype(BF16), w_branch_b[l].astype(BF16), w_out[l].astype(BF16))
        dn_params = (dn_conv_w[l], dn_A_log[l], dn_dt_bias[l], dn_out_norm[l])
        cmp_l = (nsa_cmp_pos[l], nsa_cmp_w1[l], nsa_cmp_w2[l])

        xp = _ffn(xp, *f1, tm_p, l)
        (qraw, qrot, small, qkv, zg, mg, rows_all, win_t, kcmp, vcmp, ksa, kw16, vst, vwt) = _mix(
            xp, mix_norm[l][None], w_mix, cos_p, sin_p, 256, seq, True, l, depth, rows_all)
        kc, vct = _compress_prompt(kcmp, vcmp, *_pack_compress(*cmp_l), b, seq)
        oa = _nsa_prompt(qraw, qrot, small, kc, vct, ksa, vst, kw16, vwt, ovt_p, tables_p, b, seq)
        ob, s_fin = _deltanet(qkv, small, zg, jnp.zeros((b, CONV_W - 1, DN_QKV_W), F32),
                              jnp.zeros((b, DN_HEADS, DN_DK, DN_DV), F32), dn_params, tri, b, seq)
        xp = _merge(xp, oa, ob, mg, *mrg, tm_p)
        xp = _ffn(xp, *f2, tm_p, l)
        outs[1].append(win_t[:, :, seq - min(WINDOW, seq):])
        outs[2].append(s_fin)
        outs[3].append(qkv.reshape(b, seq, DN_QKV_W)[:, seq - (CONV_W - 1):])

        xs = _ffn(xs, *f1, tm_s, l)
        (qraw, qrot, small, qkv, zg, mg, rows, win) = _mix(
            xs, mix_norm[l][None], w_mix, cos_s, sin_s, tm_s, tm_s, False, l)
        oa = _nsa_sample(l, cache, page_table, _sample_rows(qraw, bs, ts), _sample_rows(qrot, bs, ts),
                         small.reshape(bs, ts, LANES), rows.reshape(bs, ts, kvw), win.reshape(bs, ts, 2 * NSA_KV_W),
                         win_state, *_pack_compress_sample(*cmp_l), ovt_s, rexp, past)
        oa = oa[:, :ts].reshape(bs * ts, NSA_Q_W).astype(BF16)
        ob, s_fin = _deltanet(qkv, small, zg, state_dn_conv[l], state_dn_S[l], dn_params, tri, bs, ts)
        xs = _merge(xs, oa, ob, mg, *mrg, tm_s)
        xs = _ffn(xs, *f2, tm_s, l)
        outs[4].append(rows.reshape(bs, ts, 4, NSA_KV_HEADS, HEAD_DIM))
        outs[5].append(win.reshape(bs, ts, 2, NSA_KV_HEADS, HEAD_DIM))
        outs[6].append(s_fin)
        outs[7].append(qkv.reshape(bs, ts, DN_QKV_W))

    y_prompt = _final_norm(xp, final_norm[None], tm_p).reshape(b, seq, D_MODEL)
    y_sample = _final_norm(xs, final_norm[None], tm_s).reshape(bs, ts, D_MODEL)
    res = [rows_all] + [jnp.stack(o) for o in outs[1:]]
    res[5] = jnp.concatenate([state_nsa_win, res[5]], axis=2)[:, :, ts:]
    res[7] = jnp.concatenate([state_dn_conv, res[7]], axis=2)[:, :, ts:]
    for i, parts in ((0, 4), (1, 2)):
        r = res[i]
        res[i] = r.reshape(depth, b, parts, NSA_KV_HEADS, HEAD_DIM, r.shape[-1]).transpose(0, 1, 5, 2, 3, 4)
    return (y_prompt, y_sample) + tuple(res)
```
